```python
import math
import jax
import jax.numpy as jnp
from jax import lax
import numpy as np

D_MODEL = 1024
BATCH = 4
SEQ = 4096
DEPTH = 2
DEC_BATCH = 128
DEC_SEQ = 4
PAST_LEN = 16384
PAGE_SIZE = 128

BRANCH_W = D_MODEL // 2
N_BRANCH = 3
HEAD_DIM = 64
N_HEADS = BRANCH_W // HEAD_DIM
N_KV_HEADS = 2
KV_GROUP = N_HEADS // N_KV_HEADS
WINDOW = 128
ROPE_THETA = 10000.0
SSM_W = BRANCH_W
SSM_GROUP_CH = 16
SSM_GROUPS = SSM_W // SSM_GROUP_CH
SSM_STATE = 64
N_MEM = 256
MEM_HEADS = 4
MEM_HEAD_DIM = BRANCH_W // MEM_HEADS
MEM_W = MEM_HEADS * MEM_HEAD_DIM
D_FF = -(-8 * D_MODEL // (3 * 256)) * 256
RMS_EPS = 1e-6

Q_OFF = 0
K_OFF = Q_OFF + N_HEADS * HEAD_DIM
V_OFF = K_OFF + N_KV_HEADS * HEAD_DIM
U_OFF = V_OFF + N_KV_HEADS * HEAD_DIM
MQ_OFF = U_OFF + SSM_W
G_OFF = MQ_OFF + MEM_W
IN_W = G_OFF + N_BRANCH * D_MODEL

kernel_name = 'hybrid_swa_s5_memxattn_decode_step'


def rms_norm(x, g):
    xf = x.astype(jnp.float32)
    xf = xf * lax.rsqrt(jnp.mean(xf * xf, axis=-1, keepdims=True) + RMS_EPS)
    return xf.astype(x.dtype) * g


def rotary(x, pos):
    half = x.shape[-1] // 2
    inv = ROPE_THETA ** (-jnp.arange(half, dtype=jnp.float32) / half)
    ang = pos[:, None] * inv[None, :]
    cos = jnp.cos(ang)[:, None, :]
    sin = jnp.sin(ang)[:, None, :]
    xf = x.astype(jnp.float32)
    x1, x2 = xf[..., :half], xf[..., half:]
    return jnp.concatenate([x1 * cos - x2 * sin, x2 * cos + x1 * sin], axis=-1).astype(x.dtype)


def sink_attention(q, k, v, mask, sinks):
    scale = 1.0 / math.sqrt(q.shape[-1])
    s = jnp.einsum('...qhgd,...khd->...hgqk', q.astype(jnp.float32), k.astype(jnp.float32)) * scale
    s = jnp.where(mask[..., None, None, :, :], s, -jnp.inf)
    sink = sinks.astype(jnp.float32)[:, :, None, None]
    m = jnp.maximum(jnp.max(s, axis=-1, keepdims=True), sink)
    p = jnp.exp(s - m)
    denom = jnp.sum(p, axis=-1, keepdims=True) + jnp.exp(sink - m)
    o = jnp.einsum('...hgqk,...khd->...qhgd', p / denom, v.astype(jnp.float32))
    return o.astype(q.dtype)


def swa_prompt(q, k, v, sinks):
    b, s = q.shape[0], q.shape[1]
    nb = s // WINDOW
    qb = q.reshape(b, nb, WINDOW, N_KV_HEADS, KV_GROUP, HEAD_DIM)

    def band(t):
        tb = t.reshape(b, nb, WINDOW, N_KV_HEADS, HEAD_DIM)
        prev = jnp.pad(tb, ((0, 0), (1, 0), (0, 0), (0, 0), (0, 0)))[:, :-1]
        return jnp.concatenate([prev, tb], axis=2)

    blk = jnp.arange(nb)[:, None] * WINDOW
    qpos = blk + jnp.arange(WINDOW)[None, :]
    kpos = blk - WINDOW + jnp.arange(2 * WINDOW)[None, :]
    diff = qpos[:, :, None] - kpos[:, None, :]
    mask = (diff >= 0) & (diff < WINDOW) & (kpos[:, None, :] >= 0)
    o = sink_attention(qb, band(k), band(v), mask, sinks.reshape(N_KV_HEADS, KV_GROUP))
    return o.reshape(b, s, BRANCH_W)


def swa_sample(q, k, v, past_k, past_v, sinks):
    b, t = q.shape[0], q.shape[1]
    kcat = jnp.concatenate([past_k.astype(k.dtype), k], axis=1)
    vcat = jnp.concatenate([past_v.astype(v.dtype), v], axis=1)
    kpos = jnp.concatenate([PAST_LEN - WINDOW + jnp.arange(WINDOW), PAST_LEN + jnp.arange(t)])
    qpos = PAST_LEN + jnp.arange(t)
    diff = qpos[:, None] - kpos[None, :]
    mask = ((diff >= 0) & (diff < WINDOW) & (kpos[None, :] >= 0))[None]
    qg = q.reshape(b, t, N_KV_HEADS, KV_GROUP, HEAD_DIM)
    o = sink_attention(qg, kcat, vcat, mask, sinks.reshape(N_KV_HEADS, KV_GROUP))
    return o.reshape(b, t, BRANCH_W), kcat[:, -WINDOW:], vcat[:, -WINDOW:]


def ssm_branch(u, h0_re, h0_im, p):
    b, t = u.shape[0], u.shape[1]
    f32 = jnp.float32
    ug = u.astype(f32).reshape(b, t, SSM_GROUPS, SSM_GROUP_CH)
    a_re = p['ssm_a_re'].astype(f32)
    a_im = p['ssm_a_im'].astype(f32)
    dt = jnp.exp(p['ssm_log_dt'].astype(f32))[:, None]
    mag = jnp.exp(a_re * dt)
    lam_re = mag * jnp.cos(a_im * dt)
    lam_im = mag * jnp.sin(a_im * dt)
    den = a_re * a_re + a_im * a_im
    nr, ni = lam_re - 1.0, lam_im
    g_re = (nr * a_re + ni * a_im) / den
    g_im = (ni * a_re - nr * a_im) / den
    bu_re = jnp.einsum('btgc,gpc->btgp', ug, p['ssm_b_re'].astype(f32))
    bu_im = jnp.einsum('btgc,gpc->btgp', ug, p['ssm_b_im'].astype(f32))
    bb_re = g_re * bu_re - g_im * bu_im
    bb_im = g_re * bu_im + g_im * bu_re
    h0r, h0i = h0_re.astype(f32), h0_im.astype(f32)
    bb_re = bb_re.at[:, 0].add(lam_re * h0r - lam_im * h0i)
    bb_im = bb_im.at[:, 0].add(lam_re * h0i + lam_im * h0r)
    ar = jnp.broadcast_to(lam_re, bb_re.shape)
    ai = jnp.broadcast_to(lam_im, bb_im.shape)

    def combine(e1, e2):
        a1r, a1i, b1r, b1i = e1
        a2r, a2i, b2r, b2i = e2
        return (a2r * a1r - a2i * a1i,
                a2r * a1i + a2i * a1r,
                a2r * b1r - a2i * b1i + b2r,
                a2r * b1i + a2i * b1r + b2i)

    _, _, xr, xi = lax.associative_scan(combine, (ar, ai, bb_re, bb_im), axis=1)
    y = (jnp.einsum('btgp,gcp->btgc', xr, p['ssm_c_re'].astype(f32))
         - jnp.einsum('btgp,gcp->btgc', xi, p['ssm_c_im'].astype(f32))
         + p['ssm_d'].astype(f32).reshape(SSM_GROUPS, SSM_GROUP_CH) * ug)
    z = jax.nn.gelu(y.reshape(b, t, SSM_W))
    out = z * jax.nn.sigmoid(z @ p['ssm_w_glu'].astype(f32))
    return out.astype(u.dtype), xr[:, -1], xi[:, -1]


def mem_kv(mem, p):
    b = mem.shape[0]
    kv = rms_norm(mem, p['mem_norm']) @ p['w_mem_kv']
    k = kv[..., :MEM_W].reshape(b, N_MEM, MEM_HEADS, MEM_HEAD_DIM)
    v = kv[..., MEM_W:].reshape(b, N_MEM, MEM_HEADS, MEM_HEAD_DIM)
    return rms_norm(k, p['mem_k_norm']), v


def mem_attend(qm, mk, mv):
    scale = 1.0 / math.sqrt(MEM_HEAD_DIM)
    s = jnp.einsum('bqhd,bkhd->bhqk', qm.astype(jnp.float32), mk.astype(jnp.float32)) * scale
    pr = jax.nn.softmax(s, axis=-1)
    o = jnp.einsum('bhqk,bkhd->bqhd', pr, mv.astype(jnp.float32))
    return o.reshape(qm.shape[0], qm.shape[1], MEM_W).astype(qm.dtype)


def trunk_layer(x, pos, past_k, past_v, h0_re, h0_im, mk, mv, p):
    b, t = x.shape[0], x.shape[1]
    h = rms_norm(x, p['attn_norm'])
    z = h @ p['w_in']
    q = z[..., Q_OFF:K_OFF].reshape(b, t, N_HEADS, HEAD_DIM)
    k = z[..., K_OFF:V_OFF].reshape(b, t, N_KV_HEADS, HEAD_DIM)
    v = z[..., V_OFF:U_OFF].reshape(b, t, N_KV_HEADS, HEAD_DIM)
    u = z[..., U_OFF:MQ_OFF]
    qm = z[..., MQ_OFF:G_OFF].reshape(b, t, MEM_HEADS, MEM_HEAD_DIM)
    gates = jax.nn.sigmoid(z[..., G_OFF:].reshape(b, t, N_BRANCH, D_MODEL))
    q = rotary(rms_norm(q, p['q_norm']), pos)
    k = rotary(rms_norm(k, p['k_norm']), pos)
    if past_k is None:
        o_a = swa_prompt(q, k, v, p['attn_sinks'])
        new_k, new_v = k[:, -WINDOW:], v[:, -WINDOW:]
    else:
        o_a, new_k, new_v = swa_sample(q, k, v, past_k, past_v, p['attn_sinks'])
    o_b, hr, hi = ssm_branch(u, h0_re, h0_im, p)
    o_c = mem_attend(rms_norm(qm, p['mem_q_norm']), mk, mv)
    branches = jnp.stack([o_a, o_b, o_c], axis=2)
    proj = jnp.einsum('btnc,ncd->btnd', branches, p['w_branch'])
    merged = jnp.sum(gates * proj, axis=2)
    x = x + merged @ p['w_out']
    gu = rms_norm(x, p['ffn_norm']) @ p['w_ffn_up']
    x = x + (jax.nn.silu(gu[..., :D_FF]) * gu[..., D_FF:]) @ p['w_ffn_down']
    return x, new_k, new_v, hr, hi


def setup_inputs(seed: int = 0) -> dict:
    key = jax.random.key(seed)
    keys = jax.random.split(key, 40)
    counter = [0]
    f32 = jnp.float32

    def nxt():
        kk = keys[counter[0]]
        counter[0] += 1
        return kk

    def nrm(shape, scale=1.0):
        return jax.random.normal(nxt(), shape, f32) * scale

    def gain(shape):
        return 1.0 + 0.1 * nrm(shape)

    L = DEPTH
    return {
        'x_prompt': nrm((BATCH, SEQ, D_MODEL)),
        'x_sample': nrm((DEC_BATCH, DEC_SEQ, D_MODEL)),
        'cache_swa_k': nrm((L, DEC_BATCH, WINDOW, N_KV_HEADS, HEAD_DIM)),
        'cache_swa_v': nrm((L, DEC_BATCH, WINDOW, N_KV_HEADS, HEAD_DIM)),
        'state_ssm_re': nrm((L, DEC_BATCH, SSM_GROUPS, SSM_STATE), 0.5),
        'state_ssm_im': nrm((L, DEC_BATCH, SSM_GROUPS, SSM_STATE), 0.5),
        'cache_mem_k': nrm((L, DEC_BATCH, N_MEM, MEM_HEADS, MEM_HEAD_DIM)),
        'cache_mem_v': nrm((L, DEC_BATCH, N_MEM, MEM_HEADS, MEM_HEAD_DIM)),
        'mem_prompt': nrm((BATCH, N_MEM, D_MODEL)),
        'attn_norm': gain((L, D_MODEL)),
        'w_in': nrm((L, D_MODEL, IN_W), D_MODEL ** -0.5),
        'q_norm': gain((L, HEAD_DIM)),
        'k_norm': gain((L, HEAD_DIM)),
        'attn_sinks': nrm((L, N_HEADS), 0.5),
        'ssm_a_re': -0.5 + 0.01 * nrm((L, SSM_GROUPS, SSM_STATE)),
        'ssm_a_im': math.pi * jnp.arange(SSM_STATE, dtype=f32) + 0.01 * nrm((L, SSM_GROUPS, SSM_STATE)),
        'ssm_log_dt': jax.random.uniform(nxt(), (L, SSM_GROUPS), f32, math.log(1e-3), math.log(1e-1)),
        'ssm_b_re': nrm((L, SSM_GROUPS, SSM_STATE, SSM_GROUP_CH), (2 * SSM_GROUP_CH) ** -0.5),
        'ssm_b_im': nrm((L, SSM_GROUPS, SSM_STATE, SSM_GROUP_CH), (2 * SSM_GROUP_CH) ** -0.5),
        'ssm_c_re': nrm((L, SSM_GROUPS, SSM_GROUP_CH, SSM_STATE), SSM_STATE ** -0.5),
        'ssm_c_im': nrm((L, SSM_GROUPS, SSM_GROUP_CH, SSM_STATE), SSM_STATE ** -0.5),
        'ssm_d': nrm((L, SSM_W)),
        'ssm_w_glu': nrm((L, SSM_W, SSM_W), SSM_W ** -0.5),
        'mem_norm': gain((L, D_MODEL)),
        'w_mem_kv': nrm((L, D_MODEL, 2 * MEM_W), D_MODEL ** -0.5),
        'mem_q_norm': gain((L, MEM_HEAD_DIM)),
        'mem_k_norm': gain((L, MEM_HEAD_DIM)),
        'w_branch': nrm((L, N_BRANCH, BRANCH_W, D_MODEL), BRANCH_W ** -0.5),
        'w_out': nrm((L, D_MODEL, D_MODEL), D_MODEL ** -0.5),
        'ffn_norm': gain((L, D_MODEL)),
        'w_ffn_up': nrm((L, D_MODEL, 2 * D_FF), D_MODEL ** -0.5),
        'w_ffn_down': nrm((L, D_FF, D_MODEL), D_FF ** -0.5),
    }


def reference(x_prompt, x_sample, cache_swa_k, cache_swa_v, state_ssm_re, state_ssm_im,
              cache_mem_k, cache_mem_v, mem_prompt, attn_norm, w_in, q_norm, k_norm, attn_sinks,
              ssm_a_re, ssm_a_im, ssm_log_dt, ssm_b_re, ssm_b_im, ssm_c_re, ssm_c_im, ssm_d,
              ssm_w_glu, mem_norm, w_mem_kv, mem_q_norm, mem_k_norm, w_branch, w_out, ffn_norm,
              w_ffn_up, w_ffn_down):
    pos_p = jnp.arange(x_prompt.shape[1], dtype=jnp.float32)
    pos_s = PAST_LEN + jnp.arange(x_sample.shape[1], dtype=jnp.float32)
    h0 = jnp.zeros((x_prompt.shape[0], SSM_GROUPS, SSM_STATE), jnp.float32)
    y_p, y_s = x_prompt, x_sample
    kp_l, vp_l, hrp_l, hip_l, mkp_l, mvp_l = [], [], [], [], [], []
    ks_l, vs_l, hrs_l, his_l = [], [], [], []
    for l in range(DEPTH):
        p = {
            'attn_norm': attn_norm[l], 'w_in': w_in[l], 'q_norm': q_norm[l], 'k_norm': k_norm[l],
            'attn_sinks': attn_sinks[l], 'ssm_a_re': ssm_a_re[l], 'ssm_a_im': ssm_a_im[l],
            'ssm_log_dt': ssm_log_dt[l], 'ssm_b_re': ssm_b_re[l], 'ssm_b_im': ssm_b_im[l],
            'ssm_c_re': ssm_c_re[l], 'ssm_c_im': ssm_c_im[l], 'ssm_d': ssm_d[l],
            'ssm_w_glu': ssm_w_glu[l], 'mem_norm': mem_norm[l], 'w_mem_kv': w_mem_kv[l],
            'mem_q_norm': mem_q_norm[l], 'mem_k_norm': mem_k_norm[l], 'w_branch': w_branch[l],
            'w_out': w_out[l], 'ffn_norm': ffn_norm[l], 'w_ffn_up': w_ffn_up[l],
            'w_ffn_down': w_ffn_down[l],
        }
        mk_p, mv_p = mem_kv(mem_prompt, p)
        y_p, kp, vp, hrp, hip = trunk_layer(y_p, pos_p, None, None, h0, h0, mk_p, mv_p, p)
        y_s, ks, vs, hrs, his = trunk_layer(y_s, pos_s, cache_swa_k[l], cache_swa_v[l],
                                            state_ssm_re[l], state_ssm_im[l],
                                            cache_mem_k[l], cache_mem_v[l], p)
        kp_l.append(kp); vp_l.append(vp); hrp_l.append(hrp); hip_l.append(hip)
        mkp_l.append(mk_p); mvp_l.append(mv_p)
        ks_l.append(ks); vs_l.append(vs); hrs_l.append(hrs); his_l.append(his)
    swa_k_prompt = jnp.stack(kp_l)
    swa_v_prompt = jnp.stack(vp_l)
    ssm_re_prompt = jnp.stack(hrp_l)
    ssm_im_prompt = jnp.stack(hip_l)
    mem_k_prompt = jnp.stack(mkp_l)
    mem_v_prompt = jnp.stack(mvp_l)
    swa_k_sample = jnp.stack(ks_l)
    swa_v_sample = jnp.stack(vs_l)
    ssm_re_sample = jnp.stack(hrs_l)
    ssm_im_sample = jnp.stack(his_l)
    return (y_p, y_s, swa_k_prompt, swa_v_prompt, ssm_re_prompt, ssm_im_prompt,
            mem_k_prompt, mem_v_prompt, swa_k_sample, swa_v_sample, ssm_re_sample, ssm_im_sample)
```

```python
import functools
import math

import jax
import jax.numpy as jnp
from jax import lax
from jax.experimental import pallas as pl
from jax.experimental.pallas import tpu as pltpu

F32 = jnp.float32
BF16 = jnp.bfloat16

D_MODEL = 1024
BRANCH_W = 512
HEAD_DIM = 64
N_HEADS = 8
N_KV_HEADS = 2
KV_GROUP = 4
KV_W = N_KV_HEADS * HEAD_DIM
WINDOW = 128
ROPE_THETA = 10000.0
PAST_LEN = 16384
SSM_GROUPS = 32
SSM_GROUP_CH = 16
SSM_STATE = 64
N_MEM = 256
MEM_HEADS = 4
MEM_HEAD_DIM = 128
D_FF = 2816
RMS_EPS = 1e-6

Q_OFF, K_OFF, V_OFF, U_OFF, MQ_OFF, G_OFF = 0, 512, 640, 768, 1280, 1792
IN_W = G_OFF + 3 * D_MODEL

LANES = 128
GROUPS_PER_LANE_BLOCK = LANES // SSM_GROUP_CH
N_LANE_BLOCKS = BRANCH_W // LANES
STATE_W = GROUPS_PER_LANE_BLOCK * SSM_STATE
PROMPT_CHUNK = 8
ROW_TILE = 512
VMEM_LIMIT = 56 * 1024 * 1024


def _dot(a, b):
    return jnp.dot(a, b, preferred_element_type=F32)


def _dot_nt(a, b):
    return lax.dot_general(a, b, (((1,), (1,)), ((), ())), preferred_element_type=F32)


def _cparams(n_axes):
    return pltpu.CompilerParams(dimension_semantics=("arbitrary",) * n_axes,
                                vmem_limit_bytes=VMEM_LIMIT)


def _full(shape):
    nd = len(shape)
    return pl.BlockSpec(shape, lambda *_: (0,) * nd)


def _layer(shape, l):
    nd = len(shape)
    return pl.BlockSpec((None,) + tuple(shape), lambda *_: (l,) + (0,) * nd)


def _rms(x):
    return x * lax.rsqrt(jnp.mean(x * x, axis=-1, keepdims=True) + RMS_EPS)


def _memkv_kernel(mem_ref, g_ref, w_ref, kg_ref, k_ref, v_ref):
    h = _rms(mem_ref[...]) * g_ref[...]
    kv = _dot(h.astype(BF16), w_ref[...])
    kg = kg_ref[...]
    for hd in range(MEM_HEADS):
        sl = slice(hd * MEM_HEAD_DIM, (hd + 1) * MEM_HEAD_DIM)
        k_ref[:, sl] = _rms(kv[:, sl]) * kg
    v_ref[...] = kv[:, BRANCH_W:]


def _memkv(l, mem2d, g, w_bf, kg):
    n = mem2d.shape[0]
    return pl.pallas_call(
        _memkv_kernel, name="memkv",
        grid=(n // ROW_TILE,),
        in_specs=[pl.BlockSpec((ROW_TILE, D_MODEL), lambda i: (i, 0)),
                  _layer((1, D_MODEL), l), _layer((D_MODEL, 2 * BRANCH_W), l), _layer((1, MEM_HEAD_DIM), l)],
        out_specs=[pl.BlockSpec((ROW_TILE, BRANCH_W), lambda i: (i, 0))] * 2,
        out_shape=[jax.ShapeDtypeStruct((n, BRANCH_W), F32)] * 2,
        compiler_params=_cparams(1),
    )(mem2d, g, w_bf, kg)


def _rotary(x, cos, sin_lo, sin_hi):
    w = x.shape[-1]
    half = HEAD_DIM // 2
    return x * cos + pltpu.roll(x, w - half, 1) * sin_lo + pltpu.roll(x, half, 1) * sin_hi


def _inproj_kernel(x_ref, g_ref, w_ref, cos_ref, slo_ref, shi_ref, qg_ref, kg_ref, mqg_ref, ones_ref,
                   q_ref, k_ref, v_ref, u_ref, mq_ref, gates_ref):
    hb = (_rms(x_ref[...]) * g_ref[...]).astype(BF16)
    ones = ones_ref[...]
    cos, slo, shi = cos_ref[...], slo_ref[...], shi_ref[...]

    zq = _dot(hb, w_ref[:, Q_OFF:K_OFF])
    sq = (zq * zq).astype(BF16)
    ssq = jnp.concatenate([_dot(sq[:, :256], ones), _dot(sq[:, 256:], ones)], axis=1)
    qn = zq * lax.rsqrt(ssq * (1.0 / HEAD_DIM) + RMS_EPS) * qg_ref[...]
    cos4, slo4, shi4 = (jnp.concatenate([t] * 4, axis=1) for t in (cos, slo, shi))
    q_ref[...] = (_rotary(qn, cos4, slo4, shi4) * (1.0 / math.sqrt(HEAD_DIM))).astype(BF16)

    zkv = _dot(hb, w_ref[:, K_OFF:U_OFF])
    zk = zkv[:, :KV_W]
    ssk = _dot((zk * zk).astype(BF16), ones[:KV_W, :KV_W])
    kn = zk * lax.rsqrt(ssk * (1.0 / HEAD_DIM) + RMS_EPS) * kg_ref[...]
    k_ref[...] = _rotary(kn, cos, slo, shi)
    v_ref[...] = zkv[:, KV_W:]

    u_ref[...] = _dot(hb, w_ref[:, U_OFF:MQ_OFF])

    zm = _dot(hb, w_ref[:, MQ_OFF:G_OFF])
    mqg = mqg_ref[...]
    for hd in range(MEM_HEADS):
        sl = slice(hd * MEM_HEAD_DIM, (hd + 1) * MEM_HEAD_DIM)
        mq_ref[:, sl] = (_rms(zm[:, sl]) * mqg * (1.0 / math.sqrt(MEM_HEAD_DIM))).astype(BF16)

    for c in range(3 * D_MODEL // 512):
        lo = G_OFF + c * 512
        gates_ref[:, c * 512:(c + 1) * 512] = jax.nn.sigmoid(_dot(hb, w_ref[:, lo:lo + 512]))


def _inproj(l, x2d, g, w_bf, rope, rope_blocks, qg, kg, mqg, ones):
    n = x2d.shape[0]
    row = lambda i: (i, 0)
    rspec = pl.BlockSpec((ROW_TILE, LANES), lambda i: (i % rope_blocks, 0))
    widths = (BRANCH_W, KV_W, KV_W, BRANCH_W, BRANCH_W, 3 * D_MODEL)
    dtypes = (BF16, F32, F32, F32, BF16, F32)
    return pl.pallas_call(
        _inproj_kernel, name="inproj",
        grid=(n // ROW_TILE,),
        in_specs=[pl.BlockSpec((ROW_TILE, D_MODEL), row), _layer((1, D_MODEL), l), _layer((D_MODEL, IN_W), l),
                  rspec, rspec, rspec, _layer((1, BRANCH_W), l), _layer((1, KV_W), l),
                  _layer((1, MEM_HEAD_DIM), l), _full((256, 256))],
        out_specs=[pl.BlockSpec((ROW_TILE, w), row) for w in widths],
        out_shape=[jax.ShapeDtypeStruct((n, w), d) for w, d in zip(widths, dtypes)],
        compiler_params=_cparams(1),
    )(x2d, g, w_bf, *rope, qg, kg, mqg, ones)


def _softmax_rows(s):
    m = jnp.max(s, axis=-1, keepdims=True)
    p = jnp.exp(s - m)
    return p, jnp.sum(p, axis=-1, keepdims=True)


def _sink_column(sink_ref, l, heads, rows_per_head):
    n = len(heads) * rows_per_head
    rcol = lax.broadcasted_iota(jnp.int32, (n, 1), 0)
    sink = jnp.full((n, 1), sink_ref[l, heads[-1]], F32)
    for g in range(len(heads) - 2, -1, -1):
        sink = jnp.where(rcol < (g + 1) * rows_per_head, sink_ref[l, heads[g]], sink)
    return sink


def _sink_softmax(s, sink):
    m = jnp.maximum(jnp.max(s, axis=-1, keepdims=True), sink)
    p = jnp.exp(s - m)
    return p, jnp.sum(p, axis=-1, keepdims=True) + jnp.exp(sink - m)


def _attn_prompt_kernel(l, sink_ref, q_ref, kp_ref, kc_ref, vp_ref, vc_ref, qm_ref, mk_ref, mv_ref,
                        oa_ref, oc_ref):
    blk = pl.program_id(1)
    rows = KV_GROUP * WINDOW
    r = lax.broadcasted_iota(jnp.int32, (rows, 2 * WINDOW), 0)
    j = lax.broadcasted_iota(jnp.int32, (rows, 2 * WINDOW), 1)
    qi = r % WINDOW
    mask = (j > qi) & (j <= qi + WINDOW) & ((blk > 0) | (j >= WINDOW))
    q = q_ref[...]
    for kvh in range(N_KV_HEADS):
        ksl = slice(kvh * HEAD_DIM, (kvh + 1) * HEAD_DIM)
        kcat = jnp.concatenate([kp_ref[:, ksl], kc_ref[:, ksl]], axis=0).astype(BF16)
        vcat = jnp.concatenate([vp_ref[:, ksl], vc_ref[:, ksl]], axis=0).astype(BF16)
        heads = [kvh * KV_GROUP + g for g in range(KV_GROUP)]
        qg = jnp.concatenate([q[:, hd * HEAD_DIM:(hd + 1) * HEAD_DIM] for hd in heads], axis=0)
        s = jnp.where(mask, _dot_nt(qg, kcat), -jnp.inf)
        p, denom = _sink_softmax(s, _sink_column(sink_ref, l, heads, WINDOW))
        o = _dot(p.astype(BF16), vcat) / denom
        for g, hd in enumerate(heads):
            oa_ref[:, hd * HEAD_DIM:(hd + 1) * HEAD_DIM] = o[g * WINDOW:(g + 1) * WINDOW].astype(BF16)

    qm = qm_ref[...]
    for hd in range(MEM_HEADS):
        sl = slice(hd * MEM_HEAD_DIM, (hd + 1) * MEM_HEAD_DIM)
        p, den = _softmax_rows(_dot_nt(qm[:, sl], mk_ref[:, sl].astype(BF16)))
        oc_ref[:, sl] = (_dot(p.astype(BF16), mv_ref[:, sl].astype(BF16)) / den).astype(BF16)


def _attn_prompt(l, sinks, q, k, v, qm, mk, mv):
    b, s, _ = q.shape
    cur = lambda bi, i: (bi, i, 0)
    prev = lambda bi, i: (bi, jnp.maximum(i - 1, 0), 0)
    per_b = lambda bi, i: (bi, 0, 0)
    kv_spec = lambda im: pl.BlockSpec((None, WINDOW, KV_W), im)
    return pl.pallas_call(
        functools.partial(_attn_prompt_kernel, l), name="attn_prompt",
        grid=(b, s // WINDOW),
        in_specs=[pl.BlockSpec(memory_space=pltpu.SMEM),
                  pl.BlockSpec((None, WINDOW, BRANCH_W), cur),
                  kv_spec(prev), kv_spec(cur), kv_spec(prev), kv_spec(cur),
                  pl.BlockSpec((None, WINDOW, BRANCH_W), cur),
                  pl.BlockSpec((None, N_MEM, BRANCH_W), per_b),
                  pl.BlockSpec((None, N_MEM, BRANCH_W), per_b)],
        out_specs=[pl.BlockSpec((None, WINDOW, BRANCH_W), cur)] * 2,
        out_shape=[jax.ShapeDtypeStruct((b, s, BRANCH_W), BF16)] * 2,
        compiler_params=_cparams(2),
    )(sinks, q, k, k, v, v, qm, mk, mv)


DEC_BLOCK = 8
TOK_PAD = 8


def _stack_heads(x, n_heads, head_w):
    t, w = x.shape
    x8 = jnp.concatenate([x, jnp.zeros((TOK_PAD - t, w), F32)], axis=0)
    rr = lax.broadcasted_iota(jnp.int32, (n_heads * TOK_PAD, w), 0)
    ll = lax.broadcasted_iota(jnp.int32, (n_heads * TOK_PAD, w), 1)
    own = (ll // head_w) == (rr // TOK_PAD)
    return jnp.where(own, jnp.concatenate([x8] * n_heads, axis=0), 0.0).astype(BF16), own


def _unstack_heads(o, own, n_heads, t):
    o = jnp.where(own, o, 0.0)
    acc = o[0:TOK_PAD]
    for hd in range(1, n_heads):
        acc = acc + o[hd * TOK_PAD:(hd + 1) * TOK_PAD]
    return acc[0:t]


def _attn_sample_kernel(l, sink_ref, q_ref, kn_ref, vn_ref, ck_ref, cv_ref, qm_ref, mk_ref, mv_ref, e_ref,
                        oa_ref, oc_ref, nk_ref, nv_ref, kc_scr, vc_scr):
    t = q_ref.shape[1]
    keys = 2 * WINDOW
    rows = N_HEADS * TOK_PAD
    rk = lax.broadcasted_iota(jnp.int32, (rows, keys), 0)
    jk = lax.broadcasted_iota(jnp.int32, (rows, keys), 1)
    tk = rk % TOK_PAD
    mask = (tk < t) & (((jk < WINDOW) & (jk > tk)) | ((jk >= WINDOW) & (jk - WINDOW <= tk)))
    sink = _sink_column(sink_ref, l, list(range(N_HEADS)), TOK_PAD)
    e = e_ref[...]
    kc_scr[WINDOW:, :] = jnp.zeros((WINDOW, KV_W), F32)
    vc_scr[WINDOW:, :] = jnp.zeros((WINDOW, KV_W), F32)
    for bi in range(DEC_BLOCK):
        kc_scr[:WINDOW, :] = ck_ref[bi]
        vc_scr[:WINDOW, :] = cv_ref[bi]
        kc_scr[WINDOW:WINDOW + t, :] = kn_ref[bi]
        vc_scr[WINDOW:WINDOW + t, :] = vn_ref[bi]
        nk_ref[bi] = kc_scr[t:WINDOW + t, :]
        nv_ref[bi] = vc_scr[t:WINDOW + t, :]
        kx = _dot(kc_scr[...].astype(BF16), e).astype(BF16)
        vx = _dot(vc_scr[...].astype(BF16), e).astype(BF16)
        qs, own = _stack_heads(q_ref[bi], N_HEADS, HEAD_DIM)
        p, denom = _sink_softmax(jnp.where(mask, _dot_nt(qs, kx), -jnp.inf), sink)
        oa_ref[bi] = _unstack_heads(_dot(p.astype(BF16), vx) / denom, own, N_HEADS, t)

        ms, mown = _stack_heads(qm_ref[bi], MEM_HEADS, MEM_HEAD_DIM)
        pm, den = _softmax_rows(_dot_nt(ms, mk_ref[bi].astype(BF16)))
        oc_ref[bi] = _unstack_heads(_dot(pm.astype(BF16), mv_ref[bi].astype(BF16)) / den, mown, MEM_HEADS, t)


def _attn_sample(l, sinks, q, kn, vn, ck, cv, qm, mk, mv, e):
    b, t, _ = q.shape
    blk = lambda i: (i, 0, 0)
    lblk = lambda i: (l, i, 0, 0)
    tok = lambda w: pl.BlockSpec((DEC_BLOCK, t, w), blk)
    cache_in = pl.BlockSpec((None, DEC_BLOCK, WINDOW, KV_W), lblk)
    cache_out = pl.BlockSpec((DEC_BLOCK, WINDOW, KV_W), blk)
    mem = pl.BlockSpec((None, DEC_BLOCK, N_MEM, BRANCH_W), lblk)
    return pl.pallas_call(
        functools.partial(_attn_sample_kernel, l), name="attn_sample",
        grid=(b // DEC_BLOCK,),
        in_specs=[pl.BlockSpec(memory_space=pltpu.SMEM), tok(BRANCH_W), tok(KV_W), tok(KV_W),
                  cache_in, cache_in, tok(BRANCH_W), mem, mem, _full((KV_W, BRANCH_W))],
        out_specs=[tok(BRANCH_W), tok(BRANCH_W), cache_out, cache_out],
        out_shape=[jax.ShapeDtypeStruct((b, t, BRANCH_W), F32)] * 2
                  + [jax.ShapeDtypeStruct((b, WINDOW, KV_W), F32)] * 2,
        scratch_shapes=[pltpu.VMEM((2 * WINDOW, KV_W), F32)] * 2,
        compiler_params=_cparams(1),
    )(sinks, q, kn, vn, ck, cv, qm, mk, mv, e)


def _ssm_tables(a_re, a_im, log_dt, b_re, b_im, c_re, c_im, chunk):
    hp = lax.Precision.HIGHEST
    dt = jnp.exp(log_dt)[..., None]
    mag = jnp.exp(a_re * dt)
    lam_re, lam_im = mag * jnp.cos(a_im * dt), mag * jnp.sin(a_im * dt)
    den = a_re * a_re + a_im * a_im
    nr, ni = lam_re - 1.0, lam_im
    g_re = (nr * a_re + ni * a_im) / den
    g_im = (ni * a_re - nr * a_im) / den
    bg_re = g_re[..., None] * b_re - g_im[..., None] * b_im
    bg_im = g_re[..., None] * b_im + g_im[..., None] * b_re
    d = jnp.arange(chunk + 1, dtype=F32)[:, None, None, None]
    pm = jnp.exp(d * (a_re * dt)[None])
    pw_re = pm * jnp.cos(d * (a_im * dt)[None])
    pw_im = pm * jnp.sin(d * (a_im * dt)[None])
    z_re = pw_re[..., None] * bg_re[None] - pw_im[..., None] * bg_im[None]
    z_im = pw_re[..., None] * bg_im[None] + pw_im[..., None] * bg_re[None]
    kern = (jnp.einsum('lgop,dlgpi->dlgio', c_re, z_re[:chunk], precision=hp)
            - jnp.einsum('lgop,dlgpi->dlgio', c_im, z_im[:chunk], precision=hp))
    nl = a_re.shape[0]
    nb, gb = N_LANE_BLOCKS, GROUPS_PER_LANE_BLOCK
    eye = jnp.eye(gb, dtype=F32)
    lag = jnp.arange(chunk)[None, :] - jnp.arange(chunk)[:, None]
    toe = jnp.where((lag >= 0)[:, :, None, None, None, None],
                    kern[jnp.clip(lag, 0, chunk - 1)], 0.0)
    toe = toe.reshape(chunk, chunk, nl, nb, gb, SSM_GROUP_CH, SSM_GROUP_CH)
    intra = jnp.einsum('stlngio,gh->lnsgitho', toe, eye, precision=hp)
    intra = intra.reshape(nl, nb, chunk * LANES, chunk * LANES)

    zr = z_re[:chunk][::-1].reshape(chunk, nl, nb, gb, SSM_STATE, SSM_GROUP_CH)
    zi = z_im[:chunk][::-1].reshape(chunk, nl, nb, gb, SSM_STATE, SSM_GROUP_CH)
    to_re = jnp.einsum('slngpi,gh->lnsgihp', zr, eye, precision=hp)
    to_im = jnp.einsum('slngpi,gh->lnsgihp', zi, eye, precision=hp)
    to_st = jnp.concatenate([to_re.reshape(nl, nb, chunk * LANES, STATE_W),
                             to_im.reshape(nl, nb, chunk * LANES, STATE_W)], axis=-1)

    pr, pi = pw_re[1:], pw_im[1:]
    f_a = c_re[None] * pr[:, :, :, None, :] - c_im[None] * pi[:, :, :, None, :]
    f_b = c_re[None] * pi[:, :, :, None, :] + c_im[None] * pr[:, :, :, None, :]
    f_a = f_a.reshape(chunk, nl, nb, gb, SSM_GROUP_CH, SSM_STATE)
    f_b = f_b.reshape(chunk, nl, nb, gb, SSM_GROUP_CH, SSM_STATE)
    fr = jnp.einsum('tlngop,gh->lngptho', f_a, eye, precision=hp).reshape(nl, nb, STATE_W, chunk * LANES)
    fi = jnp.einsum('tlngop,gh->lngptho', -f_b, eye, precision=hp).reshape(nl, nb, STATE_W, chunk * LANES)
    from_st = jnp.concatenate([fr, fi], axis=2)

    dec_re = pw_re[chunk].reshape(nl, nb, 1, STATE_W)
    dec_im = pw_im[chunk].reshape(nl, nb, 1, STATE_W)
    return intra.astype(BF16), to_st.astype(BF16), from_st.astype(BF16), dec_re, dec_im


SSM_SEQS = 2


def _ssm_prompt_kernel(u_ref, intra_ref, to_ref, from_ref, dre_ref, dim_ref,
                       y_ref, hre_ref, him_ref, st_scr):
    chunk = u_ref.shape[1]
    nchunks = u_ref.shape[0] // SSM_SEQS
    x = jnp.concatenate([u_ref[:, s, :] for s in range(chunk)], axis=1).astype(BF16)
    y = _dot(x, intra_ref[...])
    add_all = _dot(x, to_ref[...])
    nblk = STATE_W // LANES
    for c in range(2 * nblk):
        st_scr[c] = add_all[:, c * LANES:(c + 1) * LANES]
    dre = [dre_ref[:, c * LANES:(c + 1) * LANES] for c in range(nblk)]
    dim = [dim_ref[:, c * LANES:(c + 1) * LANES] for c in range(nblk)]

    def step(k, carry):
        hr, hi = carry
        rows = pl.ds(k, SSM_SEQS, stride=nchunks)
        new_r, new_i = [], []
        for c in range(nblk):
            add_r, add_i = st_scr[c, rows, :], st_scr[nblk + c, rows, :]
            st_scr[c, rows, :] = hr[c]
            st_scr[nblk + c, rows, :] = hi[c]
            new_r.append(dre[c] * hr[c] - dim[c] * hi[c] + add_r)
            new_i.append(dre[c] * hi[c] + dim[c] * hr[c] + add_i)
        return tuple(new_r), tuple(new_i)

    zero = tuple(jnp.zeros((SSM_SEQS, LANES), F32) for _ in range(nblk))
    hr, hi = lax.fori_loop(0, nchunks, step, (zero, zero))
    hre_ref[...] = jnp.concatenate(hr, axis=1)
    him_ref[...] = jnp.concatenate(hi, axis=1)
    h_in = jnp.concatenate([st_scr[c] for c in range(2 * nblk)], axis=1).astype(BF16)
    y = y + _dot(h_in, from_ref[...])
    for t in range(chunk):
        y_ref[:, t, :] = y[:, t * LANES:(t + 1) * LANES]


def _ssm_prompt(l, u, intra, to_st, from_st, dec_re, dec_im):
    b, s, _ = u.shape
    chunk = PROMPT_CHUNK
    nchunks = s // chunk
    u4 = u.reshape(b * nchunks, chunk, BRANCH_W)
    rows = SSM_SEQS * nchunks
    kw = chunk * LANES
    data = pl.BlockSpec((rows, chunk, LANES), lambda j, h: (h, 0, j))
    wspec = lambda r, c: pl.BlockSpec((None, None, r, c), lambda j, h: (l, j, 0, 0))
    st_out = pl.BlockSpec((None, None, SSM_SEQS, STATE_W), lambda j, h: (j, h, 0, 0))
    st_shape = jax.ShapeDtypeStruct((N_LANE_BLOCKS, b // SSM_SEQS, SSM_SEQS, STATE_W), F32)
    y, hre, him = pl.pallas_call(
        _ssm_prompt_kernel, name="ssm_prompt",
        grid=(N_LANE_BLOCKS, b // SSM_SEQS),
        in_specs=[data, wspec(kw, kw), wspec(kw, 2 * STATE_W), wspec(2 * STATE_W, kw),
                  wspec(1, STATE_W), wspec(1, STATE_W)],
        out_specs=[data, st_out, st_out],
        out_shape=[jax.ShapeDtypeStruct(u4.shape, F32), st_shape, st_shape],
        scratch_shapes=[pltpu.VMEM((2 * STATE_W // LANES, rows, LANES), F32)],
        compiler_params=_cparams(2),
    )(u4, intra, to_st, from_st, dec_re, dec_im)
    return y.reshape(b, s, BRANCH_W), hre, him


def _ssm_sample_kernel(u_ref, h0r_ref, h0i_ref, intra_ref, to_ref, from_ref, dre_ref, dim_ref,
                       y_ref, hre_ref, him_ref):
    chunk = u_ref.shape[1]
    x = jnp.concatenate([u_ref[:, s, :] for s in range(chunk)], axis=1).astype(BF16)
    hr, hi = h0r_ref[...], h0i_ref[...]
    dre, dim = dre_ref[...], dim_ref[...]
    add = _dot(x, to_ref[...])
    hre_ref[...] = dre * hr - dim * hi + add[:, :STATE_W]
    him_ref[...] = dre * hi + dim * hr + add[:, STATE_W:]
    h0 = jnp.concatenate([hr, hi], axis=1).astype(BF16)
    y = _dot(x, intra_ref[...]) + _dot(h0, from_ref[...])
    for t in range(chunk):
        y_ref[:, t, :] = y[:, t * LANES:(t + 1) * LANES]


def _ssm_sample(l, u, h0_re, h0_im, intra, to_st, from_st, dec_re, dec_im):
    b, t, _ = u.shape
    kw = t * LANES
    data = pl.BlockSpec((b, t, LANES), lambda j: (0, 0, j))
    st_in = pl.BlockSpec((None, b, STATE_W), lambda j: (l, 0, j))
    st_out = pl.BlockSpec((b, STATE_W), lambda j: (0, j))
    wspec = lambda r, c: pl.BlockSpec((None, None, r, c), lambda j: (l, j, 0, 0))
    return pl.pallas_call(
        _ssm_sample_kernel, name="ssm_sample",
        grid=(N_LANE_BLOCKS,),
        in_specs=[data, st_in, st_in, wspec(kw, kw), wspec(kw, 2 * STATE_W), wspec(2 * STATE_W, kw),
                  wspec(1, STATE_W), wspec(1, STATE_W)],
        out_specs=[data, st_out, st_out],
        out_shape=[jax.ShapeDtypeStruct(u.shape, F32),
                   jax.ShapeDtypeStruct((b, N_LANE_BLOCKS * STATE_W), F32),
                   jax.ShapeDtypeStruct((b, N_LANE_BLOCKS * STATE_W), F32)],
        compiler_params=_cparams(1),
    )(u, h0_re, h0_im, intra, to_st, from_st, dec_re, dec_im)


def _merge_kernel(x_ref, oa_ref, y_ref, u_ref, oc_ref, gates_ref, d_ref, wglu_ref, wb_ref, wout_ref,
                  o_ref):
    z = jax.nn.gelu(y_ref[...] + d_ref[...] * u_ref[...])
    ob = (z * jax.nn.sigmoid(_dot(z.astype(BF16), wglu_ref[...]))).astype(BF16)
    merged = gates_ref[:, 0:D_MODEL] * _dot(oa_ref[...], wb_ref[0])
    merged = merged + gates_ref[:, D_MODEL:2 * D_MODEL] * _dot(ob, wb_ref[1])
    merged = merged + gates_ref[:, 2 * D_MODEL:3 * D_MODEL] * _dot(oc_ref[...], wb_ref[2])
    o_ref[...] = x_ref[...] + _dot(merged.astype(BF16), wout_ref[...])


def _merge(l, x2d, oa, y, u, oc, gates, d, wglu, wb, wout):
    n = x2d.shape[0]
    row = lambda w: pl.BlockSpec((ROW_TILE, w), lambda i: (i, 0))
    return pl.pallas_call(
        _merge_kernel, name="merge",
        grid=(n // ROW_TILE,),
        in_specs=[row(D_MODEL), row(BRANCH_W), row(BRANCH_W), row(BRANCH_W), row(BRANCH_W),
                  row(3 * D_MODEL), _layer((1, BRANCH_W), l), _layer((BRANCH_W, BRANCH_W), l),
                  _layer((3, BRANCH_W, D_MODEL), l), _layer((D_MODEL, D_MODEL), l)],
        out_specs=row(D_MODEL),
        out_shape=jax.ShapeDtypeStruct((n, D_MODEL), F32),
        compiler_params=_cparams(1),
    )(x2d, oa, y, u, oc, gates, d, wglu, wb, wout)


FF_CHUNKS = ((0, 512), (512, 512), (1024, 512), (1536, 512), (2048, 512), (2560, 256))


def _ffn_kernel(x_ref, g_ref, wup_ref, wdn_ref, o_ref):
    x = x_ref[...]
    h = (_rms(x) * g_ref[...]).astype(BF16)
    acc = x
    for lo, w in FF_CHUNKS:
        gate = _dot(h, wup_ref[:, lo:lo + w])
        up = _dot(h, wup_ref[:, D_FF + lo:D_FF + lo + w])
        acc = acc + _dot((jax.nn.silu(gate) * up).astype(BF16), wdn_ref[lo:lo + w, :])
    o_ref[...] = acc


def _ffn(l, x2d, g, wup, wdn):
    n = x2d.shape[0]
    row = pl.BlockSpec((ROW_TILE, D_MODEL), lambda i: (i, 0))
    return pl.pallas_call(
        _ffn_kernel, name="ffn",
        grid=(n // ROW_TILE,),
        in_specs=[row, _layer((1, D_MODEL), l), _layer((D_MODEL, 2 * D_FF), l), _layer((D_FF, D_MODEL), l)],
        out_specs=row,
        out_shape=jax.ShapeDtypeStruct((n, D_MODEL), F32),
        compiler_params=_cparams(1),
    )(x2d, g, wup, wdn)


def _rope_tables(pos):
    half = HEAD_DIM // 2
    inv = ROPE_THETA ** (-jnp.arange(half, dtype=F32) / half)
    ang = pos[:, None] * inv[None, :]
    cos, sin = jnp.cos(ang), jnp.sin(ang)
    zero = jnp.zeros_like(sin)
    two = lambda a: jnp.concatenate([a, a], axis=1)
    return (two(jnp.concatenate([cos, cos], axis=1)), two(jnp.concatenate([-sin, zero], axis=1)),
            two(jnp.concatenate([zero, sin], axis=1)))


def _state_from_blocks(h):
    nb, hb, sq, _ = h.shape
    return jnp.transpose(h.reshape(nb, hb * sq, STATE_W), (1, 0, 2)).reshape(hb * sq, SSM_GROUPS, SSM_STATE)


def kernel(x_prompt, x_sample, cache_swa_k, cache_swa_v, state_ssm_re, state_ssm_im, cache_mem_k,
           cache_mem_v, mem_prompt, attn_norm, w_in, q_norm, k_norm, attn_sinks, ssm_a_re, ssm_a_im,
           ssm_log_dt, ssm_b_re, ssm_b_im, ssm_c_re, ssm_c_im, ssm_d, ssm_w_glu, mem_norm, w_mem_kv,
           mem_q_norm, mem_k_norm, w_branch, w_out, ffn_norm, w_ffn_up, w_ffn_down):
    depth = w_in.shape[0]
    b, s, _ = x_prompt.shape
    db, t, _ = x_sample.shape
    assert db * t == ROW_TILE and s % ROW_TILE == 0

    rope_p = _rope_tables(jnp.arange(s, dtype=F32))
    rope_s = _rope_tables(jnp.tile(PAST_LEN + jnp.arange(t, dtype=F32), db))
    ssm_params = (ssm_a_re, ssm_a_im, ssm_log_dt, ssm_b_re, ssm_b_im, ssm_c_re, ssm_c_im)
    tab_p = _ssm_tables(*ssm_params, PROMPT_CHUNK)
    tab_s = _ssm_tables(*ssm_params, t)
    blk = jnp.arange(256) // HEAD_DIM
    ones = (blk[:, None] == blk[None, :]).astype(BF16)
    lane = jnp.arange(BRANCH_W)[None, :]
    src = jnp.arange(KV_W)[:, None]
    expand = ((lane // (HEAD_DIM * KV_GROUP) == src // HEAD_DIM)
              & (lane % HEAD_DIM == src % HEAD_DIM)).astype(BF16)

    w_in_bf, w_mem_bf = w_in.astype(BF16), w_mem_kv.astype(BF16)
    wglu, wb, wout = ssm_w_glu.astype(BF16), w_branch.astype(BF16), w_out.astype(BF16)
    wup, wdn = w_ffn_up.astype(BF16), w_ffn_down.astype(BF16)
    row = lambda a: a[:, None, :]
    g_attn, g_mem, g_ffn, d_row = row(attn_norm), row(mem_norm), row(ffn_norm), row(ssm_d)
    qg, kg = row(jnp.tile(q_norm, (1, N_HEADS))), row(jnp.tile(k_norm, (1, N_KV_HEADS)))
    mqg, mkg = row(mem_q_norm), row(mem_k_norm)
    ck = cache_swa_k.reshape(depth, db, WINDOW, KV_W)
    cv = cache_swa_v.reshape(depth, db, WINDOW, KV_W)
    cmk = cache_mem_k.reshape(depth, db, N_MEM, BRANCH_W)
    cmv = cache_mem_v.reshape(depth, db, N_MEM, BRANCH_W)
    h0r = state_ssm_re.reshape(depth, db, SSM_GROUPS * SSM_STATE)
    h0i = state_ssm_im.reshape(depth, db, SSM_GROUPS * SSM_STATE)

    yp = x_prompt.reshape(b * s, D_MODEL)
    ys = x_sample.reshape(db * t, D_MODEL)
    mem2d = mem_prompt.reshape(b * N_MEM, D_MODEL)
    outs = [[] for _ in range(10)]
    for l in range(depth):
        mk, mv = _memkv(l, mem2d, g_mem, w_mem_bf, mkg)
        mk3, mv3 = mk.reshape(b, N_MEM, BRANCH_W), mv.reshape(b, N_MEM, BRANCH_W)

        q, k, v, u, mq, gates = _inproj(l, yp, g_attn, w_in_bf, rope_p, s // ROW_TILE, qg, kg, mqg, ones)
        k3, v3 = k.reshape(b, s, KV_W), v.reshape(b, s, KV_W)
        oa, oc = _attn_prompt(l, attn_sinks, q.reshape(b, s, BRANCH_W), k3, v3,
                              mq.reshape(b, s, BRANCH_W), mk3, mv3)
        ysm, hre, him = _ssm_prompt(l, u.reshape(b, s, BRANCH_W), *tab_p)
        x1 = _merge(l, yp, oa.reshape(b * s, BRANCH_W), ysm.reshape(b * s, BRANCH_W), u,
                    oc.reshape(b * s, BRANCH_W), gates, d_row, wglu, wb, wout)
        yp = _ffn(l, x1, g_ffn, wup, wdn)

        qs, ks, vs, us, mqs, gates_s = _inproj(l, ys, g_attn, w_in_bf, rope_s, 1, qg, kg, mqg, ones)
        oas, ocs, nk, nv = _attn_sample(
            l, attn_sinks, qs.astype(F32).reshape(db, t, BRANCH_W), ks.reshape(db, t, KV_W),
            vs.reshape(db, t, KV_W), ck, cv, mqs.astype(F32).reshape(db, t, BRANCH_W), cmk, cmv, expand)
        yss, hrs, his = _ssm_sample(l, us.reshape(db, t, BRANCH_W), h0r, h0i, *tab_s)
        x1s = _merge(l, ys, oas.astype(BF16).reshape(db * t, BRANCH_W), yss.reshape(db * t, BRANCH_W), us,
                     ocs.astype(BF16).reshape(db * t, BRANCH_W), gates_s, d_row, wglu, wb, wout)
        ys = _ffn(l, x1s, g_ffn, wup, wdn)

        new = (k3[:, -WINDOW:].reshape(b, WINDOW, N_KV_HEADS, HEAD_DIM),
               v3[:, -WINDOW:].reshape(b, WINDOW, N_KV_HEADS, HEAD_DIM),
               _state_from_blocks(hre), _state_from_blocks(him),
               mk3.reshape(b, N_MEM, MEM_HEADS, MEM_HEAD_DIM), mv3.reshape(b, N_MEM, MEM_HEADS, MEM_HEAD_DIM),
               nk.reshape(db, WINDOW, N_KV_HEADS, HEAD_DIM), nv.reshape(db, WINDOW, N_KV_HEADS, HEAD_DIM),
               hrs.reshape(db, SSM_GROUPS, SSM_STATE), his.reshape(db, SSM_GROUPS, SSM_STATE))
        for lst, a in zip(outs, new):
            lst.append(a)

    return (yp.reshape(b, s, D_MODEL), ys.reshape(db, t, D_MODEL)) + tuple(jnp.stack(o) for o in outs)
```

```python
import functools
import math

import jax
import jax.numpy as jnp
from jax import lax
from jax.experimental import pallas as pl
from jax.experimental.pallas import tpu as pltpu

F32 = jnp.float32
BF16 = jnp.bfloat16

D_MODEL = 1024
BRANCH_W = 512
HEAD_DIM = 64
N_HEADS = 8
N_KV_HEADS = 2
KV_GROUP = 4
KV_W = N_KV_HEADS * HEAD_DIM
WINDOW = 128
ROPE_THETA = 10000.0
PAST_LEN = 16384
SSM_GROUPS = 32
SSM_GROUP_CH = 16
SSM_STATE = 64
N_MEM = 256
MEM_HEADS = 4
MEM_HEAD_DIM = 128
D_FF = 2816
RMS_EPS = 1e-6

Q_OFF, K_OFF, V_OFF, U_OFF, MQ_OFF, G_OFF = 0, 512, 640, 768, 1280, 1792
IN_W = G_OFF + 3 * D_MODEL

LANES = 128
GROUPS_PER_LANE_BLOCK = LANES // SSM_GROUP_CH
N_LANE_BLOCKS = BRANCH_W // LANES
STATE_W = GROUPS_PER_LANE_BLOCK * SSM_STATE
PROMPT_CHUNK = 8
ROW_TILE = 512
VMEM_LIMIT = 56 * 1024 * 1024


def _dot(a, b):
    return jnp.dot(a, b, preferred_element_type=F32)


def _dot_nt(a, b):
    return lax.dot_general(a, b, (((1,), (1,)), ((), ())), preferred_element_type=F32)


def _cparams(n_axes):
    return pltpu.CompilerParams(dimension_semantics=("arbitrary",) * n_axes,
                                vmem_limit_bytes=VMEM_LIMIT)


def _full(shape):
    nd = len(shape)
    return pl.BlockSpec(shape, lambda *_: (0,) * nd)


def _layer(shape, l):
    nd = len(shape)
    return pl.BlockSpec((None,) + tuple(shape), lambda *_: (l,) + (0,) * nd)


def _rms(x):
    return x * lax.rsqrt(jnp.mean(x * x, axis=-1, keepdims=True) + RMS_EPS)


def _memkv_kernel(mem_ref, g_ref, w_ref, kg_ref, k_ref, v_ref):
    h = _rms(mem_ref[...]) * g_ref[...]
    kv = _dot(h.astype(BF16), w_ref[...])
    kg = kg_ref[...]
    for hd in range(MEM_HEADS):
        sl = slice(hd * MEM_HEAD_DIM, (hd + 1) * MEM_HEAD_DIM)
        k_ref[:, sl] = _rms(kv[:, sl]) * kg
    v_ref[...] = kv[:, BRANCH_W:]


def _memkv(l, mem2d, g, w_bf, kg):
    n = mem2d.shape[0]
    return pl.pallas_call(
        _memkv_kernel, name="memkv",
        grid=(n // ROW_TILE,),
        in_specs=[pl.BlockSpec((ROW_TILE, D_MODEL), lambda i: (i, 0)),
                  _layer((1, D_MODEL), l), _layer((D_MODEL, 2 * BRANCH_W), l), _layer((1, MEM_HEAD_DIM), l)],
        out_specs=[pl.BlockSpec((ROW_TILE, BRANCH_W), lambda i: (i, 0))] * 2,
        out_shape=[jax.ShapeDtypeStruct((n, BRANCH_W), F32)] * 2,
        compiler_params=_cparams(1),
    )(mem2d, g, w_bf, kg)


def _rotary(x, cos, sin_lo, sin_hi):
    w = x.shape[-1]
    half = HEAD_DIM // 2
    return x * cos + pltpu.roll(x, w - half, 1) * sin_lo + pltpu.roll(x, half, 1) * sin_hi


def _inproj_kernel(x_ref, g_ref, w_ref, cos_ref, slo_ref, shi_ref, qg_ref, kg_ref, mqg_ref, ones_ref,
                   q_ref, k_ref, v_ref, u_ref, mq_ref, gates_ref):
    hb = (_rms(x_ref[...]) * g_ref[...]).astype(BF16)
    ones = ones_ref[...]
    cos, slo, shi = cos_ref[...], slo_ref[...], shi_ref[...]

    zq = _dot(hb, w_ref[:, Q_OFF:K_OFF])
    sq = (zq * zq).astype(BF16)
    ssq = jnp.concatenate([_dot(sq[:, :256], ones), _dot(sq[:, 256:], ones)], axis=1)
    qn = zq * lax.rsqrt(ssq * (1.0 / HEAD_DIM) + RMS_EPS) * qg_ref[...]
    cos4, slo4, shi4 = (jnp.concatenate([t] * 4, axis=1) for t in (cos, slo, shi))
    q_ref[...] = (_rotary(qn, cos4, slo4, shi4) * (1.0 / math.sqrt(HEAD_DIM))).astype(BF16)

    zkv = _dot(hb, w_ref[:, K_OFF:U_OFF])
    zk = zkv[:, :KV_W]
    ssk = _dot((zk * zk).astype(BF16), ones[:KV_W, :KV_W])
    kn = zk * lax.rsqrt(ssk * (1.0 / HEAD_DIM) + RMS_EPS) * kg_ref[...]
    k_ref[...] = _rotary(kn, cos, slo, shi)
    v_ref[...] = zkv[:, KV_W:]

    u_ref[...] = _dot(hb, w_ref[:, U_OFF:MQ_OFF])

    zm = _dot(hb, w_ref[:, MQ_OFF:G_OFF])
    mqg = mqg_ref[...]
    for hd in range(MEM_HEADS):
        sl = slice(hd * MEM_HEAD_DIM, (hd + 1) * MEM_HEAD_DIM)
        mq_ref[:, sl] = (_rms(zm[:, sl]) * mqg * (1.0 / math.sqrt(MEM_HEAD_DIM))).astype(BF16)

    for c in range(3 * D_MODEL // 512):
        lo = G_OFF + c * 512
        gates_ref[:, c * 512:(c + 1) * 512] = jax.nn.sigmoid(_dot(hb, w_ref[:, lo:lo + 512]))


def _inproj(l, x2d, g, w_bf, rope, rope_blocks, qg, kg, mqg, ones):
    n = x2d.shape[0]
    row = lambda i: (i, 0)
    rspec = pl.BlockSpec((ROW_TILE, LANES), lambda i: (i % rope_blocks, 0))
    widths = (BRANCH_W, KV_W, KV_W, BRANCH_W, BRANCH_W, 3 * D_MODEL)
    dtypes = (BF16, F32, F32, F32, BF16, F32)
    return pl.pallas_call(
        _inproj_kernel, name="inproj",
        grid=(n // ROW_TILE,),
        in_specs=[pl.BlockSpec((ROW_TILE, D_MODEL), row), _layer((1, D_MODEL), l), _layer((D_MODEL, IN_W), l),
                  rspec, rspec, rspec, _layer((1, BRANCH_W), l), _layer((1, KV_W), l),
                  _layer((1, MEM_HEAD_DIM), l), _full((256, 256))],
        out_specs=[pl.BlockSpec((ROW_TILE, w), row) for w in widths],
        out_shape=[jax.ShapeDtypeStruct((n, w), d) for w, d in zip(widths, dtypes)],
        compiler_params=_cparams(1),
    )(x2d, g, w_bf, *rope, qg, kg, mqg, ones)


ATTN_TILE = 512


def _softmax_rows(s):
    m = jnp.max(s, axis=-1, keepdims=True)
    p = jnp.exp(s - m)
    return p, jnp.sum(p, axis=-1, keepdims=True)


def _sink_column(sink_ref, l, heads, rows_per_head):
    n = len(heads) * rows_per_head
    rcol = lax.broadcasted_iota(jnp.int32, (n, 1), 0)
    sink = jnp.full((n, 1), sink_ref[l, heads[-1]], F32)
    for g in range(len(heads) - 2, -1, -1):
        sink = jnp.where(rcol < (g + 1) * rows_per_head, sink_ref[l, heads[g]], sink)
    return sink


def _sink_softmax(s, sink):
    m = jnp.maximum(jnp.max(s, axis=-1, keepdims=True), sink)
    p = jnp.exp(s - m)
    return p, jnp.sum(p, axis=-1, keepdims=True) + jnp.exp(sink - m)


def _attn_prompt_kernel(l, sink_ref, q_ref, kp_ref, kc_ref, vp_ref, vc_ref, qm_ref, mk_ref, mv_ref,
                        oa_ref, oc_ref):
    first = pl.program_id(1) == 0
    rows = KV_GROUP * WINDOW
    r = lax.broadcasted_iota(jnp.int32, (rows, 2 * WINDOW), 0)
    j = lax.broadcasted_iota(jnp.int32, (rows, 2 * WINDOW), 1)
    qi = r % WINDOW
    band = (j > qi) & (j <= qi + WINDOW)
    for sub in range(ATTN_TILE // WINDOW):
        qrows = slice(sub * WINDOW, (sub + 1) * WINDOW)
        mask = band & (jnp.logical_not(first) | (j >= WINDOW)) if sub == 0 else band
        for kvh in range(N_KV_HEADS):
            ksl = slice(kvh * HEAD_DIM, (kvh + 1) * HEAD_DIM)
            if sub == 0:
                kcat = jnp.concatenate([kp_ref[:, ksl], kc_ref[qrows, ksl]], axis=0).astype(BF16)
                vcat = jnp.concatenate([vp_ref[:, ksl], vc_ref[qrows, ksl]], axis=0).astype(BF16)
            else:
                krows = slice((sub - 1) * WINDOW, (sub + 1) * WINDOW)
                kcat, vcat = kc_ref[krows, ksl].astype(BF16), vc_ref[krows, ksl].astype(BF16)
            heads = [kvh * KV_GROUP + g for g in range(KV_GROUP)]
            qg = jnp.concatenate([q_ref[qrows, hd * HEAD_DIM:(hd + 1) * HEAD_DIM] for hd in heads], axis=0)
            s = jnp.where(mask, _dot_nt(qg, kcat), -jnp.inf)
            p, denom = _sink_softmax(s, _sink_column(sink_ref, l, heads, WINDOW))
            o = _dot(p.astype(BF16), vcat) / denom
            for g, hd in enumerate(heads):
                oa_ref[qrows, hd * HEAD_DIM:(hd + 1) * HEAD_DIM] = o[g * WINDOW:(g + 1) * WINDOW].astype(BF16)

    qm = qm_ref[...]
    for hd in range(MEM_HEADS):
        sl = slice(hd * MEM_HEAD_DIM, (hd + 1) * MEM_HEAD_DIM)
        p, den = _softmax_rows(_dot_nt(qm[:, sl], mk_ref[:, sl].astype(BF16)))
        oc_ref[:, sl] = (_dot(p.astype(BF16), mv_ref[:, sl].astype(BF16)) / den).astype(BF16)


def _attn_prompt(l, sinks, q, k, v, qm, mk, mv):
    b, s, _ = q.shape
    per_tile = ATTN_TILE // WINDOW
    cur = lambda bi, i: (bi, i, 0)
    prev = lambda bi, i: (bi, jnp.maximum(i * per_tile - 1, 0), 0)
    per_b = lambda bi, i: (bi, 0, 0)
    kv_prev = pl.BlockSpec((None, WINDOW, KV_W), prev)
    kv_cur = pl.BlockSpec((None, ATTN_TILE, KV_W), cur)
    return pl.pallas_call(
        functools.partial(_attn_prompt_kernel, l), name="attn_prompt",
        grid=(b, s // ATTN_TILE),
        in_specs=[pl.BlockSpec(memory_space=pltpu.SMEM),
                  pl.BlockSpec((None, ATTN_TILE, BRANCH_W), cur),
                  kv_prev, kv_cur, kv_prev, kv_cur,
                  pl.BlockSpec((None, ATTN_TILE, BRANCH_W), cur),
                  pl.BlockSpec((None, N_MEM, BRANCH_W), per_b),
                  pl.BlockSpec((None, N_MEM, BRANCH_W), per_b)],
        out_specs=[pl.BlockSpec((None, ATTN_TILE, BRANCH_W), cur)] * 2,
        out_shape=[jax.ShapeDtypeStruct((b, s, BRANCH_W), BF16)] * 2,
        compiler_params=_cparams(2),
    )(sinks, q, k, k, v, v, qm, mk, mv)


DEC_BLOCK = 8
TOK_PAD = 8


def _stack_heads(x, n_heads, head_w):
    t, w = x.shape
    x8 = jnp.concatenate([x, jnp.zeros((TOK_PAD - t, w), F32)], axis=0)
    rr = lax.broadcasted_iota(jnp.int32, (n_heads * TOK_PAD, w), 0)
    ll = lax.broadcasted_iota(jnp.int32, (n_heads * TOK_PAD, w), 1)
    own = (ll // head_w) == (rr // TOK_PAD)
    return jnp.where(own, jnp.concatenate([x8] * n_heads, axis=0), 0.0).astype(BF16), own


def _unstack_heads(o, own, n_heads, t):
    o = jnp.where(own, o, 0.0)
    acc = o[0:TOK_PAD]
    for hd in range(1, n_heads):
        acc = acc + o[hd * TOK_PAD:(hd + 1) * TOK_PAD]
    return acc[0:t]


def _attn_sample_kernel(l, sink_ref, q_ref, kn_ref, vn_ref, ck_ref, cv_ref, qm_ref, mk_ref, mv_ref, e_ref,
                        oa_ref, oc_ref, nk_ref, nv_ref, kc_scr, vc_scr):
    t = q_ref.shape[1]
    keys = 2 * WINDOW
    rows = N_HEADS * TOK_PAD
    rk = lax.broadcasted_iota(jnp.int32, (rows, keys), 0)
    jk = lax.broadcasted_iota(jnp.int32, (rows, keys), 1)
    tk = rk % TOK_PAD
    mask = (tk < t) & (((jk < WINDOW) & (jk > tk)) | ((jk >= WINDOW) & (jk - WINDOW <= tk)))
    sink = _sink_column(sink_ref, l, list(range(N_HEADS)), TOK_PAD)
    e = e_ref[...]
    kc_scr[WINDOW:, :] = jnp.zeros((WINDOW, KV_W), F32)
    vc_scr[WINDOW:, :] = jnp.zeros((WINDOW, KV_W), F32)
    for bi in range(DEC_BLOCK):
        kc_scr[:WINDOW, :] = ck_ref[bi]
        vc_scr[:WINDOW, :] = cv_ref[bi]
        kc_scr[WINDOW:WINDOW + t, :] = kn_ref[bi]
        vc_scr[WINDOW:WINDOW + t, :] = vn_ref[bi]
        nk_ref[bi] = kc_scr[t:WINDOW + t, :]
        nv_ref[bi] = vc_scr[t:WINDOW + t, :]
        kx = _dot(kc_scr[...].astype(BF16), e).astype(BF16)
        vx = _dot(vc_scr[...].astype(BF16), e).astype(BF16)
        qs, own = _stack_heads(q_ref[bi], N_HEADS, HEAD_DIM)
        p, denom = _sink_softmax(jnp.where(mask, _dot_nt(qs, kx), -jnp.inf), sink)
        oa_ref[bi] = _unstack_heads(_dot(p.astype(BF16), vx) / denom, own, N_HEADS, t)

        qm8 = jnp.concatenate([qm_ref[bi], jnp.zeros((TOK_PAD - t, BRANCH_W), F32)], axis=0).astype(BF16)
        om = []
        for hd in range(MEM_HEADS):
            own_rows = pl.ds(hd, N_MEM, stride=MEM_HEADS)
            kh = mk_ref[bi, own_rows, :].astype(BF16)
            vh = mv_ref[bi, own_rows, :].astype(BF16)
            pm, den = _softmax_rows(_dot_nt(qm8[:, hd * MEM_HEAD_DIM:(hd + 1) * MEM_HEAD_DIM], kh))
            om.append(_dot(pm.astype(BF16), vh) / den)
        oc_ref[bi] = jnp.concatenate(om, axis=1)[0:t]


def _attn_sample(l, sinks, q, kn, vn, ck, cv, qm, mk, mv, e):
    b, t, _ = q.shape
    blk = lambda i: (i, 0, 0)
    lblk = lambda i: (l, i, 0, 0)
    tok = lambda w: pl.BlockSpec((DEC_BLOCK, t, w), blk)
    cache_in = pl.BlockSpec((None, DEC_BLOCK, WINDOW, KV_W), lblk)
    cache_out = pl.BlockSpec((DEC_BLOCK, WINDOW, KV_W), blk)
    mem = pl.BlockSpec((None, DEC_BLOCK, N_MEM * MEM_HEADS, MEM_HEAD_DIM), lblk)
    return pl.pallas_call(
        functools.partial(_attn_sample_kernel, l), name="attn_sample",
        grid=(b // DEC_BLOCK,),
        in_specs=[pl.BlockSpec(memory_space=pltpu.SMEM), tok(BRANCH_W), tok(KV_W), tok(KV_W),
                  cache_in, cache_in, tok(BRANCH_W), mem, mem, _full((KV_W, BRANCH_W))],
        out_specs=[tok(BRANCH_W), tok(BRANCH_W), cache_out, cache_out],
        out_shape=[jax.ShapeDtypeStruct((b, t, BRANCH_W), F32)] * 2
                  + [jax.ShapeDtypeStruct((b, WINDOW, KV_W), F32)] * 2,
        scratch_shapes=[pltpu.VMEM((2 * WINDOW, KV_W), F32)] * 2,
        compiler_params=_cparams(1),
    )(sinks, q, kn, vn, ck, cv, qm, mk, mv, e)


def _ssm_tables(a_re, a_im, log_dt, b_re, b_im, c_re, c_im, chunk):
    hp = lax.Precision.HIGHEST
    dt = jnp.exp(log_dt)[..., None]
    mag = jnp.exp(a_re * dt)
    lam_re, lam_im = mag * jnp.cos(a_im * dt), mag * jnp.sin(a_im * dt)
    den = a_re * a_re + a_im * a_im
    nr, ni = lam_re - 1.0, lam_im
    g_re = (nr * a_re + ni * a_im) / den
    g_im = (ni * a_re - nr * a_im) / den
    bg_re = g_re[..., None] * b_re - g_im[..., None] * b_im
    bg_im = g_re[..., None] * b_im + g_im[..., None] * b_re
    d = jnp.arange(chunk + 1, dtype=F32)[:, None, None, None]
    pm = jnp.exp(d * (a_re * dt)[None])
    pw_re = pm * jnp.cos(d * (a_im * dt)[None])
    pw_im = pm * jnp.sin(d * (a_im * dt)[None])
    z_re = pw_re[..., None] * bg_re[None] - pw_im[..., None] * bg_im[None]
    z_im = pw_re[..., None] * bg_im[None] + pw_im[..., None] * bg_re[None]
    kern = (jnp.einsum('lgop,dlgpi->dlgio', c_re, z_re[:chunk], precision=hp)
            - jnp.einsum('lgop,dlgpi->dlgio', c_im, z_im[:chunk], precision=hp))
    nl = a_re.shape[0]
    nb, gb = N_LANE_BLOCKS, GROUPS_PER_LANE_BLOCK
    kd = jnp.transpose(kern.reshape(chunk, nl, nb, gb, SSM_GROUP_CH, SSM_GROUP_CH), (1, 2, 0, 3, 4, 5))
    kd = kd.reshape(nl, nb, chunk, LANES, SSM_GROUP_CH)

    def to_rows(z):
        z = z[:chunk][::-1].reshape(chunk, nl, nb, gb, SSM_STATE, SSM_GROUP_CH)
        return jnp.transpose(z, (1, 2, 0, 3, 5, 4)).reshape(nl, nb, chunk, LANES, SSM_STATE)

    zr, zi = to_rows(z_re), to_rows(z_im)
    zc = jnp.concatenate([zr, zr, zi, zi], axis=-1)

    pr, pi = pw_re[1:], pw_im[1:]
    f_a = c_re[None] * pr[:, :, :, None, :] - c_im[None] * pi[:, :, :, None, :]
    f_b = c_re[None] * pi[:, :, :, None, :] + c_im[None] * pr[:, :, :, None, :]

    def from_cols(f):
        f = f.reshape(chunk, nl, nb, gb, SSM_GROUP_CH, SSM_STATE)
        return jnp.transpose(f, (1, 2, 0, 4, 3, 5)).reshape(nl, nb, chunk, SSM_GROUP_CH, STATE_W)

    ft = jnp.concatenate([from_cols(f_a), from_cols(-f_b)], axis=-1)

    kw = chunk * LANES
    spec = lambda *shape: pl.BlockSpec((None, None) + shape, lambda l, n: (l, n) + (0,) * len(shape))
    intra, to_st, from_st_t = pl.pallas_call(
        _ssm_table_kernel, name="ssm_tables",
        grid=(nl, nb),
        in_specs=[spec(chunk, LANES, SSM_GROUP_CH), spec(chunk, LANES, 2 * LANES),
                  spec(chunk, SSM_GROUP_CH, 2 * STATE_W)],
        out_specs=[spec(kw, kw), spec(kw, 2 * STATE_W), spec(kw, 2 * STATE_W)],
        out_shape=[jax.ShapeDtypeStruct((nl, nb, kw, kw), BF16),
                   jax.ShapeDtypeStruct((nl, nb, kw, 2 * STATE_W), BF16),
                   jax.ShapeDtypeStruct((nl, nb, kw, 2 * STATE_W), BF16)],
        compiler_params=_cparams(2),
    )(kd, zc, ft)
    dec_re = pw_re[chunk].reshape(nl, nb, 1, STATE_W)
    dec_im = pw_im[chunk].reshape(nl, nb, 1, STATE_W)
    return intra, to_st, from_st_t, dec_re, dec_im


def _ssm_table_kernel(kd_ref, zc_ref, ft_ref, intra_ref, to_ref, fromt_ref):
    chunk = kd_ref.shape[0]
    gch = SSM_GROUP_CH
    r = lax.broadcasted_iota(jnp.int32, (LANES, LANES), 0)
    c = lax.broadcasted_iota(jnp.int32, (LANES, LANES), 1)
    same_group = (r // gch) == (c // gch)
    spread = (lax.broadcasted_iota(jnp.int32, (gch, LANES), 1) % gch
              == lax.broadcasted_iota(jnp.int32, (gch, LANES), 0)).astype(BF16)
    spread_t = (lax.broadcasted_iota(jnp.int32, (LANES, gch), 0) % gch
                == lax.broadcasted_iota(jnp.int32, (LANES, gch), 1)).astype(BF16)
    rs = lax.broadcasted_iota(jnp.int32, (LANES, 2 * STATE_W), 0)
    cs = lax.broadcasted_iota(jnp.int32, (LANES, 2 * STATE_W), 1)
    own_state = (rs // gch) == ((cs % STATE_W) // SSM_STATE)

    lag_blocks = [jnp.where(same_group, _dot(kd_ref[d].astype(BF16), spread), 0.0).astype(BF16)
                  for d in range(chunk)]
    zero = jnp.zeros((LANES, LANES), BF16)
    for s in range(chunk):
        for t in range(chunk):
            intra_ref[s * LANES:(s + 1) * LANES, t * LANES:(t + 1) * LANES] = (
                lag_blocks[t - s] if t >= s else zero)
        z = zc_ref[s]
        full = jnp.concatenate([z[:, :LANES]] * (STATE_W // LANES) + [z[:, LANES:]] * (STATE_W // LANES), axis=1)
        to_ref[s * LANES:(s + 1) * LANES, :] = jnp.where(own_state, full, 0.0).astype(BF16)
        f = _dot(spread_t, ft_ref[s].astype(BF16))
        fromt_ref[s * LANES:(s + 1) * LANES, :] = jnp.where(own_state, f, 0.0).astype(BF16)


SSM_SEQS = 2


def _chunk_rows(ref, chunk):
    n = ref.shape[0] // chunk
    return jnp.concatenate([ref[pl.ds(s, n, stride=chunk), :] for s in range(chunk)], axis=1)


def _store_chunk_rows(ref, y, chunk):
    n = ref.shape[0] // chunk
    for t in range(chunk):
        ref[pl.ds(t, n, stride=chunk), :] = y[:, t * LANES:(t + 1) * LANES]


def _ssm_prompt_kernel(u_ref, intra_ref, to_ref, fromt_ref, dre_ref, dim_ref,
                       y_ref, hre_ref, him_ref, st_scr):
    chunk = PROMPT_CHUNK
    nchunks = u_ref.shape[0] // (chunk * SSM_SEQS)
    x = _chunk_rows(u_ref, chunk).astype(BF16)
    y = _dot(x, intra_ref[...])
    add_all = _dot(x, to_ref[...])
    nblk = STATE_W // LANES
    for c in range(2 * nblk):
        st_scr[c] = add_all[:, c * LANES:(c + 1) * LANES]
    dre = [dre_ref[:, c * LANES:(c + 1) * LANES] for c in range(nblk)]
    dim = [dim_ref[:, c * LANES:(c + 1) * LANES] for c in range(nblk)]

    def step(k, carry):
        hr, hi = carry
        rows = pl.ds(k, SSM_SEQS, stride=nchunks)
        new_r, new_i = [], []
        for c in range(nblk):
            add_r, add_i = st_scr[c, rows, :], st_scr[nblk + c, rows, :]
            st_scr[c, rows, :] = hr[c]
            st_scr[nblk + c, rows, :] = hi[c]
            new_r.append(dre[c] * hr[c] - dim[c] * hi[c] + add_r)
            new_i.append(dre[c] * hi[c] + dim[c] * hr[c] + add_i)
        return tuple(new_r), tuple(new_i)

    zero = tuple(jnp.zeros((SSM_SEQS, LANES), F32) for _ in range(nblk))
    hr, hi = lax.fori_loop(0, nchunks, step, (zero, zero), unroll=4)
    hre_ref[...] = jnp.concatenate(hr, axis=1)
    him_ref[...] = jnp.concatenate(hi, axis=1)
    h_in = jnp.concatenate([st_scr[c] for c in range(2 * nblk)], axis=1).astype(BF16)
    _store_chunk_rows(y_ref, y + _dot_nt(h_in, fromt_ref[...]), chunk)


def _ssm_prompt(l, u2d, seq_len, intra, to_st, from_st_t, dec_re, dec_im):
    n = u2d.shape[0]
    chunk = PROMPT_CHUNK
    rows = SSM_SEQS * seq_len
    steps = n // rows
    kw = chunk * LANES
    data = pl.BlockSpec((rows, LANES), lambda j, h: (h, j))
    wspec = lambda r, c: pl.BlockSpec((None, None, r, c), lambda j, h: (l, j, 0, 0))
    st_out = pl.BlockSpec((None, None, SSM_SEQS, STATE_W), lambda j, h: (j, h, 0, 0))
    st_shape = jax.ShapeDtypeStruct((N_LANE_BLOCKS, steps, SSM_SEQS, STATE_W), F32)
    return pl.pallas_call(
        _ssm_prompt_kernel, name="ssm_prompt",
        grid=(N_LANE_BLOCKS, steps),
        in_specs=[data, wspec(kw, kw), wspec(kw, 2 * STATE_W), wspec(kw, 2 * STATE_W),
                  wspec(1, STATE_W), wspec(1, STATE_W)],
        out_specs=[data, st_out, st_out],
        out_shape=[jax.ShapeDtypeStruct(u2d.shape, F32), st_shape, st_shape],
        scratch_shapes=[pltpu.VMEM((2 * STATE_W // LANES, rows // chunk, LANES), F32)],
        compiler_params=_cparams(2),
    )(u2d, intra, to_st, from_st_t, dec_re, dec_im)


def _ssm_sample_kernel(chunk, u_ref, h0r_ref, h0i_ref, intra_ref, to_ref, fromt_ref, dre_ref, dim_ref,
                       y_ref, hre_ref, him_ref):
    x = _chunk_rows(u_ref, chunk).astype(BF16)
    hr, hi = h0r_ref[...], h0i_ref[...]
    dre, dim = dre_ref[...], dim_ref[...]
    add = _dot(x, to_ref[...])
    hre_ref[...] = dre * hr - dim * hi + add[:, :STATE_W]
    him_ref[...] = dre * hi + dim * hr + add[:, STATE_W:]
    h0 = jnp.concatenate([hr, hi], axis=1).astype(BF16)
    _store_chunk_rows(y_ref, _dot(x, intra_ref[...]) + _dot_nt(h0, fromt_ref[...]), chunk)


def _ssm_sample(l, u2d, t, h0_re, h0_im, intra, to_st, from_st_t, dec_re, dec_im):
    n = u2d.shape[0]
    b = n // t
    kw = t * LANES
    data = pl.BlockSpec((n, LANES), lambda j: (0, j))
    st_in = pl.BlockSpec((None, b, STATE_W), lambda j: (l, 0, j))
    st_out = pl.BlockSpec((b, STATE_W), lambda j: (0, j))
    wspec = lambda r, c: pl.BlockSpec((None, None, r, c), lambda j: (l, j, 0, 0))
    return pl.pallas_call(
        functools.partial(_ssm_sample_kernel, t), name="ssm_sample",
        grid=(N_LANE_BLOCKS,),
        in_specs=[data, st_in, st_in, wspec(kw, kw), wspec(kw, 2 * STATE_W), wspec(kw, 2 * STATE_W),
                  wspec(1, STATE_W), wspec(1, STATE_W)],
        out_specs=[data, st_out, st_out],
        out_shape=[jax.ShapeDtypeStruct(u2d.shape, F32),
                   jax.ShapeDtypeStruct((b, N_LANE_BLOCKS * STATE_W), F32),
                   jax.ShapeDtypeStruct((b, N_LANE_BLOCKS * STATE_W), F32)],
        compiler_params=_cparams(1),
    )(u2d, h0_re, h0_im, intra, to_st, from_st_t, dec_re, dec_im)


def _merge_kernel(x_ref, oa_ref, y_ref, u_ref, oc_ref, gates_ref, d_ref, wglu_ref, wb_ref, wout_ref,
                  o_ref):
    z = jax.nn.gelu(y_ref[...] + d_ref[...] * u_ref[...])
    ob = (z * jax.nn.sigmoid(_dot(z.astype(BF16), wglu_ref[...]))).astype(BF16)
    merged = gates_ref[:, 0:D_MODEL] * _dot(oa_ref[...], wb_ref[0])
    merged = merged + gates_ref[:, D_MODEL:2 * D_MODEL] * _dot(ob, wb_ref[1])
    merged = merged + gates_ref[:, 2 * D_MODEL:3 * D_MODEL] * _dot(oc_ref[...], wb_ref[2])
    o_ref[...] = x_ref[...] + _dot(merged.astype(BF16), wout_ref[...])


def _merge(l, x2d, oa, y, u, oc, gates, d, wglu, wb, wout):
    n = x2d.shape[0]
    row = lambda w: pl.BlockSpec((ROW_TILE, w), lambda i: (i, 0))
    return pl.pallas_call(
        _merge_kernel, name="merge",
        grid=(n // ROW_TILE,),
        in_specs=[row(D_MODEL), row(BRANCH_W), row(BRANCH_W), row(BRANCH_W), row(BRANCH_W),
                  row(3 * D_MODEL), _layer((1, BRANCH_W), l), _layer((BRANCH_W, BRANCH_W), l),
                  _layer((3, BRANCH_W, D_MODEL), l), _layer((D_MODEL, D_MODEL), l)],
        out_specs=row(D_MODEL),
        out_shape=jax.ShapeDtypeStruct((n, D_MODEL), F32),
        compiler_params=_cparams(1),
    )(x2d, oa, y, u, oc, gates, d, wglu, wb, wout)


FF_CHUNKS = ((0, 512), (512, 512), (1024, 512), (1536, 512), (2048, 512), (2560, 256))


def _ffn_kernel(x_ref, g_ref, wup_ref, wdn_ref, o_ref):
    x = x_ref[...]
    h = (_rms(x) * g_ref[...]).astype(BF16)
    acc = x
    for lo, w in FF_CHUNKS:
        gate = _dot(h, wup_ref[:, lo:lo + w])
        up = _dot(h, wup_ref[:, D_FF + lo:D_FF + lo + w])
        acc = acc + _dot((jax.nn.silu(gate) * up).astype(BF16), wdn_ref[lo:lo + w, :])
    o_ref[...] = acc


def _ffn(l, x2d, g, wup, wdn):
    n = x2d.shape[0]
    row = pl.BlockSpec((ROW_TILE, D_MODEL), lambda i: (i, 0))
    return pl.pallas_call(
        _ffn_kernel, name="ffn",
        grid=(n // ROW_TILE,),
        in_specs=[row, _layer((1, D_MODEL), l), _layer((D_MODEL, 2 * D_FF), l), _layer((D_FF, D_MODEL), l)],
        out_specs=row,
        out_shape=jax.ShapeDtypeStruct((n, D_MODEL), F32),
        compiler_params=_cparams(1),
    )(x2d, g, wup, wdn)


def _rope_tables(pos):
    half = HEAD_DIM // 2
    inv = ROPE_THETA ** (-jnp.arange(half, dtype=F32) / half)
    ang = pos[:, None] * inv[None, :]
    cos, sin = jnp.cos(ang), jnp.sin(ang)
    zero = jnp.zeros_like(sin)
    two = lambda a: jnp.concatenate([a, a], axis=1)
    return (two(jnp.concatenate([cos, cos], axis=1)), two(jnp.concatenate([-sin, zero], axis=1)),
            two(jnp.concatenate([zero, sin], axis=1)))


def _state_from_blocks(h):
    nb, hb, sq, _ = h.shape
    return jnp.transpose(h.reshape(nb, hb * sq, STATE_W), (1, 0, 2)).reshape(hb * sq, SSM_GROUPS, SSM_STATE)


def kernel(x_prompt, x_sample, cache_swa_k, cache_swa_v, state_ssm_re, state_ssm_im, cache_mem_k,
           cache_mem_v, mem_prompt, attn_norm, w_in, q_norm, k_norm, attn_sinks, ssm_a_re, ssm_a_im,
           ssm_log_dt, ssm_b_re, ssm_b_im, ssm_c_re, ssm_c_im, ssm_d, ssm_w_glu, mem_norm, w_mem_kv,
           mem_q_norm, mem_k_norm, w_branch, w_out, ffn_norm, w_ffn_up, w_ffn_down):
    depth = w_in.shape[0]
    b, s, _ = x_prompt.shape
    db, t, _ = x_sample.shape
    assert db * t == ROW_TILE and s % ROW_TILE == 0

    rope_p = _rope_tables(jnp.arange(s, dtype=F32))
    rope_s = _rope_tables(jnp.tile(PAST_LEN + jnp.arange(t, dtype=F32), db))
    ssm_params = (ssm_a_re, ssm_a_im, ssm_log_dt, ssm_b_re, ssm_b_im, ssm_c_re, ssm_c_im)
    tab_p = _ssm_tables(*ssm_params, PROMPT_CHUNK)
    tab_s = _ssm_tables(*ssm_params, t)
    blk = jnp.arange(256) // HEAD_DIM
    ones = (blk[:, None] == blk[None, :]).astype(BF16)
    lane = jnp.arange(BRANCH_W)[None, :]
    src = jnp.arange(KV_W)[:, None]
    expand = ((lane // (HEAD_DIM * KV_GROUP) == src // HEAD_DIM)
              & (lane % HEAD_DIM == src % HEAD_DIM)).astype(BF16)

    w_in_bf, w_mem_bf = w_in.astype(BF16), w_mem_kv.astype(BF16)
    wglu, wb, wout = ssm_w_glu.astype(BF16), w_branch.astype(BF16), w_out.astype(BF16)
    wup, wdn = w_ffn_up.astype(BF16), w_ffn_down.astype(BF16)
    row = lambda a: a[:, None, :]
    g_attn, g_mem, g_ffn, d_row = row(attn_norm), row(mem_norm), row(ffn_norm), row(ssm_d)
    qg, kg = row(jnp.tile(q_norm, (1, N_HEADS))), row(jnp.tile(k_norm, (1, N_KV_HEADS)))
    mqg, mkg = row(mem_q_norm), row(mem_k_norm)
    ck = cache_swa_k.reshape(depth, db, WINDOW, KV_W)
    cv = cache_swa_v.reshape(depth, db, WINDOW, KV_W)
    cmk = cache_mem_k.reshape(depth, db, N_MEM * MEM_HEADS, MEM_HEAD_DIM)
    cmv = cache_mem_v.reshape(depth, db, N_MEM * MEM_HEADS, MEM_HEAD_DIM)
    h0r = state_ssm_re.reshape(depth, db, SSM_GROUPS * SSM_STATE)
    h0i = state_ssm_im.reshape(depth, db, SSM_GROUPS * SSM_STATE)

    yp = x_prompt.reshape(b * s, D_MODEL)
    ys = x_sample.reshape(db * t, D_MODEL)
    mem2d = mem_prompt.reshape(b * N_MEM, D_MODEL)
    outs = [[] for _ in range(10)]
    for l in range(depth):
        mk, mv = _memkv(l, mem2d, g_mem, w_mem_bf, mkg)
        mk3, mv3 = mk.reshape(b, N_MEM, BRANCH_W), mv.reshape(b, N_MEM, BRANCH_W)

        q, k, v, u, mq, gates = _inproj(l, yp, g_attn, w_in_bf, rope_p, s // ROW_TILE, qg, kg, mqg, ones)
        k3, v3 = k.reshape(b, s, KV_W), v.reshape(b, s, KV_W)
        oa, oc = _attn_prompt(l, attn_sinks, q.reshape(b, s, BRANCH_W), k3, v3,
                              mq.reshape(b, s, BRANCH_W), mk3, mv3)
        ysm, hre, him = _ssm_prompt(l, u, s, *tab_p)
        x1 = _merge(l, yp, oa.reshape(b * s, BRANCH_W), ysm, u,
                    oc.reshape(b * s, BRANCH_W), gates, d_row, wglu, wb, wout)
        yp = _ffn(l, x1, g_ffn, wup, wdn)

        qs, ks, vs, us, mqs, gates_s = _inproj(l, ys, g_attn, w_in_bf, rope_s, 1, qg, kg, mqg, ones)
        oas, ocs, nk, nv = _attn_sample(
            l, attn_sinks, qs.astype(F32).reshape(db, t, BRANCH_W), ks.reshape(db, t, KV_W),
            vs.reshape(db, t, KV_W), ck, cv, mqs.astype(F32).reshape(db, t, BRANCH_W), cmk, cmv, expand)
        yss, hrs, his = _ssm_sample(l, us, t, h0r, h0i, *tab_s)
        x1s = _merge(l, ys, oas.astype(BF16).reshape(db * t, BRANCH_W), yss, us,
                     ocs.astype(BF16).reshape(db * t, BRANCH_W), gates_s, d_row, wglu, wb, wout)
        ys = _ffn(l, x1s, g_ffn, wup, wdn)

        new = (k3[:, -WINDOW:].reshape(b, WINDOW, N_KV_HEADS, HEAD_DIM),
               v3[:, -WINDOW:].reshape(b, WINDOW, N_KV_HEADS, HEAD_DIM),
               _state_from_blocks(hre), _state_from_blocks(him),
               mk3.reshape(b, N_MEM, MEM_HEADS, MEM_HEAD_DIM), mv3.reshape(b, N_MEM, MEM_HEADS, MEM_HEAD_DIM),
               nk.reshape(db, WINDOW, N_KV_HEADS, HEAD_DIM), nv.reshape(db, WINDOW, N_KV_HEADS, HEAD_DIM),
               hrs.reshape(db, SSM_GROUPS, SSM_STATE), his.reshape(db, SSM_GROUPS, SSM_STATE))
        for lst, a in zip(outs, new):
            lst.append(a)

    return (yp.reshape(b, s, D_MODEL), ys.reshape(db, t, D_MODEL)) + tuple(jnp.stack(o) for o in outs)
```

```python
import functools
import math

import jax
import jax.numpy as jnp
from jax import lax
from jax.experimental import pallas as pl
from jax.experimental.pallas import tpu as pltpu

F32 = jnp.float32
BF16 = jnp.bfloat16

D_MODEL = 1024
BRANCH_W = 512
HEAD_DIM = 64
N_HEADS = 8
N_KV_HEADS = 2
KV_GROUP = 4
KV_W = N_KV_HEADS * HEAD_DIM
WINDOW = 128
ROPE_THETA = 10000.0
PAST_LEN = 16384
SSM_GROUPS = 32
SSM_GROUP_CH = 16
SSM_STATE = 64
N_MEM = 256
MEM_HEADS = 4
MEM_HEAD_DIM = 128
D_FF = 2816
RMS_EPS = 1e-6

Q_OFF, K_OFF, V_OFF, U_OFF, MQ_OFF, G_OFF = 0, 512, 640, 768, 1280, 1792
IN_W = G_OFF + 3 * D_MODEL

LANES = 128
GROUPS_PER_LANE_BLOCK = LANES // SSM_GROUP_CH
N_LANE_BLOCKS = BRANCH_W // LANES
STATE_W = GROUPS_PER_LANE_BLOCK * SSM_STATE
PROMPT_CHUNK = 8
ROW_TILE = 512
VMEM_LIMIT = 56 * 1024 * 1024


def _dot(a, b):
    return jnp.dot(a, b, preferred_element_type=F32)


def _dot_nt(a, b):
    return lax.dot_general(a, b, (((1,), (1,)), ((), ())), preferred_element_type=F32)


def _cparams(n_axes):
    return pltpu.CompilerParams(dimension_semantics=("arbitrary",) * n_axes,
                                vmem_limit_bytes=VMEM_LIMIT)


def _full(shape):
    nd = len(shape)
    return pl.BlockSpec(shape, lambda *_: (0,) * nd)


def _layer(shape, l):
    nd = len(shape)
    return pl.BlockSpec((None,) + tuple(shape), lambda *_: (l,) + (0,) * nd, pipeline_mode=pl.Buffered(1))


def _rms(x):
    return x * lax.rsqrt(jnp.mean(x * x, axis=-1, keepdims=True) + RMS_EPS)


def _memkv_kernel(mem_ref, g_ref, w_ref, kg_ref, k_ref, v_ref):
    h = _rms(mem_ref[...]) * g_ref[...]
    kv = _dot(h.astype(BF16), w_ref[...])
    kg = kg_ref[...]
    for hd in range(MEM_HEADS):
        sl = slice(hd * MEM_HEAD_DIM, (hd + 1) * MEM_HEAD_DIM)
        k_ref[:, sl] = _rms(kv[:, sl]) * kg
    v_ref[...] = kv[:, BRANCH_W:]


def _memkv(l, mem2d, g, w_bf, kg):
    n = mem2d.shape[0]
    return pl.pallas_call(
        _memkv_kernel, name="memkv",
        grid=(n // ROW_TILE,),
        in_specs=[pl.BlockSpec((ROW_TILE, D_MODEL), lambda i: (i, 0)),
                  _layer((1, D_MODEL), l), _layer((D_MODEL, 2 * BRANCH_W), l), _layer((1, MEM_HEAD_DIM), l)],
        out_specs=[pl.BlockSpec((ROW_TILE, BRANCH_W), lambda i: (i, 0))] * 2,
        out_shape=[jax.ShapeDtypeStruct((n, BRANCH_W), F32)] * 2,
        compiler_params=_cparams(1),
    )(mem2d, g, w_bf, kg)


def _rotary(x, cos, sin_lo, sin_hi):
    w = x.shape[-1]
    half = HEAD_DIM // 2
    return x * cos + pltpu.roll(x, w - half, 1) * sin_lo + pltpu.roll(x, half, 1) * sin_hi


def _inproj_kernel(x_ref, g_ref, w_ref, cos_ref, slo_ref, shi_ref, qg_ref, kg_ref, mqg_ref, ones_ref,
                   q_ref, k_ref, v_ref, u_ref, mq_ref, gates_ref):
    hb = (_rms(x_ref[...]) * g_ref[...]).astype(BF16)
    ones = ones_ref[...]
    cos, slo, shi = cos_ref[...], slo_ref[...], shi_ref[...]

    zq = _dot(hb, w_ref[:, Q_OFF:K_OFF])
    sq = (zq * zq).astype(BF16)
    ssq = jnp.concatenate([_dot(sq[:, :256], ones), _dot(sq[:, 256:], ones)], axis=1)
    qn = zq * lax.rsqrt(ssq * (1.0 / HEAD_DIM) + RMS_EPS) * qg_ref[...]
    cos4, slo4, shi4 = (jnp.concatenate([t] * 4, axis=1) for t in (cos, slo, shi))
    q_ref[...] = (_rotary(qn, cos4, slo4, shi4) * (1.0 / math.sqrt(HEAD_DIM))).astype(BF16)

    zkv = _dot(hb, w_ref[:, K_OFF:U_OFF])
    zk = zkv[:, :KV_W]
    ssk = _dot((zk * zk).astype(BF16), ones[:KV_W, :KV_W])
    kn = zk * lax.rsqrt(ssk * (1.0 / HEAD_DIM) + RMS_EPS) * kg_ref[...]
    k_ref[...] = _rotary(kn, cos, slo, shi)
    v_ref[...] = zkv[:, KV_W:]

    u_ref[...] = _dot(hb, w_ref[:, U_OFF:MQ_OFF])

    zm = _dot(hb, w_ref[:, MQ_OFF:G_OFF])
    mqg = mqg_ref[...]
    for hd in range(MEM_HEADS):
        sl = slice(hd * MEM_HEAD_DIM, (hd + 1) * MEM_HEAD_DIM)
        mq_ref[:, sl] = (_rms(zm[:, sl]) * mqg * (1.0 / math.sqrt(MEM_HEAD_DIM))).astype(BF16)

    for c in range(3 * D_MODEL // 512):
        lo = G_OFF + c * 512
        gates_ref[:, c * 512:(c + 1) * 512] = jax.nn.sigmoid(_dot(hb, w_ref[:, lo:lo + 512])).astype(BF16)


def _inproj(l, x2d, g, w_bf, rope, rope_blocks, qg, kg, mqg, ones):
    n = x2d.shape[0]
    row = lambda i: (i, 0)
    rspec = pl.BlockSpec((ROW_TILE, LANES), lambda i: (i % rope_blocks, 0))
    widths = (BRANCH_W, KV_W, KV_W, BRANCH_W, BRANCH_W, 3 * D_MODEL)
    dtypes = (BF16, F32, F32, F32, BF16, BF16)
    return pl.pallas_call(
        _inproj_kernel, name="inproj",
        grid=(n // ROW_TILE,),
        in_specs=[pl.BlockSpec((ROW_TILE, D_MODEL), row), _layer((1, D_MODEL), l), _layer((D_MODEL, IN_W), l),
                  rspec, rspec, rspec, _layer((1, BRANCH_W), l), _layer((1, KV_W), l),
                  _layer((1, MEM_HEAD_DIM), l), _full((256, 256))],
        out_specs=[pl.BlockSpec((ROW_TILE, w), row) for w in widths],
        out_shape=[jax.ShapeDtypeStruct((n, w), d) for w, d in zip(widths, dtypes)],
        compiler_params=_cparams(1),
    )(x2d, g, w_bf, *rope, qg, kg, mqg, ones)


ATTN_TILE = 512


def _softmax_rows(s):
    m = jnp.max(s, axis=-1, keepdims=True)
    p = jnp.exp(s - m)
    return p, jnp.sum(p, axis=-1, keepdims=True)


def _sink_column(sink_ref, l, heads, rows_per_head):
    n = len(heads) * rows_per_head
    rcol = lax.broadcasted_iota(jnp.int32, (n, 1), 0)
    sink = jnp.full((n, 1), sink_ref[l, heads[-1]], F32)
    for g in range(len(heads) - 2, -1, -1):
        sink = jnp.where(rcol < (g + 1) * rows_per_head, sink_ref[l, heads[g]], sink)
    return sink


def _sink_softmax(s, sink):
    m = jnp.maximum(jnp.max(s, axis=-1, keepdims=True), sink)
    p = jnp.exp(s - m)
    return p, jnp.sum(p, axis=-1, keepdims=True) + jnp.exp(sink - m)


def _attn_prompt_kernel(l, sink_ref, q_ref, kp_ref, kc_ref, vp_ref, vc_ref, qm_ref, mk_ref, mv_ref,
                        oa_ref, oc_ref, k_scr, v_scr, mk_scr, mv_scr):
    first = pl.program_id(1) == 0
    rows = KV_GROUP * WINDOW
    r = lax.broadcasted_iota(jnp.int32, (rows, 2 * WINDOW), 0)
    j = lax.broadcasted_iota(jnp.int32, (rows, 2 * WINDOW), 1)
    qi = r % WINDOW
    band = (j > qi) & (j <= qi + WINDOW)
    sinks = [_sink_column(sink_ref, l, [kvh * KV_GROUP + g for g in range(KV_GROUP)], WINDOW)
             for kvh in range(N_KV_HEADS)]
    k_scr[:WINDOW, :] = kp_ref[...].astype(BF16)
    k_scr[WINDOW:, :] = kc_ref[...].astype(BF16)
    v_scr[:WINDOW, :] = vp_ref[...].astype(BF16)
    v_scr[WINDOW:, :] = vc_ref[...].astype(BF16)
    mk_scr[...] = mk_ref[...].astype(BF16)
    mv_scr[...] = mv_ref[...].astype(BF16)

    def sub_block(sub, carry):
        r0 = pl.multiple_of(sub * WINDOW, WINDOW)
        qrows, krows = pl.ds(r0, WINDOW), pl.ds(r0, 2 * WINDOW)
        mask = band & ((j >= WINDOW) | jnp.logical_not(first & (sub == 0)))
        for kvh in range(N_KV_HEADS):
            ksl = slice(kvh * HEAD_DIM, (kvh + 1) * HEAD_DIM)
            kcat, vcat = k_scr[krows, ksl], v_scr[krows, ksl]
            heads = [kvh * KV_GROUP + g for g in range(KV_GROUP)]
            qg = jnp.concatenate([q_ref[qrows, hd * HEAD_DIM:(hd + 1) * HEAD_DIM] for hd in heads], axis=0)
            p, denom = _sink_softmax(jnp.where(mask, _dot_nt(qg, kcat), -jnp.inf), sinks[kvh])
            o = _dot(p.astype(BF16), vcat) / denom
            for g, hd in enumerate(heads):
                oa_ref[qrows, hd * HEAD_DIM:(hd + 1) * HEAD_DIM] = o[g * WINDOW:(g + 1) * WINDOW].astype(BF16)
        for hd in range(MEM_HEADS):
            sl = slice(hd * MEM_HEAD_DIM, (hd + 1) * MEM_HEAD_DIM)
            p, den = _softmax_rows(_dot_nt(qm_ref[qrows, sl], mk_scr[:, sl]))
            oc_ref[qrows, sl] = (_dot(p.astype(BF16), mv_scr[:, sl]) / den).astype(BF16)
        return carry

    lax.fori_loop(0, ATTN_TILE // WINDOW, sub_block, 0)


def _attn_prompt(l, sinks, q, k, v, qm, mk, mv):
    b, s, _ = q.shape
    per_tile = ATTN_TILE // WINDOW
    cur = lambda bi, i: (bi, i, 0)
    prev = lambda bi, i: (bi, jnp.maximum(i * per_tile - 1, 0), 0)
    per_b = lambda bi, i: (bi, 0, 0)
    kv_prev = pl.BlockSpec((None, WINDOW, KV_W), prev)
    kv_cur = pl.BlockSpec((None, ATTN_TILE, KV_W), cur)
    return pl.pallas_call(
        functools.partial(_attn_prompt_kernel, l), name="attn_prompt",
        grid=(b, s // ATTN_TILE),
        in_specs=[pl.BlockSpec(memory_space=pltpu.SMEM),
                  pl.BlockSpec((None, ATTN_TILE, BRANCH_W), cur),
                  kv_prev, kv_cur, kv_prev, kv_cur,
                  pl.BlockSpec((None, ATTN_TILE, BRANCH_W), cur),
                  pl.BlockSpec((None, N_MEM, BRANCH_W), per_b),
                  pl.BlockSpec((None, N_MEM, BRANCH_W), per_b)],
        out_specs=[pl.BlockSpec((None, ATTN_TILE, BRANCH_W), cur)] * 2,
        out_shape=[jax.ShapeDtypeStruct((b, s, BRANCH_W), BF16)] * 2,
        scratch_shapes=[pltpu.VMEM((WINDOW + ATTN_TILE, KV_W), BF16)] * 2
                       + [pltpu.VMEM((N_MEM, BRANCH_W), BF16)] * 2,
        compiler_params=_cparams(2),
    )(sinks, q, k, k, v, v, qm, mk, mv)


DEC_BLOCK = 8
TOK_PAD = 8


def _stack_heads(x, n_heads, head_w):
    t, w = x.shape
    x8 = jnp.concatenate([x, jnp.zeros((TOK_PAD - t, w), F32)], axis=0)
    rr = lax.broadcasted_iota(jnp.int32, (n_heads * TOK_PAD, w), 0)
    ll = lax.broadcasted_iota(jnp.int32, (n_heads * TOK_PAD, w), 1)
    own = (ll // head_w) == (rr // TOK_PAD)
    return jnp.where(own, jnp.concatenate([x8] * n_heads, axis=0), 0.0).astype(BF16), own


def _unstack_heads(o, own, n_heads, t):
    o = jnp.where(own, o, 0.0)
    acc = o[0:TOK_PAD]
    for hd in range(1, n_heads):
        acc = acc + o[hd * TOK_PAD:(hd + 1) * TOK_PAD]
    return acc[0:t]


def _attn_sample_kernel(l, sink_ref, q_ref, kn_ref, vn_ref, ck_ref, cv_ref, qm_ref, mk_ref, mv_ref, e_ref,
                        oa_ref, oc_ref, nk_ref, nv_ref, kc_scr, vc_scr):
    t = q_ref.shape[1]
    keys = 2 * WINDOW
    rows = N_HEADS * TOK_PAD
    rk = lax.broadcasted_iota(jnp.int32, (rows, keys), 0)
    jk = lax.broadcasted_iota(jnp.int32, (rows, keys), 1)
    tk = rk % TOK_PAD
    mask = (tk < t) & (((jk < WINDOW) & (jk > tk)) | ((jk >= WINDOW) & (jk - WINDOW <= tk)))
    sink = _sink_column(sink_ref, l, list(range(N_HEADS)), TOK_PAD)
    mem_rows = N_MEM * MEM_HEADS
    mem_own = (lax.broadcasted_iota(jnp.int32, (MEM_HEADS * TOK_PAD, mem_rows), 1) % MEM_HEADS
               == lax.broadcasted_iota(jnp.int32, (MEM_HEADS * TOK_PAD, mem_rows), 0) // TOK_PAD)
    e = e_ref[...]
    kc_scr[WINDOW:, :] = jnp.zeros((WINDOW, KV_W), F32)
    vc_scr[WINDOW:, :] = jnp.zeros((WINDOW, KV_W), F32)
    for bi in range(DEC_BLOCK):
        kc_scr[:WINDOW, :] = ck_ref[bi]
        vc_scr[:WINDOW, :] = cv_ref[bi]
        kc_scr[WINDOW:WINDOW + t, :] = kn_ref[bi]
        vc_scr[WINDOW:WINDOW + t, :] = vn_ref[bi]
        nk_ref[bi] = kc_scr[t:WINDOW + t, :]
        nv_ref[bi] = vc_scr[t:WINDOW + t, :]
        kx = _dot(kc_scr[...].astype(BF16), e).astype(BF16)
        vx = _dot(vc_scr[...].astype(BF16), e).astype(BF16)
        qs, own = _stack_heads(q_ref[bi], N_HEADS, HEAD_DIM)
        p, denom = _sink_softmax(jnp.where(mask, _dot_nt(qs, kx), -jnp.inf), sink)
        oa_ref[bi] = _unstack_heads(_dot(p.astype(BF16), vx) / denom, own, N_HEADS, t)

        qm8 = jnp.concatenate([qm_ref[bi], jnp.zeros((TOK_PAD - t, BRANCH_W), F32)], axis=0).astype(BF16)
        qst = jnp.concatenate([qm8[:, hd * MEM_HEAD_DIM:(hd + 1) * MEM_HEAD_DIM] for hd in range(MEM_HEADS)],
                              axis=0)
        pm, den = _softmax_rows(jnp.where(mem_own, _dot_nt(qst, mk_ref[bi].astype(BF16)), -jnp.inf))
        om = _dot(pm.astype(BF16), mv_ref[bi].astype(BF16)) / den
        oc_ref[bi] = jnp.concatenate([om[hd * TOK_PAD:(hd + 1) * TOK_PAD] for hd in range(MEM_HEADS)],
                                     axis=1)[0:t]


def _attn_sample(l, sinks, q, kn, vn, ck, cv, qm, mk, mv, e):
    b, t, _ = q.shape
    blk = lambda i: (i, 0, 0)
    lblk = lambda i: (l, i, 0, 0)
    tok = lambda w: pl.BlockSpec((DEC_BLOCK, t, w), blk)
    cache_in = pl.BlockSpec((None, DEC_BLOCK, WINDOW, KV_W), lblk)
    cache_out = pl.BlockSpec((DEC_BLOCK, WINDOW, KV_W), blk)
    mem = pl.BlockSpec((None, DEC_BLOCK, N_MEM * MEM_HEADS, MEM_HEAD_DIM), lblk)
    return pl.pallas_call(
        functools.partial(_attn_sample_kernel, l), name="attn_sample",
        grid=(b // DEC_BLOCK,),
        in_specs=[pl.BlockSpec(memory_space=pltpu.SMEM), tok(BRANCH_W), tok(KV_W), tok(KV_W),
                  cache_in, cache_in, tok(BRANCH_W), mem, mem, _full((KV_W, BRANCH_W))],
        out_specs=[tok(BRANCH_W), tok(BRANCH_W), cache_out, cache_out],
        out_shape=[jax.ShapeDtypeStruct((b, t, BRANCH_W), F32)] * 2
                  + [jax.ShapeDtypeStruct((b, WINDOW, KV_W), F32)] * 2,
        scratch_shapes=[pltpu.VMEM((2 * WINDOW, KV_W), F32)] * 2,
        compiler_params=_cparams(1),
    )(sinks, q, kn, vn, ck, cv, qm, mk, mv, e)


def _ssm_tables(a_re, a_im, log_dt, b_re, b_im, c_re, c_im, chunk):
    hp = lax.Precision.HIGHEST
    dt = jnp.exp(log_dt)[..., None]
    mag = jnp.exp(a_re * dt)
    lam_re, lam_im = mag * jnp.cos(a_im * dt), mag * jnp.sin(a_im * dt)
    den = a_re * a_re + a_im * a_im
    nr, ni = lam_re - 1.0, lam_im
    g_re = (nr * a_re + ni * a_im) / den
    g_im = (ni * a_re - nr * a_im) / den
    bg_re = g_re[..., None] * b_re - g_im[..., None] * b_im
    bg_im = g_re[..., None] * b_im + g_im[..., None] * b_re
    d = jnp.arange(chunk + 1, dtype=F32)[:, None, None, None]
    pm = jnp.exp(d * (a_re * dt)[None])
    pw_re = pm * jnp.cos(d * (a_im * dt)[None])
    pw_im = pm * jnp.sin(d * (a_im * dt)[None])
    z_re = pw_re[..., None] * bg_re[None] - pw_im[..., None] * bg_im[None]
    z_im = pw_re[..., None] * bg_im[None] + pw_im[..., None] * bg_re[None]
    kern = (jnp.einsum('lgop,dlgpi->dlgio', c_re, z_re[:chunk], precision=hp)
            - jnp.einsum('lgop,dlgpi->dlgio', c_im, z_im[:chunk], precision=hp))
    nl = a_re.shape[0]
    nb, gb = N_LANE_BLOCKS, GROUPS_PER_LANE_BLOCK
    kd = jnp.transpose(kern.reshape(chunk, nl, nb, gb, SSM_GROUP_CH, SSM_GROUP_CH), (1, 2, 0, 3, 4, 5))
    kd = kd.reshape(nl, nb, chunk, LANES, SSM_GROUP_CH)

    def to_rows(z):
        z = z[:chunk][::-1].reshape(chunk, nl, nb, gb, SSM_STATE, SSM_GROUP_CH)
        return jnp.transpose(z, (1, 2, 0, 3, 5, 4)).reshape(nl, nb, chunk, LANES, SSM_STATE)

    zr, zi = to_rows(z_re), to_rows(z_im)
    zc = jnp.concatenate([zr, zr, zi, zi], axis=-1)

    pr, pi = pw_re[1:], pw_im[1:]
    f_a = c_re[None] * pr[:, :, :, None, :] - c_im[None] * pi[:, :, :, None, :]
    f_b = c_re[None] * pi[:, :, :, None, :] + c_im[None] * pr[:, :, :, None, :]

    def from_cols(f):
        f = f.reshape(chunk, nl, nb, gb, SSM_GROUP_CH, SSM_STATE)
        return jnp.transpose(f, (1, 2, 0, 4, 3, 5)).reshape(nl, nb, chunk, SSM_GROUP_CH, STATE_W)

    ft = jnp.concatenate([from_cols(f_a), from_cols(-f_b)], axis=-1)

    kw = chunk * LANES
    spec = lambda *shape: pl.BlockSpec((None, None) + shape, lambda l, n: (l, n) + (0,) * len(shape))
    intra, to_st, from_st_t = pl.pallas_call(
        _ssm_table_kernel, name="ssm_tables",
        grid=(nl, nb),
        in_specs=[spec(chunk, LANES, SSM_GROUP_CH), spec(chunk, LANES, 2 * LANES),
                  spec(chunk, SSM_GROUP_CH, 2 * STATE_W)],
        out_specs=[spec(kw, kw), spec(kw, 2 * STATE_W), spec(kw, 2 * STATE_W)],
        out_shape=[jax.ShapeDtypeStruct((nl, nb, kw, kw), BF16),
                   jax.ShapeDtypeStruct((nl, nb, kw, 2 * STATE_W), BF16),
                   jax.ShapeDtypeStruct((nl, nb, kw, 2 * STATE_W), BF16)],
        compiler_params=_cparams(2),
    )(kd, zc, ft)
    dec_re = pw_re[chunk].reshape(nl, nb, 1, STATE_W)
    dec_im = pw_im[chunk].reshape(nl, nb, 1, STATE_W)
    return intra, to_st, from_st_t, dec_re, dec_im


def _ssm_table_kernel(kd_ref, zc_ref, ft_ref, intra_ref, to_ref, fromt_ref):
    chunk = kd_ref.shape[0]
    gch = SSM_GROUP_CH
    r = lax.broadcasted_iota(jnp.int32, (LANES, LANES), 0)
    c = lax.broadcasted_iota(jnp.int32, (LANES, LANES), 1)
    same_group = (r // gch) == (c // gch)
    spread = (lax.broadcasted_iota(jnp.int32, (gch, LANES), 1) % gch
              == lax.broadcasted_iota(jnp.int32, (gch, LANES), 0)).astype(BF16)
    spread_t = (lax.broadcasted_iota(jnp.int32, (LANES, gch), 0) % gch
                == lax.broadcasted_iota(jnp.int32, (LANES, gch), 1)).astype(BF16)
    rs = lax.broadcasted_iota(jnp.int32, (LANES, 2 * STATE_W), 0)
    cs = lax.broadcasted_iota(jnp.int32, (LANES, 2 * STATE_W), 1)
    own_state = (rs // gch) == ((cs % STATE_W) // SSM_STATE)

    lag_blocks = [jnp.where(same_group, _dot(kd_ref[d].astype(BF16), spread), 0.0).astype(BF16)
                  for d in range(chunk)]
    zero = jnp.zeros((LANES, LANES), BF16)
    for s in range(chunk):
        for t in range(chunk):
            intra_ref[s * LANES:(s + 1) * LANES, t * LANES:(t + 1) * LANES] = (
                lag_blocks[t - s] if t >= s else zero)
        z = zc_ref[s]
        full = jnp.concatenate([z[:, :LANES]] * (STATE_W // LANES) + [z[:, LANES:]] * (STATE_W // LANES), axis=1)
        to_ref[s * LANES:(s + 1) * LANES, :] = jnp.where(own_state, full, 0.0).astype(BF16)
        f = _dot(spread_t, ft_ref[s].astype(BF16))
        fromt_ref[s * LANES:(s + 1) * LANES, :] = jnp.where(own_state, f, 0.0).astype(BF16)


SSM_SEQS = 2


def _chunk_rows(ref, chunk):
    n = ref.shape[0] // chunk
    return jnp.concatenate([ref[pl.ds(s, n, stride=chunk), :] for s in range(chunk)], axis=1)


def _store_chunk_rows(ref, y, chunk):
    n = ref.shape[0] // chunk
    for t in range(chunk):
        ref[pl.ds(t, n, stride=chunk), :] = y[:, t * LANES:(t + 1) * LANES]


def _ssm_prompt_kernel(u_ref, intra_ref, to_ref, fromt_ref, dre_ref, dim_ref,
                       y_ref, hre_ref, him_ref, st_scr):
    chunk = PROMPT_CHUNK
    nchunks = u_ref.shape[0] // (chunk * SSM_SEQS)
    x = _chunk_rows(u_ref, chunk).astype(BF16)
    y = _dot(x, intra_ref[...])
    add_all = _dot(x, to_ref[...])
    nblk = STATE_W // LANES
    for c in range(2 * nblk):
        st_scr[c] = add_all[:, c * LANES:(c + 1) * LANES]
    dre = [dre_ref[:, c * LANES:(c + 1) * LANES] for c in range(nblk)]
    dim = [dim_ref[:, c * LANES:(c + 1) * LANES] for c in range(nblk)]

    def step(k, carry):
        hr, hi = carry
        rows = pl.ds(k, SSM_SEQS, stride=nchunks)
        new_r, new_i = [], []
        for c in range(nblk):
            add_r, add_i = st_scr[c, rows, :], st_scr[nblk + c, rows, :]
            st_scr[c, rows, :] = hr[c]
            st_scr[nblk + c, rows, :] = hi[c]
            new_r.append(dre[c] * hr[c] - dim[c] * hi[c] + add_r)
            new_i.append(dre[c] * hi[c] + dim[c] * hr[c] + add_i)
        return tuple(new_r), tuple(new_i)

    zero = tuple(jnp.zeros((SSM_SEQS, LANES), F32) for _ in range(nblk))
    hr, hi = lax.fori_loop(0, nchunks, step, (zero, zero), unroll=4)
    hre_ref[...] = jnp.concatenate(hr, axis=1)
    him_ref[...] = jnp.concatenate(hi, axis=1)
    h_in = jnp.concatenate([st_scr[c] for c in range(2 * nblk)], axis=1).astype(BF16)
    _store_chunk_rows(y_ref, y + _dot_nt(h_in, fromt_ref[...]), chunk)


def _ssm_prompt(l, u2d, seq_len, intra, to_st, from_st_t, dec_re, dec_im):
    n = u2d.shape[0]
    chunk = PROMPT_CHUNK
    rows = SSM_SEQS * seq_len
    steps = n // rows
    kw = chunk * LANES
    data = pl.BlockSpec((rows, LANES), lambda j, h: (h, j))
    wspec = lambda r, c: pl.BlockSpec((None, None, r, c), lambda j, h: (l, j, 0, 0))
    st_out = pl.BlockSpec((None, None, SSM_SEQS, STATE_W), lambda j, h: (j, h, 0, 0))
    st_shape = jax.ShapeDtypeStruct((N_LANE_BLOCKS, steps, SSM_SEQS, STATE_W), F32)
    return pl.pallas_call(
        _ssm_prompt_kernel, name="ssm_prompt",
        grid=(N_LANE_BLOCKS, steps),
        in_specs=[data, wspec(kw, kw), wspec(kw, 2 * STATE_W), wspec(kw, 2 * STATE_W),
                  wspec(1, STATE_W), wspec(1, STATE_W)],
        out_specs=[data, st_out, st_out],
        out_shape=[jax.ShapeDtypeStruct(u2d.shape, F32), st_shape, st_shape],
        scratch_shapes=[pltpu.VMEM((2 * STATE_W // LANES, rows // chunk, LANES), F32)],
        compiler_params=_cparams(2),
    )(u2d, intra, to_st, from_st_t, dec_re, dec_im)


def _ssm_sample_kernel(chunk, u_ref, h0r_ref, h0i_ref, intra_ref, to_ref, fromt_ref, dre_ref, dim_ref,
                       y_ref, hre_ref, him_ref):
    x = _chunk_rows(u_ref, chunk).astype(BF16)
    hr, hi = h0r_ref[...], h0i_ref[...]
    dre, dim = dre_ref[...], dim_ref[...]
    add = _dot(x, to_ref[...])
    hre_ref[...] = dre * hr - dim * hi + add[:, :STATE_W]
    him_ref[...] = dre * hi + dim * hr + add[:, STATE_W:]
    h0 = jnp.concatenate([hr, hi], axis=1).astype(BF16)
    _store_chunk_rows(y_ref, _dot(x, intra_ref[...]) + _dot_nt(h0, fromt_ref[...]), chunk)


def _ssm_sample(l, u2d, t, h0_re, h0_im, intra, to_st, from_st_t, dec_re, dec_im):
    n = u2d.shape[0]
    b = n // t
    kw = t * LANES
    data = pl.BlockSpec((n, LANES), lambda j: (0, j))
    st_in = pl.BlockSpec((None, b, STATE_W), lambda j: (l, 0, j))
    st_out = pl.BlockSpec((b, STATE_W), lambda j: (0, j))
    wspec = lambda r, c: pl.BlockSpec((None, None, r, c), lambda j: (l, j, 0, 0))
    return pl.pallas_call(
        functools.partial(_ssm_sample_kernel, t), name="ssm_sample",
        grid=(N_LANE_BLOCKS,),
        in_specs=[data, st_in, st_in, wspec(kw, kw), wspec(kw, 2 * STATE_W), wspec(kw, 2 * STATE_W),
                  wspec(1, STATE_W), wspec(1, STATE_W)],
        out_specs=[data, st_out, st_out],
        out_shape=[jax.ShapeDtypeStruct(u2d.shape, F32),
                   jax.ShapeDtypeStruct((b, N_LANE_BLOCKS * STATE_W), F32),
                   jax.ShapeDtypeStruct((b, N_LANE_BLOCKS * STATE_W), F32)],
        compiler_params=_cparams(1),
    )(u2d, h0_re, h0_im, intra, to_st, from_st_t, dec_re, dec_im)


FF_CHUNKS = ((0, 512), (512, 512), (1024, 512), (1536, 512), (2048, 512), (2560, 256))


def _merge_ffn_kernel(x_ref, oa_ref, y_ref, u_ref, oc_ref, gates_ref, d_ref, wglu_ref, wb_ref, wout_ref,
                      g_ref, wup_ref, wdn_ref, o_ref):
    z = jax.nn.gelu(y_ref[...] + d_ref[...] * u_ref[...])
    ob = (z * jax.nn.sigmoid(_dot(z.astype(BF16), wglu_ref[...]))).astype(BF16)
    gate = lambda n: gates_ref[:, n * D_MODEL:(n + 1) * D_MODEL].astype(F32)
    merged = gate(0) * _dot(oa_ref[...], wb_ref[0])
    merged = merged + gate(1) * _dot(ob, wb_ref[1])
    merged = merged + gate(2) * _dot(oc_ref[...], wb_ref[2])
    x = x_ref[...] + _dot(merged.astype(BF16), wout_ref[...])
    h = (_rms(x) * g_ref[...]).astype(BF16)
    acc = x
    for lo, w in FF_CHUNKS:
        up_gate = _dot(h, wup_ref[:, lo:lo + w])
        up = _dot(h, wup_ref[:, D_FF + lo:D_FF + lo + w])
        acc = acc + _dot((jax.nn.silu(up_gate) * up).astype(BF16), wdn_ref[lo:lo + w, :])
    o_ref[...] = acc


def _merge_ffn(l, x2d, oa, y, u, oc, gates, d, wglu, wb, wout, g_ffn, wup, wdn):
    n = x2d.shape[0]
    row = lambda w: pl.BlockSpec((ROW_TILE, w), lambda i: (i, 0))
    return pl.pallas_call(
        _merge_ffn_kernel, name="merge_ffn",
        grid=(n // ROW_TILE,),
        in_specs=[row(D_MODEL), row(BRANCH_W), row(BRANCH_W), row(BRANCH_W), row(BRANCH_W),
                  row(3 * D_MODEL), _layer((1, BRANCH_W), l), _layer((BRANCH_W, BRANCH_W), l),
                  _layer((3, BRANCH_W, D_MODEL), l), _layer((D_MODEL, D_MODEL), l),
                  _layer((1, D_MODEL), l), _layer((D_MODEL, 2 * D_FF), l), _layer((D_FF, D_MODEL), l)],
        out_specs=row(D_MODEL),
        out_shape=jax.ShapeDtypeStruct((n, D_MODEL), F32),
        compiler_params=_cparams(1),
    )(x2d, oa, y, u, oc, gates, d, wglu, wb, wout, g_ffn, wup, wdn)


def _rope_tables(pos):
    half = HEAD_DIM // 2
    inv = ROPE_THETA ** (-jnp.arange(half, dtype=F32) / half)
    ang = pos[:, None] * inv[None, :]
    cos, sin = jnp.cos(ang), jnp.sin(ang)
    zero = jnp.zeros_like(sin)
    two = lambda a: jnp.concatenate([a, a], axis=1)
    return (two(jnp.concatenate([cos, cos], axis=1)), two(jnp.concatenate([-sin, zero], axis=1)),
            two(jnp.concatenate([zero, sin], axis=1)))


def _state_from_blocks(h):
    nb, hb, sq, _ = h.shape
    return jnp.transpose(h.reshape(nb, hb * sq, STATE_W), (1, 0, 2)).reshape(hb * sq, SSM_GROUPS, SSM_STATE)


def kernel(x_prompt, x_sample, cache_swa_k, cache_swa_v, state_ssm_re, state_ssm_im, cache_mem_k,
           cache_mem_v, mem_prompt, attn_norm, w_in, q_norm, k_norm, attn_sinks, ssm_a_re, ssm_a_im,
           ssm_log_dt, ssm_b_re, ssm_b_im, ssm_c_re, ssm_c_im, ssm_d, ssm_w_glu, mem_norm, w_mem_kv,
           mem_q_norm, mem_k_norm, w_branch, w_out, ffn_norm, w_ffn_up, w_ffn_down):
    depth = w_in.shape[0]
    b, s, _ = x_prompt.shape
    db, t, _ = x_sample.shape
    assert db * t == ROW_TILE and s % ROW_TILE == 0

    rope_p = _rope_tables(jnp.arange(s, dtype=F32))
    rope_s = _rope_tables(jnp.tile(PAST_LEN + jnp.arange(t, dtype=F32), db))
    ssm_params = (ssm_a_re, ssm_a_im, ssm_log_dt, ssm_b_re, ssm_b_im, ssm_c_re, ssm_c_im)
    tab_p = _ssm_tables(*ssm_params, PROMPT_CHUNK)
    tab_s = _ssm_tables(*ssm_params, t)
    blk = jnp.arange(256) // HEAD_DIM
    ones = (blk[:, None] == blk[None, :]).astype(BF16)
    lane = jnp.arange(BRANCH_W)[None, :]
    src = jnp.arange(KV_W)[:, None]
    expand = ((lane // (HEAD_DIM * KV_GROUP) == src // HEAD_DIM)
              & (lane % HEAD_DIM == src % HEAD_DIM)).astype(BF16)

    w_in_bf, w_mem_bf = w_in.astype(BF16), w_mem_kv.astype(BF16)
    wglu, wb, wout = ssm_w_glu.astype(BF16), w_branch.astype(BF16), w_out.astype(BF16)
    wup, wdn = w_ffn_up.astype(BF16), w_ffn_down.astype(BF16)
    row = lambda a: a[:, None, :]
    g_attn, g_mem, g_ffn, d_row = row(attn_norm), row(mem_norm), row(ffn_norm), row(ssm_d)
    qg, kg = row(jnp.tile(q_norm, (1, N_HEADS))), row(jnp.tile(k_norm, (1, N_KV_HEADS)))
    mqg, mkg = row(mem_q_norm), row(mem_k_norm)
    ck = cache_swa_k.reshape(depth, db, WINDOW, KV_W)
    cv = cache_swa_v.reshape(depth, db, WINDOW, KV_W)
    cmk = cache_mem_k.reshape(depth, db, N_MEM * MEM_HEADS, MEM_HEAD_DIM)
    cmv = cache_mem_v.reshape(depth, db, N_MEM * MEM_HEADS, MEM_HEAD_DIM)
    h0r = state_ssm_re.reshape(depth, db, SSM_GROUPS * SSM_STATE)
    h0i = state_ssm_im.reshape(depth, db, SSM_GROUPS * SSM_STATE)

    yp = x_prompt.reshape(b * s, D_MODEL)
    ys = x_sample.reshape(db * t, D_MODEL)
    mem2d = mem_prompt.reshape(b * N_MEM, D_MODEL)
    outs = [[] for _ in range(10)]
    for l in range(depth):
        mk, mv = _memkv(l, mem2d, g_mem, w_mem_bf, mkg)
        mk3, mv3 = mk.reshape(b, N_MEM, BRANCH_W), mv.reshape(b, N_MEM, BRANCH_W)

        q, k, v, u, mq, gates = _inproj(l, yp, g_attn, w_in_bf, rope_p, s // ROW_TILE, qg, kg, mqg, ones)
        k3, v3 = k.reshape(b, s, KV_W), v.reshape(b, s, KV_W)
        oa, oc = _attn_prompt(l, attn_sinks, q.reshape(b, s, BRANCH_W), k3, v3,
                              mq.reshape(b, s, BRANCH_W), mk3, mv3)
        ysm, hre, him = _ssm_prompt(l, u, s, *tab_p)
        yp = _merge_ffn(l, yp, oa.reshape(b * s, BRANCH_W), ysm, u, oc.reshape(b * s, BRANCH_W), gates,
                        d_row, wglu, wb, wout, g_ffn, wup, wdn)

        qs, ks, vs, us, mqs, gates_s = _inproj(l, ys, g_attn, w_in_bf, rope_s, 1, qg, kg, mqg, ones)
        oas, ocs, nk, nv = _attn_sample(
            l, attn_sinks, qs.astype(F32).reshape(db, t, BRANCH_W), ks.reshape(db, t, KV_W),
            vs.reshape(db, t, KV_W), ck, cv, mqs.astype(F32).reshape(db, t, BRANCH_W), cmk, cmv, expand)
        yss, hrs, his = _ssm_sample(l, us, t, h0r, h0i, *tab_s)
        ys = _merge_ffn(l, ys, oas.astype(BF16).reshape(db * t, BRANCH_W), yss, us,
                        ocs.astype(BF16).reshape(db * t, BRANCH_W), gates_s, d_row, wglu, wb, wout,
                        g_ffn, wup, wdn)

        new = (k3[:, -WINDOW:].reshape(b, WINDOW, N_KV_HEADS, HEAD_DIM),
               v3[:, -WINDOW:].reshape(b, WINDOW, N_KV_HEADS, HEAD_DIM),
               _state_from_blocks(hre), _state_from_blocks(him),
               mk3.reshape(b, N_MEM, MEM_HEADS, MEM_HEAD_DIM), mv3.reshape(b, N_MEM, MEM_HEADS, MEM_HEAD_DIM),
               nk.reshape(db, WINDOW, N_KV_HEADS, HEAD_DIM), nv.reshape(db, WINDOW, N_KV_HEADS, HEAD_DIM),
               hrs.reshape(db, SSM_GROUPS, SSM_STATE), his.reshape(db, SSM_GROUPS, SSM_STATE))
        for lst, a in zip(outs, new):
            lst.append(a)

    return (yp.reshape(b, s, D_MODEL), ys.reshape(db, t, D_MODEL)) + tuple(jnp.stack(o) for o in outs)
```

```python
import functools
import math

import jax
import jax.numpy as jnp
from jax import lax
from jax.experimental import pallas as pl
from jax.experimental.pallas import tpu as pltpu

F32 = jnp.float32
BF16 = jnp.bfloat16

D_MODEL = 1024
BRANCH_W = 512
HEAD_DIM = 64
N_HEADS = 8
N_KV_HEADS = 2
KV_GROUP = 4
KV_W = N_KV_HEADS * HEAD_DIM
WINDOW = 128
ROPE_THETA = 10000.0
PAST_LEN = 16384
SSM_GROUPS = 32
SSM_GROUP_CH = 16
SSM_STATE = 64
N_MEM = 256
MEM_HEADS = 4
MEM_HEAD_DIM = 128
D_FF = 2816
RMS_EPS = 1e-6

Q_OFF, K_OFF, V_OFF, U_OFF, MQ_OFF, G_OFF = 0, 512, 640, 768, 1280, 1792
IN_W = G_OFF + 3 * D_MODEL

LANES = 128
SUBLANES = 8
GROUPS_PER_LANE_BLOCK = LANES // SSM_GROUP_CH
N_LANE_BLOCKS = BRANCH_W // LANES
STATE_W = GROUPS_PER_LANE_BLOCK * SSM_STATE
PROMPT_CHUNK = 8
ROW_TILE = 512
VMEM_LIMIT = 56 * 1024 * 1024


def _dot(a, b):
    return jnp.dot(a, b, preferred_element_type=F32)


def _dot_nt(a, b):
    return lax.dot_general(a, b, (((1,), (1,)), ((), ())), preferred_element_type=F32)


def _cparams(n_axes):
    return pltpu.CompilerParams(dimension_semantics=("arbitrary",) * n_axes,
                                vmem_limit_bytes=VMEM_LIMIT)


def _full(shape):
    nd = len(shape)
    return pl.BlockSpec(shape, lambda *_: (0,) * nd)


def _layer(shape, l):
    nd = len(shape)
    return pl.BlockSpec((None,) + tuple(shape), lambda *_: (l,) + (0,) * nd, pipeline_mode=pl.Buffered(1))


def _rms(x):
    return x * lax.rsqrt(jnp.mean(x * x, axis=-1, keepdims=True) + RMS_EPS)


def _memkv_kernel(mem_ref, g_ref, w_ref, kg_ref, k_ref, v_ref):
    h = _rms(mem_ref[...]) * g_ref[...]
    kv = _dot(h.astype(BF16), w_ref[...])
    kg = kg_ref[...]
    for hd in range(MEM_HEADS):
        sl = slice(hd * MEM_HEAD_DIM, (hd + 1) * MEM_HEAD_DIM)
        k_ref[:, sl] = _rms(kv[:, sl]) * kg
    v_ref[...] = kv[:, BRANCH_W:]


def _memkv(l, mem2d, g, w_bf, kg):
    n = mem2d.shape[0]
    return pl.pallas_call(
        _memkv_kernel, name="memkv",
        grid=(n // ROW_TILE,),
        in_specs=[pl.BlockSpec((ROW_TILE, D_MODEL), lambda i: (i, 0)),
                  _layer((1, D_MODEL), l), _layer((D_MODEL, 2 * BRANCH_W), l), _layer((1, MEM_HEAD_DIM), l)],
        out_specs=[pl.BlockSpec((ROW_TILE, BRANCH_W), lambda i: (i, 0))] * 2,
        out_shape=[jax.ShapeDtypeStruct((n, BRANCH_W), F32)] * 2,
        compiler_params=_cparams(1),
    )(mem2d, g, w_bf, kg)


def _rotary(x, cos, sin_lo, sin_hi):
    w = x.shape[-1]
    half = HEAD_DIM // 2
    return x * cos + pltpu.roll(x, w - half, 1) * sin_lo + pltpu.roll(x, half, 1) * sin_hi


def _inproj_kernel(x_ref, g_ref, w_ref, cos_ref, slo_ref, shi_ref, qg_ref, kg_ref, mqg_ref, ones_ref,
                   q_ref, k_ref, v_ref, u_ref, mq_ref, gates_ref):
    hb = (_rms(x_ref[...]) * g_ref[...]).astype(BF16)
    ones = ones_ref[...]
    cos, slo, shi = cos_ref[...], slo_ref[...], shi_ref[...]

    zq = _dot(hb, w_ref[:, Q_OFF:K_OFF])
    sq = (zq * zq).astype(BF16)
    ssq = jnp.concatenate([_dot(sq[:, :256], ones), _dot(sq[:, 256:], ones)], axis=1)
    qn = zq * lax.rsqrt(ssq * (1.0 / HEAD_DIM) + RMS_EPS) * qg_ref[...]
    cos4, slo4, shi4 = (jnp.concatenate([t] * 4, axis=1) for t in (cos, slo, shi))
    q_ref[...] = (_rotary(qn, cos4, slo4, shi4) * (1.0 / math.sqrt(HEAD_DIM))).astype(BF16)

    zkv = _dot(hb, w_ref[:, K_OFF:U_OFF])
    zk = zkv[:, :KV_W]
    ssk = _dot((zk * zk).astype(BF16), ones[:KV_W, :KV_W])
    kn = zk * lax.rsqrt(ssk * (1.0 / HEAD_DIM) + RMS_EPS) * kg_ref[...]
    k_ref[...] = _rotary(kn, cos, slo, shi)
    v_ref[...] = zkv[:, KV_W:]

    u_ref[...] = _dot(hb, w_ref[:, U_OFF:MQ_OFF])

    zm = _dot(hb, w_ref[:, MQ_OFF:G_OFF])
    mqg = mqg_ref[...]
    for hd in range(MEM_HEADS):
        sl = slice(hd * MEM_HEAD_DIM, (hd + 1) * MEM_HEAD_DIM)
        mq_ref[:, sl] = (_rms(zm[:, sl]) * mqg * (1.0 / math.sqrt(MEM_HEAD_DIM))).astype(BF16)

    for c in range(3 * D_MODEL // 512):
        lo = G_OFF + c * 512
        gates_ref[:, c * 512:(c + 1) * 512] = jax.nn.sigmoid(_dot(hb, w_ref[:, lo:lo + 512])).astype(BF16)


def _inproj(l, x2d, g, w_bf, rope, rope_blocks, qg, kg, mqg, ones):
    n = x2d.shape[0]
    row = lambda i: (i, 0)
    rspec = pl.BlockSpec((ROW_TILE, LANES), lambda i: (i % rope_blocks, 0))
    widths = (BRANCH_W, KV_W, KV_W, BRANCH_W, BRANCH_W, 3 * D_MODEL)
    dtypes = (BF16, F32, F32, F32, BF16, BF16)
    return pl.pallas_call(
        _inproj_kernel, name="inproj",
        grid=(n // ROW_TILE,),
        in_specs=[pl.BlockSpec((ROW_TILE, D_MODEL), row), _layer((1, D_MODEL), l), _layer((D_MODEL, IN_W), l),
                  rspec, rspec, rspec, _layer((1, BRANCH_W), l), _layer((1, KV_W), l),
                  _layer((1, MEM_HEAD_DIM), l), _full((256, 256))],
        out_specs=[pl.BlockSpec((ROW_TILE, w), row) for w in widths],
        out_shape=[jax.ShapeDtypeStruct((n, w), d) for w, d in zip(widths, dtypes)],
        compiler_params=_cparams(1),
    )(x2d, g, w_bf, *rope, qg, kg, mqg, ones)


ATTN_TILE = 512


def _softmax_rows(s):
    m = jnp.max(s, axis=-1, keepdims=True)
    p = jnp.exp(s - m)
    return p, jnp.sum(p, axis=-1, keepdims=True)


def _sink_column(sink_ref, l, heads, rows_per_head):
    n = len(heads) * rows_per_head
    rcol = lax.broadcasted_iota(jnp.int32, (n, 1), 0)
    sink = jnp.full((n, 1), sink_ref[l, heads[-1]], F32)
    for g in range(len(heads) - 2, -1, -1):
        sink = jnp.where(rcol < (g + 1) * rows_per_head, sink_ref[l, heads[g]], sink)
    return sink


def _sink_softmax(s, sink):
    m = jnp.maximum(jnp.max(s, axis=-1, keepdims=True), sink)
    p = jnp.exp(s - m)
    return p, jnp.sum(p, axis=-1, keepdims=True) + jnp.exp(sink - m)


def _attn_prompt_kernel(l, sink_ref, q_ref, kp_ref, kc_ref, vp_ref, vc_ref, qm_ref, mk_ref, mv_ref,
                        oa_ref, oc_ref, k_scr, v_scr, mk_scr, mv_scr):
    first = pl.program_id(1) == 0
    rows = KV_GROUP * WINDOW
    r = lax.broadcasted_iota(jnp.int32, (rows, 2 * WINDOW), 0)
    j = lax.broadcasted_iota(jnp.int32, (rows, 2 * WINDOW), 1)
    qi = r % WINDOW
    band = (j > qi) & (j <= qi + WINDOW)
    sinks = [_sink_column(sink_ref, l, [kvh * KV_GROUP + g for g in range(KV_GROUP)], WINDOW)
             for kvh in range(N_KV_HEADS)]
    k_scr[:WINDOW, :] = kp_ref[...].astype(BF16)
    k_scr[WINDOW:, :] = kc_ref[...].astype(BF16)
    v_scr[:WINDOW, :] = vp_ref[...].astype(BF16)
    v_scr[WINDOW:, :] = vc_ref[...].astype(BF16)
    mk_scr[...] = mk_ref[...].astype(BF16)
    mv_scr[...] = mv_ref[...].astype(BF16)

    def sub_block(sub, carry):
        r0 = pl.multiple_of(sub * WINDOW, WINDOW)
        qrows, krows = pl.ds(r0, WINDOW), pl.ds(r0, 2 * WINDOW)
        mask = band & ((j >= WINDOW) | jnp.logical_not(first & (sub == 0)))
        ksl = [slice(kvh * HEAD_DIM, (kvh + 1) * HEAD_DIM) for kvh in range(N_KV_HEADS)]
        msl = [slice(hd * MEM_HEAD_DIM, (hd + 1) * MEM_HEAD_DIM) for hd in range(MEM_HEADS)]
        heads = [[kvh * KV_GROUP + g for g in range(KV_GROUP)] for kvh in range(N_KV_HEADS)]
        scores = []
        for kvh in range(N_KV_HEADS):
            qg = jnp.concatenate([q_ref[qrows, hd * HEAD_DIM:(hd + 1) * HEAD_DIM] for hd in heads[kvh]], axis=0)
            scores.append(jnp.where(mask, _dot_nt(qg, k_scr[krows, ksl[kvh]]), -jnp.inf))
        mscores = [_dot_nt(qm_ref[qrows, sl], mk_scr[:, sl]) for sl in msl]
        probs = [_sink_softmax(scores[kvh], sinks[kvh]) for kvh in range(N_KV_HEADS)]
        mprobs = [_softmax_rows(s) for s in mscores]
        outs = [_dot(probs[kvh][0].astype(BF16), v_scr[krows, ksl[kvh]]) / probs[kvh][1]
                for kvh in range(N_KV_HEADS)]
        mouts = [_dot(mprobs[hd][0].astype(BF16), mv_scr[:, msl[hd]]) / mprobs[hd][1] for hd in range(MEM_HEADS)]
        for kvh in range(N_KV_HEADS):
            for g, hd in enumerate(heads[kvh]):
                oa_ref[qrows, hd * HEAD_DIM:(hd + 1) * HEAD_DIM] = (
                    outs[kvh][g * WINDOW:(g + 1) * WINDOW].astype(BF16))
        for hd in range(MEM_HEADS):
            oc_ref[qrows, msl[hd]] = mouts[hd].astype(BF16)
        return carry

    lax.fori_loop(0, ATTN_TILE // WINDOW, sub_block, 0)


def _attn_prompt(l, sinks, q, k, v, qm, mk, mv):
    b, s, _ = q.shape
    per_tile = ATTN_TILE // WINDOW
    cur = lambda bi, i: (bi, i, 0)
    prev = lambda bi, i: (bi, jnp.maximum(i * per_tile - 1, 0), 0)
    per_b = lambda bi, i: (bi, 0, 0)
    kv_prev = pl.BlockSpec((None, WINDOW, KV_W), prev)
    kv_cur = pl.BlockSpec((None, ATTN_TILE, KV_W), cur)
    return pl.pallas_call(
        functools.partial(_attn_prompt_kernel, l), name="attn_prompt",
        grid=(b, s // ATTN_TILE),
        in_specs=[pl.BlockSpec(memory_space=pltpu.SMEM),
                  pl.BlockSpec((None, ATTN_TILE, BRANCH_W), cur),
                  kv_prev, kv_cur, kv_prev, kv_cur,
                  pl.BlockSpec((None, ATTN_TILE, BRANCH_W), cur),
                  pl.BlockSpec((None, N_MEM, BRANCH_W), per_b),
                  pl.BlockSpec((None, N_MEM, BRANCH_W), per_b)],
        out_specs=[pl.BlockSpec((None, ATTN_TILE, BRANCH_W), cur)] * 2,
        out_shape=[jax.ShapeDtypeStruct((b, s, BRANCH_W), BF16)] * 2,
        scratch_shapes=[pltpu.VMEM((WINDOW + ATTN_TILE, KV_W), BF16)] * 2
                       + [pltpu.VMEM((N_MEM, BRANCH_W), BF16)] * 2,
        compiler_params=_cparams(2),
    )(sinks, q, k, k, v, v, qm, mk, mv)


DEC_BLOCK = 8
TOK_PAD = 8


def _stack_heads(x, n_heads, head_w):
    t, w = x.shape
    x8 = jnp.concatenate([x, jnp.zeros((TOK_PAD - t, w), F32)], axis=0)
    rr = lax.broadcasted_iota(jnp.int32, (n_heads * TOK_PAD, w), 0)
    ll = lax.broadcasted_iota(jnp.int32, (n_heads * TOK_PAD, w), 1)
    own = (ll // head_w) == (rr // TOK_PAD)
    return jnp.where(own, jnp.concatenate([x8] * n_heads, axis=0), 0.0).astype(BF16), own


def _unstack_heads(o, own, n_heads, t):
    o = jnp.where(own, o, 0.0)
    acc = o[0:TOK_PAD]
    for hd in range(1, n_heads):
        acc = acc + o[hd * TOK_PAD:(hd + 1) * TOK_PAD]
    return acc[0:t]


def _attn_sample_kernel(l, t, sink_ref, q_ref, kn_ref, vn_ref, ck_ref, cv_ref, qm_ref, mk_ref, mv_ref, e_ref,
                        oa_ref, oc_ref, nk_ref, nv_ref, kc_scr, vc_scr):
    keys = 2 * WINDOW
    rows = N_HEADS * TOK_PAD
    rk = lax.broadcasted_iota(jnp.int32, (rows, keys), 0)
    jk = lax.broadcasted_iota(jnp.int32, (rows, keys), 1)
    tk = rk % TOK_PAD
    mask = (tk < t) & (((jk < WINDOW) & (jk > tk)) | ((jk >= WINDOW) & (jk - WINDOW <= tk)))
    sink = _sink_column(sink_ref, l, list(range(N_HEADS)), TOK_PAD)
    mem_rows = N_MEM * MEM_HEADS
    mem_own = (lax.broadcasted_iota(jnp.int32, (MEM_HEADS * TOK_PAD, mem_rows), 1) % MEM_HEADS
               == lax.broadcasted_iota(jnp.int32, (MEM_HEADS * TOK_PAD, mem_rows), 0) // TOK_PAD)
    e = e_ref[...]

    @pl.when(pl.program_id(0) == 0)
    def _():
        kc_scr[:, WINDOW:, :] = jnp.zeros((DEC_BLOCK, WINDOW, KV_W), F32)
        vc_scr[:, WINDOW:, :] = jnp.zeros((DEC_BLOCK, WINDOW, KV_W), F32)

    q_all, qm_all = q_ref[...].astype(F32), qm_ref[...].astype(F32)
    seqs = range(DEC_BLOCK)
    toks = [slice(bi * t, (bi + 1) * t) for bi in seqs]
    for bi in seqs:
        kc_scr[bi, :WINDOW, :] = ck_ref[bi]
        vc_scr[bi, :WINDOW, :] = cv_ref[bi]
        kc_scr[bi, WINDOW:WINDOW + t, :] = kn_ref[toks[bi], :]
        vc_scr[bi, WINDOW:WINDOW + t, :] = vn_ref[toks[bi], :]
    for bi in seqs:
        nk_ref[bi] = kc_scr[bi, t:WINDOW + t, :]
        nv_ref[bi] = vc_scr[bi, t:WINDOW + t, :]
    kx = [_dot(kc_scr[bi].astype(BF16), e).astype(BF16) for bi in seqs]
    vx = [_dot(vc_scr[bi].astype(BF16), e).astype(BF16) for bi in seqs]
    stacked = [_stack_heads(q_all[toks[bi]], N_HEADS, HEAD_DIM) for bi in seqs]
    scores = [jnp.where(mask, _dot_nt(stacked[bi][0], kx[bi]), -jnp.inf) for bi in seqs]
    mscores = []
    for bi in seqs:
        qm8 = jnp.concatenate([qm_all[toks[bi]], jnp.zeros((TOK_PAD - t, BRANCH_W), F32)], axis=0).astype(BF16)
        qst = jnp.concatenate([qm8[:, hd * MEM_HEAD_DIM:(hd + 1) * MEM_HEAD_DIM] for hd in range(MEM_HEADS)],
                              axis=0)
        mscores.append(jnp.where(mem_own, _dot_nt(qst, mk_ref[bi].astype(BF16)), -jnp.inf))
    probs = [_sink_softmax(scores[bi], sink) for bi in seqs]
    mprobs = [_softmax_rows(mscores[bi]) for bi in seqs]
    outs = [_dot(probs[bi][0].astype(BF16), vx[bi]) / probs[bi][1] for bi in seqs]
    mouts = [_dot(mprobs[bi][0].astype(BF16), mv_ref[bi].astype(BF16)) / mprobs[bi][1] for bi in seqs]
    for bi in seqs:
        oa_ref[toks[bi], :] = _unstack_heads(outs[bi], stacked[bi][1], N_HEADS, t)
        oc_ref[toks[bi], :] = jnp.concatenate(
            [mouts[bi][hd * TOK_PAD:(hd + 1) * TOK_PAD] for hd in range(MEM_HEADS)], axis=1)[0:t]


def _attn_sample(l, t, sinks, q, kn, vn, ck, cv, qm, mk, mv, e):
    n = q.shape[0]
    blk = lambda i: (i, 0, 0)
    lblk = lambda i: (l, i, 0, 0)
    tok = lambda w: pl.BlockSpec((DEC_BLOCK * t, w), lambda i: (i, 0))
    cache_in = pl.BlockSpec((None, DEC_BLOCK, WINDOW, KV_W), lblk)
    cache_out = pl.BlockSpec((DEC_BLOCK, WINDOW, KV_W), blk)
    mem = pl.BlockSpec((None, DEC_BLOCK, N_MEM * MEM_HEADS, MEM_HEAD_DIM), lblk)
    return pl.pallas_call(
        functools.partial(_attn_sample_kernel, l, t), name="attn_sample",
        grid=(n // (DEC_BLOCK * t),),
        in_specs=[pl.BlockSpec(memory_space=pltpu.SMEM), tok(BRANCH_W), tok(KV_W), tok(KV_W),
                  cache_in, cache_in, tok(BRANCH_W), mem, mem, _full((KV_W, BRANCH_W))],
        out_specs=[tok(BRANCH_W), tok(BRANCH_W), cache_out, cache_out],
        out_shape=[jax.ShapeDtypeStruct((n, BRANCH_W), F32)] * 2
                  + [jax.ShapeDtypeStruct((n // t, WINDOW, KV_W), F32)] * 2,
        scratch_shapes=[pltpu.VMEM((DEC_BLOCK, 2 * WINDOW, KV_W), F32)] * 2,
        compiler_params=_cparams(1),
    )(sinks, q, kn, vn, ck, cv, qm, mk, mv, e)


def _ssm_tables(a_re, a_im, log_dt, b_re, b_im, c_re, c_im, chunk):
    hp = lax.Precision.HIGHEST
    dt = jnp.exp(log_dt)[..., None]
    mag = jnp.exp(a_re * dt)
    lam_re, lam_im = mag * jnp.cos(a_im * dt), mag * jnp.sin(a_im * dt)
    den = a_re * a_re + a_im * a_im
    nr, ni = lam_re - 1.0, lam_im
    g_re = (nr * a_re + ni * a_im) / den
    g_im = (ni * a_re - nr * a_im) / den
    bg_re = g_re[..., None] * b_re - g_im[..., None] * b_im
    bg_im = g_re[..., None] * b_im + g_im[..., None] * b_re
    d = jnp.arange(chunk + 1, dtype=F32)[:, None, None, None]
    pm = jnp.exp(d * (a_re * dt)[None])
    pw_re = pm * jnp.cos(d * (a_im * dt)[None])
    pw_im = pm * jnp.sin(d * (a_im * dt)[None])
    z_re = pw_re[..., None] * bg_re[None] - pw_im[..., None] * bg_im[None]
    z_im = pw_re[..., None] * bg_im[None] + pw_im[..., None] * bg_re[None]
    kern = (jnp.einsum('lgop,dlgpi->dlgio', c_re, z_re[:chunk], precision=hp)
            - jnp.einsum('lgop,dlgpi->dlgio', c_im, z_im[:chunk], precision=hp))
    nl = a_re.shape[0]
    nb, gb = N_LANE_BLOCKS, GROUPS_PER_LANE_BLOCK
    kd = jnp.transpose(kern.reshape(chunk, nl, nb, gb, SSM_GROUP_CH, SSM_GROUP_CH), (1, 2, 0, 3, 4, 5))
    kd = kd.reshape(nl, nb, chunk, LANES, SSM_GROUP_CH)

    def to_rows(z):
        z = z[:chunk][::-1].reshape(chunk, nl, nb, gb, SSM_STATE, SSM_GROUP_CH)
        return jnp.transpose(z, (1, 2, 0, 3, 5, 4)).reshape(nl, nb, chunk, LANES, SSM_STATE)

    zr, zi = to_rows(z_re), to_rows(z_im)
    zc = jnp.concatenate([zr, zr, zi, zi], axis=-1)

    pr, pi = pw_re[1:], pw_im[1:]
    f_a = c_re[None] * pr[:, :, :, None, :] - c_im[None] * pi[:, :, :, None, :]
    f_b = c_re[None] * pi[:, :, :, None, :] + c_im[None] * pr[:, :, :, None, :]

    def from_cols(f):
        f = f.reshape(chunk, nl, nb, gb, SSM_GROUP_CH, SSM_STATE)
        return jnp.transpose(f, (1, 2, 0, 4, 3, 5)).reshape(nl, nb, chunk, SSM_GROUP_CH, STATE_W)

    ft = jnp.concatenate([from_cols(f_a), from_cols(-f_b)], axis=-1)

    kw = chunk * LANES
    spec = lambda *shape: pl.BlockSpec((None, None) + shape, lambda l, n: (l, n) + (0,) * len(shape))
    intra, to_st, from_st_t = pl.pallas_call(
        _ssm_table_kernel, name="ssm_tables",
        grid=(nl, nb),
        in_specs=[spec(chunk, LANES, SSM_GROUP_CH), spec(chunk, LANES, 2 * LANES),
                  spec(chunk, SSM_GROUP_CH, 2 * STATE_W)],
        out_specs=[spec(kw, kw), spec(kw, 2 * STATE_W), spec(kw, 2 * STATE_W)],
        out_shape=[jax.ShapeDtypeStruct((nl, nb, kw, kw), BF16),
                   jax.ShapeDtypeStruct((nl, nb, kw, 2 * STATE_W), BF16),
                   jax.ShapeDtypeStruct((nl, nb, kw, 2 * STATE_W), BF16)],
        compiler_params=_cparams(2),
    )(kd, zc, ft)
    decay = lambda d: (pw_re[d].reshape(nl, nb, 1, STATE_W), pw_im[d].reshape(nl, nb, 1, STATE_W))
    return (intra, to_st, from_st_t), decay


def _ssm_table_kernel(kd_ref, zc_ref, ft_ref, intra_ref, to_ref, fromt_ref):
    chunk = kd_ref.shape[0]
    gch = SSM_GROUP_CH
    r = lax.broadcasted_iota(jnp.int32, (LANES, LANES), 0)
    c = lax.broadcasted_iota(jnp.int32, (LANES, LANES), 1)
    same_group = (r // gch) == (c // gch)
    spread = (lax.broadcasted_iota(jnp.int32, (gch, LANES), 1) % gch
              == lax.broadcasted_iota(jnp.int32, (gch, LANES), 0)).astype(BF16)
    spread_t = (lax.broadcasted_iota(jnp.int32, (LANES, gch), 0) % gch
                == lax.broadcasted_iota(jnp.int32, (LANES, gch), 1)).astype(BF16)
    rs = lax.broadcasted_iota(jnp.int32, (LANES, 2 * STATE_W), 0)
    cs = lax.broadcasted_iota(jnp.int32, (LANES, 2 * STATE_W), 1)
    own_state = (rs // gch) == ((cs % STATE_W) // SSM_STATE)

    lag_blocks = [jnp.where(same_group, _dot(kd_ref[d].astype(BF16), spread), 0.0).astype(BF16)
                  for d in range(chunk)]
    zero = jnp.zeros((LANES, LANES), BF16)
    for s in range(chunk):
        for t in range(chunk):
            intra_ref[s * LANES:(s + 1) * LANES, t * LANES:(t + 1) * LANES] = (
                lag_blocks[t - s] if t >= s else zero)
        z = zc_ref[s]
        full = jnp.concatenate([z[:, :LANES]] * (STATE_W // LANES) + [z[:, LANES:]] * (STATE_W // LANES), axis=1)
        to_ref[s * LANES:(s + 1) * LANES, :] = jnp.where(own_state, full, 0.0).astype(BF16)
        f = _dot(spread_t, ft_ref[s].astype(BF16))
        fromt_ref[s * LANES:(s + 1) * LANES, :] = jnp.where(own_state, f, 0.0).astype(BF16)


SSM_SEQS = 2


def _chunk_rows(ref, chunk):
    n = ref.shape[0] // chunk
    return jnp.concatenate([ref[pl.ds(s, n, stride=chunk), :] for s in range(chunk)], axis=1)


def _store_chunk_rows(ref, y, chunk):
    n = ref.shape[0] // chunk
    for t in range(chunk):
        ref[pl.ds(t, n, stride=chunk), :] = y[:, t * LANES:(t + 1) * LANES]


def _ssm_prompt_kernel(u_ref, intra_ref, to_ref, fromt_ref, dre_ref, dim_ref,
                       y_ref, hre_ref, him_ref, st_scr):
    chunk = PROMPT_CHUNK
    nchunks = u_ref.shape[0] // (chunk * SSM_SEQS)
    x = _chunk_rows(u_ref, chunk).astype(BF16)
    y = _dot(x, intra_ref[...])
    add_all = _dot(x, to_ref[...])
    nblk = STATE_W // LANES
    assert SSM_SEQS * nblk == SUBLANES
    slots = [(seq, cb) for seq in range(SSM_SEQS) for cb in range(nblk)]

    def slot_rows(seq, cb):
        return pl.ds(seq * nblk + cb, nchunks, stride=SUBLANES)

    for c in range(2):
        for seq, cb in slots:
            col = (c * nblk + cb) * LANES
            st_scr[c, slot_rows(seq, cb), :] = add_all[seq * nchunks:(seq + 1) * nchunks, col:col + LANES]
    dre = jnp.concatenate([dre_ref[:, cb * LANES:(cb + 1) * LANES] for _, cb in slots], axis=0)
    dim = jnp.concatenate([dim_ref[:, cb * LANES:(cb + 1) * LANES] for _, cb in slots], axis=0)

    def step(k, carry):
        hr, hi = carry
        tile = pl.ds(pl.multiple_of(k * SUBLANES, SUBLANES), SUBLANES)
        add_r, add_i = st_scr[0, tile, :], st_scr[1, tile, :]
        st_scr[0, tile, :] = hr
        st_scr[1, tile, :] = hi
        return dre * hr - dim * hi + add_r, dre * hi + dim * hr + add_i

    zero = jnp.zeros((SUBLANES, LANES), F32)
    hr, hi = lax.fori_loop(0, nchunks, step, (zero, zero), unroll=8)
    hre_ref[...] = hr
    him_ref[...] = hi
    h_in = jnp.concatenate(
        [jnp.concatenate([st_scr[c, slot_rows(seq, cb), :] for seq in range(SSM_SEQS)], axis=0)
         for c in range(2) for cb in range(nblk)], axis=1).astype(BF16)
    _store_chunk_rows(y_ref, y + _dot_nt(h_in, fromt_ref[...]), chunk)


def _ssm_prompt(l, u2d, seq_len, intra, to_st, from_st_t, dec_re, dec_im):
    n = u2d.shape[0]
    chunk = PROMPT_CHUNK
    rows = SSM_SEQS * seq_len
    steps = n // rows
    kw = chunk * LANES
    data = pl.BlockSpec((rows, LANES), lambda j, h: (h, j))
    wspec = lambda r, c: pl.BlockSpec((None, None, r, c), lambda j, h: (l, j, 0, 0))
    st_out = pl.BlockSpec((None, None, SUBLANES, LANES), lambda j, h: (j, h, 0, 0))
    st_shape = jax.ShapeDtypeStruct((N_LANE_BLOCKS, steps, SUBLANES, LANES), F32)
    return pl.pallas_call(
        _ssm_prompt_kernel, name="ssm_prompt",
        grid=(N_LANE_BLOCKS, steps),
        in_specs=[data, wspec(kw, kw), wspec(kw, 2 * STATE_W), wspec(kw, 2 * STATE_W),
                  wspec(1, STATE_W), wspec(1, STATE_W)],
        out_specs=[data, st_out, st_out],
        out_shape=[jax.ShapeDtypeStruct(u2d.shape, F32), st_shape, st_shape],
        scratch_shapes=[pltpu.VMEM((2, SUBLANES * seq_len // chunk, LANES), F32)],
        compiler_params=_cparams(2),
    )(u2d, intra, to_st, from_st_t, dec_re, dec_im)


def _ssm_sample_kernel(chunk, u_ref, h0r_ref, h0i_ref, intra_ref, to_ref, fromt_ref, dre_ref, dim_ref,
                       y_ref, hre_ref, him_ref):
    x = _chunk_rows(u_ref, chunk).astype(BF16)
    hr, hi = h0r_ref[...], h0i_ref[...]
    dre, dim = dre_ref[...], dim_ref[...]
    add = _dot(x, to_ref[...])
    hre_ref[...] = dre * hr - dim * hi + add[:, :STATE_W]
    him_ref[...] = dre * hi + dim * hr + add[:, STATE_W:]
    h0 = jnp.concatenate([hr, hi], axis=1).astype(BF16)
    _store_chunk_rows(y_ref, _dot(x, intra_ref[...]) + _dot_nt(h0, fromt_ref[...]), chunk)


def _ssm_sample(l, u2d, t, h0_re, h0_im, intra, to_st, from_st_t, dec_re, dec_im):
    n = u2d.shape[0]
    b = n // t
    kw = t * LANES
    assert PROMPT_CHUNK % t == 0
    last = PROMPT_CHUNK // t - 1
    data = pl.BlockSpec((n, LANES), lambda j: (0, j))
    st_in = pl.BlockSpec((None, b, STATE_W), lambda j: (l, 0, j))
    st_out = pl.BlockSpec((b, STATE_W), lambda j: (0, j))
    wspec = lambda r, c, rb=0: pl.BlockSpec((None, None, r, c), lambda j: (l, j, rb, 0))
    return pl.pallas_call(
        functools.partial(_ssm_sample_kernel, t), name="ssm_sample",
        grid=(N_LANE_BLOCKS,),
        in_specs=[data, st_in, st_in, wspec(kw, kw), wspec(kw, 2 * STATE_W, last), wspec(kw, 2 * STATE_W),
                  wspec(1, STATE_W), wspec(1, STATE_W)],
        out_specs=[data, st_out, st_out],
        out_shape=[jax.ShapeDtypeStruct(u2d.shape, F32),
                   jax.ShapeDtypeStruct((b, N_LANE_BLOCKS * STATE_W), F32),
                   jax.ShapeDtypeStruct((b, N_LANE_BLOCKS * STATE_W), F32)],
        compiler_params=_cparams(1),
    )(u2d, h0_re, h0_im, intra, to_st, from_st_t, dec_re, dec_im)


FF_CHUNKS = ((0, 512), (512, 512), (1024, 512), (1536, 512), (2048, 512), (2560, 256))


def _merge_ffn_kernel(x_ref, oa_ref, y_ref, u_ref, oc_ref, gates_ref, d_ref, wglu_ref, wb_ref, wout_ref,
                      g_ref, wup_ref, wdn_ref, o_ref):
    z = jax.nn.gelu(y_ref[...] + d_ref[...] * u_ref[...])
    ob = (z * jax.nn.sigmoid(_dot(z.astype(BF16), wglu_ref[...]))).astype(BF16)
    gate = lambda n: gates_ref[:, n * D_MODEL:(n + 1) * D_MODEL].astype(F32)
    merged = gate(0) * _dot(oa_ref[...], wb_ref[0])
    merged = merged + gate(1) * _dot(ob, wb_ref[1])
    merged = merged + gate(2) * _dot(oc_ref[...], wb_ref[2])
    x = x_ref[...] + _dot(merged.astype(BF16), wout_ref[...])
    h = (_rms(x) * g_ref[...]).astype(BF16)
    acc = x
    for lo, w in FF_CHUNKS:
        up_gate = _dot(h, wup_ref[:, lo:lo + w])
        up = _dot(h, wup_ref[:, D_FF + lo:D_FF + lo + w])
        acc = acc + _dot((jax.nn.silu(up_gate) * up).astype(BF16), wdn_ref[lo:lo + w, :])
    o_ref[...] = acc


def _merge_ffn(l, x2d, oa, y, u, oc, gates, d, wglu, wb, wout, g_ffn, wup, wdn):
    n = x2d.shape[0]
    row = lambda w: pl.BlockSpec((ROW_TILE, w), lambda i: (i, 0))
    return pl.pallas_call(
        _merge_ffn_kernel, name="merge_ffn",
        grid=(n // ROW_TILE,),
        in_specs=[row(D_MODEL), row(BRANCH_W), row(BRANCH_W), row(BRANCH_W), row(BRANCH_W),
                  row(3 * D_MODEL), _layer((1, BRANCH_W), l), _layer((BRANCH_W, BRANCH_W), l),
                  _layer((3, BRANCH_W, D_MODEL), l), _layer((D_MODEL, D_MODEL), l),
                  _layer((1, D_MODEL), l), _layer((D_MODEL, 2 * D_FF), l), _layer((D_FF, D_MODEL), l)],
        out_specs=row(D_MODEL),
        out_shape=jax.ShapeDtypeStruct((n, D_MODEL), F32),
        compiler_params=_cparams(1),
    )(x2d, oa, y, u, oc, gates, d, wglu, wb, wout, g_ffn, wup, wdn)


def _rope_tables(pos):
    half = HEAD_DIM // 2
    inv = ROPE_THETA ** (-jnp.arange(half, dtype=F32) / half)
    ang = pos[:, None] * inv[None, :]
    cos, sin = jnp.cos(ang), jnp.sin(ang)
    zero = jnp.zeros_like(sin)
    two = lambda a: jnp.concatenate([a, a], axis=1)
    return (two(jnp.concatenate([cos, cos], axis=1)), two(jnp.concatenate([-sin, zero], axis=1)),
            two(jnp.concatenate([zero, sin], axis=1)))


def _state_from_blocks(h):
    nb, steps, _, _ = h.shape
    h = h.reshape(nb, steps, SSM_SEQS, STATE_W // LANES, LANES)
    return jnp.transpose(h, (1, 2, 0, 3, 4)).reshape(steps * SSM_SEQS, SSM_GROUPS, SSM_STATE)


def kernel(x_prompt, x_sample, cache_swa_k, cache_swa_v, state_ssm_re, state_ssm_im, cache_mem_k,
           cache_mem_v, mem_prompt, attn_norm, w_in, q_norm, k_norm, attn_sinks, ssm_a_re, ssm_a_im,
           ssm_log_dt, ssm_b_re, ssm_b_im, ssm_c_re, ssm_c_im, ssm_d, ssm_w_glu, mem_norm, w_mem_kv,
           mem_q_norm, mem_k_norm, w_branch, w_out, ffn_norm, w_ffn_up, w_ffn_down):
    depth = w_in.shape[0]
    b, s, _ = x_prompt.shape
    db, t, _ = x_sample.shape
    assert db * t == ROW_TILE and s % ROW_TILE == 0

    rope_p = _rope_tables(jnp.arange(s, dtype=F32))
    rope_s = _rope_tables(jnp.tile(PAST_LEN + jnp.arange(t, dtype=F32), db))
    ssm_params = (ssm_a_re, ssm_a_im, ssm_log_dt, ssm_b_re, ssm_b_im, ssm_c_re, ssm_c_im)
    tables, decay = _ssm_tables(*ssm_params, PROMPT_CHUNK)
    tab_p, tab_s = tables + decay(PROMPT_CHUNK), tables + decay(t)
    blk = jnp.arange(256) // HEAD_DIM
    ones = (blk[:, None] == blk[None, :]).astype(BF16)
    lane = jnp.arange(BRANCH_W)[None, :]
    src = jnp.arange(KV_W)[:, None]
    expand = ((lane // (HEAD_DIM * KV_GROUP) == src // HEAD_DIM)
              & (lane % HEAD_DIM == src % HEAD_DIM)).astype(BF16)

    w_in_bf, w_mem_bf = w_in.astype(BF16), w_mem_kv.astype(BF16)
    wglu, wb, wout = ssm_w_glu.astype(BF16), w_branch.astype(BF16), w_out.astype(BF16)
    wup, wdn = w_ffn_up.astype(BF16), w_ffn_down.astype(BF16)
    row = lambda a: a[:, None, :]
    g_attn, g_mem, g_ffn, d_row = row(attn_norm), row(mem_norm), row(ffn_norm), row(ssm_d)
    qg, kg = row(jnp.tile(q_norm, (1, N_HEADS))), row(jnp.tile(k_norm, (1, N_KV_HEADS)))
    mqg, mkg = row(mem_q_norm), row(mem_k_norm)
    ck = cache_swa_k.reshape(depth, db, WINDOW, KV_W)
    cv = cache_swa_v.reshape(depth, db, WINDOW, KV_W)
    cmk = cache_mem_k.reshape(depth, db, N_MEM * MEM_HEADS, MEM_HEAD_DIM)
    cmv = cache_mem_v.reshape(depth, db, N_MEM * MEM_HEADS, MEM_HEAD_DIM)
    h0r = state_ssm_re.reshape(depth, db, SSM_GROUPS * SSM_STATE)
    h0i = state_ssm_im.reshape(depth, db, SSM_GROUPS * SSM_STATE)

    yp = x_prompt.reshape(b * s, D_MODEL)
    ys = x_sample.reshape(db * t, D_MODEL)
    mem2d = mem_prompt.reshape(b * N_MEM, D_MODEL)
    outs = [[] for _ in range(10)]
    for l in range(depth):
        mk, mv = _memkv(l, mem2d, g_mem, w_mem_bf, mkg)
        mk3, mv3 = mk.reshape(b, N_MEM, BRANCH_W), mv.reshape(b, N_MEM, BRANCH_W)

        q, k, v, u, mq, gates = _inproj(l, yp, g_attn, w_in_bf, rope_p, s // ROW_TILE, qg, kg, mqg, ones)
        k3, v3 = k.reshape(b, s, KV_W), v.reshape(b, s, KV_W)
        oa, oc = _attn_prompt(l, attn_sinks, q.reshape(b, s, BRANCH_W), k3, v3,
                              mq.reshape(b, s, BRANCH_W), mk3, mv3)
        ysm, hre, him = _ssm_prompt(l, u, s, *tab_p)
        yp = _merge_ffn(l, yp, oa.reshape(b * s, BRANCH_W), ysm, u, oc.reshape(b * s, BRANCH_W), gates,
                        d_row, wglu, wb, wout, g_ffn, wup, wdn)

        qs, ks, vs, us, mqs, gates_s = _inproj(l, ys, g_attn, w_in_bf, rope_s, 1, qg, kg, mqg, ones)
        oas, ocs, nk, nv = _attn_sample(l, t, attn_sinks, qs, ks, vs, ck, cv, mqs, cmk, cmv, expand)
        yss, hrs, his = _ssm_sample(l, us, t, h0r, h0i, *tab_s)
        ys = _merge_ffn(l, ys, oas.astype(BF16), yss, us, ocs.astype(BF16), gates_s, d_row, wglu, wb, wout,
                        g_ffn, wup, wdn)

        new = (k3[:, -WINDOW:].reshape(b, WINDOW, N_KV_HEADS, HEAD_DIM),
               v3[:, -WINDOW:].reshape(b, WINDOW, N_KV_HEADS, HEAD_DIM),
               _state_from_blocks(hre), _state_from_blocks(him),
               mk3.reshape(b, N_MEM, MEM_HEADS, MEM_HEAD_DIM), mv3.reshape(b, N_MEM, MEM_HEADS, MEM_HEAD_DIM),
               nk.reshape(db, WINDOW, N_KV_HEADS, HEAD_DIM), nv.reshape(db, WINDOW, N_KV_HEADS, HEAD_DIM),
               hrs.reshape(db, SSM_GROUPS, SSM_STATE), his.reshape(db, SSM_GROUPS, SSM_STATE))
        for lst, a in zip(outs, new):
            lst.append(a)

    return (yp.reshape(b, s, D_MODEL), ys.reshape(db, t, D_MODEL)) + tuple(jnp.stack(o) for o in outs)
```

```python
import functools
import math

import jax
import jax.numpy as jnp
from jax import lax
from jax.experimental import pallas as pl
from jax.experimental.pallas import tpu as pltpu

F32 = jnp.float32
BF16 = jnp.bfloat16

D_MODEL = 1024
BRANCH_W = 512
HEAD_DIM = 64
N_HEADS = 8
N_KV_HEADS = 2
KV_GROUP = 4
KV_W = N_KV_HEADS * HEAD_DIM
WINDOW = 128
ROPE_THETA = 10000.0
PAST_LEN = 16384
SSM_GROUPS = 32
SSM_GROUP_CH = 16
SSM_STATE = 64
N_MEM = 256
MEM_HEADS = 4
MEM_HEAD_DIM = 128
D_FF = 2816
RMS_EPS = 1e-6

Q_OFF, K_OFF, V_OFF, U_OFF, MQ_OFF, G_OFF = 0, 512, 640, 768, 1280, 1792
IN_W = G_OFF + 3 * D_MODEL

LANES = 128
SUBLANES = 8
GROUPS_PER_LANE_BLOCK = LANES // SSM_GROUP_CH
N_LANE_BLOCKS = BRANCH_W // LANES
STATE_W = GROUPS_PER_LANE_BLOCK * SSM_STATE
PROMPT_CHUNK = 8
ROW_TILE = 512
VMEM_LIMIT = 56 * 1024 * 1024


def _dot(a, b):
    return jnp.dot(a, b, preferred_element_type=F32)


def _dot_nt(a, b):
    return lax.dot_general(a, b, (((1,), (1,)), ((), ())), preferred_element_type=F32)


def _cparams(n_axes):
    return pltpu.CompilerParams(dimension_semantics=("arbitrary",) * n_axes,
                                vmem_limit_bytes=VMEM_LIMIT)


def _full(shape):
    nd = len(shape)
    return pl.BlockSpec(shape, lambda *_: (0,) * nd)


def _layer(shape, l):
    nd = len(shape)
    return pl.BlockSpec((None,) + tuple(shape), lambda *_: (l,) + (0,) * nd, pipeline_mode=pl.Buffered(1))


def _rms(x):
    return x * lax.rsqrt(jnp.mean(x * x, axis=-1, keepdims=True) + RMS_EPS)


def _memkv_kernel(mem_ref, g_ref, w_ref, kg_ref, k_ref, v_ref):
    h = _rms(mem_ref[...]) * g_ref[...]
    kv = _dot(h.astype(BF16), w_ref[...])
    kg = kg_ref[...]
    for hd in range(MEM_HEADS):
        sl = slice(hd * MEM_HEAD_DIM, (hd + 1) * MEM_HEAD_DIM)
        k_ref[:, sl] = _rms(kv[:, sl]) * kg
    v_ref[...] = kv[:, BRANCH_W:]


def _memkv(l, mem2d, g, w_bf, kg):
    n = mem2d.shape[0]
    return pl.pallas_call(
        _memkv_kernel, name="memkv",
        grid=(n // ROW_TILE,),
        in_specs=[pl.BlockSpec((ROW_TILE, D_MODEL), lambda i: (i, 0)),
                  _layer((1, D_MODEL), l), _layer((D_MODEL, 2 * BRANCH_W), l), _layer((1, MEM_HEAD_DIM), l)],
        out_specs=[pl.BlockSpec((ROW_TILE, BRANCH_W), lambda i: (i, 0))] * 2,
        out_shape=[jax.ShapeDtypeStruct((n, BRANCH_W), F32)] * 2,
        compiler_params=_cparams(1),
    )(mem2d, g, w_bf, kg)


def _rotary(x, cos, sin_lo, sin_hi):
    w = x.shape[-1]
    half = HEAD_DIM // 2
    return x * cos + pltpu.roll(x, w - half, 1) * sin_lo + pltpu.roll(x, half, 1) * sin_hi


GATE_CHUNK = 512
N_GATE_CHUNKS = 3 * D_MODEL // GATE_CHUNK


class _Projection:
    def __init__(self, x_ref, g_ref, w_ref, cos_ref, slo_ref, shi_ref, qg_ref, kg_ref, mqg_ref, ones_ref):
        self.hb = (_rms(x_ref[...]) * g_ref[...]).astype(BF16)
        self.w_ref, self.qg_ref, self.kg_ref, self.mqg_ref = w_ref, qg_ref, kg_ref, mqg_ref
        self.ones = ones_ref[...]
        self.rope = (cos_ref[...], slo_ref[...], shi_ref[...])

    def queries(self):
        zq = _dot(self.hb, self.w_ref[:, Q_OFF:K_OFF])
        sq = (zq * zq).astype(BF16)
        ssq = jnp.concatenate([_dot(sq[:, :256], self.ones), _dot(sq[:, 256:], self.ones)], axis=1)
        qn = zq * lax.rsqrt(ssq * (1.0 / HEAD_DIM) + RMS_EPS) * self.qg_ref[...]
        rope4 = (jnp.concatenate([t] * 4, axis=1) for t in self.rope)
        return (_rotary(qn, *rope4) * (1.0 / math.sqrt(HEAD_DIM))).astype(BF16)

    def keys_values(self):
        zkv = _dot(self.hb, self.w_ref[:, K_OFF:U_OFF])
        zk = zkv[:, :KV_W]
        ssk = _dot((zk * zk).astype(BF16), self.ones[:KV_W, :KV_W])
        kn = zk * lax.rsqrt(ssk * (1.0 / HEAD_DIM) + RMS_EPS) * self.kg_ref[...]
        return _rotary(kn, *self.rope), zkv[:, KV_W:]

    def ssm_input(self):
        return _dot(self.hb, self.w_ref[:, U_OFF:MQ_OFF])

    def memory_queries(self):
        zm = _dot(self.hb, self.w_ref[:, MQ_OFF:G_OFF])
        mqg = self.mqg_ref[...] * (1.0 / math.sqrt(MEM_HEAD_DIM))
        return jnp.concatenate([_rms(zm[:, hd * MEM_HEAD_DIM:(hd + 1) * MEM_HEAD_DIM]) * mqg
                                for hd in range(MEM_HEADS)], axis=1).astype(BF16)

    def gates(self, c):
        lo = G_OFF + c * GATE_CHUNK
        return jax.nn.sigmoid(_dot(self.hb, self.w_ref[:, lo:lo + GATE_CHUNK])).astype(BF16)


def _inproj_kernel(x_ref, g_ref, w_ref, cos_ref, slo_ref, shi_ref, qg_ref, kg_ref, mqg_ref, ones_ref,
                   q_ref, k_ref, v_ref, u_ref, mq_ref, gates_ref):
    proj = _Projection(x_ref, g_ref, w_ref, cos_ref, slo_ref, shi_ref, qg_ref, kg_ref, mqg_ref, ones_ref)
    q_ref[...] = proj.queries()
    k_ref[...], v_ref[...] = proj.keys_values()
    u_ref[...] = proj.ssm_input()
    mq_ref[...] = proj.memory_queries()
    for c in range(N_GATE_CHUNKS):
        gates_ref[:, c * GATE_CHUNK:(c + 1) * GATE_CHUNK] = proj.gates(c)


def _inproj(l, x2d, g, w_bf, rope, rope_blocks, qg, kg, mqg, ones):
    n = x2d.shape[0]
    row = lambda i: (i, 0)
    rspec = pl.BlockSpec((ROW_TILE, LANES), lambda i: (i % rope_blocks, 0))
    widths = (BRANCH_W, KV_W, KV_W, BRANCH_W, BRANCH_W, 3 * D_MODEL)
    dtypes = (BF16, F32, F32, F32, BF16, BF16)
    return pl.pallas_call(
        _inproj_kernel, name="inproj",
        grid=(n // ROW_TILE,),
        in_specs=[pl.BlockSpec((ROW_TILE, D_MODEL), row), _layer((1, D_MODEL), l), _layer((D_MODEL, IN_W), l),
                  rspec, rspec, rspec, _layer((1, BRANCH_W), l), _layer((1, KV_W), l),
                  _layer((1, MEM_HEAD_DIM), l), _full((256, 256))],
        out_specs=[pl.BlockSpec((ROW_TILE, w), row) for w in widths],
        out_shape=[jax.ShapeDtypeStruct((n, w), d) for w, d in zip(widths, dtypes)],
        compiler_params=_cparams(1),
    )(x2d, g, w_bf, *rope, qg, kg, mqg, ones)


ATTN_TILE = 512


def _softmax_rows(s):
    m = jnp.max(s, axis=-1, keepdims=True)
    p = jnp.exp(s - m)
    return p, jnp.sum(p, axis=-1, keepdims=True)


def _sink_column(sink_ref, l, heads, rows_per_head):
    n = len(heads) * rows_per_head
    rcol = lax.broadcasted_iota(jnp.int32, (n, 1), 0)
    sink = jnp.full((n, 1), sink_ref[l, heads[-1]], F32)
    for g in range(len(heads) - 2, -1, -1):
        sink = jnp.where(rcol < (g + 1) * rows_per_head, sink_ref[l, heads[g]], sink)
    return sink


def _sink_softmax(s, sink):
    m = jnp.maximum(jnp.max(s, axis=-1, keepdims=True), sink)
    p = jnp.exp(s - m)
    return p, jnp.sum(p, axis=-1, keepdims=True) + jnp.exp(sink - m)


def _attend_sub_block(sub, mask, sinks, q_ref, mq_ref, k_ref, v_ref, mk_ref, mv_ref, oa_ref, oc_ref):
    qrows = slice(sub * WINDOW, (sub + 1) * WINDOW)
    krows = slice(sub * WINDOW, (sub + 2) * WINDOW)
    ksl = [slice(kvh * HEAD_DIM, (kvh + 1) * HEAD_DIM) for kvh in range(N_KV_HEADS)]
    msl = [slice(hd * MEM_HEAD_DIM, (hd + 1) * MEM_HEAD_DIM) for hd in range(MEM_HEADS)]
    heads = [[kvh * KV_GROUP + g for g in range(KV_GROUP)] for kvh in range(N_KV_HEADS)]
    scores = []
    for kvh in range(N_KV_HEADS):
        qg = jnp.concatenate([q_ref[qrows, hd * HEAD_DIM:(hd + 1) * HEAD_DIM] for hd in heads[kvh]], axis=0)
        scores.append(jnp.where(mask, _dot_nt(qg, k_ref[krows, ksl[kvh]]), -jnp.inf))
    mscores = [_dot_nt(mq_ref[qrows, sl], mk_ref[:, sl]) for sl in msl]
    yield
    probs = [_sink_softmax(scores[kvh], sinks[kvh]) for kvh in range(N_KV_HEADS)]
    mprobs = [_softmax_rows(s) for s in mscores]
    yield
    outs = [_dot(probs[kvh][0].astype(BF16), v_ref[krows, ksl[kvh]]) / probs[kvh][1]
            for kvh in range(N_KV_HEADS)]
    mouts = [_dot(mprobs[hd][0].astype(BF16), mv_ref[:, msl[hd]]) / mprobs[hd][1] for hd in range(MEM_HEADS)]
    for kvh in range(N_KV_HEADS):
        for g, hd in enumerate(heads[kvh]):
            oa_ref[qrows, hd * HEAD_DIM:(hd + 1) * HEAD_DIM] = outs[kvh][g * WINDOW:(g + 1) * WINDOW].astype(BF16)
    for hd in range(MEM_HEADS):
        oc_ref[qrows, msl[hd]] = mouts[hd].astype(BF16)


def _inproj_attn_kernel(l, tiles_per_seq, sink_ref, x_ref, g_ref, w_ref, cos_ref, slo_ref, shi_ref,
                        qg_ref, kg_ref, mqg_ref, ones_ref, mk_ref, mv_ref,
                        k_ref, v_ref, u_ref, gates_ref, oa_ref, oc_ref,
                        q_scr, mq_scr, k_scr, v_scr, mk_scr, mv_scr):
    i = pl.program_id(0)
    cur, prv = i % 2, (i + 1) % 2

    @pl.when(i == 0)
    def _():
        for scr in (q_scr, mq_scr, k_scr, v_scr, mk_scr, mv_scr):
            scr[...] = jnp.zeros(scr.shape, scr.dtype)

    @pl.when(i % tiles_per_seq == 1)
    def _():
        mk_scr[...] = mk_ref[...].astype(BF16)
        mv_scr[...] = mv_ref[...].astype(BF16)

    rows = KV_GROUP * WINDOW
    r = lax.broadcasted_iota(jnp.int32, (rows, 2 * WINDOW), 0)
    j = lax.broadcasted_iota(jnp.int32, (rows, 2 * WINDOW), 1)
    qi = r % WINDOW
    band = (j > qi) & (j <= qi + WINDOW)
    first_band = band & ((j >= WINDOW) | (i % tiles_per_seq != 1))
    sinks = [_sink_column(sink_ref, l, [kvh * KV_GROUP + g for g in range(KV_GROUP)], WINDOW)
             for kvh in range(N_KV_HEADS)]

    def attention_stages():
        for sub in range(ATTN_TILE // WINDOW):
            yield from _attend_sub_block(sub, first_band if sub == 0 else band, sinks, q_scr.at[prv],
                                         mq_scr.at[prv], k_scr.at[prv], v_scr.at[prv], mk_scr, mv_scr,
                                         oa_ref, oc_ref)
            yield

    proj = _Projection(x_ref, g_ref, w_ref, cos_ref, slo_ref, shi_ref, qg_ref, kg_ref, mqg_ref, ones_ref)

    def store_queries():
        q_scr[cur] = proj.queries()

    def store_keys_values():
        k, v = proj.keys_values()
        k_ref[...], v_ref[...] = k, v
        k_scr[cur, :WINDOW, :] = k_scr[prv, ATTN_TILE:, :]
        v_scr[cur, :WINDOW, :] = v_scr[prv, ATTN_TILE:, :]
        k_scr[cur, WINDOW:, :] = k.astype(BF16)
        v_scr[cur, WINDOW:, :] = v.astype(BF16)

    def store_ssm_input():
        u_ref[...] = proj.ssm_input()

    def store_memory_queries():
        mq_scr[cur] = proj.memory_queries()

    def store_gates(c):
        gates_ref[:, c * GATE_CHUNK:(c + 1) * GATE_CHUNK] = proj.gates(c)

    segments = [store_queries, store_keys_values, store_ssm_input, store_memory_queries]
    segments += [functools.partial(store_gates, c) for c in range(N_GATE_CHUNKS)]
    stages = attention_stages()
    n_stages = 3 * (ATTN_TILE // WINDOW)
    for n, segment in enumerate(segments):
        for _ in range((n + 1) * n_stages // len(segments) - n * n_stages // len(segments)):
            next(stages)
        segment()


def _inproj_attn(l, sinks, x2d, seq_len, g, w_bf, rope, qg, kg, mqg, ones, mk, mv):
    n = x2d.shape[0]
    n_tiles = n // ROW_TILE
    tiles_per_seq = seq_len // ROW_TILE
    assert ROW_TILE == ATTN_TILE and tiles_per_seq > 1
    proj_tile = lambda i: jnp.minimum(i, n_tiles - 1)
    attn_tile = lambda i: jnp.maximum(i - 1, 0)
    prow = lambda w: pl.BlockSpec((ROW_TILE, w), lambda i: (proj_tile(i), 0))
    arow = pl.BlockSpec((ROW_TILE, BRANCH_W), lambda i: (attn_tile(i), 0))
    rspec = pl.BlockSpec((ROW_TILE, LANES), lambda i: (proj_tile(i) % tiles_per_seq, 0))
    mem = pl.BlockSpec((None, N_MEM, BRANCH_W), lambda i: (attn_tile(i) // tiles_per_seq, 0, 0))
    widths = (KV_W, KV_W, BRANCH_W, 3 * D_MODEL)
    dtypes = (F32, F32, F32, BF16)
    return pl.pallas_call(
        functools.partial(_inproj_attn_kernel, l, tiles_per_seq), name="inproj_attn",
        grid=(n_tiles + 1,),
        in_specs=[pl.BlockSpec(memory_space=pltpu.SMEM), prow(D_MODEL), _layer((1, D_MODEL), l),
                  _layer((D_MODEL, IN_W), l), rspec, rspec, rspec, _layer((1, BRANCH_W), l),
                  _layer((1, KV_W), l), _layer((1, MEM_HEAD_DIM), l), _full((256, 256)), mem, mem],
        out_specs=[prow(w) for w in widths] + [arow, arow],
        out_shape=[jax.ShapeDtypeStruct((n, w), d) for w, d in zip(widths, dtypes)]
                  + [jax.ShapeDtypeStruct((n, BRANCH_W), BF16)] * 2,
        scratch_shapes=[pltpu.VMEM((2, ROW_TILE, BRANCH_W), BF16)] * 2
                       + [pltpu.VMEM((2, WINDOW + ROW_TILE, KV_W), BF16)] * 2
                       + [pltpu.VMEM((N_MEM, BRANCH_W), BF16)] * 2,
        compiler_params=_cparams(1),
    )(sinks, x2d, g, w_bf, *rope, qg, kg, mqg, ones, mk, mv)


DEC_BLOCK = 8
TOK_PAD = 8


def _stack_heads(x, n_heads, head_w):
    t, w = x.shape
    x8 = jnp.concatenate([x, jnp.zeros((TOK_PAD - t, w), F32)], axis=0)
    rr = lax.broadcasted_iota(jnp.int32, (n_heads * TOK_PAD, w), 0)
    ll = lax.broadcasted_iota(jnp.int32, (n_heads * TOK_PAD, w), 1)
    own = (ll // head_w) == (rr // TOK_PAD)
    return jnp.where(own, jnp.concatenate([x8] * n_heads, axis=0), 0.0).astype(BF16), own


def _unstack_heads(o, own, n_heads, t):
    o = jnp.where(own, o, 0.0)
    acc = o[0:TOK_PAD]
    for hd in range(1, n_heads):
        acc = acc + o[hd * TOK_PAD:(hd + 1) * TOK_PAD]
    return acc[0:t]


def _attn_sample_kernel(l, t, sink_ref, q_ref, kn_ref, vn_ref, ck_ref, cv_ref, qm_ref, mk_ref, mv_ref, e_ref,
                        oa_ref, oc_ref, nk_ref, nv_ref, kc_scr, vc_scr):
    keys = 2 * WINDOW
    rows = N_HEADS * TOK_PAD
    rk = lax.broadcasted_iota(jnp.int32, (rows, keys), 0)
    jk = lax.broadcasted_iota(jnp.int32, (rows, keys), 1)
    tk = rk % TOK_PAD
    mask = (tk < t) & (((jk < WINDOW) & (jk > tk)) | ((jk >= WINDOW) & (jk - WINDOW <= tk)))
    sink = _sink_column(sink_ref, l, list(range(N_HEADS)), TOK_PAD)
    mem_rows = N_MEM * MEM_HEADS
    mem_own = (lax.broadcasted_iota(jnp.int32, (MEM_HEADS * TOK_PAD, mem_rows), 1) % MEM_HEADS
               == lax.broadcasted_iota(jnp.int32, (MEM_HEADS * TOK_PAD, mem_rows), 0) // TOK_PAD)
    e = e_ref[...]

    @pl.when(pl.program_id(0) == 0)
    def _():
        kc_scr[:, WINDOW:, :] = jnp.zeros((DEC_BLOCK, WINDOW, KV_W), F32)
        vc_scr[:, WINDOW:, :] = jnp.zeros((DEC_BLOCK, WINDOW, KV_W), F32)

    q_all, qm_all = q_ref[...].astype(F32), qm_ref[...].astype(F32)
    seqs = range(DEC_BLOCK)
    toks = [slice(bi * t, (bi + 1) * t) for bi in seqs]
    for bi in seqs:
        kc_scr[bi, :WINDOW, :] = ck_ref[bi]
        vc_scr[bi, :WINDOW, :] = cv_ref[bi]
        kc_scr[bi, WINDOW:WINDOW + t, :] = kn_ref[toks[bi], :]
        vc_scr[bi, WINDOW:WINDOW + t, :] = vn_ref[toks[bi], :]
    for bi in seqs:
        nk_ref[bi] = kc_scr[bi, t:WINDOW + t, :]
        nv_ref[bi] = vc_scr[bi, t:WINDOW + t, :]
    kx = [_dot(kc_scr[bi].astype(BF16), e).astype(BF16) for bi in seqs]
    vx = [_dot(vc_scr[bi].astype(BF16), e).astype(BF16) for bi in seqs]
    stacked = [_stack_heads(q_all[toks[bi]], N_HEADS, HEAD_DIM) for bi in seqs]
    scores = [jnp.where(mask, _dot_nt(stacked[bi][0], kx[bi]), -jnp.inf) for bi in seqs]
    mscores = []
    for bi in seqs:
        qm8 = jnp.concatenate([qm_all[toks[bi]], jnp.zeros((TOK_PAD - t, BRANCH_W), F32)], axis=0).astype(BF16)
        qst = jnp.concatenate([qm8[:, hd * MEM_HEAD_DIM:(hd + 1) * MEM_HEAD_DIM] for hd in range(MEM_HEADS)],
                              axis=0)
        mscores.append(jnp.where(mem_own, _dot_nt(qst, mk_ref[bi].astype(BF16)), -jnp.inf))
    probs = [_sink_softmax(scores[bi], sink) for bi in seqs]
    mprobs = [_softmax_rows(mscores[bi]) for bi in seqs]
    outs = [_dot(probs[bi][0].astype(BF16), vx[bi]) / probs[bi][1] for bi in seqs]
    mouts = [_dot(mprobs[bi][0].astype(BF16), mv_ref[bi].astype(BF16)) / mprobs[bi][1] for bi in seqs]
    for bi in seqs:
        oa_ref[toks[bi], :] = _unstack_heads(outs[bi], stacked[bi][1], N_HEADS, t)
        oc_ref[toks[bi], :] = jnp.concatenate(
            [mouts[bi][hd * TOK_PAD:(hd + 1) * TOK_PAD] for hd in range(MEM_HEADS)], axis=1)[0:t]


def _attn_sample(l, t, sinks, q, kn, vn, ck, cv, qm, mk, mv, e):
    n = q.shape[0]
    blk = lambda i: (i, 0, 0)
    lblk = lambda i: (l, i, 0, 0)
    tok = lambda w: pl.BlockSpec((DEC_BLOCK * t, w), lambda i: (i, 0))
    cache_in = pl.BlockSpec((None, DEC_BLOCK, WINDOW, KV_W), lblk)
    cache_out = pl.BlockSpec((DEC_BLOCK, WINDOW, KV_W), blk)
    mem = pl.BlockSpec((None, DEC_BLOCK, N_MEM * MEM_HEADS, MEM_HEAD_DIM), lblk)
    return pl.pallas_call(
        functools.partial(_attn_sample_kernel, l, t), name="attn_sample",
        grid=(n // (DEC_BLOCK * t),),
        in_specs=[pl.BlockSpec(memory_space=pltpu.SMEM), tok(BRANCH_W), tok(KV_W), tok(KV_W),
                  cache_in, cache_in, tok(BRANCH_W), mem, mem, _full((KV_W, BRANCH_W))],
        out_specs=[tok(BRANCH_W), tok(BRANCH_W), cache_out, cache_out],
        out_shape=[jax.ShapeDtypeStruct((n, BRANCH_W), F32)] * 2
                  + [jax.ShapeDtypeStruct((n // t, WINDOW, KV_W), F32)] * 2,
        scratch_shapes=[pltpu.VMEM((DEC_BLOCK, 2 * WINDOW, KV_W), F32)] * 2,
        compiler_params=_cparams(1),
    )(sinks, q, kn, vn, ck, cv, qm, mk, mv, e)


def _ssm_tables(a_re, a_im, log_dt, b_re, b_im, c_re, c_im, chunk):
    hp = lax.Precision.HIGHEST
    dt = jnp.exp(log_dt)[..., None]
    mag = jnp.exp(a_re * dt)
    lam_re, lam_im = mag * jnp.cos(a_im * dt), mag * jnp.sin(a_im * dt)
    den = a_re * a_re + a_im * a_im
    nr, ni = lam_re - 1.0, lam_im
    g_re = (nr * a_re + ni * a_im) / den
    g_im = (ni * a_re - nr * a_im) / den
    bg_re = g_re[..., None] * b_re - g_im[..., None] * b_im
    bg_im = g_re[..., None] * b_im + g_im[..., None] * b_re
    def powers(d):
        d = d.astype(F32)[:, None, None, None]
        pm = jnp.exp(d * (a_re * dt)[None])
        return pm * jnp.cos(d * (a_im * dt)[None]), pm * jnp.sin(d * (a_im * dt)[None])

    def times_gb(p_re, p_im):
        return (p_re[..., None] * bg_re[None] - p_im[..., None] * bg_im[None],
                p_re[..., None] * bg_im[None] + p_im[..., None] * bg_re[None])

    pw_re, pw_im = powers(jnp.arange(chunk + 1))
    z_re, z_im = times_gb(pw_re[:chunk], pw_im[:chunk])
    zrev_re, zrev_im = times_gb(*powers(jnp.arange(chunk - 1, -1, -1)))
    kern = (jnp.einsum('lgop,dlgpi->dlgio', c_re, z_re, precision=hp)
            - jnp.einsum('lgop,dlgpi->dlgio', c_im, z_im, precision=hp))
    nl = a_re.shape[0]
    nb, gb = N_LANE_BLOCKS, GROUPS_PER_LANE_BLOCK
    kd = jnp.transpose(kern.reshape(chunk, nl, nb, gb, SSM_GROUP_CH, SSM_GROUP_CH), (1, 2, 0, 3, 4, 5))
    kd = kd.reshape(nl, nb, chunk, LANES, SSM_GROUP_CH)

    def to_rows(z):
        z = z.reshape(chunk, nl, nb, gb, SSM_STATE, SSM_GROUP_CH)
        return jnp.transpose(z, (1, 2, 0, 3, 5, 4)).reshape(nl, nb, chunk, LANES, SSM_STATE)

    zr, zi = to_rows(zrev_re), to_rows(zrev_im)
    zc = jnp.concatenate([zr, zr, zi, zi], axis=-1)

    pr, pi = pw_re[1:], pw_im[1:]
    f_a = c_re[None] * pr[:, :, :, None, :] - c_im[None] * pi[:, :, :, None, :]
    f_b = c_re[None] * pi[:, :, :, None, :] + c_im[None] * pr[:, :, :, None, :]

    def from_cols(f):
        f = f.reshape(chunk, nl, nb, gb, SSM_GROUP_CH, SSM_STATE)
        return jnp.transpose(f, (1, 2, 0, 4, 3, 5)).reshape(nl, nb, chunk, SSM_GROUP_CH, STATE_W)

    ft = jnp.concatenate([from_cols(f_a), from_cols(-f_b)], axis=-1)

    kw = chunk * LANES
    spec = lambda *shape: pl.BlockSpec((None, None) + shape, lambda l, n: (l, n) + (0,) * len(shape))
    intra, to_st, from_st_t = pl.pallas_call(
        _ssm_table_kernel, name="ssm_tables",
        grid=(nl, nb),
        in_specs=[spec(chunk, LANES, SSM_GROUP_CH), spec(chunk, LANES, 2 * LANES),
                  spec(chunk, SSM_GROUP_CH, 2 * STATE_W)],
        out_specs=[spec(kw, kw), spec(kw, 2 * STATE_W), spec(kw, 2 * STATE_W)],
        out_shape=[jax.ShapeDtypeStruct((nl, nb, kw, kw), BF16),
                   jax.ShapeDtypeStruct((nl, nb, kw, 2 * STATE_W), BF16),
                   jax.ShapeDtypeStruct((nl, nb, kw, 2 * STATE_W), BF16)],
        compiler_params=_cparams(2),
    )(kd, zc, ft)
    decay = lambda d: (pw_re[d].reshape(nl, nb, 1, STATE_W), pw_im[d].reshape(nl, nb, 1, STATE_W))
    return (intra, to_st, from_st_t), decay


def _ssm_table_kernel(kd_ref, zc_ref, ft_ref, intra_ref, to_ref, fromt_ref):
    chunk = kd_ref.shape[0]
    gch = SSM_GROUP_CH
    r = lax.broadcasted_iota(jnp.int32, (LANES, LANES), 0)
    c = lax.broadcasted_iota(jnp.int32, (LANES, LANES), 1)
    same_group = (r // gch) == (c // gch)
    spread = (lax.broadcasted_iota(jnp.int32, (gch, LANES), 1) % gch
              == lax.broadcasted_iota(jnp.int32, (gch, LANES), 0)).astype(BF16)
    spread_t = (lax.broadcasted_iota(jnp.int32, (LANES, gch), 0) % gch
                == lax.broadcasted_iota(jnp.int32, (LANES, gch), 1)).astype(BF16)
    rs = lax.broadcasted_iota(jnp.int32, (LANES, 2 * STATE_W), 0)
    cs = lax.broadcasted_iota(jnp.int32, (LANES, 2 * STATE_W), 1)
    own_state = (rs // gch) == ((cs % STATE_W) // SSM_STATE)

    lag_blocks = [jnp.where(same_group, _dot(kd_ref[d].astype(BF16), spread), 0.0).astype(BF16)
                  for d in range(chunk)]
    zero = jnp.zeros((LANES, LANES), BF16)
    for s in range(chunk):
        for t in range(chunk):
            intra_ref[s * LANES:(s + 1) * LANES, t * LANES:(t + 1) * LANES] = (
                lag_blocks[t - s] if t >= s else zero)
        z = zc_ref[s]
        full = jnp.concatenate([z[:, :LANES]] * (STATE_W // LANES) + [z[:, LANES:]] * (STATE_W // LANES), axis=1)
        to_ref[s * LANES:(s + 1) * LANES, :] = jnp.where(own_state, full, 0.0).astype(BF16)
        f = _dot(spread_t, ft_ref[s].astype(BF16))
        fromt_ref[s * LANES:(s + 1) * LANES, :] = jnp.where(own_state, f, 0.0).astype(BF16)


SSM_SEQS = 2


def _chunk_rows(ref, chunk):
    n = ref.shape[0] // chunk
    return jnp.concatenate([ref[pl.ds(s, n, stride=chunk), :] for s in range(chunk)], axis=1)


def _store_chunk_rows(ref, y, chunk):
    n = ref.shape[0] // chunk
    for t in range(chunk):
        ref[pl.ds(t, n, stride=chunk), :] = y[:, t * LANES:(t + 1) * LANES]


def _ssm_prompt_kernel(u_ref, intra_ref, to_ref, fromt_ref, dre_ref, dim_ref,
                       y_ref, hre_ref, him_ref, st_scr):
    chunk = PROMPT_CHUNK
    nchunks = u_ref.shape[0] // (chunk * SSM_SEQS)
    x = _chunk_rows(u_ref, chunk).astype(BF16)
    y = _dot(x, intra_ref[...])
    add_all = _dot(x, to_ref[...])
    nblk = STATE_W // LANES
    assert SSM_SEQS * nblk == SUBLANES
    slots = [(seq, cb) for seq in range(SSM_SEQS) for cb in range(nblk)]

    def slot_rows(seq, cb):
        return pl.ds(seq * nblk + cb, nchunks, stride=SUBLANES)

    for c in range(2):
        for seq, cb in slots:
            col = (c * nblk + cb) * LANES
            st_scr[c, slot_rows(seq, cb), :] = add_all[seq * nchunks:(seq + 1) * nchunks, col:col + LANES]
    dre = jnp.concatenate([dre_ref[:, cb * LANES:(cb + 1) * LANES] for _, cb in slots], axis=0)
    dim = jnp.concatenate([dim_ref[:, cb * LANES:(cb + 1) * LANES] for _, cb in slots], axis=0)

    def step(k, carry):
        hr, hi = carry
        tile = pl.ds(pl.multiple_of(k * SUBLANES, SUBLANES), SUBLANES)
        add_r, add_i = st_scr[0, tile, :], st_scr[1, tile, :]
        st_scr[0, tile, :] = hr
        st_scr[1, tile, :] = hi
        return dre * hr - dim * hi + add_r, dre * hi + dim * hr + add_i

    zero = jnp.zeros((SUBLANES, LANES), F32)
    hr, hi = lax.fori_loop(0, nchunks, step, (zero, zero), unroll=8)
    hre_ref[...] = hr
    him_ref[...] = hi
    h_in = jnp.concatenate(
        [jnp.concatenate([st_scr[c, slot_rows(seq, cb), :] for seq in range(SSM_SEQS)], axis=0)
         for c in range(2) for cb in range(nblk)], axis=1).astype(BF16)
    _store_chunk_rows(y_ref, y + _dot_nt(h_in, fromt_ref[...]), chunk)


def _ssm_prompt(l, u2d, seq_len, intra, to_st, from_st_t, dec_re, dec_im):
    n = u2d.shape[0]
    chunk = PROMPT_CHUNK
    rows = SSM_SEQS * seq_len
    steps = n // rows
    kw = chunk * LANES
    data = pl.BlockSpec((rows, LANES), lambda j, h: (h, j))
    wspec = lambda r, c: pl.BlockSpec((None, None, r, c), lambda j, h: (l, j, 0, 0))
    st_out = pl.BlockSpec((None, None, SUBLANES, LANES), lambda j, h: (j, h, 0, 0))
    st_shape = jax.ShapeDtypeStruct((N_LANE_BLOCKS, steps, SUBLANES, LANES), F32)
    return pl.pallas_call(
        _ssm_prompt_kernel, name="ssm_prompt",
        grid=(N_LANE_BLOCKS, steps),
        in_specs=[data, wspec(kw, kw), wspec(kw, 2 * STATE_W), wspec(kw, 2 * STATE_W),
                  wspec(1, STATE_W), wspec(1, STATE_W)],
        out_specs=[data, st_out, st_out],
        out_shape=[jax.ShapeDtypeStruct(u2d.shape, F32), st_shape, st_shape],
        scratch_shapes=[pltpu.VMEM((2, SUBLANES * seq_len // chunk, LANES), F32)],
        compiler_params=_cparams(2),
    )(u2d, intra, to_st, from_st_t, dec_re, dec_im)


def _ssm_sample_kernel(chunk, u_ref, h0r_ref, h0i_ref, intra_ref, to_ref, fromt_ref, dre_ref, dim_ref,
                       y_ref, hre_ref, him_ref):
    x = _chunk_rows(u_ref, chunk).astype(BF16)
    hr, hi = h0r_ref[...], h0i_ref[...]
    dre, dim = dre_ref[...], dim_ref[...]
    add = _dot(x, to_ref[...])
    hre_ref[...] = dre * hr - dim * hi + add[:, :STATE_W]
    him_ref[...] = dre * hi + dim * hr + add[:, STATE_W:]
    h0 = jnp.concatenate([hr, hi], axis=1).astype(BF16)
    _store_chunk_rows(y_ref, _dot(x, intra_ref[...]) + _dot_nt(h0, fromt_ref[...]), chunk)


def _ssm_sample(l, u2d, t, h0_re, h0_im, intra, to_st, from_st_t, dec_re, dec_im):
    n = u2d.shape[0]
    b = n // t
    kw = t * LANES
    assert PROMPT_CHUNK % t == 0
    last = PROMPT_CHUNK // t - 1
    data = pl.BlockSpec((n, LANES), lambda j: (0, j))
    st_in = pl.BlockSpec((None, b, STATE_W), lambda j: (l, 0, j))
    st_out = pl.BlockSpec((b, STATE_W), lambda j: (0, j))
    wspec = lambda r, c, rb=0: pl.BlockSpec((None, None, r, c), lambda j: (l, j, rb, 0))
    return pl.pallas_call(
        functools.partial(_ssm_sample_kernel, t), name="ssm_sample",
        grid=(N_LANE_BLOCKS,),
        in_specs=[data, st_in, st_in, wspec(kw, kw), wspec(kw, 2 * STATE_W, last), wspec(kw, 2 * STATE_W),
                  wspec(1, STATE_W), wspec(1, STATE_W)],
        out_specs=[data, st_out, st_out],
        out_shape=[jax.ShapeDtypeStruct(u2d.shape, F32),
                   jax.ShapeDtypeStruct((b, N_LANE_BLOCKS * STATE_W), F32),
                   jax.ShapeDtypeStruct((b, N_LANE_BLOCKS * STATE_W), F32)],
        compiler_params=_cparams(1),
    )(u2d, h0_re, h0_im, intra, to_st, from_st_t, dec_re, dec_im)


FF_CHUNKS = ((0, 512), (512, 512), (1024, 512), (1536, 512), (2048, 512), (2560, 256))


def _merge_ffn_kernel(x_ref, oa_ref, y_ref, u_ref, oc_ref, gates_ref, d_ref, wglu_ref, wb_ref, wout_ref,
                      g_ref, wup_ref, wdn_ref, o_ref):
    z = jax.nn.gelu(y_ref[...] + d_ref[...] * u_ref[...])
    ob = (z * jax.nn.sigmoid(_dot(z.astype(BF16), wglu_ref[...]))).astype(BF16)
    gate = lambda n: gates_ref[:, n * D_MODEL:(n + 1) * D_MODEL].astype(F32)
    merged = gate(0) * _dot(oa_ref[...], wb_ref[0])
    merged = merged + gate(1) * _dot(ob, wb_ref[1])
    merged = merged + gate(2) * _dot(oc_ref[...], wb_ref[2])
    x = x_ref[...] + _dot(merged.astype(BF16), wout_ref[...])
    h = (_rms(x) * g_ref[...]).astype(BF16)
    acc = x
    for lo, w in FF_CHUNKS:
        up_gate = _dot(h, wup_ref[:, lo:lo + w])
        up = _dot(h, wup_ref[:, D_FF + lo:D_FF + lo + w])
        acc = acc + _dot((jax.nn.silu(up_gate) * up).astype(BF16), wdn_ref[lo:lo + w, :])
    o_ref[...] = acc


def _merge_ffn(l, x2d, oa, y, u, oc, gates, d, wglu, wb, wout, g_ffn, wup, wdn):
    n = x2d.shape[0]
    row = lambda w: pl.BlockSpec((ROW_TILE, w), lambda i: (i, 0))
    return pl.pallas_call(
        _merge_ffn_kernel, name="merge_ffn",
        grid=(n // ROW_TILE,),
        in_specs=[row(D_MODEL), row(BRANCH_W), row(BRANCH_W), row(BRANCH_W), row(BRANCH_W),
                  row(3 * D_MODEL), _layer((1, BRANCH_W), l), _layer((BRANCH_W, BRANCH_W), l),
                  _layer((3, BRANCH_W, D_MODEL), l), _layer((D_MODEL, D_MODEL), l),
                  _layer((1, D_MODEL), l), _layer((D_MODEL, 2 * D_FF), l), _layer((D_FF, D_MODEL), l)],
        out_specs=row(D_MODEL),
        out_shape=jax.ShapeDtypeStruct((n, D_MODEL), F32),
        compiler_params=_cparams(1),
    )(x2d, oa, y, u, oc, gates, d, wglu, wb, wout, g_ffn, wup, wdn)


def _rope_tables(pos):
    half = HEAD_DIM // 2
    inv = ROPE_THETA ** (-jnp.arange(half, dtype=F32) / half)
    ang = pos[:, None] * inv[None, :]
    cos, sin = jnp.cos(ang), jnp.sin(ang)
    zero = jnp.zeros_like(sin)
    two = lambda a: jnp.concatenate([a, a], axis=1)
    return (two(jnp.concatenate([cos, cos], axis=1)), two(jnp.concatenate([-sin, zero], axis=1)),
            two(jnp.concatenate([zero, sin], axis=1)))


def _state_from_blocks(h):
    nb, steps, _, _ = h.shape
    h = h.reshape(nb, steps, SSM_SEQS, STATE_W // LANES, LANES)
    return jnp.transpose(h, (1, 2, 0, 3, 4)).reshape(steps * SSM_SEQS, SSM_GROUPS, SSM_STATE)


def kernel(x_prompt, x_sample, cache_swa_k, cache_swa_v, state_ssm_re, state_ssm_im, cache_mem_k,
           cache_mem_v, mem_prompt, attn_norm, w_in, q_norm, k_norm, attn_sinks, ssm_a_re, ssm_a_im,
           ssm_log_dt, ssm_b_re, ssm_b_im, ssm_c_re, ssm_c_im, ssm_d, ssm_w_glu, mem_norm, w_mem_kv,
           mem_q_norm, mem_k_norm, w_branch, w_out, ffn_norm, w_ffn_up, w_ffn_down):
    depth = w_in.shape[0]
    b, s, _ = x_prompt.shape
    db, t, _ = x_sample.shape
    assert db * t == ROW_TILE and s % ROW_TILE == 0

    rope_p = _rope_tables(jnp.arange(s, dtype=F32))
    rope_s = _rope_tables(jnp.tile(PAST_LEN + jnp.arange(t, dtype=F32), db))
    ssm_params = (ssm_a_re, ssm_a_im, ssm_log_dt, ssm_b_re, ssm_b_im, ssm_c_re, ssm_c_im)
    tables, decay = _ssm_tables(*ssm_params, PROMPT_CHUNK)
    tab_p, tab_s = tables + decay(PROMPT_CHUNK), tables + decay(t)
    blk = jnp.arange(256) // HEAD_DIM
    ones = (blk[:, None] == blk[None, :]).astype(BF16)
    lane = jnp.arange(BRANCH_W)[None, :]
    src = jnp.arange(KV_W)[:, None]
    expand = ((lane // (HEAD_DIM * KV_GROUP) == src // HEAD_DIM)
              & (lane % HEAD_DIM == src % HEAD_DIM)).astype(BF16)

    w_in_bf, w_mem_bf = w_in.astype(BF16), w_mem_kv.astype(BF16)
    wglu, wb, wout = ssm_w_glu.astype(BF16), w_branch.astype(BF16), w_out.astype(BF16)
    wup, wdn = w_ffn_up.astype(BF16), w_ffn_down.astype(BF16)
    row = lambda a: a[:, None, :]
    g_attn, g_mem, g_ffn, d_row = row(attn_norm), row(mem_norm), row(ffn_norm), row(ssm_d)
    qg, kg = row(jnp.tile(q_norm, (1, N_HEADS))), row(jnp.tile(k_norm, (1, N_KV_HEADS)))
    mqg, mkg = row(mem_q_norm), row(mem_k_norm)
    ck = cache_swa_k.reshape(depth, db, WINDOW, KV_W)
    cv = cache_swa_v.reshape(depth, db, WINDOW, KV_W)
    cmk = cache_mem_k.reshape(depth, db, N_MEM * MEM_HEADS, MEM_HEAD_DIM)
    cmv = cache_mem_v.reshape(depth, db, N_MEM * MEM_HEADS, MEM_HEAD_DIM)
    h0r = state_ssm_re.reshape(depth, db, SSM_GROUPS * SSM_STATE)
    h0i = state_ssm_im.reshape(depth, db, SSM_GROUPS * SSM_STATE)

    yp = x_prompt.reshape(b * s, D_MODEL)
    ys = x_sample.reshape(db * t, D_MODEL)
    mem2d = mem_prompt.reshape(b * N_MEM, D_MODEL)
    outs = [[] for _ in range(10)]
    for l in range(depth):
        mk, mv = _memkv(l, mem2d, g_mem, w_mem_bf, mkg)
        mk3, mv3 = mk.reshape(b, N_MEM, BRANCH_W), mv.reshape(b, N_MEM, BRANCH_W)

        k, v, u, gates, oa, oc = _inproj_attn(l, attn_sinks, yp, s, g_attn, w_in_bf, rope_p, qg, kg, mqg, ones,
                                              mk3, mv3)
        k3, v3 = k.reshape(b, s, KV_W), v.reshape(b, s, KV_W)
        ysm, hre, him = _ssm_prompt(l, u, s, *tab_p)
        yp = _merge_ffn(l, yp, oa, ysm, u, oc, gates, d_row, wglu, wb, wout, g_ffn, wup, wdn)

        qs, ks, vs, us, mqs, gates_s = _inproj(l, ys, g_attn, w_in_bf, rope_s, 1, qg, kg, mqg, ones)
        oas, ocs, nk, nv = _attn_sample(l, t, attn_sinks, qs, ks, vs, ck, cv, mqs, cmk, cmv, expand)
        yss, hrs, his = _ssm_sample(l, us, t, h0r, h0i, *tab_s)
        ys = _merge_ffn(l, ys, oas.astype(BF16), yss, us, ocs.astype(BF16), gates_s, d_row, wglu, wb, wout,
                        g_ffn, wup, wdn)

        new = (k3[:, -WINDOW:].reshape(b, WINDOW, N_KV_HEADS, HEAD_DIM),
               v3[:, -WINDOW:].reshape(b, WINDOW, N_KV_HEADS, HEAD_DIM),
               _state_from_blocks(hre), _state_from_blocks(him),
               mk3.reshape(b, N_MEM, MEM_HEADS, MEM_HEAD_DIM), mv3.reshape(b, N_MEM, MEM_HEADS, MEM_HEAD_DIM),
               nk.reshape(db, WINDOW, N_KV_HEADS, HEAD_DIM), nv.reshape(db, WINDOW, N_KV_HEADS, HEAD_DIM),
               hrs.reshape(db, SSM_GROUPS, SSM_STATE), his.reshape(db, SSM_GROUPS, SSM_STATE))
        for lst, a in zip(outs, new):
            lst.append(a)

    return (yp.reshape(b, s, D_MODEL), ys.reshape(db, t, D_MODEL)) + tuple(jnp.stack(o) for o in outs)
```

```python
import functools
import math

import jax
import jax.numpy as jnp
from jax import lax
from jax.experimental import pallas as pl
from jax.experimental.pallas import tpu as pltpu

F32 = jnp.float32
BF16 = jnp.bfloat16

D_MODEL = 1024
BRANCH_W = 512
HEAD_DIM = 64
N_HEADS = 8
N_KV_HEADS = 2
KV_GROUP = 4
KV_W = N_KV_HEADS * HEAD_DIM
WINDOW = 128
ROPE_THETA = 10000.0
PAST_LEN = 16384
SSM_GROUPS = 32
SSM_GROUP_CH = 16
SSM_STATE = 64
N_MEM = 256
MEM_HEADS = 4
MEM_HEAD_DIM = 128
D_FF = 2816
RMS_EPS = 1e-6

Q_OFF, K_OFF, V_OFF, U_OFF, MQ_OFF, G_OFF = 0, 512, 640, 768, 1280, 1792
IN_W = G_OFF + 3 * D_MODEL

LANES = 128
SUBLANES = 8
MXU_TILE = 256
GROUPS_PER_LANE_BLOCK = LANES // SSM_GROUP_CH
N_LANE_BLOCKS = BRANCH_W // LANES
STATE_W = GROUPS_PER_LANE_BLOCK * SSM_STATE
PROMPT_CHUNK = 8
ROW_TILE = 512
VMEM_LIMIT = 56 * 1024 * 1024


def _dot(a, b):
    return jnp.dot(a, b, preferred_element_type=F32)


def _dot_nt(a, b):
    return lax.dot_general(a, b, (((1,), (1,)), ((), ())), preferred_element_type=F32)


def _cparams(n_axes):
    return pltpu.CompilerParams(dimension_semantics=("arbitrary",) * n_axes,
                                vmem_limit_bytes=VMEM_LIMIT)


def _full(shape):
    nd = len(shape)
    return pl.BlockSpec(shape, lambda *_: (0,) * nd)


def _layer(shape, l):
    nd = len(shape)
    return pl.BlockSpec((None,) + tuple(shape), lambda *_: (l,) + (0,) * nd, pipeline_mode=pl.Buffered(1))


def _rms(x):
    return x * lax.rsqrt(jnp.mean(x * x, axis=-1, keepdims=True) + RMS_EPS)


def _memkv_kernel(l, mem_ref, g_ref, w_ref, kg_ref, k_ref, v_ref):
    h = _rms(mem_ref[...]) * g_ref[l:l + 1, :]
    kv = _dot(h.astype(BF16), w_ref[...])
    kg = kg_ref[l:l + 1, :]
    for hd in range(MEM_HEADS):
        sl = slice(hd * MEM_HEAD_DIM, (hd + 1) * MEM_HEAD_DIM)
        k_ref[:, sl] = _rms(kv[:, sl]) * kg
    v_ref[...] = kv[:, BRANCH_W:]


def _memkv(l, mem2d, g, w_bf, kg):
    n = mem2d.shape[0]
    return pl.pallas_call(
        functools.partial(_memkv_kernel, l), name="memkv",
        grid=(n // ROW_TILE,),
        in_specs=[pl.BlockSpec((ROW_TILE, D_MODEL), lambda i: (i, 0)),
                  _full(g.shape), _layer((D_MODEL, 2 * BRANCH_W), l), _full(kg.shape)],
        out_specs=[pl.BlockSpec((ROW_TILE, BRANCH_W), lambda i: (i, 0))] * 2,
        out_shape=[jax.ShapeDtypeStruct((n, BRANCH_W), F32)] * 2,
        compiler_params=_cparams(1),
    )(mem2d, g, w_bf, kg)


def _rotary(x, cos, sin_lo, sin_hi):
    w = x.shape[-1]
    half = HEAD_DIM // 2
    return x * cos + pltpu.roll(x, w - half, 1) * sin_lo + pltpu.roll(x, half, 1) * sin_hi


GATE_CHUNK = 512
N_GATE_CHUNKS = 3 * D_MODEL // GATE_CHUNK


class _Projection:
    def __init__(self, x, g, w_ref, rope, qg, kg, mqg, ones):
        self.hb = (_rms(x) * g).astype(BF16)
        self.w_ref, self.qg, self.kg, self.mqg = w_ref, qg, kg, mqg
        self.ones = ones
        self.rope = rope

    def queries(self):
        zq = _dot(self.hb, self.w_ref[:, Q_OFF:K_OFF])
        sq = (zq * zq).astype(BF16)
        ssq = jnp.concatenate([_dot(sq[:, :256], self.ones), _dot(sq[:, 256:], self.ones)], axis=1)
        qn = zq * lax.rsqrt(ssq * (1.0 / HEAD_DIM) + RMS_EPS) * self.qg
        rope4 = (jnp.concatenate([t] * 4, axis=1) for t in self.rope)
        return (_rotary(qn, *rope4) * (1.0 / math.sqrt(HEAD_DIM))).astype(BF16)

    def keys_values(self):
        zkv = _dot(self.hb, self.w_ref[:, K_OFF:U_OFF])
        zk = zkv[:, :KV_W]
        ssk = _dot((zk * zk).astype(BF16), self.ones[:KV_W, :KV_W])
        kn = zk * lax.rsqrt(ssk * (1.0 / HEAD_DIM) + RMS_EPS) * self.kg
        return _rotary(kn, *self.rope), zkv[:, KV_W:]

    def ssm_input(self):
        return _dot(self.hb, self.w_ref[:, U_OFF:MQ_OFF])

    def memory_queries(self):
        zm = _dot(self.hb, self.w_ref[:, MQ_OFF:G_OFF])
        mqg = self.mqg * (1.0 / math.sqrt(MEM_HEAD_DIM))
        return jnp.concatenate([_rms(zm[:, hd * MEM_HEAD_DIM:(hd + 1) * MEM_HEAD_DIM]) * mqg
                                for hd in range(MEM_HEADS)], axis=1).astype(BF16)

    def gates(self, c):
        lo = G_OFF + c * GATE_CHUNK
        return jax.nn.sigmoid(_dot(self.hb, self.w_ref[:, lo:lo + GATE_CHUNK])).astype(BF16)


ATTN_TILE = 512


def _softmax_rows(s):
    m = jnp.max(s, axis=-1, keepdims=True)
    p = jnp.exp(s - m)
    return p, jnp.sum(p, axis=-1, keepdims=True)


def _sink_column(sink_ref, l, heads, rows_per_head):
    n = len(heads) * rows_per_head
    rcol = lax.broadcasted_iota(jnp.int32, (n, 1), 0)
    sink = jnp.full((n, 1), sink_ref[l, heads[-1]], F32)
    for g in range(len(heads) - 2, -1, -1):
        sink = jnp.where(rcol < (g + 1) * rows_per_head, sink_ref[l, heads[g]], sink)
    return sink


def _sink_softmax(s, sink):
    m = jnp.maximum(jnp.max(s, axis=-1, keepdims=True), sink)
    p = jnp.exp(s - m)
    return p, jnp.sum(p, axis=-1, keepdims=True) + jnp.exp(sink - m)


def _attend_sub_block(sub, mask, sinks, q_ref, mq_ref, k_ref, v_ref, mk_ref, mv_ref, oa_ref, oc_ref):
    qrows = slice(sub * WINDOW, (sub + 1) * WINDOW)
    krows = slice(sub * WINDOW, (sub + 2) * WINDOW)
    ksl = [slice(kvh * HEAD_DIM, (kvh + 1) * HEAD_DIM) for kvh in range(N_KV_HEADS)]
    msl = [slice(hd * MEM_HEAD_DIM, (hd + 1) * MEM_HEAD_DIM) for hd in range(MEM_HEADS)]
    heads = [[kvh * KV_GROUP + g for g in range(KV_GROUP)] for kvh in range(N_KV_HEADS)]
    scores = []
    for kvh in range(N_KV_HEADS):
        qg = jnp.concatenate([q_ref[qrows, hd * HEAD_DIM:(hd + 1) * HEAD_DIM] for hd in heads[kvh]], axis=0)
        scores.append(jnp.where(mask, _dot_nt(qg, k_ref[krows, ksl[kvh]]), -jnp.inf))
    mscores = [_dot_nt(mq_ref[qrows, sl], mk_ref[:, sl]) for sl in msl]
    yield
    probs = [_sink_softmax(scores[kvh], sinks[kvh]) for kvh in range(N_KV_HEADS)]
    mprobs = [_softmax_rows(s) for s in mscores]
    yield
    outs = [_dot(probs[kvh][0].astype(BF16), v_ref[krows, ksl[kvh]]) / probs[kvh][1]
            for kvh in range(N_KV_HEADS)]
    mouts = [_dot(mprobs[hd][0].astype(BF16), mv_ref[:, msl[hd]]) / mprobs[hd][1] for hd in range(MEM_HEADS)]
    for kvh in range(N_KV_HEADS):
        for g, hd in enumerate(heads[kvh]):
            oa_ref[qrows, hd * HEAD_DIM:(hd + 1) * HEAD_DIM] = outs[kvh][g * WINDOW:(g + 1) * WINDOW].astype(BF16)
    for hd in range(MEM_HEADS):
        oc_ref[qrows, msl[hd]] = mouts[hd].astype(BF16)


def _inproj_attn_kernel(l, tiles_per_seq, sink_ref, x_ref, xs_ref, g_ref, w_ref, cos_ref, slo_ref, shi_ref,
                        qg_ref, kg_ref, mqg_ref, ones_ref, mk_ref, mv_ref,
                        k_ref, v_ref, u_ref, gates_ref, oa_ref, oc_ref, qs_ref, mqs_ref,
                        q_scr, mq_scr, k_scr, v_scr, mk_scr, mv_scr):
    i = pl.program_id(0)
    cur, prv = i % 2, (i + 1) % 2
    decode_step = i == pl.num_programs(0) - 1

    @pl.when(i == 0)
    def _():
        for scr in (q_scr, mq_scr, k_scr, v_scr, mk_scr, mv_scr):
            scr[...] = jnp.zeros(scr.shape, scr.dtype)

    @pl.when(i % tiles_per_seq == 1)
    def _():
        mk_scr[...] = mk_ref[...].astype(BF16)
        mv_scr[...] = mv_ref[...].astype(BF16)

    rows = KV_GROUP * WINDOW
    r = lax.broadcasted_iota(jnp.int32, (rows, 2 * WINDOW), 0)
    j = lax.broadcasted_iota(jnp.int32, (rows, 2 * WINDOW), 1)
    qi = r % WINDOW
    band = (j > qi) & (j <= qi + WINDOW)
    first_band = band & ((j >= WINDOW) | (i % tiles_per_seq != 1))
    sinks = [_sink_column(sink_ref, l, [kvh * KV_GROUP + g for g in range(KV_GROUP)], WINDOW)
             for kvh in range(N_KV_HEADS)]

    def attention_stages():
        for sub in range(ATTN_TILE // WINDOW):
            yield from _attend_sub_block(sub, first_band if sub == 0 else band, sinks, q_scr.at[prv],
                                         mq_scr.at[prv], k_scr.at[prv], v_scr.at[prv], mk_scr, mv_scr,
                                         oa_ref, oc_ref)
            yield

    proj = _Projection(jnp.where(decode_step, xs_ref[...], x_ref[...]), g_ref[l:l + 1, :], w_ref,
                       (cos_ref[...], slo_ref[...], shi_ref[...]), qg_ref[l:l + 1, :], kg_ref[l:l + 1, :],
                       mqg_ref[l:l + 1, :], ones_ref[...])

    def store_queries():
        q = proj.queries()
        q_scr[cur] = q
        qs_ref[...] = q

    def store_keys_values():
        k, v = proj.keys_values()
        k_ref[...], v_ref[...] = k, v
        k_scr[cur, :WINDOW, :] = k_scr[prv, ATTN_TILE:, :]
        v_scr[cur, :WINDOW, :] = v_scr[prv, ATTN_TILE:, :]
        k_scr[cur, WINDOW:, :] = k.astype(BF16)
        v_scr[cur, WINDOW:, :] = v.astype(BF16)

    def store_ssm_input():
        u_ref[...] = proj.ssm_input()

    def store_memory_queries():
        mq = proj.memory_queries()
        mq_scr[cur] = mq
        mqs_ref[...] = mq

    def store_gates(c):
        gates_ref[:, c * GATE_CHUNK:(c + 1) * GATE_CHUNK] = proj.gates(c)

    segments = [store_queries, store_keys_values, store_ssm_input, store_memory_queries]
    segments += [functools.partial(store_gates, c) for c in range(N_GATE_CHUNKS)]
    stages = attention_stages()
    n_stages = 3 * (ATTN_TILE // WINDOW)
    for n, segment in enumerate(segments):
        for _ in range((n + 1) * n_stages // len(segments) - n * n_stages // len(segments)):
            next(stages)
        segment()


def _inproj_attn(l, sinks, x2d, xs2d, seq_len, g, w_bf, rope, qg, kg, mqg, ones, mk, mv):
    n = x2d.shape[0]
    n_tiles = n // ROW_TILE
    tiles_per_seq = seq_len // ROW_TILE
    assert ROW_TILE == ATTN_TILE and tiles_per_seq > 1 and xs2d.shape[0] == ROW_TILE
    attn_tile = lambda i: jnp.maximum(i - 1, 0)
    xrow = pl.BlockSpec((ROW_TILE, D_MODEL), lambda i: (jnp.minimum(i, n_tiles - 1), 0))
    orow = lambda w: pl.BlockSpec((ROW_TILE, w), lambda i: (i, 0))
    arow = pl.BlockSpec((ROW_TILE, BRANCH_W), lambda i: (attn_tile(i), 0))
    rspec = pl.BlockSpec((ROW_TILE, LANES), lambda i: (jnp.where(i == n_tiles, tiles_per_seq, i % tiles_per_seq), 0))
    mem = pl.BlockSpec((None, N_MEM, BRANCH_W), lambda i: (attn_tile(i) // tiles_per_seq, 0, 0))
    widths = (KV_W, KV_W, BRANCH_W, 3 * D_MODEL)
    dtypes = (F32, F32, F32, BF16)
    return pl.pallas_call(
        functools.partial(_inproj_attn_kernel, l, tiles_per_seq), name="inproj_attn",
        grid=(n_tiles + 1,),
        in_specs=[pl.BlockSpec(memory_space=pltpu.SMEM), xrow, _full((ROW_TILE, D_MODEL)), _full(g.shape),
                  _layer((D_MODEL, IN_W), l), rspec, rspec, rspec, _full(qg.shape), _full(kg.shape),
                  _full(mqg.shape), _full((256, 256)), mem, mem],
        out_specs=[orow(w) for w in widths] + [arow, arow] + [_full((ROW_TILE, BRANCH_W))] * 2,
        out_shape=[jax.ShapeDtypeStruct((n + ROW_TILE, w), d) for w, d in zip(widths, dtypes)]
                  + [jax.ShapeDtypeStruct((n, BRANCH_W), BF16)] * 2
                  + [jax.ShapeDtypeStruct((ROW_TILE, BRANCH_W), BF16)] * 2,
        scratch_shapes=[pltpu.VMEM((2, ROW_TILE, BRANCH_W), BF16)] * 2
                       + [pltpu.VMEM((2, WINDOW + ROW_TILE, KV_W), BF16)] * 2
                       + [pltpu.VMEM((N_MEM, BRANCH_W), BF16)] * 2,
        compiler_params=_cparams(1),
    )(sinks, x2d, xs2d, g, w_bf, *rope, qg, kg, mqg, ones, mk, mv)


DEC_BLOCK = 8
TOK_PAD = 8


def _stack_heads(x, n_heads, head_w):
    t, w = x.shape
    x8 = jnp.concatenate([x, jnp.zeros((TOK_PAD - t, w), F32)], axis=0)
    rr = lax.broadcasted_iota(jnp.int32, (n_heads * TOK_PAD, w), 0)
    ll = lax.broadcasted_iota(jnp.int32, (n_heads * TOK_PAD, w), 1)
    own = (ll // head_w) == (rr // TOK_PAD)
    return jnp.where(own, jnp.concatenate([x8] * n_heads, axis=0), 0.0).astype(BF16), own


def _unstack_heads(o, own, n_heads, t):
    o = jnp.where(own, o, 0.0)
    acc = o[0:TOK_PAD]
    for hd in range(1, n_heads):
        acc = acc + o[hd * TOK_PAD:(hd + 1) * TOK_PAD]
    return acc[0:t]


def _attn_sample_kernel(l, t, sink_ref, q_ref, kn_ref, vn_ref, ck_ref, cv_ref, qm_ref, mk_ref, mv_ref, e_ref,
                        oa_ref, oc_ref, nk_ref, nv_ref, kc_scr, vc_scr):
    keys = 2 * WINDOW
    rows = N_HEADS * TOK_PAD
    rk = lax.broadcasted_iota(jnp.int32, (rows, keys), 0)
    jk = lax.broadcasted_iota(jnp.int32, (rows, keys), 1)
    tk = rk % TOK_PAD
    mask = (tk < t) & (((jk < WINDOW) & (jk > tk)) | ((jk >= WINDOW) & (jk - WINDOW <= tk)))
    sink = _sink_column(sink_ref, l, list(range(N_HEADS)), TOK_PAD)
    mem_rows = N_MEM * MEM_HEADS
    mem_own = (lax.broadcasted_iota(jnp.int32, (MEM_HEADS * TOK_PAD, mem_rows), 1) % MEM_HEADS
               == lax.broadcasted_iota(jnp.int32, (MEM_HEADS * TOK_PAD, mem_rows), 0) // TOK_PAD)
    e = e_ref[...]

    @pl.when(pl.program_id(0) == 0)
    def _():
        kc_scr[:, WINDOW:, :] = jnp.zeros((DEC_BLOCK, WINDOW, KV_W), F32)
        vc_scr[:, WINDOW:, :] = jnp.zeros((DEC_BLOCK, WINDOW, KV_W), F32)

    q_all, qm_all = q_ref[...].astype(F32), qm_ref[...].astype(F32)
    seqs = range(DEC_BLOCK)
    toks = [slice(bi * t, (bi + 1) * t) for bi in seqs]
    for bi in seqs:
        kc_scr[bi, :WINDOW, :] = ck_ref[bi]
        vc_scr[bi, :WINDOW, :] = cv_ref[bi]
        kc_scr[bi, WINDOW:WINDOW + t, :] = kn_ref[toks[bi], :]
        vc_scr[bi, WINDOW:WINDOW + t, :] = vn_ref[toks[bi], :]
    for bi in seqs:
        nk_ref[bi] = kc_scr[bi, t:WINDOW + t, :]
        nv_ref[bi] = vc_scr[bi, t:WINDOW + t, :]
    kx = [_dot(kc_scr[bi].astype(BF16), e).astype(BF16) for bi in seqs]
    vx = [_dot(vc_scr[bi].astype(BF16), e).astype(BF16) for bi in seqs]
    stacked = [_stack_heads(q_all[toks[bi]], N_HEADS, HEAD_DIM) for bi in seqs]
    scores = [jnp.where(mask, _dot_nt(stacked[bi][0], kx[bi]), -jnp.inf) for bi in seqs]
    mscores = []
    for bi in seqs:
        qm8 = jnp.concatenate([qm_all[toks[bi]], jnp.zeros((TOK_PAD - t, BRANCH_W), F32)], axis=0).astype(BF16)
        qst = jnp.concatenate([qm8[:, hd * MEM_HEAD_DIM:(hd + 1) * MEM_HEAD_DIM] for hd in range(MEM_HEADS)],
                              axis=0)
        mscores.append(jnp.where(mem_own, _dot_nt(qst, mk_ref[bi].astype(BF16)), -jnp.inf))
    probs = [_sink_softmax(scores[bi], sink) for bi in seqs]
    mprobs = [_softmax_rows(mscores[bi]) for bi in seqs]
    outs = [_dot(probs[bi][0].astype(BF16), vx[bi]) / probs[bi][1] for bi in seqs]
    mouts = [_dot(mprobs[bi][0].astype(BF16), mv_ref[bi].astype(BF16)) / mprobs[bi][1] for bi in seqs]
    for bi in seqs:
        oa_ref[toks[bi], :] = _unstack_heads(outs[bi], stacked[bi][1], N_HEADS, t)
        oc_ref[toks[bi], :] = jnp.concatenate(
            [mouts[bi][hd * TOK_PAD:(hd + 1) * TOK_PAD] for hd in range(MEM_HEADS)], axis=1)[0:t]


def _attn_sample(l, t, sinks, q, kn, vn, ck, cv, qm, mk, mv, e):
    n = q.shape[0]
    blk = lambda i: (i, 0, 0)
    lblk = lambda i: (l, i, 0, 0)
    tok = lambda w: pl.BlockSpec((DEC_BLOCK * t, w), lambda i: (i, 0))
    cache_in = pl.BlockSpec((None, DEC_BLOCK, WINDOW, KV_W), lblk)
    cache_out = pl.BlockSpec((DEC_BLOCK, WINDOW, KV_W), blk)
    mem = pl.BlockSpec((None, DEC_BLOCK, N_MEM * MEM_HEADS, MEM_HEAD_DIM), lblk)
    return pl.pallas_call(
        functools.partial(_attn_sample_kernel, l, t), name="attn_sample",
        grid=(n // (DEC_BLOCK * t),),
        in_specs=[pl.BlockSpec(memory_space=pltpu.SMEM), tok(BRANCH_W), tok(KV_W), tok(KV_W),
                  cache_in, cache_in, tok(BRANCH_W), mem, mem, _full((KV_W, BRANCH_W))],
        out_specs=[tok(BRANCH_W), tok(BRANCH_W), cache_out, cache_out],
        out_shape=[jax.ShapeDtypeStruct((n, BRANCH_W), F32)] * 2
                  + [jax.ShapeDtypeStruct((n // t, WINDOW, KV_W), F32)] * 2,
        scratch_shapes=[pltpu.VMEM((DEC_BLOCK, 2 * WINDOW, KV_W), F32)] * 2,
        compiler_params=_cparams(1),
    )(sinks, q, kn, vn, ck, cv, qm, mk, mv, e)


def _ssm_tables(a_re, a_im, log_dt, b_re, b_im, c_re, c_im, chunk):
    hp = lax.Precision.HIGHEST
    dt = jnp.exp(log_dt)[..., None]
    mag = jnp.exp(a_re * dt)
    lam_re, lam_im = mag * jnp.cos(a_im * dt), mag * jnp.sin(a_im * dt)
    den = a_re * a_re + a_im * a_im
    nr, ni = lam_re - 1.0, lam_im
    g_re = (nr * a_re + ni * a_im) / den
    g_im = (ni * a_re - nr * a_im) / den
    bg_re = g_re[..., None] * b_re - g_im[..., None] * b_im
    bg_im = g_re[..., None] * b_im + g_im[..., None] * b_re
    def powers(d):
        d = d.astype(F32)[:, None, None, None]
        pm = jnp.exp(d * (a_re * dt)[None])
        return pm * jnp.cos(d * (a_im * dt)[None]), pm * jnp.sin(d * (a_im * dt)[None])

    def times_gb(p_re, p_im):
        return (p_re[..., None] * bg_re[None] - p_im[..., None] * bg_im[None],
                p_re[..., None] * bg_im[None] + p_im[..., None] * bg_re[None])

    pw_re, pw_im = powers(jnp.arange(chunk + 1))
    z_re, z_im = times_gb(pw_re[:chunk], pw_im[:chunk])
    zrev_re, zrev_im = times_gb(*powers(jnp.arange(chunk - 1, -1, -1)))
    kern = (jnp.einsum('lgop,dlgpi->dlgio', c_re, z_re, precision=hp)
            - jnp.einsum('lgop,dlgpi->dlgio', c_im, z_im, precision=hp))
    nl = a_re.shape[0]
    nb, gb = N_LANE_BLOCKS, GROUPS_PER_LANE_BLOCK
    kd = jnp.transpose(kern.reshape(chunk, nl, nb, gb, SSM_GROUP_CH, SSM_GROUP_CH), (1, 2, 0, 3, 4, 5))
    kd = kd.reshape(nl, nb, chunk, LANES, SSM_GROUP_CH)

    def to_rows(z):
        z = z.reshape(chunk, nl, nb, gb, SSM_STATE, SSM_GROUP_CH)
        return jnp.transpose(z, (1, 2, 0, 3, 5, 4)).reshape(nl, nb, chunk, LANES, SSM_STATE)

    zr, zi = to_rows(zrev_re), to_rows(zrev_im)
    zc = jnp.concatenate([zr, zr, zi, zi], axis=-1)

    pr, pi = pw_re[1:], pw_im[1:]
    f_a = c_re[None] * pr[:, :, :, None, :] - c_im[None] * pi[:, :, :, None, :]
    f_b = c_re[None] * pi[:, :, :, None, :] + c_im[None] * pr[:, :, :, None, :]

    def from_cols(f):
        f = f.reshape(chunk, nl, nb, gb, SSM_GROUP_CH, SSM_STATE)
        return jnp.transpose(f, (1, 2, 0, 4, 3, 5)).reshape(nl, nb, chunk, SSM_GROUP_CH, STATE_W)

    ft = jnp.concatenate([from_cols(f_a), from_cols(-f_b)], axis=-1)

    kw = chunk * LANES
    spec = lambda *shape: pl.BlockSpec((None, None) + shape, lambda l, n: (l, n) + (0,) * len(shape))
    intra, to_st, from_st_t = pl.pallas_call(
        _ssm_table_kernel, name="ssm_tables",
        grid=(nl, nb),
        in_specs=[spec(chunk, LANES, SSM_GROUP_CH), spec(chunk, LANES, 2 * LANES),
                  spec(chunk, SSM_GROUP_CH, 2 * STATE_W)],
        out_specs=[spec(kw, kw), spec(kw, 2 * STATE_W), spec(kw, 2 * STATE_W)],
        out_shape=[jax.ShapeDtypeStruct((nl, nb, kw, kw), BF16),
                   jax.ShapeDtypeStruct((nl, nb, kw, 2 * STATE_W), BF16),
                   jax.ShapeDtypeStruct((nl, nb, kw, 2 * STATE_W), BF16)],
        compiler_params=_cparams(2),
    )(kd, zc, ft)
    decay = lambda d: (pw_re[d].reshape(nl, nb, 1, STATE_W), pw_im[d].reshape(nl, nb, 1, STATE_W))
    return (intra, to_st, from_st_t), decay


def _ssm_table_kernel(kd_ref, zc_ref, ft_ref, intra_ref, to_ref, fromt_ref):
    chunk = kd_ref.shape[0]
    gch = SSM_GROUP_CH
    r = lax.broadcasted_iota(jnp.int32, (LANES, LANES), 0)
    c = lax.broadcasted_iota(jnp.int32, (LANES, LANES), 1)
    same_group = (r // gch) == (c // gch)
    spread = (lax.broadcasted_iota(jnp.int32, (gch, LANES), 1) % gch
              == lax.broadcasted_iota(jnp.int32, (gch, LANES), 0)).astype(BF16)
    spread_t = (lax.broadcasted_iota(jnp.int32, (LANES, gch), 0) % gch
                == lax.broadcasted_iota(jnp.int32, (LANES, gch), 1)).astype(BF16)
    rs = lax.broadcasted_iota(jnp.int32, (LANES, 2 * STATE_W), 0)
    cs = lax.broadcasted_iota(jnp.int32, (LANES, 2 * STATE_W), 1)
    own_state = (rs // gch) == ((cs % STATE_W) // SSM_STATE)

    lag_blocks = [jnp.where(same_group, _dot(kd_ref[d].astype(BF16), spread), 0.0).astype(BF16)
                  for d in range(chunk)]
    zero = jnp.zeros((LANES, LANES), BF16)
    for s in range(chunk):
        for t in range(chunk):
            intra_ref[s * LANES:(s + 1) * LANES, t * LANES:(t + 1) * LANES] = (
                lag_blocks[t - s] if t >= s else zero)
        z = zc_ref[s]
        full = jnp.concatenate([z[:, :LANES]] * (STATE_W // LANES) + [z[:, LANES:]] * (STATE_W // LANES), axis=1)
        to_ref[s * LANES:(s + 1) * LANES, :] = jnp.where(own_state, full, 0.0).astype(BF16)
        f = _dot(spread_t, ft_ref[s].astype(BF16))
        fromt_ref[s * LANES:(s + 1) * LANES, :] = jnp.where(own_state, f, 0.0).astype(BF16)


SSM_SEQS = 2


def _chunk_rows(ref, chunk):
    n = ref.shape[0] // chunk
    return jnp.concatenate([ref[pl.ds(s, n, stride=chunk), :] for s in range(chunk)], axis=1)


def _store_chunk_rows(ref, y, chunk):
    n = ref.shape[0] // chunk
    for t in range(chunk):
        ref[pl.ds(t, n, stride=chunk), :] = y[:, t * LANES:(t + 1) * LANES]


def _ssm_prompt_kernel(u_ref, intra_ref, to_ref, fromt_ref, dre_ref, dim_ref,
                       y_ref, hre_ref, him_ref, st_scr):
    chunk = PROMPT_CHUNK
    nchunks = u_ref.shape[0] // (chunk * SSM_SEQS)
    x = _chunk_rows(u_ref, chunk).astype(BF16)
    kw = chunk * LANES
    y = jnp.concatenate([_dot(x[:, :c + MXU_TILE], intra_ref[:c + MXU_TILE, c:c + MXU_TILE])
                         for c in range(0, kw, MXU_TILE)], axis=1)
    add_all = _dot(x, to_ref[...])
    nblk = STATE_W // LANES
    assert SSM_SEQS * nblk == SUBLANES
    slots = [(seq, cb) for seq in range(SSM_SEQS) for cb in range(nblk)]

    def slot_rows(seq, cb):
        return pl.ds(seq * nblk + cb, nchunks, stride=SUBLANES)

    for c in range(2):
        for seq, cb in slots:
            col = (c * nblk + cb) * LANES
            st_scr[c, slot_rows(seq, cb), :] = add_all[seq * nchunks:(seq + 1) * nchunks, col:col + LANES]
    dre = jnp.concatenate([dre_ref[:, cb * LANES:(cb + 1) * LANES] for _, cb in slots], axis=0)
    dim = jnp.concatenate([dim_ref[:, cb * LANES:(cb + 1) * LANES] for _, cb in slots], axis=0)

    def step(k, carry):
        hr, hi = carry
        tile = pl.ds(pl.multiple_of(k * SUBLANES, SUBLANES), SUBLANES)
        add_r, add_i = st_scr[0, tile, :], st_scr[1, tile, :]
        st_scr[0, tile, :] = hr
        st_scr[1, tile, :] = hi
        return dre * hr - dim * hi + add_r, dre * hi + dim * hr + add_i

    zero = jnp.zeros((SUBLANES, LANES), F32)
    hr, hi = lax.fori_loop(0, nchunks, step, (zero, zero), unroll=8)
    hre_ref[...] = hr
    him_ref[...] = hi
    h_in = jnp.concatenate(
        [jnp.concatenate([st_scr[c, slot_rows(seq, cb), :] for seq in range(SSM_SEQS)], axis=0)
         for c in range(2) for cb in range(nblk)], axis=1).astype(BF16)
    _store_chunk_rows(y_ref, y + _dot_nt(h_in, fromt_ref[...]), chunk)


def _ssm_prompt(l, u2d, n_seqs, seq_len, intra, to_st, from_st_t, dec_re, dec_im):
    n = n_seqs * seq_len
    chunk = PROMPT_CHUNK
    rows = SSM_SEQS * seq_len
    steps = n // rows
    kw = chunk * LANES
    data = pl.BlockSpec((rows, LANES), lambda j, h: (h, j))
    wspec = lambda r, c: pl.BlockSpec((None, None, r, c), lambda j, h: (l, j, 0, 0))
    st_out = pl.BlockSpec((None, None, SUBLANES, LANES), lambda j, h: (j, h, 0, 0))
    st_shape = jax.ShapeDtypeStruct((N_LANE_BLOCKS, steps, SUBLANES, LANES), F32)
    return pl.pallas_call(
        _ssm_prompt_kernel, name="ssm_prompt",
        grid=(N_LANE_BLOCKS, steps),
        in_specs=[data, wspec(kw, kw), wspec(kw, 2 * STATE_W), wspec(kw, 2 * STATE_W),
                  wspec(1, STATE_W), wspec(1, STATE_W)],
        out_specs=[data, st_out, st_out],
        out_shape=[jax.ShapeDtypeStruct((n, BRANCH_W), F32), st_shape, st_shape],
        scratch_shapes=[pltpu.VMEM((2, SUBLANES * seq_len // chunk, LANES), F32)],
        compiler_params=_cparams(2),
    )(u2d, intra, to_st, from_st_t, dec_re, dec_im)


def _ssm_sample_kernel(chunk, u_ref, h0r_ref, h0i_ref, intra_ref, to_ref, fromt_ref, dre_ref, dim_ref,
                       y_ref, hre_ref, him_ref):
    x = _chunk_rows(u_ref, chunk).astype(BF16)
    hr, hi = h0r_ref[...], h0i_ref[...]
    dre, dim = dre_ref[...], dim_ref[...]
    add = _dot(x, to_ref[...])
    hre_ref[...] = dre * hr - dim * hi + add[:, :STATE_W]
    him_ref[...] = dre * hi + dim * hr + add[:, STATE_W:]
    h0 = jnp.concatenate([hr, hi], axis=1).astype(BF16)
    _store_chunk_rows(y_ref, _dot(x, intra_ref[...]) + _dot_nt(h0, fromt_ref[...]), chunk)


def _ssm_sample(l, u2d, t, h0_re, h0_im, intra, to_st, from_st_t, dec_re, dec_im):
    n = u2d.shape[0]
    b = n // t
    kw = t * LANES
    assert PROMPT_CHUNK % t == 0
    last = PROMPT_CHUNK // t - 1
    data = pl.BlockSpec((n, LANES), lambda j: (0, j))
    st_in = pl.BlockSpec((None, b, STATE_W), lambda j: (l, 0, j))
    st_out = pl.BlockSpec((b, STATE_W), lambda j: (0, j))
    wspec = lambda r, c, rb=0: pl.BlockSpec((None, None, r, c), lambda j: (l, j, rb, 0))
    return pl.pallas_call(
        functools.partial(_ssm_sample_kernel, t), name="ssm_sample",
        grid=(N_LANE_BLOCKS,),
        in_specs=[data, st_in, st_in, wspec(kw, kw), wspec(kw, 2 * STATE_W, last), wspec(kw, 2 * STATE_W),
                  wspec(1, STATE_W), wspec(1, STATE_W)],
        out_specs=[data, st_out, st_out],
        out_shape=[jax.ShapeDtypeStruct(u2d.shape, F32),
                   jax.ShapeDtypeStruct((b, N_LANE_BLOCKS * STATE_W), F32),
                   jax.ShapeDtypeStruct((b, N_LANE_BLOCKS * STATE_W), F32)],
        compiler_params=_cparams(1),
    )(u2d, h0_re, h0_im, intra, to_st, from_st_t, dec_re, dec_im)


FF_CHUNKS = ((0, 512), (512, 512), (1024, 512), (1536, 512), (2048, 512), (2560, 256))


def _merge_ffn_kernel(l, x_ref, oa_ref, y_ref, u_ref, oc_ref, gates_ref, d_ref, wglu_ref, wb_ref, wout_ref,
                      g_ref, wup_ref, wdn_ref, o_ref):
    z = jax.nn.gelu(y_ref[...] + d_ref[l:l + 1, :] * u_ref[...])
    ob = (z * jax.nn.sigmoid(_dot(z.astype(BF16), wglu_ref[...]))).astype(BF16)
    gate = lambda n: gates_ref[:, n * D_MODEL:(n + 1) * D_MODEL].astype(F32)
    merged = gate(0) * _dot(oa_ref[...], wb_ref[0])
    merged = merged + gate(1) * _dot(ob, wb_ref[1])
    merged = merged + gate(2) * _dot(oc_ref[...], wb_ref[2])
    x = x_ref[...] + _dot(merged.astype(BF16), wout_ref[...])
    h = (_rms(x) * g_ref[l:l + 1, :]).astype(BF16)
    acc = x
    for lo, w in FF_CHUNKS:
        up_gate = _dot(h, wup_ref[:, lo:lo + w])
        up = _dot(h, wup_ref[:, D_FF + lo:D_FF + lo + w])
        acc = acc + _dot((jax.nn.silu(up_gate) * up).astype(BF16), wdn_ref[lo:lo + w, :])
    o_ref[...] = acc


def _merge_ffn(l, x2d, oa, y, u, oc, gates, d, wglu, wb, wout, g_ffn, wup, wdn):
    n = x2d.shape[0]
    row = lambda w: pl.BlockSpec((ROW_TILE, w), lambda i: (i, 0))
    return pl.pallas_call(
        functools.partial(_merge_ffn_kernel, l), name="merge_ffn",
        grid=(n // ROW_TILE,),
        in_specs=[row(D_MODEL), row(BRANCH_W), row(BRANCH_W), row(BRANCH_W), row(BRANCH_W),
                  row(3 * D_MODEL), _full(d.shape), _layer((BRANCH_W, BRANCH_W), l),
                  _layer((3, BRANCH_W, D_MODEL), l), _layer((D_MODEL, D_MODEL), l),
                  _full(g_ffn.shape), _layer((D_MODEL, 2 * D_FF), l), _layer((D_FF, D_MODEL), l)],
        out_specs=row(D_MODEL),
        out_shape=jax.ShapeDtypeStruct((n, D_MODEL), F32),
        compiler_params=_cparams(1),
    )(x2d, oa, y, u, oc, gates, d, wglu, wb, wout, g_ffn, wup, wdn)


def _rope_tables(pos):
    half = HEAD_DIM // 2
    inv = ROPE_THETA ** (-jnp.arange(half, dtype=F32) / half)
    ang = pos[:, None] * inv[None, :]
    cos, sin = jnp.cos(ang), jnp.sin(ang)
    zero = jnp.zeros_like(sin)
    two = lambda a: jnp.concatenate([a, a], axis=1)
    return (two(jnp.concatenate([cos, cos], axis=1)), two(jnp.concatenate([-sin, zero], axis=1)),
            two(jnp.concatenate([zero, sin], axis=1)))


def _state_from_blocks(h):
    nb, steps, _, _ = h.shape
    h = h.reshape(nb, steps, SSM_SEQS, STATE_W // LANES, LANES)
    return jnp.transpose(h, (1, 2, 0, 3, 4)).reshape(steps * SSM_SEQS, SSM_GROUPS, SSM_STATE)


def kernel(x_prompt, x_sample, cache_swa_k, cache_swa_v, state_ssm_re, state_ssm_im, cache_mem_k,
           cache_mem_v, mem_prompt, attn_norm, w_in, q_norm, k_norm, attn_sinks, ssm_a_re, ssm_a_im,
           ssm_log_dt, ssm_b_re, ssm_b_im, ssm_c_re, ssm_c_im, ssm_d, ssm_w_glu, mem_norm, w_mem_kv,
           mem_q_norm, mem_k_norm, w_branch, w_out, ffn_norm, w_ffn_up, w_ffn_down):
    depth = w_in.shape[0]
    b, s, _ = x_prompt.shape
    db, t, _ = x_sample.shape
    assert db * t == ROW_TILE and s % ROW_TILE == 0

    rope = _rope_tables(jnp.concatenate([jnp.arange(s, dtype=F32),
                                         jnp.tile(PAST_LEN + jnp.arange(t, dtype=F32), db)]))
    ssm_params = (ssm_a_re, ssm_a_im, ssm_log_dt, ssm_b_re, ssm_b_im, ssm_c_re, ssm_c_im)
    tables, decay = _ssm_tables(*ssm_params, PROMPT_CHUNK)
    tab_p, tab_s = tables + decay(PROMPT_CHUNK), tables + decay(t)
    blk = jnp.arange(256) // HEAD_DIM
    ones = (blk[:, None] == blk[None, :]).astype(BF16)
    lane = jnp.arange(BRANCH_W)[None, :]
    src = jnp.arange(KV_W)[:, None]
    expand = ((lane // (HEAD_DIM * KV_GROUP) == src // HEAD_DIM)
              & (lane % HEAD_DIM == src % HEAD_DIM)).astype(BF16)

    w_in_bf, w_mem_bf = w_in.astype(BF16), w_mem_kv.astype(BF16)
    wglu, wb, wout = ssm_w_glu.astype(BF16), w_branch.astype(BF16), w_out.astype(BF16)
    wup, wdn = w_ffn_up.astype(BF16), w_ffn_down.astype(BF16)
    qg, kg = jnp.tile(q_norm, (1, N_HEADS)), jnp.tile(k_norm, (1, N_KV_HEADS))
    ck = cache_swa_k.reshape(depth, db, WINDOW, KV_W)
    cv = cache_swa_v.reshape(depth, db, WINDOW, KV_W)
    cmk = cache_mem_k.reshape(depth, db, N_MEM * MEM_HEADS, MEM_HEAD_DIM)
    cmv = cache_mem_v.reshape(depth, db, N_MEM * MEM_HEADS, MEM_HEAD_DIM)
    h0r = state_ssm_re.reshape(depth, db, SSM_GROUPS * SSM_STATE)
    h0i = state_ssm_im.reshape(depth, db, SSM_GROUPS * SSM_STATE)

    yp = x_prompt.reshape(b * s, D_MODEL)
    ys = x_sample.reshape(db * t, D_MODEL)
    mem2d = mem_prompt.reshape(b * N_MEM, D_MODEL)
    outs = [[] for _ in range(10)]
    n = b * s
    for l in range(depth):
        mk, mv = _memkv(l, mem2d, mem_norm, w_mem_bf, mem_k_norm)
        mk3, mv3 = mk.reshape(b, N_MEM, BRANCH_W), mv.reshape(b, N_MEM, BRANCH_W)

        k, v, u, gates, oa, oc, qs, mqs = _inproj_attn(l, attn_sinks, yp, ys, s, attn_norm, w_in_bf, rope, qg, kg,
                                                       mem_q_norm, ones, mk3, mv3)
        ysm, hre, him = _ssm_prompt(l, u, b, s, *tab_p)
        yp = _merge_ffn(l, yp, oa, ysm, u, oc, gates, ssm_d, wglu, wb, wout, ffn_norm, wup, wdn)

        ks, vs, us, gates_s = k[n:], v[n:], u[n:], gates[n:]
        oas, ocs, nk, nv = _attn_sample(l, t, attn_sinks, qs, ks, vs, ck, cv, mqs, cmk, cmv, expand)
        yss, hrs, his = _ssm_sample(l, us, t, h0r, h0i, *tab_s)
        ys = _merge_ffn(l, ys, oas.astype(BF16), yss, us, ocs.astype(BF16), gates_s, ssm_d, wglu, wb, wout,
                        ffn_norm, wup, wdn)

        last_rows = lambda a: a[:n].reshape(b, s, KV_W)[:, -WINDOW:].reshape(b, WINDOW, N_KV_HEADS, HEAD_DIM)
        new = (last_rows(k), last_rows(v),
               _state_from_blocks(hre), _state_from_blocks(him),
               mk3.reshape(b, N_MEM, MEM_HEADS, MEM_HEAD_DIM), mv3.reshape(b, N_MEM, MEM_HEADS, MEM_HEAD_DIM),
               nk.reshape(db, WINDOW, N_KV_HEADS, HEAD_DIM), nv.reshape(db, WINDOW, N_KV_HEADS, HEAD_DIM),
               hrs.reshape(db, SSM_GROUPS, SSM_STATE), his.reshape(db, SSM_GROUPS, SSM_STATE))
        for lst, a in zip(outs, new):
            lst.append(a)

    return (yp.reshape(b, s, D_MODEL), ys.reshape(db, t, D_MODEL)) + tuple(jnp.stack(o) for o in outs)
```

```python
import functools
import math

import jax
import jax.numpy as jnp
from jax import lax
from jax.experimental import pallas as pl
from jax.experimental.pallas import tpu as pltpu

F32 = jnp.float32
BF16 = jnp.bfloat16

D_MODEL = 1024
BRANCH_W = 512
HEAD_DIM = 64
N_HEADS = 8
N_KV_HEADS = 2
KV_GROUP = 4
KV_W = N_KV_HEADS * HEAD_DIM
WINDOW = 128
ROPE_THETA = 10000.0
PAST_LEN = 16384
SSM_GROUPS = 32
SSM_GROUP_CH = 16
SSM_STATE = 64
N_MEM = 256
MEM_HEADS = 4
MEM_HEAD_DIM = 128
D_FF = 2816
RMS_EPS = 1e-6

Q_OFF, K_OFF, V_OFF, U_OFF, MQ_OFF, G_OFF = 0, 512, 640, 768, 1280, 1792
IN_W = G_OFF + 3 * D_MODEL

LANES = 128
SUBLANES = 8
MXU_TILE = 256
GROUPS_PER_LANE_BLOCK = LANES // SSM_GROUP_CH
N_LANE_BLOCKS = BRANCH_W // LANES
STATE_W = GROUPS_PER_LANE_BLOCK * SSM_STATE
PROMPT_CHUNK = 8
ROW_TILE = 512
VMEM_LIMIT = 56 * 1024 * 1024


def _dot(a, b):
    return jnp.dot(a, b, preferred_element_type=F32)


def _dot_nt(a, b):
    return lax.dot_general(a, b, (((1,), (1,)), ((), ())), preferred_element_type=F32)


def _cparams(n_axes):
    return pltpu.CompilerParams(dimension_semantics=("arbitrary",) * n_axes,
                                vmem_limit_bytes=VMEM_LIMIT)


def _full(shape):
    nd = len(shape)
    return pl.BlockSpec(shape, lambda *_: (0,) * nd)


def _layer(shape, l):
    nd = len(shape)
    return pl.BlockSpec((None,) + tuple(shape), lambda *_: (l,) + (0,) * nd, pipeline_mode=pl.Buffered(1))


def _rms(x):
    return x * lax.rsqrt(jnp.mean(x * x, axis=-1, keepdims=True) + RMS_EPS)


def _memkv_kernel(l, mem_ref, g_ref, w_ref, kg_ref, k_ref, v_ref):
    h = _rms(mem_ref[...]) * g_ref[l:l + 1, :]
    kv = _dot(h.astype(BF16), w_ref[...])
    kg = kg_ref[l:l + 1, :]
    for hd in range(MEM_HEADS):
        sl = slice(hd * MEM_HEAD_DIM, (hd + 1) * MEM_HEAD_DIM)
        k_ref[:, sl] = _rms(kv[:, sl]) * kg
    v_ref[...] = kv[:, BRANCH_W:]


def _memkv(l, mem2d, g, w_bf, kg):
    n = mem2d.shape[0]
    return pl.pallas_call(
        functools.partial(_memkv_kernel, l), name="memkv",
        grid=(n // ROW_TILE,),
        in_specs=[pl.BlockSpec((ROW_TILE, D_MODEL), lambda i: (i, 0)),
                  _full(g.shape), _layer((D_MODEL, 2 * BRANCH_W), l), _full(kg.shape)],
        out_specs=[pl.BlockSpec((ROW_TILE, BRANCH_W), lambda i: (i, 0))] * 2,
        out_shape=[jax.ShapeDtypeStruct((n, BRANCH_W), F32)] * 2,
        compiler_params=_cparams(1),
    )(mem2d, g, w_bf, kg)


def _rotary(x, cos, sin_lo, sin_hi):
    w = x.shape[-1]
    half = HEAD_DIM // 2
    return x * cos + pltpu.roll(x, w - half, 1) * sin_lo + pltpu.roll(x, half, 1) * sin_hi


GATE_CHUNK = 512
N_GATE_CHUNKS = 3 * D_MODEL // GATE_CHUNK


class _Projection:
    def __init__(self, x, g, w_ref, rope, qg, kg, mqg, ones):
        self.hb = (_rms(x) * g).astype(BF16)
        self.w_ref, self.qg, self.kg, self.mqg = w_ref, qg, kg, mqg
        self.ones = ones
        self.rope = rope

    def queries(self):
        zq = _dot(self.hb, self.w_ref[:, Q_OFF:K_OFF])
        sq = (zq * zq).astype(BF16)
        ssq = jnp.concatenate([_dot(sq[:, :256], self.ones), _dot(sq[:, 256:], self.ones)], axis=1)
        qn = zq * lax.rsqrt(ssq * (1.0 / HEAD_DIM) + RMS_EPS) * self.qg
        rope4 = (jnp.concatenate([t] * 4, axis=1) for t in self.rope)
        return (_rotary(qn, *rope4) * (1.0 / math.sqrt(HEAD_DIM))).astype(BF16)

    def keys_values(self):
        zkv = _dot(self.hb, self.w_ref[:, K_OFF:U_OFF])
        zk = zkv[:, :KV_W]
        ssk = _dot((zk * zk).astype(BF16), self.ones[:KV_W, :KV_W])
        kn = zk * lax.rsqrt(ssk * (1.0 / HEAD_DIM) + RMS_EPS) * self.kg
        return _rotary(kn, *self.rope), zkv[:, KV_W:]

    def ssm_input(self):
        return _dot(self.hb, self.w_ref[:, U_OFF:MQ_OFF])

    def memory_queries(self):
        zm = _dot(self.hb, self.w_ref[:, MQ_OFF:G_OFF])
        mqg = self.mqg * (1.0 / math.sqrt(MEM_HEAD_DIM))
        return jnp.concatenate([_rms(zm[:, hd * MEM_HEAD_DIM:(hd + 1) * MEM_HEAD_DIM]) * mqg
                                for hd in range(MEM_HEADS)], axis=1).astype(BF16)

    def gates(self, c):
        lo = G_OFF + c * GATE_CHUNK
        return jax.nn.sigmoid(_dot(self.hb, self.w_ref[:, lo:lo + GATE_CHUNK])).astype(BF16)


ATTN_TILE = 512


def _softmax_rows(s):
    m = jnp.max(s, axis=-1, keepdims=True)
    p = jnp.exp(s - m)
    return p, jnp.sum(p, axis=-1, keepdims=True)


def _sink_column(sink_ref, l, heads, rows_per_head):
    n = len(heads) * rows_per_head
    rcol = lax.broadcasted_iota(jnp.int32, (n, 1), 0)
    sink = jnp.full((n, 1), sink_ref[l, heads[-1]], F32)
    for g in range(len(heads) - 2, -1, -1):
        sink = jnp.where(rcol < (g + 1) * rows_per_head, sink_ref[l, heads[g]], sink)
    return sink


def _sink_softmax(s, sink):
    m = jnp.maximum(jnp.max(s, axis=-1, keepdims=True), sink)
    p = jnp.exp(s - m)
    return p, jnp.sum(p, axis=-1, keepdims=True) + jnp.exp(sink - m)


def _attend_sub_block(sub, mask, sinks, q_ref, mq_ref, k_ref, v_ref, mk_ref, mv_ref, oa_ref, oc_ref):
    qrows = slice(sub * WINDOW, (sub + 1) * WINDOW)
    krows = slice(sub * WINDOW, (sub + 2) * WINDOW)
    ksl = [slice(kvh * HEAD_DIM, (kvh + 1) * HEAD_DIM) for kvh in range(N_KV_HEADS)]
    msl = [slice(hd * MEM_HEAD_DIM, (hd + 1) * MEM_HEAD_DIM) for hd in range(MEM_HEADS)]
    heads = [[kvh * KV_GROUP + g for g in range(KV_GROUP)] for kvh in range(N_KV_HEADS)]
    scores = []
    for kvh in range(N_KV_HEADS):
        qg = jnp.concatenate([q_ref[qrows, hd * HEAD_DIM:(hd + 1) * HEAD_DIM] for hd in heads[kvh]], axis=0)
        scores.append(jnp.where(mask, _dot_nt(qg, k_ref[krows, ksl[kvh]]), -jnp.inf))
    mscores = [_dot_nt(mq_ref[qrows, sl], mk_ref[:, sl]) for sl in msl]
    yield
    probs = [_sink_softmax(scores[kvh], sinks[kvh]) for kvh in range(N_KV_HEADS)]
    mprobs = [_softmax_rows(s) for s in mscores]
    yield
    outs = [_dot(probs[kvh][0].astype(BF16), v_ref[krows, ksl[kvh]]) / probs[kvh][1]
            for kvh in range(N_KV_HEADS)]
    mouts = [_dot(mprobs[hd][0].astype(BF16), mv_ref[:, msl[hd]]) / mprobs[hd][1] for hd in range(MEM_HEADS)]
    for kvh in range(N_KV_HEADS):
        for g, hd in enumerate(heads[kvh]):
            oa_ref[qrows, hd * HEAD_DIM:(hd + 1) * HEAD_DIM] = outs[kvh][g * WINDOW:(g + 1) * WINDOW].astype(BF16)
    for hd in range(MEM_HEADS):
        oc_ref[qrows, msl[hd]] = mouts[hd].astype(BF16)


def _inproj_attn_kernel(l, tiles_per_seq, sink_ref, x_ref, xs_ref, g_ref, w_ref, cos_ref, slo_ref, shi_ref,
                        qg_ref, kg_ref, mqg_ref, ones_ref, mk_ref, mv_ref,
                        k_ref, v_ref, u_ref, gates_ref, oa_ref, oc_ref, qs_ref, mqs_ref,
                        q_scr, mq_scr, k_scr, v_scr, mk_scr, mv_scr):
    i = pl.program_id(0)
    cur, prv = i % 2, (i + 1) % 2
    decode_step = i == pl.num_programs(0) - 1

    @pl.when(i == 0)
    def _():
        for scr in (q_scr, mq_scr, k_scr, v_scr, mk_scr, mv_scr):
            scr[...] = jnp.zeros(scr.shape, scr.dtype)

    @pl.when(i % tiles_per_seq == 1)
    def _():
        mk_scr[...] = mk_ref[...].astype(BF16)
        mv_scr[...] = mv_ref[...].astype(BF16)

    rows = KV_GROUP * WINDOW
    r = lax.broadcasted_iota(jnp.int32, (rows, 2 * WINDOW), 0)
    j = lax.broadcasted_iota(jnp.int32, (rows, 2 * WINDOW), 1)
    qi = r % WINDOW
    band = (j > qi) & (j <= qi + WINDOW)
    first_band = band & ((j >= WINDOW) | (i % tiles_per_seq != 1))
    sinks = [_sink_column(sink_ref, l, [kvh * KV_GROUP + g for g in range(KV_GROUP)], WINDOW)
             for kvh in range(N_KV_HEADS)]

    def attention_stages():
        for sub in range(ATTN_TILE // WINDOW):
            yield from _attend_sub_block(sub, first_band if sub == 0 else band, sinks, q_scr.at[prv],
                                         mq_scr.at[prv], k_scr.at[prv], v_scr.at[prv], mk_scr, mv_scr,
                                         oa_ref, oc_ref)
            yield

    proj = _Projection(jnp.where(decode_step, xs_ref[...], x_ref[...]), g_ref[l:l + 1, :], w_ref,
                       (cos_ref[...], slo_ref[...], shi_ref[...]), qg_ref[l:l + 1, :], kg_ref[l:l + 1, :],
                       mqg_ref[l:l + 1, :], ones_ref[...])

    def store_queries():
        q = proj.queries()
        q_scr[cur] = q
        qs_ref[...] = q

    def store_keys_values():
        k, v = proj.keys_values()
        k_ref[...], v_ref[...] = k, v
        k_scr[cur, :WINDOW, :] = k_scr[prv, ATTN_TILE:, :]
        v_scr[cur, :WINDOW, :] = v_scr[prv, ATTN_TILE:, :]
        k_scr[cur, WINDOW:, :] = k.astype(BF16)
        v_scr[cur, WINDOW:, :] = v.astype(BF16)

    def store_ssm_input():
        u_ref[...] = proj.ssm_input()

    def store_memory_queries():
        mq = proj.memory_queries()
        mq_scr[cur] = mq
        mqs_ref[...] = mq

    def store_gates(c):
        gates_ref[:, c * GATE_CHUNK:(c + 1) * GATE_CHUNK] = proj.gates(c)

    segments = [store_queries, store_keys_values, store_ssm_input, store_memory_queries]
    segments += [functools.partial(store_gates, c) for c in range(N_GATE_CHUNKS)]
    stages = attention_stages()
    n_stages = 3 * (ATTN_TILE // WINDOW)
    for n, segment in enumerate(segments):
        for _ in range((n + 1) * n_stages // len(segments) - n * n_stages // len(segments)):
            next(stages)
        segment()


def _inproj_attn(l, sinks, x2d, xs2d, seq_len, g, w_bf, rope, qg, kg, mqg, ones, mk, mv):
    n = x2d.shape[0]
    n_tiles = n // ROW_TILE
    tiles_per_seq = seq_len // ROW_TILE
    assert ROW_TILE == ATTN_TILE and tiles_per_seq > 1 and xs2d.shape[0] == ROW_TILE
    attn_tile = lambda i: jnp.maximum(i - 1, 0)
    xrow = pl.BlockSpec((ROW_TILE, D_MODEL), lambda i: (jnp.minimum(i, n_tiles - 1), 0))
    orow = lambda w: pl.BlockSpec((ROW_TILE, w), lambda i: (i, 0))
    arow = pl.BlockSpec((ROW_TILE, BRANCH_W), lambda i: (attn_tile(i), 0))
    rspec = pl.BlockSpec((ROW_TILE, LANES), lambda i: (jnp.where(i == n_tiles, tiles_per_seq, i % tiles_per_seq), 0))
    mem = pl.BlockSpec((None, N_MEM, BRANCH_W), lambda i: (attn_tile(i) // tiles_per_seq, 0, 0))
    widths = (KV_W, KV_W, BRANCH_W, 3 * D_MODEL)
    dtypes = (F32, F32, F32, BF16)
    return pl.pallas_call(
        functools.partial(_inproj_attn_kernel, l, tiles_per_seq), name="inproj_attn",
        grid=(n_tiles + 1,),
        in_specs=[pl.BlockSpec(memory_space=pltpu.SMEM), xrow, _full((ROW_TILE, D_MODEL)), _full(g.shape),
                  _layer((D_MODEL, IN_W), l), rspec, rspec, rspec, _full(qg.shape), _full(kg.shape),
                  _full(mqg.shape), _full((256, 256)), mem, mem],
        out_specs=[orow(w) for w in widths] + [arow, arow] + [_full((ROW_TILE, BRANCH_W))] * 2,
        out_shape=[jax.ShapeDtypeStruct((n + ROW_TILE, w), d) for w, d in zip(widths, dtypes)]
                  + [jax.ShapeDtypeStruct((n, BRANCH_W), BF16)] * 2
                  + [jax.ShapeDtypeStruct((ROW_TILE, BRANCH_W), BF16)] * 2,
        scratch_shapes=[pltpu.VMEM((2, ROW_TILE, BRANCH_W), BF16)] * 2
                       + [pltpu.VMEM((2, WINDOW + ROW_TILE, KV_W), BF16)] * 2
                       + [pltpu.VMEM((N_MEM, BRANCH_W), BF16)] * 2,
        compiler_params=_cparams(1),
    )(sinks, x2d, xs2d, g, w_bf, *rope, qg, kg, mqg, ones, mk, mv)


DEC_BLOCK = 8
TOK_PAD = 8


def _stack_heads(x, n_heads, head_w):
    t, w = x.shape
    x8 = jnp.concatenate([x, jnp.zeros((TOK_PAD - t, w), F32)], axis=0)
    rr = lax.broadcasted_iota(jnp.int32, (n_heads * TOK_PAD, w), 0)
    ll = lax.broadcasted_iota(jnp.int32, (n_heads * TOK_PAD, w), 1)
    own = (ll // head_w) == (rr // TOK_PAD)
    return jnp.where(own, jnp.concatenate([x8] * n_heads, axis=0), 0.0).astype(BF16), own


def _unstack_heads(o, own, n_heads, t):
    o = jnp.where(own, o, 0.0)
    acc = o[0:TOK_PAD]
    for hd in range(1, n_heads):
        acc = acc + o[hd * TOK_PAD:(hd + 1) * TOK_PAD]
    return acc[0:t]


def _attn_sample_kernel(l, t, sink_ref, q_ref, kn_ref, vn_ref, ck_ref, cv_ref, qm_ref, mk_ref, mv_ref, e_ref,
                        oa_ref, oc_ref, nk_ref, nv_ref, kc_scr, vc_scr):
    keys = 2 * WINDOW
    rows = N_HEADS * TOK_PAD
    rk = lax.broadcasted_iota(jnp.int32, (rows, keys), 0)
    jk = lax.broadcasted_iota(jnp.int32, (rows, keys), 1)
    tk = rk % TOK_PAD
    mask = (tk < t) & (((jk < WINDOW) & (jk > tk)) | ((jk >= WINDOW) & (jk - WINDOW <= tk)))
    sink = _sink_column(sink_ref, l, list(range(N_HEADS)), TOK_PAD)
    mem_rows = N_MEM * MEM_HEADS
    mem_own = (lax.broadcasted_iota(jnp.int32, (MEM_HEADS * TOK_PAD, mem_rows), 1) % MEM_HEADS
               == lax.broadcasted_iota(jnp.int32, (MEM_HEADS * TOK_PAD, mem_rows), 0) // TOK_PAD)
    e = e_ref[...]

    @pl.when(pl.program_id(0) == 0)
    def _():
        kc_scr[:, WINDOW:, :] = jnp.zeros((DEC_BLOCK, WINDOW, KV_W), F32)
        vc_scr[:, WINDOW:, :] = jnp.zeros((DEC_BLOCK, WINDOW, KV_W), F32)

    q_all, qm_all = q_ref[...].astype(F32), qm_ref[...].astype(F32)
    seqs = range(DEC_BLOCK)
    toks = [slice(bi * t, (bi + 1) * t) for bi in seqs]
    for bi in seqs:
        kc_scr[bi, :WINDOW, :] = ck_ref[bi]
        vc_scr[bi, :WINDOW, :] = cv_ref[bi]
        kc_scr[bi, WINDOW:WINDOW + t, :] = kn_ref[toks[bi], :]
        vc_scr[bi, WINDOW:WINDOW + t, :] = vn_ref[toks[bi], :]
    for bi in seqs:
        nk_ref[bi] = kc_scr[bi, t:WINDOW + t, :]
        nv_ref[bi] = vc_scr[bi, t:WINDOW + t, :]
    kx = [_dot(kc_scr[bi].astype(BF16), e).astype(BF16) for bi in seqs]
    vx = [_dot(vc_scr[bi].astype(BF16), e).astype(BF16) for bi in seqs]
    stacked = [_stack_heads(q_all[toks[bi]], N_HEADS, HEAD_DIM) for bi in seqs]
    scores = [jnp.where(mask, _dot_nt(stacked[bi][0], kx[bi]), -jnp.inf) for bi in seqs]
    mscores = []
    for bi in seqs:
        qm8 = jnp.concatenate([qm_all[toks[bi]], jnp.zeros((TOK_PAD - t, BRANCH_W), F32)], axis=0).astype(BF16)
        qst = jnp.concatenate([qm8[:, hd * MEM_HEAD_DIM:(hd + 1) * MEM_HEAD_DIM] for hd in range(MEM_HEADS)],
                              axis=0)
        mscores.append(jnp.where(mem_own, _dot_nt(qst, mk_ref[bi].astype(BF16)), -jnp.inf))
    probs = [_sink_softmax(scores[bi], sink) for bi in seqs]
    mprobs = [_softmax_rows(mscores[bi]) for bi in seqs]
    outs = [_dot(probs[bi][0].astype(BF16), vx[bi]) / probs[bi][1] for bi in seqs]
    mouts = [_dot(mprobs[bi][0].astype(BF16), mv_ref[bi].astype(BF16)) / mprobs[bi][1] for bi in seqs]
    for bi in seqs:
        oa_ref[toks[bi], :] = _unstack_heads(outs[bi], stacked[bi][1], N_HEADS, t)
        oc_ref[toks[bi], :] = jnp.concatenate(
            [mouts[bi][hd * TOK_PAD:(hd + 1) * TOK_PAD] for hd in range(MEM_HEADS)], axis=1)[0:t]


def _attn_sample(l, t, sinks, q, kn, vn, kv_row0, ck, cv, qm, mk, mv, e):
    n = q.shape[0]
    blk = lambda i: (i, 0, 0)
    lblk = lambda i: (l, i, 0, 0)
    tok = lambda w, row0=0: pl.BlockSpec((DEC_BLOCK * t, w), lambda i: (i + row0 // (DEC_BLOCK * t), 0))
    cache_in = pl.BlockSpec((None, DEC_BLOCK, WINDOW, KV_W), lblk)
    cache_out = pl.BlockSpec((DEC_BLOCK, WINDOW, KV_W), blk)
    mem = pl.BlockSpec((None, DEC_BLOCK, N_MEM * MEM_HEADS, MEM_HEAD_DIM), lblk)
    return pl.pallas_call(
        functools.partial(_attn_sample_kernel, l, t), name="attn_sample",
        grid=(n // (DEC_BLOCK * t),),
        in_specs=[pl.BlockSpec(memory_space=pltpu.SMEM), tok(BRANCH_W), tok(KV_W, kv_row0), tok(KV_W, kv_row0),
                  cache_in, cache_in, tok(BRANCH_W), mem, mem, _full((KV_W, BRANCH_W))],
        out_specs=[tok(BRANCH_W), tok(BRANCH_W), cache_out, cache_out],
        out_shape=[jax.ShapeDtypeStruct((n, BRANCH_W), F32)] * 2
                  + [jax.ShapeDtypeStruct((n // t, WINDOW, KV_W), F32)] * 2,
        scratch_shapes=[pltpu.VMEM((DEC_BLOCK, 2 * WINDOW, KV_W), F32)] * 2,
        compiler_params=_cparams(1),
    )(sinks, q, kn, vn, ck, cv, qm, mk, mv, e)


def _ssm_tables(a_re, a_im, log_dt, b_re, b_im, c_re, c_im, chunk):
    hp = lax.Precision.HIGHEST
    dt = jnp.exp(log_dt)[..., None]
    mag = jnp.exp(a_re * dt)
    lam_re, lam_im = mag * jnp.cos(a_im * dt), mag * jnp.sin(a_im * dt)
    den = a_re * a_re + a_im * a_im
    nr, ni = lam_re - 1.0, lam_im
    g_re = (nr * a_re + ni * a_im) / den
    g_im = (ni * a_re - nr * a_im) / den
    bg_re = g_re[..., None] * b_re - g_im[..., None] * b_im
    bg_im = g_re[..., None] * b_im + g_im[..., None] * b_re
    def powers(d):
        d = d.astype(F32)[:, None, None, None]
        pm = jnp.exp(d * (a_re * dt)[None])
        return pm * jnp.cos(d * (a_im * dt)[None]), pm * jnp.sin(d * (a_im * dt)[None])

    def times_gb(p_re, p_im):
        return (p_re[..., None] * bg_re[None] - p_im[..., None] * bg_im[None],
                p_re[..., None] * bg_im[None] + p_im[..., None] * bg_re[None])

    pw_re, pw_im = powers(jnp.arange(chunk + 1))
    z_re, z_im = times_gb(pw_re[:chunk], pw_im[:chunk])
    zrev_re, zrev_im = times_gb(*powers(jnp.arange(chunk - 1, -1, -1)))
    kern = (jnp.einsum('lgop,dlgpi->dlgio', c_re, z_re, precision=hp)
            - jnp.einsum('lgop,dlgpi->dlgio', c_im, z_im, precision=hp))
    nl = a_re.shape[0]
    nb, gb = N_LANE_BLOCKS, GROUPS_PER_LANE_BLOCK
    kd = jnp.transpose(kern.reshape(chunk, nl, nb, gb, SSM_GROUP_CH, SSM_GROUP_CH), (1, 2, 0, 3, 4, 5))
    kd = kd.reshape(nl, nb, chunk, LANES, SSM_GROUP_CH)

    def to_rows(z):
        z = z.reshape(chunk, nl, nb, gb, SSM_STATE, SSM_GROUP_CH)
        return jnp.transpose(z, (1, 2, 0, 3, 5, 4)).reshape(nl, nb, chunk, LANES, SSM_STATE)

    zr, zi = to_rows(zrev_re), to_rows(zrev_im)
    zc = jnp.concatenate([zr, zr, zi, zi], axis=-1)

    pr, pi = pw_re[1:], pw_im[1:]
    f_a = c_re[None] * pr[:, :, :, None, :] - c_im[None] * pi[:, :, :, None, :]
    f_b = c_re[None] * pi[:, :, :, None, :] + c_im[None] * pr[:, :, :, None, :]

    def from_cols(f):
        f = f.reshape(chunk, nl, nb, gb, SSM_GROUP_CH, SSM_STATE)
        return jnp.transpose(f, (1, 2, 0, 4, 3, 5)).reshape(nl, nb, chunk, SSM_GROUP_CH, STATE_W)

    ft = jnp.concatenate([from_cols(f_a), from_cols(-f_b)], axis=-1)

    kw = chunk * LANES
    spec = lambda *shape: pl.BlockSpec((None, None) + shape, lambda l, n: (l, n) + (0,) * len(shape))
    intra, to_st, from_st_t = pl.pallas_call(
        _ssm_table_kernel, name="ssm_tables",
        grid=(nl, nb),
        in_specs=[spec(chunk, LANES, SSM_GROUP_CH), spec(chunk, LANES, 2 * LANES),
                  spec(chunk, SSM_GROUP_CH, 2 * STATE_W)],
        out_specs=[spec(kw, kw), spec(kw, 2 * STATE_W), spec(kw, 2 * STATE_W)],
        out_shape=[jax.ShapeDtypeStruct((nl, nb, kw, kw), BF16),
                   jax.ShapeDtypeStruct((nl, nb, kw, 2 * STATE_W), BF16),
                   jax.ShapeDtypeStruct((nl, nb, kw, 2 * STATE_W), BF16)],
        compiler_params=_cparams(2),
    )(kd, zc, ft)
    decay = lambda d: (pw_re[d].reshape(nl, nb, 1, STATE_W), pw_im[d].reshape(nl, nb, 1, STATE_W))
    return (intra, to_st, from_st_t), decay


def _ssm_table_kernel(kd_ref, zc_ref, ft_ref, intra_ref, to_ref, fromt_ref):
    chunk = kd_ref.shape[0]
    gch = SSM_GROUP_CH
    r = lax.broadcasted_iota(jnp.int32, (LANES, LANES), 0)
    c = lax.broadcasted_iota(jnp.int32, (LANES, LANES), 1)
    same_group = (r // gch) == (c // gch)
    spread = (lax.broadcasted_iota(jnp.int32, (gch, LANES), 1) % gch
              == lax.broadcasted_iota(jnp.int32, (gch, LANES), 0)).astype(BF16)
    spread_t = (lax.broadcasted_iota(jnp.int32, (LANES, gch), 0) % gch
                == lax.broadcasted_iota(jnp.int32, (LANES, gch), 1)).astype(BF16)
    rs = lax.broadcasted_iota(jnp.int32, (LANES, 2 * STATE_W), 0)
    cs = lax.broadcasted_iota(jnp.int32, (LANES, 2 * STATE_W), 1)
    own_state = (rs // gch) == ((cs % STATE_W) // SSM_STATE)

    lag_blocks = [jnp.where(same_group, _dot(kd_ref[d].astype(BF16), spread), 0.0).astype(BF16)
                  for d in range(chunk)]
    zero = jnp.zeros((LANES, LANES), BF16)
    for s in range(chunk):
        for t in range(chunk):
            intra_ref[s * LANES:(s + 1) * LANES, t * LANES:(t + 1) * LANES] = (
                lag_blocks[t - s] if t >= s else zero)
        z = zc_ref[s]
        full = jnp.concatenate([z[:, :LANES]] * (STATE_W // LANES) + [z[:, LANES:]] * (STATE_W // LANES), axis=1)
        to_ref[s * LANES:(s + 1) * LANES, :] = jnp.where(own_state, full, 0.0).astype(BF16)
        f = _dot(spread_t, ft_ref[s].astype(BF16))
        fromt_ref[s * LANES:(s + 1) * LANES, :] = jnp.where(own_state, f, 0.0).astype(BF16)


SSM_SEQS = 2


def _chunk_rows(ref, chunk):
    n = ref.shape[0] // chunk
    return jnp.concatenate([ref[pl.ds(s, n, stride=chunk), :] for s in range(chunk)], axis=1)


def _store_chunk_rows(ref, y, chunk):
    n = ref.shape[0] // chunk
    for t in range(chunk):
        ref[pl.ds(t, n, stride=chunk), :] = y[:, t * LANES:(t + 1) * LANES]


def _ssm_prompt_kernel(u_ref, intra_ref, to_ref, fromt_ref, dre_ref, dim_ref,
                       y_ref, hre_ref, him_ref, st_scr):
    chunk = PROMPT_CHUNK
    nchunks = u_ref.shape[0] // (chunk * SSM_SEQS)
    x = _chunk_rows(u_ref, chunk).astype(BF16)
    kw = chunk * LANES
    y = jnp.concatenate([_dot(x[:, :c + MXU_TILE], intra_ref[:c + MXU_TILE, c:c + MXU_TILE])
                         for c in range(0, kw, MXU_TILE)], axis=1)
    add_all = _dot(x, to_ref[...])
    nblk = STATE_W // LANES
    assert SSM_SEQS * nblk == SUBLANES
    slots = [(seq, cb) for seq in range(SSM_SEQS) for cb in range(nblk)]

    def slot_rows(seq, cb):
        return pl.ds(seq * nblk + cb, nchunks, stride=SUBLANES)

    for c in range(2):
        for seq, cb in slots:
            col = (c * nblk + cb) * LANES
            st_scr[c, slot_rows(seq, cb), :] = add_all[seq * nchunks:(seq + 1) * nchunks, col:col + LANES]
    dre = jnp.concatenate([dre_ref[:, cb * LANES:(cb + 1) * LANES] for _, cb in slots], axis=0)
    dim = jnp.concatenate([dim_ref[:, cb * LANES:(cb + 1) * LANES] for _, cb in slots], axis=0)

    def step(k, carry):
        hr, hi = carry
        tile = pl.ds(pl.multiple_of(k * SUBLANES, SUBLANES), SUBLANES)
        add_r, add_i = st_scr[0, tile, :], st_scr[1, tile, :]
        st_scr[0, tile, :] = hr
        st_scr[1, tile, :] = hi
        return dre * hr - dim * hi + add_r, dre * hi + dim * hr + add_i

    zero = jnp.zeros((SUBLANES, LANES), F32)
    hr, hi = lax.fori_loop(0, nchunks, step, (zero, zero), unroll=8)
    hre_ref[...] = hr
    him_ref[...] = hi
    h_in = jnp.concatenate(
        [jnp.concatenate([st_scr[c, slot_rows(seq, cb), :] for seq in range(SSM_SEQS)], axis=0)
         for c in range(2) for cb in range(nblk)], axis=1).astype(BF16)
    _store_chunk_rows(y_ref, y + _dot_nt(h_in, fromt_ref[...]), chunk)


def _ssm_prompt(l, u2d, n_seqs, seq_len, intra, to_st, from_st_t, dec_re, dec_im):
    n = n_seqs * seq_len
    chunk = PROMPT_CHUNK
    rows = SSM_SEQS * seq_len
    steps = n // rows
    kw = chunk * LANES
    data = pl.BlockSpec((rows, LANES), lambda j, h: (h, j))
    wspec = lambda r, c: pl.BlockSpec((None, None, r, c), lambda j, h: (l, j, 0, 0))
    st_out = pl.BlockSpec((None, None, SUBLANES, LANES), lambda j, h: (j, h, 0, 0))
    st_shape = jax.ShapeDtypeStruct((N_LANE_BLOCKS, steps, SUBLANES, LANES), F32)
    return pl.pallas_call(
        _ssm_prompt_kernel, name="ssm_prompt",
        grid=(N_LANE_BLOCKS, steps),
        in_specs=[data, wspec(kw, kw), wspec(kw, 2 * STATE_W), wspec(kw, 2 * STATE_W),
                  wspec(1, STATE_W), wspec(1, STATE_W)],
        out_specs=[data, st_out, st_out],
        out_shape=[jax.ShapeDtypeStruct((n, BRANCH_W), F32), st_shape, st_shape],
        scratch_shapes=[pltpu.VMEM((2, SUBLANES * seq_len // chunk, LANES), F32)],
        compiler_params=_cparams(2),
    )(u2d, intra, to_st, from_st_t, dec_re, dec_im)


def _ssm_sample_kernel(chunk, u_ref, h0r_ref, h0i_ref, intra_ref, to_ref, fromt_ref, dre_ref, dim_ref,
                       y_ref, hre_ref, him_ref):
    x = _chunk_rows(u_ref, chunk).astype(BF16)
    hr, hi = h0r_ref[...], h0i_ref[...]
    dre, dim = dre_ref[...], dim_ref[...]
    add = _dot(x, to_ref[...])
    hre_ref[...] = dre * hr - dim * hi + add[:, :STATE_W]
    him_ref[...] = dre * hi + dim * hr + add[:, STATE_W:]
    h0 = jnp.concatenate([hr, hi], axis=1).astype(BF16)
    _store_chunk_rows(y_ref, _dot(x, intra_ref[...]) + _dot_nt(h0, fromt_ref[...]), chunk)


def _ssm_sample(l, u2d, row0, n, t, h0_re, h0_im, intra, to_st, from_st_t, dec_re, dec_im):
    b = n // t
    kw = t * LANES
    assert PROMPT_CHUNK % t == 0 and row0 % n == 0
    last = PROMPT_CHUNK // t - 1
    data_in = pl.BlockSpec((n, LANES), lambda j: (row0 // n, j))
    data = pl.BlockSpec((n, LANES), lambda j: (0, j))
    st_in = pl.BlockSpec((None, b, STATE_W), lambda j: (l, 0, j))
    st_out = pl.BlockSpec((b, STATE_W), lambda j: (0, j))
    wspec = lambda r, c, rb=0: pl.BlockSpec((None, None, r, c), lambda j: (l, j, rb, 0))
    return pl.pallas_call(
        functools.partial(_ssm_sample_kernel, t), name="ssm_sample",
        grid=(N_LANE_BLOCKS,),
        in_specs=[data_in, st_in, st_in, wspec(kw, kw), wspec(kw, 2 * STATE_W, last), wspec(kw, 2 * STATE_W),
                  wspec(1, STATE_W), wspec(1, STATE_W)],
        out_specs=[data, st_out, st_out],
        out_shape=[jax.ShapeDtypeStruct((n, BRANCH_W), F32),
                   jax.ShapeDtypeStruct((b, N_LANE_BLOCKS * STATE_W), F32),
                   jax.ShapeDtypeStruct((b, N_LANE_BLOCKS * STATE_W), F32)],
        compiler_params=_cparams(1),
    )(u2d, h0_re, h0_im, intra, to_st, from_st_t, dec_re, dec_im)


FF_CHUNKS = ((0, 512), (512, 512), (1024, 512), (1536, 512), (2048, 512), (2560, 256))


def _merge_ffn_kernel(l, x_ref, oa_ref, y_ref, u_ref, oc_ref, gates_ref, d_ref, wglu_ref, wb_ref, wout_ref,
                      g_ref, wup_ref, wdn_ref, o_ref):
    z = jax.nn.gelu(y_ref[...] + d_ref[l:l + 1, :] * u_ref[...])
    ob = (z * jax.nn.sigmoid(_dot(z.astype(BF16), wglu_ref[...]))).astype(BF16)
    gate = lambda n: gates_ref[:, n * D_MODEL:(n + 1) * D_MODEL].astype(F32)
    merged = gate(0) * _dot(oa_ref[...], wb_ref[0])
    merged = merged + gate(1) * _dot(ob, wb_ref[1])
    merged = merged + gate(2) * _dot(oc_ref[...], wb_ref[2])
    x = x_ref[...] + _dot(merged.astype(BF16), wout_ref[...])
    h = (_rms(x) * g_ref[l:l + 1, :]).astype(BF16)
    acc = x
    for lo, w in FF_CHUNKS:
        up_gate = _dot(h, wup_ref[:, lo:lo + w])
        up = _dot(h, wup_ref[:, D_FF + lo:D_FF + lo + w])
        acc = acc + _dot((jax.nn.silu(up_gate) * up).astype(BF16), wdn_ref[lo:lo + w, :])
    o_ref[...] = acc


def _merge_ffn(l, x2d, oa, y, u, oc, gates, proj_row0, d, wglu, wb, wout, g_ffn, wup, wdn):
    n = x2d.shape[0]
    assert proj_row0 % ROW_TILE == 0
    row = lambda w, row0=0: pl.BlockSpec((ROW_TILE, w), lambda i: (i + row0 // ROW_TILE, 0))
    return pl.pallas_call(
        functools.partial(_merge_ffn_kernel, l), name="merge_ffn",
        grid=(n // ROW_TILE,),
        in_specs=[row(D_MODEL), row(BRANCH_W), row(BRANCH_W), row(BRANCH_W, proj_row0), row(BRANCH_W),
                  row(3 * D_MODEL, proj_row0), _full(d.shape), _layer((BRANCH_W, BRANCH_W), l),
                  _layer((3, BRANCH_W, D_MODEL), l), _layer((D_MODEL, D_MODEL), l),
                  _full(g_ffn.shape), _layer((D_MODEL, 2 * D_FF), l), _layer((D_FF, D_MODEL), l)],
        out_specs=row(D_MODEL),
        out_shape=jax.ShapeDtypeStruct((n, D_MODEL), F32),
        compiler_params=_cparams(1),
    )(x2d, oa, y, u, oc, gates, d, wglu, wb, wout, g_ffn, wup, wdn)


def _rope_tables(pos):
    half = HEAD_DIM // 2
    inv = ROPE_THETA ** (-jnp.arange(half, dtype=F32) / half)
    ang = pos[:, None] * inv[None, :]
    cos, sin = jnp.cos(ang), jnp.sin(ang)
    zero = jnp.zeros_like(sin)
    two = lambda a: jnp.concatenate([a, a], axis=1)
    return (two(jnp.concatenate([cos, cos], axis=1)), two(jnp.concatenate([-sin, zero], axis=1)),
            two(jnp.concatenate([zero, sin], axis=1)))


def _state_from_blocks(h):
    nb, steps, _, _ = h.shape
    h = h.reshape(nb, steps, SSM_SEQS, STATE_W // LANES, LANES)
    return jnp.transpose(h, (1, 2, 0, 3, 4)).reshape(steps * SSM_SEQS, SSM_GROUPS, SSM_STATE)


def kernel(x_prompt, x_sample, cache_swa_k, cache_swa_v, state_ssm_re, state_ssm_im, cache_mem_k,
           cache_mem_v, mem_prompt, attn_norm, w_in, q_norm, k_norm, attn_sinks, ssm_a_re, ssm_a_im,
           ssm_log_dt, ssm_b_re, ssm_b_im, ssm_c_re, ssm_c_im, ssm_d, ssm_w_glu, mem_norm, w_mem_kv,
           mem_q_norm, mem_k_norm, w_branch, w_out, ffn_norm, w_ffn_up, w_ffn_down):
    depth = w_in.shape[0]
    b, s, _ = x_prompt.shape
    db, t, _ = x_sample.shape
    assert db * t == ROW_TILE and s % ROW_TILE == 0

    rope = _rope_tables(jnp.concatenate([jnp.arange(s, dtype=F32),
                                         jnp.tile(PAST_LEN + jnp.arange(t, dtype=F32), db)]))
    ssm_params = (ssm_a_re, ssm_a_im, ssm_log_dt, ssm_b_re, ssm_b_im, ssm_c_re, ssm_c_im)
    tables, decay = _ssm_tables(*ssm_params, PROMPT_CHUNK)
    tab_p, tab_s = tables + decay(PROMPT_CHUNK), tables + decay(t)
    blk = jnp.arange(256) // HEAD_DIM
    ones = (blk[:, None] == blk[None, :]).astype(BF16)
    lane = jnp.arange(BRANCH_W)[None, :]
    src = jnp.arange(KV_W)[:, None]
    expand = ((lane // (HEAD_DIM * KV_GROUP) == src // HEAD_DIM)
              & (lane % HEAD_DIM == src % HEAD_DIM)).astype(BF16)

    w_in_bf, w_mem_bf = w_in.astype(BF16), w_mem_kv.astype(BF16)
    wglu, wb, wout = ssm_w_glu.astype(BF16), w_branch.astype(BF16), w_out.astype(BF16)
    wup, wdn = w_ffn_up.astype(BF16), w_ffn_down.astype(BF16)
    qg, kg = jnp.tile(q_norm, (1, N_HEADS)), jnp.tile(k_norm, (1, N_KV_HEADS))
    ck = cache_swa_k.reshape(depth, db, WINDOW, KV_W)
    cv = cache_swa_v.reshape(depth, db, WINDOW, KV_W)
    cmk = cache_mem_k.reshape(depth, db, N_MEM * MEM_HEADS, MEM_HEAD_DIM)
    cmv = cache_mem_v.reshape(depth, db, N_MEM * MEM_HEADS, MEM_HEAD_DIM)
    h0r = state_ssm_re.reshape(depth, db, SSM_GROUPS * SSM_STATE)
    h0i = state_ssm_im.reshape(depth, db, SSM_GROUPS * SSM_STATE)

    yp = x_prompt.reshape(b * s, D_MODEL)
    ys = x_sample.reshape(db * t, D_MODEL)
    mem2d = mem_prompt.reshape(b * N_MEM, D_MODEL)
    outs = [[] for _ in range(10)]
    n = b * s
    for l in range(depth):
        mk, mv = _memkv(l, mem2d, mem_norm, w_mem_bf, mem_k_norm)
        mk3, mv3 = mk.reshape(b, N_MEM, BRANCH_W), mv.reshape(b, N_MEM, BRANCH_W)

        k, v, u, gates, oa, oc, qs, mqs = _inproj_attn(l, attn_sinks, yp, ys, s, attn_norm, w_in_bf, rope, qg, kg,
                                                       mem_q_norm, ones, mk3, mv3)
        ysm, hre, him = _ssm_prompt(l, u, b, s, *tab_p)
        yp = _merge_ffn(l, yp, oa, ysm, u, oc, gates, 0, ssm_d, wglu, wb, wout, ffn_norm, wup, wdn)

        oas, ocs, nk, nv = _attn_sample(l, t, attn_sinks, qs, k, v, n, ck, cv, mqs, cmk, cmv, expand)
        yss, hrs, his = _ssm_sample(l, u, n, db * t, t, h0r, h0i, *tab_s)
        ys = _merge_ffn(l, ys, oas.astype(BF16), yss, u, ocs.astype(BF16), gates, n, ssm_d, wglu, wb, wout,
                        ffn_norm, wup, wdn)

        last_rows = lambda a: jnp.stack([a[(bi + 1) * s - WINDOW:(bi + 1) * s] for bi in range(b)]).reshape(
            b, WINDOW, N_KV_HEADS, HEAD_DIM)
        new = (last_rows(k), last_rows(v),
               _state_from_blocks(hre), _state_from_blocks(him),
               mk3.reshape(b, N_MEM, MEM_HEADS, MEM_HEAD_DIM), mv3.reshape(b, N_MEM, MEM_HEADS, MEM_HEAD_DIM),
               nk.reshape(db, WINDOW, N_KV_HEADS, HEAD_DIM), nv.reshape(db, WINDOW, N_KV_HEADS, HEAD_DIM),
               hrs.reshape(db, SSM_GROUPS, SSM_STATE), his.reshape(db, SSM_GROUPS, SSM_STATE))
        for lst, a in zip(outs, new):
            lst.append(a)

    return (yp.reshape(b, s, D_MODEL), ys.reshape(db, t, D_MODEL)) + tuple(jnp.stack(o) for o in outs)
```

```python
import functools
import math

import jax
import jax.numpy as jnp
from jax import lax
from jax.experimental import pallas as pl
from jax.experimental.pallas import tpu as pltpu

F32 = jnp.float32
BF16 = jnp.bfloat16

D_MODEL = 1024
BRANCH_W = 512
HEAD_DIM = 64
N_HEADS = 8
N_KV_HEADS = 2
KV_GROUP = 4
KV_W = N_KV_HEADS * HEAD_DIM
WINDOW = 128
ROPE_THETA = 10000.0
PAST_LEN = 16384
SSM_GROUPS = 32
SSM_GROUP_CH = 16
SSM_STATE = 64
N_MEM = 256
MEM_HEADS = 4
MEM_HEAD_DIM = 128
D_FF = 2816
RMS_EPS = 1e-6

Q_OFF, K_OFF, V_OFF, U_OFF, MQ_OFF, G_OFF = 0, 512, 640, 768, 1280, 1792
IN_W = G_OFF + 3 * D_MODEL

LANES = 128
SUBLANES = 8
MXU_TILE = 256
GROUPS_PER_LANE_BLOCK = LANES // SSM_GROUP_CH
N_LANE_BLOCKS = BRANCH_W // LANES
STATE_W = GROUPS_PER_LANE_BLOCK * SSM_STATE
PROMPT_CHUNK = 8
ROW_TILE = 512
VMEM_LIMIT = 56 * 1024 * 1024


def _dot(a, b):
    return jnp.dot(a, b, preferred_element_type=F32)


def _dot_nt(a, b):
    return lax.dot_general(a, b, (((1,), (1,)), ((), ())), preferred_element_type=F32)


def _cparams(n_axes):
    return pltpu.CompilerParams(dimension_semantics=("arbitrary",) * n_axes,
                                vmem_limit_bytes=VMEM_LIMIT)


def _full(shape):
    nd = len(shape)
    return pl.BlockSpec(shape, lambda *_: (0,) * nd)


def _layer(shape, l):
    nd = len(shape)
    return pl.BlockSpec((None,) + tuple(shape), lambda *_: (l,) + (0,) * nd, pipeline_mode=pl.Buffered(1))


def _rms(x):
    return x * lax.rsqrt(jnp.mean(x * x, axis=-1, keepdims=True) + RMS_EPS)


def _memkv_kernel(l, mem_ref, g_ref, w_ref, kg_ref, k_ref, v_ref):
    h = _rms(mem_ref[...]) * g_ref[l:l + 1, :]
    kv = _dot(h.astype(BF16), w_ref[...])
    kg = kg_ref[l:l + 1, :]
    for hd in range(MEM_HEADS):
        sl = slice(hd * MEM_HEAD_DIM, (hd + 1) * MEM_HEAD_DIM)
        k_ref[:, sl] = _rms(kv[:, sl]) * kg
    v_ref[...] = kv[:, BRANCH_W:]


def _memkv(l, mem2d, g, w_bf, kg):
    n = mem2d.shape[0]
    return pl.pallas_call(
        functools.partial(_memkv_kernel, l), name="memkv",
        grid=(n // ROW_TILE,),
        in_specs=[pl.BlockSpec((ROW_TILE, D_MODEL), lambda i: (i, 0)),
                  _full(g.shape), _layer((D_MODEL, 2 * BRANCH_W), l), _full(kg.shape)],
        out_specs=[pl.BlockSpec((ROW_TILE, BRANCH_W), lambda i: (i, 0))] * 2,
        out_shape=[jax.ShapeDtypeStruct((n, BRANCH_W), F32)] * 2,
        compiler_params=_cparams(1),
    )(mem2d, g, w_bf, kg)


def _rotary(x, cos, sin_lo, sin_hi):
    w = x.shape[-1]
    half = HEAD_DIM // 2
    return x * cos + pltpu.roll(x, w - half, 1) * sin_lo + pltpu.roll(x, half, 1) * sin_hi


GATE_CHUNK = 512
N_GATE_CHUNKS = 3 * D_MODEL // GATE_CHUNK


class _Projection:
    def __init__(self, x, g, w_ref, rope, qg, kg, mqg, ones):
        self.hb = (_rms(x) * g).astype(BF16)
        self.w_ref, self.qg, self.kg, self.mqg = w_ref, qg, kg, mqg
        self.ones = ones
        self.rope = rope

    def queries(self):
        zq = _dot(self.hb, self.w_ref[:, Q_OFF:K_OFF])
        sq = (zq * zq).astype(BF16)
        ssq = jnp.concatenate([_dot(sq[:, :256], self.ones), _dot(sq[:, 256:], self.ones)], axis=1)
        qn = zq * lax.rsqrt(ssq * (1.0 / HEAD_DIM) + RMS_EPS) * self.qg
        rope4 = (jnp.concatenate([t] * 4, axis=1) for t in self.rope)
        return (_rotary(qn, *rope4) * (1.0 / math.sqrt(HEAD_DIM))).astype(BF16)

    def keys_values(self):
        zkv = _dot(self.hb, self.w_ref[:, K_OFF:U_OFF])
        zk = zkv[:, :KV_W]
        ssk = _dot((zk * zk).astype(BF16), self.ones[:KV_W, :KV_W])
        kn = zk * lax.rsqrt(ssk * (1.0 / HEAD_DIM) + RMS_EPS) * self.kg
        return _rotary(kn, *self.rope), zkv[:, KV_W:]

    def ssm_input(self):
        return _dot(self.hb, self.w_ref[:, U_OFF:MQ_OFF])

    def memory_queries(self):
        zm = _dot(self.hb, self.w_ref[:, MQ_OFF:G_OFF])
        mqg = self.mqg * (1.0 / math.sqrt(MEM_HEAD_DIM))
        return jnp.concatenate([_rms(zm[:, hd * MEM_HEAD_DIM:(hd + 1) * MEM_HEAD_DIM]) * mqg
                                for hd in range(MEM_HEADS)], axis=1).astype(BF16)

    def gates(self, c):
        lo = G_OFF + c * GATE_CHUNK
        return jax.nn.sigmoid(_dot(self.hb, self.w_ref[:, lo:lo + GATE_CHUNK])).astype(BF16)


ATTN_TILE = 512


def _softmax_rows(s):
    m = jnp.max(s, axis=-1, keepdims=True)
    p = jnp.exp(s - m)
    return p, jnp.sum(p, axis=-1, keepdims=True)


def _sink_column(sink_ref, l, heads, rows_per_head):
    n = len(heads) * rows_per_head
    rcol = lax.broadcasted_iota(jnp.int32, (n, 1), 0)
    sink = jnp.full((n, 1), sink_ref[l, heads[-1]], F32)
    for g in range(len(heads) - 2, -1, -1):
        sink = jnp.where(rcol < (g + 1) * rows_per_head, sink_ref[l, heads[g]], sink)
    return sink


def _sink_softmax(s, sink):
    m = jnp.maximum(jnp.max(s, axis=-1, keepdims=True), sink)
    p = jnp.exp(s - m)
    return p, jnp.sum(p, axis=-1, keepdims=True) + jnp.exp(sink - m)


def _attend_sub_block(sub, mask, sinks, q_ref, mq_ref, k_ref, v_ref, mk_ref, mv_ref, oa_ref, oc_ref):
    qrows = slice(sub * WINDOW, (sub + 1) * WINDOW)
    krows = slice(sub * WINDOW, (sub + 2) * WINDOW)
    ksl = [slice(kvh * HEAD_DIM, (kvh + 1) * HEAD_DIM) for kvh in range(N_KV_HEADS)]
    msl = [slice(hd * MEM_HEAD_DIM, (hd + 1) * MEM_HEAD_DIM) for hd in range(MEM_HEADS)]
    heads = [[kvh * KV_GROUP + g for g in range(KV_GROUP)] for kvh in range(N_KV_HEADS)]
    scores = []
    for kvh in range(N_KV_HEADS):
        qg = jnp.concatenate([q_ref[qrows, hd * HEAD_DIM:(hd + 1) * HEAD_DIM] for hd in heads[kvh]], axis=0)
        scores.append(jnp.where(mask, _dot_nt(qg, k_ref[krows, ksl[kvh]]), -jnp.inf))
    mscores = [_dot_nt(mq_ref[qrows, sl], mk_ref[:, sl]) for sl in msl]
    yield
    probs = [_sink_softmax(scores[kvh], sinks[kvh]) for kvh in range(N_KV_HEADS)]
    mprobs = [_softmax_rows(s) for s in mscores]
    yield
    outs = [_dot(probs[kvh][0].astype(BF16), v_ref[krows, ksl[kvh]]) / probs[kvh][1]
            for kvh in range(N_KV_HEADS)]
    mouts = [_dot(mprobs[hd][0].astype(BF16), mv_ref[:, msl[hd]]) / mprobs[hd][1] for hd in range(MEM_HEADS)]
    for kvh in range(N_KV_HEADS):
        for g, hd in enumerate(heads[kvh]):
            oa_ref[qrows, hd * HEAD_DIM:(hd + 1) * HEAD_DIM] = outs[kvh][g * WINDOW:(g + 1) * WINDOW].astype(BF16)
    for hd in range(MEM_HEADS):
        oc_ref[qrows, msl[hd]] = mouts[hd].astype(BF16)


def _inproj_attn_kernel(l, tiles_per_seq, sink_ref, x_ref, xs_ref, g_ref, w_ref, cos_ref, slo_ref, shi_ref,
                        qg_ref, kg_ref, mqg_ref, ones_ref, mk_ref, mv_ref,
                        k_ref, v_ref, u_ref, gates_ref, oa_ref, oc_ref, qs_ref, mqs_ref,
                        q_scr, mq_scr, k_scr, v_scr, mk_scr, mv_scr):
    i = pl.program_id(0)
    cur, prv = i % 2, (i + 1) % 2
    decode_step = i == pl.num_programs(0) - 1

    @pl.when(i == 0)
    def _():
        for scr in (q_scr, mq_scr, k_scr, v_scr, mk_scr, mv_scr):
            scr[...] = jnp.zeros(scr.shape, scr.dtype)

    @pl.when(i % tiles_per_seq == 1)
    def _():
        mk_scr[...] = mk_ref[...].astype(BF16)
        mv_scr[...] = mv_ref[...].astype(BF16)

    rows = KV_GROUP * WINDOW
    r = lax.broadcasted_iota(jnp.int32, (rows, 2 * WINDOW), 0)
    j = lax.broadcasted_iota(jnp.int32, (rows, 2 * WINDOW), 1)
    qi = r % WINDOW
    band = (j > qi) & (j <= qi + WINDOW)
    first_band = band & ((j >= WINDOW) | (i % tiles_per_seq != 1))
    sinks = [_sink_column(sink_ref, l, [kvh * KV_GROUP + g for g in range(KV_GROUP)], WINDOW)
             for kvh in range(N_KV_HEADS)]

    def attention_stages():
        for sub in range(ATTN_TILE // WINDOW):
            yield from _attend_sub_block(sub, first_band if sub == 0 else band, sinks, q_scr.at[prv],
                                         mq_scr.at[prv], k_scr.at[prv], v_scr.at[prv], mk_scr, mv_scr,
                                         oa_ref, oc_ref)
            yield

    proj = _Projection(jnp.where(decode_step, xs_ref[...], x_ref[...]), g_ref[l:l + 1, :], w_ref,
                       (cos_ref[...], slo_ref[...], shi_ref[...]), qg_ref[l:l + 1, :], kg_ref[l:l + 1, :],
                       mqg_ref[l:l + 1, :], ones_ref[...])

    def store_queries():
        q = proj.queries()
        q_scr[cur] = q
        qs_ref[...] = q

    def store_keys_values():
        k, v = proj.keys_values()
        k_ref[...], v_ref[...] = k, v
        k_scr[cur, :WINDOW, :] = k_scr[prv, ATTN_TILE:, :]
        v_scr[cur, :WINDOW, :] = v_scr[prv, ATTN_TILE:, :]
        k_scr[cur, WINDOW:, :] = k.astype(BF16)
        v_scr[cur, WINDOW:, :] = v.astype(BF16)

    def store_ssm_input():
        u_ref[...] = proj.ssm_input()

    def store_memory_queries():
        mq = proj.memory_queries()
        mq_scr[cur] = mq
        mqs_ref[...] = mq

    def store_gates(c):
        gates_ref[:, c * GATE_CHUNK:(c + 1) * GATE_CHUNK] = proj.gates(c)

    segments = [store_queries, store_keys_values, store_ssm_input, store_memory_queries]
    segments += [functools.partial(store_gates, c) for c in range(N_GATE_CHUNKS)]
    stages = attention_stages()
    n_stages = 3 * (ATTN_TILE // WINDOW)
    for n, segment in enumerate(segments):
        for _ in range((n + 1) * n_stages // len(segments) - n * n_stages // len(segments)):
            next(stages)
        segment()


def _inproj_attn(l, sinks, x2d, xs2d, seq_len, g, w_bf, rope, qg, kg, mqg, ones, mk, mv):
    n = x2d.shape[0]
    n_tiles = n // ROW_TILE
    tiles_per_seq = seq_len // ROW_TILE
    assert ROW_TILE == ATTN_TILE and tiles_per_seq > 1 and xs2d.shape[0] == ROW_TILE
    attn_tile = lambda i: jnp.maximum(i - 1, 0)
    xrow = pl.BlockSpec((ROW_TILE, D_MODEL), lambda i: (jnp.minimum(i, n_tiles - 1), 0))
    orow = lambda w: pl.BlockSpec((ROW_TILE, w), lambda i: (i, 0))
    arow = pl.BlockSpec((ROW_TILE, BRANCH_W), lambda i: (attn_tile(i), 0))
    rspec = pl.BlockSpec((ROW_TILE, LANES), lambda i: (jnp.where(i == n_tiles, tiles_per_seq, i % tiles_per_seq), 0))
    mem = pl.BlockSpec((None, N_MEM, BRANCH_W), lambda i: (attn_tile(i) // tiles_per_seq, 0, 0))
    widths = (KV_W, KV_W, BRANCH_W, 3 * D_MODEL)
    dtypes = (F32, F32, F32, BF16)
    return pl.pallas_call(
        functools.partial(_inproj_attn_kernel, l, tiles_per_seq), name="inproj_attn",
        grid=(n_tiles + 1,),
        in_specs=[pl.BlockSpec(memory_space=pltpu.SMEM), xrow, _full((ROW_TILE, D_MODEL)), _full(g.shape),
                  _layer((D_MODEL, IN_W), l), rspec, rspec, rspec, _full(qg.shape), _full(kg.shape),
                  _full(mqg.shape), _full((256, 256)), mem, mem],
        out_specs=[orow(w) for w in widths] + [arow, arow] + [_full((ROW_TILE, BRANCH_W))] * 2,
        out_shape=[jax.ShapeDtypeStruct((n + ROW_TILE, w), d) for w, d in zip(widths, dtypes)]
                  + [jax.ShapeDtypeStruct((n, BRANCH_W), BF16)] * 2
                  + [jax.ShapeDtypeStruct((ROW_TILE, BRANCH_W), BF16)] * 2,
        scratch_shapes=[pltpu.VMEM((2, ROW_TILE, BRANCH_W), BF16)] * 2
                       + [pltpu.VMEM((2, WINDOW + ROW_TILE, KV_W), BF16)] * 2
                       + [pltpu.VMEM((N_MEM, BRANCH_W), BF16)] * 2,
        compiler_params=_cparams(1),
    )(sinks, x2d, xs2d, g, w_bf, *rope, qg, kg, mqg, ones, mk, mv)


DEC_BLOCK = 8
TOK_PAD = 8


def _pad_rows(x, n):
    return jnp.concatenate([x, jnp.zeros((n - x.shape[0], x.shape[1]), x.dtype)], axis=0)


def _attn_sample_kernel(l, t, sink_ref, q_ref, kn_ref, vn_ref, ckt_ref, cvt_ref, qm_ref, mk_ref, mv_ref,
                        oa_ref, oc_ref, nkt_ref, nvt_ref):
    keys = 2 * WINDOW
    rows = N_HEADS * TOK_PAD
    rk = lax.broadcasted_iota(jnp.int32, (rows, keys), 0)
    jk = lax.broadcasted_iota(jnp.int32, (rows, keys), 1)
    tk = rk % TOK_PAD
    mask = (tk < t) & (((jk < WINDOW) & (jk > tk)) | ((jk >= WINDOW) & (jk - WINDOW <= tk)))
    sink = _sink_column(sink_ref, l, list(range(N_HEADS)), TOK_PAD)
    mem_rows = N_MEM * MEM_HEADS
    mem_own = (lax.broadcasted_iota(jnp.int32, (MEM_HEADS * TOK_PAD, mem_rows), 1) % MEM_HEADS
               == lax.broadcasted_iota(jnp.int32, (MEM_HEADS * TOK_PAD, mem_rows), 0) // TOK_PAD)
    is_new = lax.broadcasted_iota(jnp.int32, (HEAD_DIM, WINDOW), 1) >= WINDOW - t
    dsl = [slice(kvh * HEAD_DIM, (kvh + 1) * HEAD_DIM) for kvh in range(N_KV_HEADS)]
    group_rows = KV_GROUP * TOK_PAD

    q_all, qm_all = q_ref[...].astype(F32), qm_ref[...].astype(F32)
    seqs = range(DEC_BLOCK)
    toks = [slice(bi * t, (bi + 1) * t) for bi in seqs]
    knp = [_pad_rows(kn_ref[toks[bi], :], WINDOW) for bi in seqs]
    vnp = [_pad_rows(vn_ref[toks[bi], :], WINDOW) for bi in seqs]
    knt, vnt = [x.T for x in knp], [x.T for x in vnp]
    for bi in seqs:
        for kvh in range(N_KV_HEADS):
            nkt_ref[bi, kvh] = jnp.where(is_new, pltpu.roll(knt[bi][dsl[kvh], :], WINDOW - t, 1),
                                         pltpu.roll(ckt_ref[bi, kvh], WINDOW - t, 1))
            nvt_ref[bi, kvh] = jnp.where(is_new, pltpu.roll(vnt[bi][dsl[kvh], :], WINDOW - t, 1),
                                         pltpu.roll(cvt_ref[bi, kvh], WINDOW - t, 1))
    qg = []
    for bi in seqs:
        q8 = _pad_rows(q_all[toks[bi]], TOK_PAD)
        qg.append([jnp.concatenate([q8[:, hd * HEAD_DIM:(hd + 1) * HEAD_DIM]
                                    for hd in range(kvh * KV_GROUP, (kvh + 1) * KV_GROUP)], axis=0).astype(BF16)
                   for kvh in range(N_KV_HEADS)])
    scores = []
    for bi in seqs:
        per_kvh = [jnp.concatenate([_dot(qg[bi][kvh], ckt_ref[bi, kvh].astype(BF16)),
                                    _dot_nt(qg[bi][kvh], knp[bi][:, dsl[kvh]].astype(BF16))], axis=1)
                   for kvh in range(N_KV_HEADS)]
        scores.append(jnp.where(mask, jnp.concatenate(per_kvh, axis=0), -jnp.inf))
    mscores = []
    for bi in seqs:
        qm8 = jnp.concatenate([qm_all[toks[bi]], jnp.zeros((TOK_PAD - t, BRANCH_W), F32)], axis=0).astype(BF16)
        qst = jnp.concatenate([qm8[:, hd * MEM_HEAD_DIM:(hd + 1) * MEM_HEAD_DIM] for hd in range(MEM_HEADS)],
                              axis=0)
        mscores.append(jnp.where(mem_own, _dot_nt(qst, mk_ref[bi].astype(BF16)), -jnp.inf))
    probs = [_sink_softmax(scores[bi], sink) for bi in seqs]
    mprobs = [_softmax_rows(mscores[bi]) for bi in seqs]
    outs = []
    for bi in seqs:
        p = probs[bi][0].astype(BF16)
        per_kvh = []
        for kvh in range(N_KV_HEADS):
            pk = p[kvh * group_rows:(kvh + 1) * group_rows]
            per_kvh.append(_dot_nt(pk[:, :WINDOW], cvt_ref[bi, kvh].astype(BF16))
                           + _dot(pk[:, WINDOW:], vnp[bi][:, dsl[kvh]].astype(BF16)))
        outs.append(jnp.concatenate(per_kvh, axis=0) / probs[bi][1])
    mouts = [_dot(mprobs[bi][0].astype(BF16), mv_ref[bi].astype(BF16)) / mprobs[bi][1] for bi in seqs]
    for bi in seqs:
        oa_ref[toks[bi], :] = jnp.concatenate(
            [outs[bi][hd * TOK_PAD:(hd + 1) * TOK_PAD] for hd in range(N_HEADS)], axis=1)[0:t]
        oc_ref[toks[bi], :] = jnp.concatenate(
            [mouts[bi][hd * TOK_PAD:(hd + 1) * TOK_PAD] for hd in range(MEM_HEADS)], axis=1)[0:t]


def _attn_sample(l, t, sinks, q, kn, vn, kv_row0, ckt, cvt, qm, mk, mv):
    n = q.shape[0]
    tok = lambda w, row0=0: pl.BlockSpec((DEC_BLOCK * t, w), lambda i: (i + row0 // (DEC_BLOCK * t), 0))
    cache_in = pl.BlockSpec((None, DEC_BLOCK, N_KV_HEADS, HEAD_DIM, WINDOW), lambda i: (l, i, 0, 0, 0))
    cache_out = pl.BlockSpec((DEC_BLOCK, N_KV_HEADS, HEAD_DIM, WINDOW), lambda i: (i, 0, 0, 0))
    mem = pl.BlockSpec((None, DEC_BLOCK, N_MEM * MEM_HEADS, MEM_HEAD_DIM), lambda i: (l, i, 0, 0))
    return pl.pallas_call(
        functools.partial(_attn_sample_kernel, l, t), name="attn_sample",
        grid=(n // (DEC_BLOCK * t),),
        in_specs=[pl.BlockSpec(memory_space=pltpu.SMEM), tok(BRANCH_W), tok(KV_W, kv_row0), tok(KV_W, kv_row0),
                  cache_in, cache_in, tok(BRANCH_W), mem, mem],
        out_specs=[tok(BRANCH_W), tok(BRANCH_W), cache_out, cache_out],
        out_shape=[jax.ShapeDtypeStruct((n, BRANCH_W), F32)] * 2
                  + [jax.ShapeDtypeStruct((n // t, N_KV_HEADS, HEAD_DIM, WINDOW), F32)] * 2,
        compiler_params=_cparams(1),
    )(sinks, q, kn, vn, ckt, cvt, qm, mk, mv)


def _ssm_tables(a_re, a_im, log_dt, b_re, b_im, c_re, c_im, chunk):
    hp = lax.Precision.HIGHEST
    dt = jnp.exp(log_dt)[..., None]
    mag = jnp.exp(a_re * dt)
    lam_re, lam_im = mag * jnp.cos(a_im * dt), mag * jnp.sin(a_im * dt)
    den = a_re * a_re + a_im * a_im
    nr, ni = lam_re - 1.0, lam_im
    g_re = (nr * a_re + ni * a_im) / den
    g_im = (ni * a_re - nr * a_im) / den
    bg_re = g_re[..., None] * b_re - g_im[..., None] * b_im
    bg_im = g_re[..., None] * b_im + g_im[..., None] * b_re
    def powers(d):
        d = d.astype(F32)[:, None, None, None]
        pm = jnp.exp(d * (a_re * dt)[None])
        return pm * jnp.cos(d * (a_im * dt)[None]), pm * jnp.sin(d * (a_im * dt)[None])

    def times_gb(p_re, p_im):
        return (p_re[..., None] * bg_re[None] - p_im[..., None] * bg_im[None],
                p_re[..., None] * bg_im[None] + p_im[..., None] * bg_re[None])

    pw_re, pw_im = powers(jnp.arange(chunk + 1))
    z_re, z_im = times_gb(pw_re[:chunk], pw_im[:chunk])
    zrev_re, zrev_im = times_gb(*powers(jnp.arange(chunk - 1, -1, -1)))
    kern = (jnp.einsum('lgop,dlgpi->dlgio', c_re, z_re, precision=hp)
            - jnp.einsum('lgop,dlgpi->dlgio', c_im, z_im, precision=hp))
    nl = a_re.shape[0]
    nb, gb = N_LANE_BLOCKS, GROUPS_PER_LANE_BLOCK
    kd = jnp.transpose(kern.reshape(chunk, nl, nb, gb, SSM_GROUP_CH, SSM_GROUP_CH), (1, 2, 0, 3, 4, 5))
    kd = kd.reshape(nl, nb, chunk, LANES, SSM_GROUP_CH)

    def to_rows(z):
        z = z.reshape(chunk, nl, nb, gb, SSM_STATE, SSM_GROUP_CH)
        return jnp.transpose(z, (1, 2, 0, 3, 5, 4)).reshape(nl, nb, chunk, LANES, SSM_STATE)

    zr, zi = to_rows(zrev_re), to_rows(zrev_im)
    zc = jnp.concatenate([zr, zr, zi, zi], axis=-1)

    pr, pi = pw_re[1:], pw_im[1:]
    f_a = c_re[None] * pr[:, :, :, None, :] - c_im[None] * pi[:, :, :, None, :]
    f_b = c_re[None] * pi[:, :, :, None, :] + c_im[None] * pr[:, :, :, None, :]

    def from_cols(f):
        f = f.reshape(chunk, nl, nb, gb, SSM_GROUP_CH, SSM_STATE)
        return jnp.transpose(f, (1, 2, 0, 4, 3, 5)).reshape(nl, nb, chunk, SSM_GROUP_CH, STATE_W)

    ft = jnp.concatenate([from_cols(f_a), from_cols(-f_b)], axis=-1)

    kw = chunk * LANES
    spec = lambda *shape: pl.BlockSpec((None, None) + shape, lambda l, n: (l, n) + (0,) * len(shape))
    intra, to_st, from_st_t = pl.pallas_call(
        _ssm_table_kernel, name="ssm_tables",
        grid=(nl, nb),
        in_specs=[spec(chunk, LANES, SSM_GROUP_CH), spec(chunk, LANES, 2 * LANES),
                  spec(chunk, SSM_GROUP_CH, 2 * STATE_W)],
        out_specs=[spec(kw, kw), spec(kw, 2 * STATE_W), spec(kw, 2 * STATE_W)],
        out_shape=[jax.ShapeDtypeStruct((nl, nb, kw, kw), BF16),
                   jax.ShapeDtypeStruct((nl, nb, kw, 2 * STATE_W), BF16),
                   jax.ShapeDtypeStruct((nl, nb, kw, 2 * STATE_W), BF16)],
        compiler_params=_cparams(2),
    )(kd, zc, ft)
    decay = lambda d: (pw_re[d].reshape(nl, nb, 1, STATE_W), pw_im[d].reshape(nl, nb, 1, STATE_W))
    return (intra, to_st, from_st_t), decay


def _ssm_table_kernel(kd_ref, zc_ref, ft_ref, intra_ref, to_ref, fromt_ref):
    chunk = kd_ref.shape[0]
    gch = SSM_GROUP_CH
    r = lax.broadcasted_iota(jnp.int32, (LANES, LANES), 0)
    c = lax.broadcasted_iota(jnp.int32, (LANES, LANES), 1)
    same_group = (r // gch) == (c // gch)
    spread = (lax.broadcasted_iota(jnp.int32, (gch, LANES), 1) % gch
              == lax.broadcasted_iota(jnp.int32, (gch, LANES), 0)).astype(BF16)
    spread_t = (lax.broadcasted_iota(jnp.int32, (LANES, gch), 0) % gch
                == lax.broadcasted_iota(jnp.int32, (LANES, gch), 1)).astype(BF16)
    rs = lax.broadcasted_iota(jnp.int32, (LANES, 2 * STATE_W), 0)
    cs = lax.broadcasted_iota(jnp.int32, (LANES, 2 * STATE_W), 1)
    own_state = (rs // gch) == ((cs % STATE_W) // SSM_STATE)

    lag_blocks = [jnp.where(same_group, _dot(kd_ref[d].astype(BF16), spread), 0.0).astype(BF16)
                  for d in range(chunk)]
    zero = jnp.zeros((LANES, LANES), BF16)
    for s in range(chunk):
        for t in range(chunk):
            intra_ref[s * LANES:(s + 1) * LANES, t * LANES:(t + 1) * LANES] = (
                lag_blocks[t - s] if t >= s else zero)
        z = zc_ref[s]
        full = jnp.concatenate([z[:, :LANES]] * (STATE_W // LANES) + [z[:, LANES:]] * (STATE_W // LANES), axis=1)
        to_ref[s * LANES:(s + 1) * LANES, :] = jnp.where(own_state, full, 0.0).astype(BF16)
        f = _dot(spread_t, ft_ref[s].astype(BF16))
        fromt_ref[s * LANES:(s + 1) * LANES, :] = jnp.where(own_state, f, 0.0).astype(BF16)


SSM_SEQS = 2


def _chunk_rows(ref, chunk):
    n = ref.shape[0] // chunk
    return jnp.concatenate([ref[pl.ds(s, n, stride=chunk), :] for s in range(chunk)], axis=1)


def _store_chunk_rows(ref, y, chunk):
    n = ref.shape[0] // chunk
    for t in range(chunk):
        ref[pl.ds(t, n, stride=chunk), :] = y[:, t * LANES:(t + 1) * LANES]


def _ssm_prompt_kernel(u_ref, intra_ref, to_ref, fromt_ref, dre_ref, dim_ref,
                       y_ref, hre_ref, him_ref, st_scr):
    chunk = PROMPT_CHUNK
    nchunks = u_ref.shape[0] // (chunk * SSM_SEQS)
    x = _chunk_rows(u_ref, chunk).astype(BF16)
    kw = chunk * LANES
    y = jnp.concatenate([_dot(x[:, :c + MXU_TILE], intra_ref[:c + MXU_TILE, c:c + MXU_TILE])
                         for c in range(0, kw, MXU_TILE)], axis=1)
    add_all = _dot(x, to_ref[...])
    nblk = STATE_W // LANES
    assert SSM_SEQS * nblk == SUBLANES
    slots = [(seq, cb) for seq in range(SSM_SEQS) for cb in range(nblk)]

    def slot_rows(seq, cb):
        return pl.ds(seq * nblk + cb, nchunks, stride=SUBLANES)

    for c in range(2):
        for seq, cb in slots:
            col = (c * nblk + cb) * LANES
            st_scr[c, slot_rows(seq, cb), :] = add_all[seq * nchunks:(seq + 1) * nchunks, col:col + LANES]
    dre = jnp.concatenate([dre_ref[:, cb * LANES:(cb + 1) * LANES] for _, cb in slots], axis=0)
    dim = jnp.concatenate([dim_ref[:, cb * LANES:(cb + 1) * LANES] for _, cb in slots], axis=0)

    def step(k, carry):
        hr, hi = carry
        tile = pl.ds(pl.multiple_of(k * SUBLANES, SUBLANES), SUBLANES)
        add_r, add_i = st_scr[0, tile, :], st_scr[1, tile, :]
        st_scr[0, tile, :] = hr
        st_scr[1, tile, :] = hi
        return dre * hr - dim * hi + add_r, dre * hi + dim * hr + add_i

    zero = jnp.zeros((SUBLANES, LANES), F32)
    hr, hi = lax.fori_loop(0, nchunks, step, (zero, zero), unroll=8)
    hre_ref[...] = hr
    him_ref[...] = hi
    h_in = jnp.concatenate(
        [jnp.concatenate([st_scr[c, slot_rows(seq, cb), :] for seq in range(SSM_SEQS)], axis=0)
         for c in range(2) for cb in range(nblk)], axis=1).astype(BF16)
    _store_chunk_rows(y_ref, y + _dot_nt(h_in, fromt_ref[...]), chunk)


def _ssm_prompt(l, u2d, n_seqs, seq_len, intra, to_st, from_st_t, dec_re, dec_im):
    n = n_seqs * seq_len
    chunk = PROMPT_CHUNK
    rows = SSM_SEQS * seq_len
    steps = n // rows
    kw = chunk * LANES
    data = pl.BlockSpec((rows, LANES), lambda j, h: (h, j))
    wspec = lambda r, c: pl.BlockSpec((None, None, r, c), lambda j, h: (l, j, 0, 0))
    st_out = pl.BlockSpec((None, None, SUBLANES, LANES), lambda j, h: (j, h, 0, 0))
    st_shape = jax.ShapeDtypeStruct((N_LANE_BLOCKS, steps, SUBLANES, LANES), F32)
    return pl.pallas_call(
        _ssm_prompt_kernel, name="ssm_prompt",
        grid=(N_LANE_BLOCKS, steps),
        in_specs=[data, wspec(kw, kw), wspec(kw, 2 * STATE_W), wspec(kw, 2 * STATE_W),
                  wspec(1, STATE_W), wspec(1, STATE_W)],
        out_specs=[data, st_out, st_out],
        out_shape=[jax.ShapeDtypeStruct((n, BRANCH_W), F32), st_shape, st_shape],
        scratch_shapes=[pltpu.VMEM((2, SUBLANES * seq_len // chunk, LANES), F32)],
        compiler_params=_cparams(2),
    )(u2d, intra, to_st, from_st_t, dec_re, dec_im)


def _ssm_sample_kernel(chunk, u_ref, h0r_ref, h0i_ref, intra_ref, to_ref, fromt_ref, dre_ref, dim_ref,
                       y_ref, hre_ref, him_ref):
    x = _chunk_rows(u_ref, chunk).astype(BF16)
    hr, hi = h0r_ref[...], h0i_ref[...]
    dre, dim = dre_ref[...], dim_ref[...]
    add = _dot(x, to_ref[...])
    hre_ref[...] = dre * hr - dim * hi + add[:, :STATE_W]
    him_ref[...] = dre * hi + dim * hr + add[:, STATE_W:]
    h0 = jnp.concatenate([hr, hi], axis=1).astype(BF16)
    _store_chunk_rows(y_ref, _dot(x, intra_ref[...]) + _dot_nt(h0, fromt_ref[...]), chunk)


def _ssm_sample(l, u2d, row0, n, t, h0_re, h0_im, intra, to_st, from_st_t, dec_re, dec_im):
    b = n // t
    kw = t * LANES
    assert PROMPT_CHUNK % t == 0 and row0 % n == 0
    last = PROMPT_CHUNK // t - 1
    data_in = pl.BlockSpec((n, LANES), lambda j: (row0 // n, j))
    data = pl.BlockSpec((n, LANES), lambda j: (0, j))
    st_in = pl.BlockSpec((None, b, STATE_W), lambda j: (l, 0, j))
    st_out = pl.BlockSpec((b, STATE_W), lambda j: (0, j))
    wspec = lambda r, c, rb=0: pl.BlockSpec((None, None, r, c), lambda j: (l, j, rb, 0))
    return pl.pallas_call(
        functools.partial(_ssm_sample_kernel, t), name="ssm_sample",
        grid=(N_LANE_BLOCKS,),
        in_specs=[data_in, st_in, st_in, wspec(kw, kw), wspec(kw, 2 * STATE_W, last), wspec(kw, 2 * STATE_W),
                  wspec(1, STATE_W), wspec(1, STATE_W)],
        out_specs=[data, st_out, st_out],
        out_shape=[jax.ShapeDtypeStruct((n, BRANCH_W), F32),
                   jax.ShapeDtypeStruct((b, N_LANE_BLOCKS * STATE_W), F32),
                   jax.ShapeDtypeStruct((b, N_LANE_BLOCKS * STATE_W), F32)],
        compiler_params=_cparams(1),
    )(u2d, h0_re, h0_im, intra, to_st, from_st_t, dec_re, dec_im)


FF_CHUNKS = ((0, 512), (512, 512), (1024, 512), (1536, 512), (2048, 512), (2560, 256))


def _merge_ffn_kernel(l, x_ref, oa_ref, y_ref, u_ref, oc_ref, gates_ref, d_ref, wglu_ref, wb_ref, wout_ref,
                      g_ref, wup_ref, wdn_ref, o_ref):
    z = jax.nn.gelu(y_ref[...] + d_ref[l:l + 1, :] * u_ref[...])
    ob = (z * jax.nn.sigmoid(_dot(z.astype(BF16), wglu_ref[...]))).astype(BF16)
    gate = lambda n: gates_ref[:, n * D_MODEL:(n + 1) * D_MODEL].astype(F32)
    merged = gate(0) * _dot(oa_ref[...], wb_ref[0])
    merged = merged + gate(1) * _dot(ob, wb_ref[1])
    merged = merged + gate(2) * _dot(oc_ref[...], wb_ref[2])
    x = x_ref[...] + _dot(merged.astype(BF16), wout_ref[...])
    h = (_rms(x) * g_ref[l:l + 1, :]).astype(BF16)
    acc = x
    for lo, w in FF_CHUNKS:
        up_gate = _dot(h, wup_ref[:, lo:lo + w])
        up = _dot(h, wup_ref[:, D_FF + lo:D_FF + lo + w])
        acc = acc + _dot((jax.nn.silu(up_gate) * up).astype(BF16), wdn_ref[lo:lo + w, :])
    o_ref[...] = acc


def _merge_ffn(l, x2d, oa, y, u, oc, gates, proj_row0, d, wglu, wb, wout, g_ffn, wup, wdn):
    n = x2d.shape[0]
    assert proj_row0 % ROW_TILE == 0
    row = lambda w, row0=0: pl.BlockSpec((ROW_TILE, w), lambda i: (i + row0 // ROW_TILE, 0))
    return pl.pallas_call(
        functools.partial(_merge_ffn_kernel, l), name="merge_ffn",
        grid=(n // ROW_TILE,),
        in_specs=[row(D_MODEL), row(BRANCH_W), row(BRANCH_W), row(BRANCH_W, proj_row0), row(BRANCH_W),
                  row(3 * D_MODEL, proj_row0), _full(d.shape), _layer((BRANCH_W, BRANCH_W), l),
                  _layer((3, BRANCH_W, D_MODEL), l), _layer((D_MODEL, D_MODEL), l),
                  _full(g_ffn.shape), _layer((D_MODEL, 2 * D_FF), l), _layer((D_FF, D_MODEL), l)],
        out_specs=row(D_MODEL),
        out_shape=jax.ShapeDtypeStruct((n, D_MODEL), F32),
        compiler_params=_cparams(1),
    )(x2d, oa, y, u, oc, gates, d, wglu, wb, wout, g_ffn, wup, wdn)


def _rope_tables(pos):
    half = HEAD_DIM // 2
    inv = ROPE_THETA ** (-jnp.arange(half, dtype=F32) / half)
    ang = pos[:, None] * inv[None, :]
    cos, sin = jnp.cos(ang), jnp.sin(ang)
    zero = jnp.zeros_like(sin)
    two = lambda a: jnp.concatenate([a, a], axis=1)
    return (two(jnp.concatenate([cos, cos], axis=1)), two(jnp.concatenate([-sin, zero], axis=1)),
            two(jnp.concatenate([zero, sin], axis=1)))


def _state_from_blocks(h):
    nb, steps, _, _ = h.shape
    h = h.reshape(nb, steps, SSM_SEQS, STATE_W // LANES, LANES)
    return jnp.transpose(h, (1, 2, 0, 3, 4)).reshape(steps * SSM_SEQS, SSM_GROUPS, SSM_STATE)


def kernel(x_prompt, x_sample, cache_swa_k, cache_swa_v, state_ssm_re, state_ssm_im, cache_mem_k,
           cache_mem_v, mem_prompt, attn_norm, w_in, q_norm, k_norm, attn_sinks, ssm_a_re, ssm_a_im,
           ssm_log_dt, ssm_b_re, ssm_b_im, ssm_c_re, ssm_c_im, ssm_d, ssm_w_glu, mem_norm, w_mem_kv,
           mem_q_norm, mem_k_norm, w_branch, w_out, ffn_norm, w_ffn_up, w_ffn_down):
    depth = w_in.shape[0]
    b, s, _ = x_prompt.shape
    db, t, _ = x_sample.shape
    assert db * t == ROW_TILE and s % ROW_TILE == 0

    rope = _rope_tables(jnp.concatenate([jnp.arange(s, dtype=F32),
                                         jnp.tile(PAST_LEN + jnp.arange(t, dtype=F32), db)]))
    ssm_params = (ssm_a_re, ssm_a_im, ssm_log_dt, ssm_b_re, ssm_b_im, ssm_c_re, ssm_c_im)
    tables, decay = _ssm_tables(*ssm_params, PROMPT_CHUNK)
    tab_p, tab_s = tables + decay(PROMPT_CHUNK), tables + decay(t)
    blk = jnp.arange(256) // HEAD_DIM
    ones = (blk[:, None] == blk[None, :]).astype(BF16)

    w_in_bf, w_mem_bf = w_in.astype(BF16), w_mem_kv.astype(BF16)
    wglu, wb, wout = ssm_w_glu.astype(BF16), w_branch.astype(BF16), w_out.astype(BF16)
    wup, wdn = w_ffn_up.astype(BF16), w_ffn_down.astype(BF16)
    qg, kg = jnp.tile(q_norm, (1, N_HEADS)), jnp.tile(k_norm, (1, N_KV_HEADS))
    ckt = jnp.transpose(cache_swa_k, (0, 1, 3, 4, 2))
    cvt = jnp.transpose(cache_swa_v, (0, 1, 3, 4, 2))
    cmk = cache_mem_k.reshape(depth, db, N_MEM * MEM_HEADS, MEM_HEAD_DIM)
    cmv = cache_mem_v.reshape(depth, db, N_MEM * MEM_HEADS, MEM_HEAD_DIM)
    h0r = state_ssm_re.reshape(depth, db, SSM_GROUPS * SSM_STATE)
    h0i = state_ssm_im.reshape(depth, db, SSM_GROUPS * SSM_STATE)

    yp = x_prompt.reshape(b * s, D_MODEL)
    ys = x_sample.reshape(db * t, D_MODEL)
    mem2d = mem_prompt.reshape(b * N_MEM, D_MODEL)
    outs = [[] for _ in range(10)]
    n = b * s
    for l in range(depth):
        mk, mv = _memkv(l, mem2d, mem_norm, w_mem_bf, mem_k_norm)
        mk3, mv3 = mk.reshape(b, N_MEM, BRANCH_W), mv.reshape(b, N_MEM, BRANCH_W)

        k, v, u, gates, oa, oc, qs, mqs = _inproj_attn(l, attn_sinks, yp, ys, s, attn_norm, w_in_bf, rope, qg, kg,
                                                       mem_q_norm, ones, mk3, mv3)
        ysm, hre, him = _ssm_prompt(l, u, b, s, *tab_p)
        yp = _merge_ffn(l, yp, oa, ysm, u, oc, gates, 0, ssm_d, wglu, wb, wout, ffn_norm, wup, wdn)

        oas, ocs, nkt, nvt = _attn_sample(l, t, attn_sinks, qs, k, v, n, ckt, cvt, mqs, cmk, cmv)
        yss, hrs, his = _ssm_sample(l, u, n, db * t, t, h0r, h0i, *tab_s)
        ys = _merge_ffn(l, ys, oas.astype(BF16), yss, u, ocs.astype(BF16), gates, n, ssm_d, wglu, wb, wout,
                        ffn_norm, wup, wdn)

        last_rows = lambda a: jnp.stack([a[(bi + 1) * s - WINDOW:(bi + 1) * s] for bi in range(b)]).reshape(
            b, WINDOW, N_KV_HEADS, HEAD_DIM)
        new = (last_rows(k), last_rows(v),
               _state_from_blocks(hre), _state_from_blocks(him),
               mk3.reshape(b, N_MEM, MEM_HEADS, MEM_HEAD_DIM), mv3.reshape(b, N_MEM, MEM_HEADS, MEM_HEAD_DIM),
               jnp.transpose(nkt, (0, 3, 1, 2)), jnp.transpose(nvt, (0, 3, 1, 2)),
               hrs.reshape(db, SSM_GROUPS, SSM_STATE), his.reshape(db, SSM_GROUPS, SSM_STATE))
        for lst, a in zip(outs, new):
            lst.append(a)

    return (yp.reshape(b, s, D_MODEL), ys.reshape(db, t, D_MODEL)) + tuple(jnp.stack(o) for o in outs)
```

```python
import functools
import math

import jax
import jax.numpy as jnp
from jax import lax
from jax.experimental import pallas as pl
from jax.experimental.pallas import tpu as pltpu

F32 = jnp.float32
BF16 = jnp.bfloat16

D_MODEL = 1024
BRANCH_W = 512
HEAD_DIM = 64
N_HEADS = 8
N_KV_HEADS = 2
KV_GROUP = 4
KV_W = N_KV_HEADS * HEAD_DIM
WINDOW = 128
ROPE_THETA = 10000.0
PAST_LEN = 16384
SSM_GROUPS = 32
SSM_GROUP_CH = 16
SSM_STATE = 64
N_MEM = 256
MEM_HEADS = 4
MEM_HEAD_DIM = 128
D_FF = 2816
RMS_EPS = 1e-6

Q_OFF, K_OFF, V_OFF, U_OFF, MQ_OFF, G_OFF = 0, 512, 640, 768, 1280, 1792
IN_W = G_OFF + 3 * D_MODEL

LANES = 128
SUBLANES = 8
MXU_TILE = 256
GROUPS_PER_LANE_BLOCK = LANES // SSM_GROUP_CH
N_LANE_BLOCKS = BRANCH_W // LANES
STATE_W = GROUPS_PER_LANE_BLOCK * SSM_STATE
PROMPT_CHUNK = 8
ROW_TILE = 512
VMEM_LIMIT = 56 * 1024 * 1024


def _dot(a, b):
    return jnp.dot(a, b, preferred_element_type=F32)


def _dot_nt(a, b):
    return lax.dot_general(a, b, (((1,), (1,)), ((), ())), preferred_element_type=F32)


def _cparams(n_axes):
    return pltpu.CompilerParams(dimension_semantics=("arbitrary",) * n_axes,
                                vmem_limit_bytes=VMEM_LIMIT)


def _full(shape):
    nd = len(shape)
    return pl.BlockSpec(shape, lambda *_: (0,) * nd)


def _layer(shape, l):
    nd = len(shape)
    return pl.BlockSpec((None,) + tuple(shape), lambda *_: (l,) + (0,) * nd, pipeline_mode=pl.Buffered(1))


def _rms(x):
    return x * lax.rsqrt(jnp.mean(x * x, axis=-1, keepdims=True) + RMS_EPS)


def _memkv_kernel(mem_ref, g_ref, w_ref, kg_ref, k_ref, v_ref):
    l = pl.program_id(0)
    h = _rms(mem_ref[...]) * g_ref[pl.ds(l, 1), :]
    kv = _dot(h.astype(BF16), w_ref[...])
    kg = kg_ref[pl.ds(l, 1), :]
    for hd in range(MEM_HEADS):
        sl = slice(hd * MEM_HEAD_DIM, (hd + 1) * MEM_HEAD_DIM)
        k_ref[:, sl] = _rms(kv[:, sl]) * kg
    v_ref[...] = kv[:, BRANCH_W:]


def _memkv(mem2d, g, w_bf, kg):
    n = mem2d.shape[0]
    depth = w_bf.shape[0]
    out = pl.BlockSpec((None, ROW_TILE, BRANCH_W), lambda l, i: (l, i, 0))
    return pl.pallas_call(
        _memkv_kernel, name="memkv",
        grid=(depth, n // ROW_TILE),
        in_specs=[pl.BlockSpec((ROW_TILE, D_MODEL), lambda l, i: (i, 0)), _full(g.shape),
                  pl.BlockSpec((None, D_MODEL, 2 * BRANCH_W), lambda l, i: (l, 0, 0)), _full(kg.shape)],
        out_specs=[out, out],
        out_shape=[jax.ShapeDtypeStruct((depth, n, BRANCH_W), F32)] * 2,
        compiler_params=_cparams(2),
    )(mem2d, g, w_bf, kg)


def _rotary(x, cos, sin_lo, sin_hi):
    w = x.shape[-1]
    half = HEAD_DIM // 2
    return x * cos + pltpu.roll(x, w - half, 1) * sin_lo + pltpu.roll(x, half, 1) * sin_hi


GATE_CHUNK = 512
N_GATE_CHUNKS = 3 * D_MODEL // GATE_CHUNK


class _Projection:
    def __init__(self, x, g, w_ref, rope, qg, kg, mqg, ones):
        self.hb = (_rms(x) * g).astype(BF16)
        self.w_ref, self.qg, self.kg, self.mqg = w_ref, qg, kg, mqg
        self.ones = ones
        self.rope = rope

    def queries(self):
        zq = _dot(self.hb, self.w_ref[:, Q_OFF:K_OFF])
        sq = (zq * zq).astype(BF16)
        ssq = jnp.concatenate([_dot(sq[:, :256], self.ones), _dot(sq[:, 256:], self.ones)], axis=1)
        qn = zq * lax.rsqrt(ssq * (1.0 / HEAD_DIM) + RMS_EPS) * self.qg
        rope4 = (jnp.concatenate([t] * 4, axis=1) for t in self.rope)
        return (_rotary(qn, *rope4) * (1.0 / math.sqrt(HEAD_DIM))).astype(BF16)

    def keys_values(self):
        zkv = _dot(self.hb, self.w_ref[:, K_OFF:U_OFF])
        zk = zkv[:, :KV_W]
        ssk = _dot((zk * zk).astype(BF16), self.ones[:KV_W, :KV_W])
        kn = zk * lax.rsqrt(ssk * (1.0 / HEAD_DIM) + RMS_EPS) * self.kg
        return _rotary(kn, *self.rope), zkv[:, KV_W:]

    def ssm_input(self):
        return _dot(self.hb, self.w_ref[:, U_OFF:MQ_OFF])

    def memory_queries(self):
        zm = _dot(self.hb, self.w_ref[:, MQ_OFF:G_OFF])
        mqg = self.mqg * (1.0 / math.sqrt(MEM_HEAD_DIM))
        return jnp.concatenate([_rms(zm[:, hd * MEM_HEAD_DIM:(hd + 1) * MEM_HEAD_DIM]) * mqg
                                for hd in range(MEM_HEADS)], axis=1).astype(BF16)

    def gates(self, c):
        lo = G_OFF + c * GATE_CHUNK
        return jax.nn.sigmoid(_dot(self.hb, self.w_ref[:, lo:lo + GATE_CHUNK])).astype(BF16)


ATTN_TILE = 512


def _softmax_rows(s):
    m = jnp.max(s, axis=-1, keepdims=True)
    p = jnp.exp(s - m)
    return p, jnp.sum(p, axis=-1, keepdims=True)


def _sink_column(sink_ref, l, heads, rows_per_head):
    n = len(heads) * rows_per_head
    rcol = lax.broadcasted_iota(jnp.int32, (n, 1), 0)
    sink = jnp.full((n, 1), sink_ref[l, heads[-1]], F32)
    for g in range(len(heads) - 2, -1, -1):
        sink = jnp.where(rcol < (g + 1) * rows_per_head, sink_ref[l, heads[g]], sink)
    return sink


def _sink_softmax(s, sink):
    m = jnp.maximum(jnp.max(s, axis=-1, keepdims=True), sink)
    p = jnp.exp(s - m)
    return p, jnp.sum(p, axis=-1, keepdims=True) + jnp.exp(sink - m)


def _attend_sub_block(sub, mask, sinks, q_ref, mq_ref, k_ref, v_ref, mk_ref, mv_ref, oa_ref, oc_ref):
    qrows = slice(sub * WINDOW, (sub + 1) * WINDOW)
    krows = slice(sub * WINDOW, (sub + 2) * WINDOW)
    ksl = [slice(kvh * HEAD_DIM, (kvh + 1) * HEAD_DIM) for kvh in range(N_KV_HEADS)]
    msl = [slice(hd * MEM_HEAD_DIM, (hd + 1) * MEM_HEAD_DIM) for hd in range(MEM_HEADS)]
    heads = [[kvh * KV_GROUP + g for g in range(KV_GROUP)] for kvh in range(N_KV_HEADS)]
    scores = []
    for kvh in range(N_KV_HEADS):
        qg = jnp.concatenate([q_ref[qrows, hd * HEAD_DIM:(hd + 1) * HEAD_DIM] for hd in heads[kvh]], axis=0)
        scores.append(jnp.where(mask, _dot_nt(qg, k_ref[krows, ksl[kvh]]), -jnp.inf))
    mscores = [_dot_nt(mq_ref[qrows, sl], mk_ref[:, sl]) for sl in msl]
    yield
    probs = [_sink_softmax(scores[kvh], sinks[kvh]) for kvh in range(N_KV_HEADS)]
    mprobs = [_softmax_rows(s) for s in mscores]
    yield
    outs = [_dot(probs[kvh][0].astype(BF16), v_ref[krows, ksl[kvh]]) / probs[kvh][1]
            for kvh in range(N_KV_HEADS)]
    mouts = [_dot(mprobs[hd][0].astype(BF16), mv_ref[:, msl[hd]]) / mprobs[hd][1] for hd in range(MEM_HEADS)]
    for kvh in range(N_KV_HEADS):
        for g, hd in enumerate(heads[kvh]):
            oa_ref[qrows, hd * HEAD_DIM:(hd + 1) * HEAD_DIM] = outs[kvh][g * WINDOW:(g + 1) * WINDOW].astype(BF16)
    for hd in range(MEM_HEADS):
        oc_ref[qrows, msl[hd]] = mouts[hd].astype(BF16)


def _inproj_attn_kernel(l, tiles_per_seq, sink_ref, x_ref, xs_ref, g_ref, w_ref, cos_ref, slo_ref, shi_ref,
                        qg_ref, kg_ref, mqg_ref, ones_ref, mk_ref, mv_ref,
                        k_ref, v_ref, u_ref, gates_ref, oa_ref, oc_ref, qs_ref, mqs_ref,
                        q_scr, mq_scr, k_scr, v_scr, mk_scr, mv_scr):
    i = pl.program_id(0)
    cur, prv = i % 2, (i + 1) % 2
    decode_step = i == pl.num_programs(0) - 1

    @pl.when(i == 0)
    def _():
        for scr in (q_scr, mq_scr, k_scr, v_scr, mk_scr, mv_scr):
            scr[...] = jnp.zeros(scr.shape, scr.dtype)

    @pl.when(i % tiles_per_seq == 1)
    def _():
        mk_scr[...] = mk_ref[...].astype(BF16)
        mv_scr[...] = mv_ref[...].astype(BF16)

    rows = KV_GROUP * WINDOW
    r = lax.broadcasted_iota(jnp.int32, (rows, 2 * WINDOW), 0)
    j = lax.broadcasted_iota(jnp.int32, (rows, 2 * WINDOW), 1)
    qi = r % WINDOW
    band = (j > qi) & (j <= qi + WINDOW)
    first_band = band & ((j >= WINDOW) | (i % tiles_per_seq != 1))
    sinks = [_sink_column(sink_ref, l, [kvh * KV_GROUP + g for g in range(KV_GROUP)], WINDOW)
             for kvh in range(N_KV_HEADS)]

    def attention_stages():
        for sub in range(ATTN_TILE // WINDOW):
            yield from _attend_sub_block(sub, first_band if sub == 0 else band, sinks, q_scr.at[prv],
                                         mq_scr.at[prv], k_scr.at[prv], v_scr.at[prv], mk_scr, mv_scr,
                                         oa_ref, oc_ref)
            yield

    proj = _Projection(jnp.where(decode_step, xs_ref[...], x_ref[...]), g_ref[l:l + 1, :], w_ref,
                       (cos_ref[...], slo_ref[...], shi_ref[...]), qg_ref[l:l + 1, :], kg_ref[l:l + 1, :],
                       mqg_ref[l:l + 1, :], ones_ref[...])

    def store_queries():
        q = proj.queries()
        q_scr[cur] = q
        qs_ref[...] = q

    def store_keys_values():
        k, v = proj.keys_values()
        k_ref[...], v_ref[...] = k, v
        k_scr[cur, :WINDOW, :] = k_scr[prv, ATTN_TILE:, :]
        v_scr[cur, :WINDOW, :] = v_scr[prv, ATTN_TILE:, :]
        k_scr[cur, WINDOW:, :] = k.astype(BF16)
        v_scr[cur, WINDOW:, :] = v.astype(BF16)

    def store_ssm_input():
        u_ref[...] = proj.ssm_input()

    def store_memory_queries():
        mq = proj.memory_queries()
        mq_scr[cur] = mq
        mqs_ref[...] = mq

    def store_gates(c):
        gates_ref[:, c * GATE_CHUNK:(c + 1) * GATE_CHUNK] = proj.gates(c)

    segments = [store_queries, store_keys_values, store_ssm_input, store_memory_queries]
    segments += [functools.partial(store_gates, c) for c in range(N_GATE_CHUNKS)]
    stages = attention_stages()
    n_stages = 3 * (ATTN_TILE // WINDOW)
    for n, segment in enumerate(segments):
        for _ in range((n + 1) * n_stages // len(segments) - n * n_stages // len(segments)):
            next(stages)
        segment()


def _inproj_attn(l, sinks, x2d, xs2d, seq_len, g, w_bf, rope, qg, kg, mqg, ones, mk, mv):
    n = x2d.shape[0]
    n_tiles = n // ROW_TILE
    tiles_per_seq = seq_len // ROW_TILE
    assert ROW_TILE == ATTN_TILE and tiles_per_seq > 1 and xs2d.shape[0] == ROW_TILE
    attn_tile = lambda i: jnp.maximum(i - 1, 0)
    xrow = pl.BlockSpec((ROW_TILE, D_MODEL), lambda i: (jnp.minimum(i, n_tiles - 1), 0))
    orow = lambda w: pl.BlockSpec((ROW_TILE, w), lambda i: (i, 0))
    arow = pl.BlockSpec((ROW_TILE, BRANCH_W), lambda i: (attn_tile(i), 0))
    rspec = pl.BlockSpec((ROW_TILE, LANES), lambda i: (jnp.where(i == n_tiles, tiles_per_seq, i % tiles_per_seq), 0))
    mem = pl.BlockSpec((None, None, N_MEM, BRANCH_W), lambda i: (l, attn_tile(i) // tiles_per_seq, 0, 0))
    widths = (KV_W, KV_W, BRANCH_W, 3 * D_MODEL)
    dtypes = (F32, F32, F32, BF16)
    return pl.pallas_call(
        functools.partial(_inproj_attn_kernel, l, tiles_per_seq), name="inproj_attn",
        grid=(n_tiles + 1,),
        in_specs=[pl.BlockSpec(memory_space=pltpu.SMEM), xrow, _full((ROW_TILE, D_MODEL)), _full(g.shape),
                  _layer((D_MODEL, IN_W), l), rspec, rspec, rspec, _full(qg.shape), _full(kg.shape),
                  _full(mqg.shape), _full((256, 256)), mem, mem],
        out_specs=[orow(w) for w in widths] + [arow, arow] + [_full((ROW_TILE, BRANCH_W))] * 2,
        out_shape=[jax.ShapeDtypeStruct((n + ROW_TILE, w), d) for w, d in zip(widths, dtypes)]
                  + [jax.ShapeDtypeStruct((n, BRANCH_W), BF16)] * 2
                  + [jax.ShapeDtypeStruct((ROW_TILE, BRANCH_W), BF16)] * 2,
        scratch_shapes=[pltpu.VMEM((2, ROW_TILE, BRANCH_W), BF16)] * 2
                       + [pltpu.VMEM((2, WINDOW + ROW_TILE, KV_W), BF16)] * 2
                       + [pltpu.VMEM((N_MEM, BRANCH_W), BF16)] * 2,
        compiler_params=_cparams(1),
    )(sinks, x2d, xs2d, g, w_bf, *rope, qg, kg, mqg, ones, mk, mv)


DEC_BLOCK = 8
TOK_PAD = 8


def _pad_rows(x, n):
    return jnp.concatenate([x, jnp.zeros((n - x.shape[0], x.shape[1]), x.dtype)], axis=0)


def _attn_sample_kernel(l, t, sink_ref, q_ref, kn_ref, vn_ref, ckt_ref, cvt_ref, qm_ref, mk_ref, mv_ref,
                        *aliased_and_outputs):
    oa_ref, oc_ref, nkt_ref, nvt_ref = aliased_and_outputs[-4:]
    keys = 2 * WINDOW
    rows = N_HEADS * TOK_PAD
    rk = lax.broadcasted_iota(jnp.int32, (rows, keys), 0)
    jk = lax.broadcasted_iota(jnp.int32, (rows, keys), 1)
    tk = rk % TOK_PAD
    mask = (tk < t) & (((jk < WINDOW) & (jk > tk)) | ((jk >= WINDOW) & (jk - WINDOW <= tk)))
    sink = _sink_column(sink_ref, l, list(range(N_HEADS)), TOK_PAD)
    mem_rows = N_MEM * MEM_HEADS
    mem_own = (lax.broadcasted_iota(jnp.int32, (MEM_HEADS * TOK_PAD, mem_rows), 1) % MEM_HEADS
               == lax.broadcasted_iota(jnp.int32, (MEM_HEADS * TOK_PAD, mem_rows), 0) // TOK_PAD)
    is_new = lax.broadcasted_iota(jnp.int32, (HEAD_DIM, WINDOW), 1) >= WINDOW - t
    dsl = [slice(kvh * HEAD_DIM, (kvh + 1) * HEAD_DIM) for kvh in range(N_KV_HEADS)]
    group_rows = KV_GROUP * TOK_PAD

    q_all, qm_all = q_ref[...].astype(F32), qm_ref[...].astype(F32)
    seqs = range(DEC_BLOCK)
    toks = [slice(bi * t, (bi + 1) * t) for bi in seqs]
    knp = [_pad_rows(kn_ref[toks[bi], :], WINDOW) for bi in seqs]
    vnp = [_pad_rows(vn_ref[toks[bi], :], WINDOW) for bi in seqs]
    knt, vnt = [x.T for x in knp], [x.T for x in vnp]
    for bi in seqs:
        for kvh in range(N_KV_HEADS):
            nkt_ref[bi, kvh] = jnp.where(is_new, pltpu.roll(knt[bi][dsl[kvh], :], WINDOW - t, 1),
                                         pltpu.roll(ckt_ref[bi, kvh], WINDOW - t, 1))
            nvt_ref[bi, kvh] = jnp.where(is_new, pltpu.roll(vnt[bi][dsl[kvh], :], WINDOW - t, 1),
                                         pltpu.roll(cvt_ref[bi, kvh], WINDOW - t, 1))
    qg = []
    for bi in seqs:
        q8 = _pad_rows(q_all[toks[bi]], TOK_PAD)
        qg.append([jnp.concatenate([q8[:, hd * HEAD_DIM:(hd + 1) * HEAD_DIM]
                                    for hd in range(kvh * KV_GROUP, (kvh + 1) * KV_GROUP)], axis=0).astype(BF16)
                   for kvh in range(N_KV_HEADS)])
    scores = []
    for bi in seqs:
        per_kvh = [jnp.concatenate([_dot(qg[bi][kvh], ckt_ref[bi, kvh].astype(BF16)),
                                    _dot_nt(qg[bi][kvh], knp[bi][:, dsl[kvh]].astype(BF16))], axis=1)
                   for kvh in range(N_KV_HEADS)]
        scores.append(jnp.where(mask, jnp.concatenate(per_kvh, axis=0), -jnp.inf))
    mscores = []
    for bi in seqs:
        qm8 = jnp.concatenate([qm_all[toks[bi]], jnp.zeros((TOK_PAD - t, BRANCH_W), F32)], axis=0).astype(BF16)
        qst = jnp.concatenate([qm8[:, hd * MEM_HEAD_DIM:(hd + 1) * MEM_HEAD_DIM] for hd in range(MEM_HEADS)],
                              axis=0)
        mscores.append(jnp.where(mem_own, _dot_nt(qst, mk_ref[bi].astype(BF16)), -jnp.inf))
    probs = [_sink_softmax(scores[bi], sink) for bi in seqs]
    mprobs = [_softmax_rows(mscores[bi]) for bi in seqs]
    outs = []
    for bi in seqs:
        p = probs[bi][0].astype(BF16)
        per_kvh = []
        for kvh in range(N_KV_HEADS):
            pk = p[kvh * group_rows:(kvh + 1) * group_rows]
            per_kvh.append(_dot_nt(pk[:, :WINDOW], cvt_ref[bi, kvh].astype(BF16))
                           + _dot(pk[:, WINDOW:], vnp[bi][:, dsl[kvh]].astype(BF16)))
        outs.append(jnp.concatenate(per_kvh, axis=0) / probs[bi][1])
    mouts = [_dot(mprobs[bi][0].astype(BF16), mv_ref[bi].astype(BF16)) / mprobs[bi][1] for bi in seqs]
    for bi in seqs:
        oa_ref[toks[bi], :] = jnp.concatenate(
            [outs[bi][hd * TOK_PAD:(hd + 1) * TOK_PAD] for hd in range(N_HEADS)], axis=1)[0:t]
        oc_ref[toks[bi], :] = jnp.concatenate(
            [mouts[bi][hd * TOK_PAD:(hd + 1) * TOK_PAD] for hd in range(MEM_HEADS)], axis=1)[0:t]


def _attn_sample(l, t, sinks, q, kn, vn, kv_row0, ckt, cvt, qm, mk, mv, new_caches):
    n = q.shape[0]
    tok = lambda w, row0=0: pl.BlockSpec((DEC_BLOCK * t, w), lambda i: (i + row0 // (DEC_BLOCK * t), 0))
    cache = pl.BlockSpec((None, DEC_BLOCK, N_KV_HEADS, HEAD_DIM, WINDOW), lambda i: (l, i, 0, 0, 0))
    mem = pl.BlockSpec((None, DEC_BLOCK, N_MEM * MEM_HEADS, MEM_HEAD_DIM), lambda i: (l, i, 0, 0))
    args = [sinks, q, kn, vn, ckt, cvt, qm, mk, mv]
    in_specs = [pl.BlockSpec(memory_space=pltpu.SMEM), tok(BRANCH_W), tok(KV_W, kv_row0), tok(KV_W, kv_row0),
                cache, cache, tok(BRANCH_W), mem, mem]
    aliases = {}
    if new_caches is not None:
        aliases = {len(args): 2, len(args) + 1: 3}
        args += list(new_caches)
        in_specs += [pl.BlockSpec(memory_space=pl.ANY)] * 2
    return pl.pallas_call(
        functools.partial(_attn_sample_kernel, l, t), name="attn_sample",
        grid=(n // (DEC_BLOCK * t),),
        in_specs=in_specs,
        out_specs=[tok(BRANCH_W), tok(BRANCH_W), cache, cache],
        out_shape=[jax.ShapeDtypeStruct((n, BRANCH_W), F32)] * 2 + [jax.ShapeDtypeStruct(ckt.shape, F32)] * 2,
        input_output_aliases=aliases,
        compiler_params=_cparams(1),
    )(*args)


def _ssm_tables(a_re, a_im, log_dt, b_re, b_im, c_re, c_im, chunk):
    hp = lax.Precision.HIGHEST
    dt = jnp.exp(log_dt)[..., None]
    mag = jnp.exp(a_re * dt)
    lam_re, lam_im = mag * jnp.cos(a_im * dt), mag * jnp.sin(a_im * dt)
    den = a_re * a_re + a_im * a_im
    nr, ni = lam_re - 1.0, lam_im
    g_re = (nr * a_re + ni * a_im) / den
    g_im = (ni * a_re - nr * a_im) / den
    bg_re = g_re[..., None] * b_re - g_im[..., None] * b_im
    bg_im = g_re[..., None] * b_im + g_im[..., None] * b_re
    def powers(d):
        d = d.astype(F32)[:, None, None, None]
        pm = jnp.exp(d * (a_re * dt)[None])
        return pm * jnp.cos(d * (a_im * dt)[None]), pm * jnp.sin(d * (a_im * dt)[None])

    def times_gb(p_re, p_im):
        return (p_re[..., None] * bg_re[None] - p_im[..., None] * bg_im[None],
                p_re[..., None] * bg_im[None] + p_im[..., None] * bg_re[None])

    pw_re, pw_im = powers(jnp.arange(chunk + 1))
    z_re, z_im = times_gb(pw_re[:chunk], pw_im[:chunk])
    zrev_re, zrev_im = times_gb(*powers(jnp.arange(chunk - 1, -1, -1)))
    kern = (jnp.einsum('lgop,dlgpi->dlgio', c_re, z_re, precision=hp)
            - jnp.einsum('lgop,dlgpi->dlgio', c_im, z_im, precision=hp))
    nl = a_re.shape[0]
    nb, gb = N_LANE_BLOCKS, GROUPS_PER_LANE_BLOCK
    kd = jnp.transpose(kern.reshape(chunk, nl, nb, gb, SSM_GROUP_CH, SSM_GROUP_CH), (1, 2, 0, 3, 4, 5))
    kd = kd.reshape(nl, nb, chunk, LANES, SSM_GROUP_CH)

    def to_rows(z):
        z = z.reshape(chunk, nl, nb, gb, SSM_STATE, SSM_GROUP_CH)
        return jnp.transpose(z, (1, 2, 0, 3, 5, 4)).reshape(nl, nb, chunk, LANES, SSM_STATE)

    zr, zi = to_rows(zrev_re), to_rows(zrev_im)
    zc = jnp.concatenate([zr, zr, zi, zi], axis=-1)

    pr, pi = pw_re[1:], pw_im[1:]
    f_a = c_re[None] * pr[:, :, :, None, :] - c_im[None] * pi[:, :, :, None, :]
    f_b = c_re[None] * pi[:, :, :, None, :] + c_im[None] * pr[:, :, :, None, :]

    def from_cols(f):
        f = f.reshape(chunk, nl, nb, gb, SSM_GROUP_CH, SSM_STATE)
        return jnp.transpose(f, (1, 2, 0, 4, 3, 5)).reshape(nl, nb, chunk, SSM_GROUP_CH, STATE_W)

    ft = jnp.concatenate([from_cols(f_a), from_cols(-f_b)], axis=-1)

    kw = chunk * LANES
    spec = lambda *shape: pl.BlockSpec((None, None) + shape, lambda l, n: (l, n) + (0,) * len(shape))
    intra, to_st, from_st_t = pl.pallas_call(
        _ssm_table_kernel, name="ssm_tables",
        grid=(nl, nb),
        in_specs=[spec(chunk, LANES, SSM_GROUP_CH), spec(chunk, LANES, 2 * LANES),
                  spec(chunk, SSM_GROUP_CH, 2 * STATE_W)],
        out_specs=[spec(kw, kw), spec(kw, 2 * STATE_W), spec(kw, 2 * STATE_W)],
        out_shape=[jax.ShapeDtypeStruct((nl, nb, kw, kw), BF16),
                   jax.ShapeDtypeStruct((nl, nb, kw, 2 * STATE_W), BF16),
                   jax.ShapeDtypeStruct((nl, nb, kw, 2 * STATE_W), BF16)],
        compiler_params=_cparams(2),
    )(kd, zc, ft)
    decay = lambda d: (pw_re[d].reshape(nl, nb, 1, STATE_W), pw_im[d].reshape(nl, nb, 1, STATE_W))
    return (intra, to_st, from_st_t), decay


def _ssm_table_kernel(kd_ref, zc_ref, ft_ref, intra_ref, to_ref, fromt_ref):
    chunk = kd_ref.shape[0]
    gch = SSM_GROUP_CH
    r = lax.broadcasted_iota(jnp.int32, (LANES, LANES), 0)
    c = lax.broadcasted_iota(jnp.int32, (LANES, LANES), 1)
    same_group = (r // gch) == (c // gch)
    spread = (lax.broadcasted_iota(jnp.int32, (gch, LANES), 1) % gch
              == lax.broadcasted_iota(jnp.int32, (gch, LANES), 0)).astype(BF16)
    spread_t = (lax.broadcasted_iota(jnp.int32, (LANES, gch), 0) % gch
                == lax.broadcasted_iota(jnp.int32, (LANES, gch), 1)).astype(BF16)
    rs = lax.broadcasted_iota(jnp.int32, (LANES, 2 * STATE_W), 0)
    cs = lax.broadcasted_iota(jnp.int32, (LANES, 2 * STATE_W), 1)
    own_state = (rs // gch) == ((cs % STATE_W) // SSM_STATE)

    lag_blocks = [jnp.where(same_group, _dot(kd_ref[d].astype(BF16), spread), 0.0).astype(BF16)
                  for d in range(chunk)]
    zero = jnp.zeros((LANES, LANES), BF16)
    for s in range(chunk):
        for t in range(chunk):
            intra_ref[s * LANES:(s + 1) * LANES, t * LANES:(t + 1) * LANES] = (
                lag_blocks[t - s] if t >= s else zero)
        z = zc_ref[s]
        full = jnp.concatenate([z[:, :LANES]] * (STATE_W // LANES) + [z[:, LANES:]] * (STATE_W // LANES), axis=1)
        to_ref[s * LANES:(s + 1) * LANES, :] = jnp.where(own_state, full, 0.0).astype(BF16)
        f = _dot(spread_t, ft_ref[s].astype(BF16))
        fromt_ref[s * LANES:(s + 1) * LANES, :] = jnp.where(own_state, f, 0.0).astype(BF16)


SSM_SEQS = 2


def _chunk_rows(ref, chunk):
    n = ref.shape[0] // chunk
    return jnp.concatenate([ref[pl.ds(s, n, stride=chunk), :] for s in range(chunk)], axis=1)


def _store_chunk_rows(ref, y, chunk):
    n = ref.shape[0] // chunk
    for t in range(chunk):
        ref[pl.ds(t, n, stride=chunk), :] = y[:, t * LANES:(t + 1) * LANES]


def _ssm_prompt_kernel(u_ref, intra_ref, to_ref, fromt_ref, dre_ref, dim_ref,
                       y_ref, hre_ref, him_ref, st_scr):
    chunk = PROMPT_CHUNK
    nchunks = u_ref.shape[0] // (chunk * SSM_SEQS)
    x = _chunk_rows(u_ref, chunk).astype(BF16)
    kw = chunk * LANES
    y = jnp.concatenate([_dot(x[:, :c + MXU_TILE], intra_ref[:c + MXU_TILE, c:c + MXU_TILE])
                         for c in range(0, kw, MXU_TILE)], axis=1)
    add_all = _dot(x, to_ref[...])
    nblk = STATE_W // LANES
    assert SSM_SEQS * nblk == SUBLANES
    slots = [(seq, cb) for seq in range(SSM_SEQS) for cb in range(nblk)]

    def slot_rows(seq, cb):
        return pl.ds(seq * nblk + cb, nchunks, stride=SUBLANES)

    for c in range(2):
        for seq, cb in slots:
            col = (c * nblk + cb) * LANES
            st_scr[c, slot_rows(seq, cb), :] = add_all[seq * nchunks:(seq + 1) * nchunks, col:col + LANES]
    dre = jnp.concatenate([dre_ref[:, cb * LANES:(cb + 1) * LANES] for _, cb in slots], axis=0)
    dim = jnp.concatenate([dim_ref[:, cb * LANES:(cb + 1) * LANES] for _, cb in slots], axis=0)

    def step(k, carry):
        hr, hi = carry
        tile = pl.ds(pl.multiple_of(k * SUBLANES, SUBLANES), SUBLANES)
        add_r, add_i = st_scr[0, tile, :], st_scr[1, tile, :]
        st_scr[0, tile, :] = hr
        st_scr[1, tile, :] = hi
        return dre * hr - dim * hi + add_r, dre * hi + dim * hr + add_i

    zero = jnp.zeros((SUBLANES, LANES), F32)
    hr, hi = lax.fori_loop(0, nchunks, step, (zero, zero), unroll=8)
    hre_ref[...] = hr
    him_ref[...] = hi
    h_in = jnp.concatenate(
        [jnp.concatenate([st_scr[c, slot_rows(seq, cb), :] for seq in range(SSM_SEQS)], axis=0)
         for c in range(2) for cb in range(nblk)], axis=1).astype(BF16)
    _store_chunk_rows(y_ref, y + _dot_nt(h_in, fromt_ref[...]), chunk)


def _ssm_prompt(l, u2d, n_seqs, seq_len, intra, to_st, from_st_t, dec_re, dec_im):
    n = n_seqs * seq_len
    chunk = PROMPT_CHUNK
    rows = SSM_SEQS * seq_len
    steps = n // rows
    kw = chunk * LANES
    data = pl.BlockSpec((rows, LANES), lambda j, h: (h, j))
    wspec = lambda r, c: pl.BlockSpec((None, None, r, c), lambda j, h: (l, j, 0, 0))
    st_out = pl.BlockSpec((None, None, SUBLANES, LANES), lambda j, h: (j, h, 0, 0))
    st_shape = jax.ShapeDtypeStruct((N_LANE_BLOCKS, steps, SUBLANES, LANES), F32)
    return pl.pallas_call(
        _ssm_prompt_kernel, name="ssm_prompt",
        grid=(N_LANE_BLOCKS, steps),
        in_specs=[data, wspec(kw, kw), wspec(kw, 2 * STATE_W), wspec(kw, 2 * STATE_W),
                  wspec(1, STATE_W), wspec(1, STATE_W)],
        out_specs=[data, st_out, st_out],
        out_shape=[jax.ShapeDtypeStruct((n, BRANCH_W), F32), st_shape, st_shape],
        scratch_shapes=[pltpu.VMEM((2, SUBLANES * seq_len // chunk, LANES), F32)],
        compiler_params=_cparams(2),
    )(u2d, intra, to_st, from_st_t, dec_re, dec_im)


def _ssm_sample_kernel(chunk, u_ref, h0r_ref, h0i_ref, intra_ref, to_ref, fromt_ref, dre_ref, dim_ref,
                       y_ref, hre_ref, him_ref):
    x = _chunk_rows(u_ref, chunk).astype(BF16)
    hr, hi = h0r_ref[...], h0i_ref[...]
    dre, dim = dre_ref[...], dim_ref[...]
    add = _dot(x, to_ref[...])
    hre_ref[...] = dre * hr - dim * hi + add[:, :STATE_W]
    him_ref[...] = dre * hi + dim * hr + add[:, STATE_W:]
    h0 = jnp.concatenate([hr, hi], axis=1).astype(BF16)
    _store_chunk_rows(y_ref, _dot(x, intra_ref[...]) + _dot_nt(h0, fromt_ref[...]), chunk)


def _ssm_sample(l, u2d, row0, n, t, h0_re, h0_im, intra, to_st, from_st_t, dec_re, dec_im):
    b = n // t
    kw = t * LANES
    assert PROMPT_CHUNK % t == 0 and row0 % n == 0
    last = PROMPT_CHUNK // t - 1
    data_in = pl.BlockSpec((n, LANES), lambda j: (row0 // n, j))
    data = pl.BlockSpec((n, LANES), lambda j: (0, j))
    st_in = pl.BlockSpec((None, b, STATE_W), lambda j: (l, 0, j))
    st_out = pl.BlockSpec((b, STATE_W), lambda j: (0, j))
    wspec = lambda r, c, rb=0: pl.BlockSpec((None, None, r, c), lambda j: (l, j, rb, 0))
    return pl.pallas_call(
        functools.partial(_ssm_sample_kernel, t), name="ssm_sample",
        grid=(N_LANE_BLOCKS,),
        in_specs=[data_in, st_in, st_in, wspec(kw, kw), wspec(kw, 2 * STATE_W, last), wspec(kw, 2 * STATE_W),
                  wspec(1, STATE_W), wspec(1, STATE_W)],
        out_specs=[data, st_out, st_out],
        out_shape=[jax.ShapeDtypeStruct((n, BRANCH_W), F32),
                   jax.ShapeDtypeStruct((b, N_LANE_BLOCKS * STATE_W), F32),
                   jax.ShapeDtypeStruct((b, N_LANE_BLOCKS * STATE_W), F32)],
        compiler_params=_cparams(1),
    )(u2d, h0_re, h0_im, intra, to_st, from_st_t, dec_re, dec_im)


FF_CHUNKS = ((0, 512), (512, 512), (1024, 512), (1536, 512), (2048, 512), (2560, 256))


def _merge_ffn_kernel(l, x_ref, oa_ref, y_ref, u_ref, oc_ref, gates_ref, d_ref, wglu_ref, wb_ref, wout_ref,
                      g_ref, wup_ref, wdn_ref, o_ref):
    z = jax.nn.gelu(y_ref[...] + d_ref[l:l + 1, :] * u_ref[...])
    ob = (z * jax.nn.sigmoid(_dot(z.astype(BF16), wglu_ref[...]))).astype(BF16)
    gate = lambda n: gates_ref[:, n * D_MODEL:(n + 1) * D_MODEL].astype(F32)
    merged = gate(0) * _dot(oa_ref[...], wb_ref[0])
    merged = merged + gate(1) * _dot(ob, wb_ref[1])
    merged = merged + gate(2) * _dot(oc_ref[...], wb_ref[2])
    x = x_ref[...] + _dot(merged.astype(BF16), wout_ref[...])
    h = (_rms(x) * g_ref[l:l + 1, :]).astype(BF16)
    acc = x
    for lo, w in FF_CHUNKS:
        up_gate = _dot(h, wup_ref[:, lo:lo + w])
        up = _dot(h, wup_ref[:, D_FF + lo:D_FF + lo + w])
        acc = acc + _dot((jax.nn.silu(up_gate) * up).astype(BF16), wdn_ref[lo:lo + w, :])
    o_ref[...] = acc


def _merge_ffn(l, x2d, oa, y, u, oc, gates, proj_row0, d, wglu, wb, wout, g_ffn, wup, wdn):
    n = x2d.shape[0]
    assert proj_row0 % ROW_TILE == 0
    row = lambda w, row0=0: pl.BlockSpec((ROW_TILE, w), lambda i: (i + row0 // ROW_TILE, 0))
    return pl.pallas_call(
        functools.partial(_merge_ffn_kernel, l), name="merge_ffn",
        grid=(n // ROW_TILE,),
        in_specs=[row(D_MODEL), row(BRANCH_W), row(BRANCH_W), row(BRANCH_W, proj_row0), row(BRANCH_W),
                  row(3 * D_MODEL, proj_row0), _full(d.shape), _layer((BRANCH_W, BRANCH_W), l),
                  _layer((3, BRANCH_W, D_MODEL), l), _layer((D_MODEL, D_MODEL), l),
                  _full(g_ffn.shape), _layer((D_MODEL, 2 * D_FF), l), _layer((D_FF, D_MODEL), l)],
        out_specs=row(D_MODEL),
        out_shape=jax.ShapeDtypeStruct((n, D_MODEL), F32),
        compiler_params=_cparams(1),
    )(x2d, oa, y, u, oc, gates, d, wglu, wb, wout, g_ffn, wup, wdn)


def _rope_tables(pos):
    half = HEAD_DIM // 2
    inv = ROPE_THETA ** (-jnp.arange(half, dtype=F32) / half)
    ang = pos[:, None] * inv[None, :]
    cos, sin = jnp.cos(ang), jnp.sin(ang)
    zero = jnp.zeros_like(sin)
    two = lambda a: jnp.concatenate([a, a], axis=1)
    return (two(jnp.concatenate([cos, cos], axis=1)), two(jnp.concatenate([-sin, zero], axis=1)),
            two(jnp.concatenate([zero, sin], axis=1)))


def _state_from_blocks(h):
    nb, steps, _, _ = h.shape
    h = h.reshape(nb, steps, SSM_SEQS, STATE_W // LANES, LANES)
    return jnp.transpose(h, (1, 2, 0, 3, 4)).reshape(steps * SSM_SEQS, SSM_GROUPS, SSM_STATE)


def kernel(x_prompt, x_sample, cache_swa_k, cache_swa_v, state_ssm_re, state_ssm_im, cache_mem_k,
           cache_mem_v, mem_prompt, attn_norm, w_in, q_norm, k_norm, attn_sinks, ssm_a_re, ssm_a_im,
           ssm_log_dt, ssm_b_re, ssm_b_im, ssm_c_re, ssm_c_im, ssm_d, ssm_w_glu, mem_norm, w_mem_kv,
           mem_q_norm, mem_k_norm, w_branch, w_out, ffn_norm, w_ffn_up, w_ffn_down):
    depth = w_in.shape[0]
    b, s, _ = x_prompt.shape
    db, t, _ = x_sample.shape
    assert db * t == ROW_TILE and s % ROW_TILE == 0

    rope = _rope_tables(jnp.concatenate([jnp.arange(s, dtype=F32),
                                         jnp.tile(PAST_LEN + jnp.arange(t, dtype=F32), db)]))
    ssm_params = (ssm_a_re, ssm_a_im, ssm_log_dt, ssm_b_re, ssm_b_im, ssm_c_re, ssm_c_im)
    tables, decay = _ssm_tables(*ssm_params, PROMPT_CHUNK)
    tab_p, tab_s = tables + decay(PROMPT_CHUNK), tables + decay(t)
    blk = jnp.arange(256) // HEAD_DIM
    ones = (blk[:, None] == blk[None, :]).astype(BF16)

    w_in_bf, w_mem_bf = w_in.astype(BF16), w_mem_kv.astype(BF16)
    wglu, wb, wout = ssm_w_glu.astype(BF16), w_branch.astype(BF16), w_out.astype(BF16)
    wup, wdn = w_ffn_up.astype(BF16), w_ffn_down.astype(BF16)
    qg, kg = jnp.tile(q_norm, (1, N_HEADS)), jnp.tile(k_norm, (1, N_KV_HEADS))
    ckt = jnp.transpose(cache_swa_k, (0, 1, 3, 4, 2))
    cvt = jnp.transpose(cache_swa_v, (0, 1, 3, 4, 2))
    cmk = cache_mem_k.reshape(depth, db, N_MEM * MEM_HEADS, MEM_HEAD_DIM)
    cmv = cache_mem_v.reshape(depth, db, N_MEM * MEM_HEADS, MEM_HEAD_DIM)
    h0r = state_ssm_re.reshape(depth, db, SSM_GROUPS * SSM_STATE)
    h0i = state_ssm_im.reshape(depth, db, SSM_GROUPS * SSM_STATE)

    yp = x_prompt.reshape(b * s, D_MODEL)
    ys = x_sample.reshape(db * t, D_MODEL)
    mem2d = mem_prompt.reshape(b * N_MEM, D_MODEL)
    outs = [[] for _ in range(6)]
    n = b * s
    mk, mv = _memkv(mem2d, mem_norm, w_mem_bf, mem_k_norm)
    mk4, mv4 = mk.reshape(depth, b, N_MEM, BRANCH_W), mv.reshape(depth, b, N_MEM, BRANCH_W)
    new_caches = None
    for l in range(depth):
        k, v, u, gates, oa, oc, qs, mqs = _inproj_attn(l, attn_sinks, yp, ys, s, attn_norm, w_in_bf, rope, qg, kg,
                                                       mem_q_norm, ones, mk4, mv4)
        ysm, hre, him = _ssm_prompt(l, u, b, s, *tab_p)
        yp = _merge_ffn(l, yp, oa, ysm, u, oc, gates, 0, ssm_d, wglu, wb, wout, ffn_norm, wup, wdn)

        oas, ocs, *new_caches = _attn_sample(l, t, attn_sinks, qs, k, v, n, ckt, cvt, mqs, cmk, cmv, new_caches)
        yss, hrs, his = _ssm_sample(l, u, n, db * t, t, h0r, h0i, *tab_s)
        ys = _merge_ffn(l, ys, oas.astype(BF16), yss, u, ocs.astype(BF16), gates, n, ssm_d, wglu, wb, wout,
                        ffn_norm, wup, wdn)

        last_rows = lambda a: jnp.stack([a[(bi + 1) * s - WINDOW:(bi + 1) * s] for bi in range(b)]).reshape(
            b, WINDOW, N_KV_HEADS, HEAD_DIM)
        new = (last_rows(k), last_rows(v), _state_from_blocks(hre), _state_from_blocks(him),
               hrs.reshape(db, SSM_GROUPS, SSM_STATE), his.reshape(db, SSM_GROUPS, SSM_STATE))
        for lst, a in zip(outs, new):
            lst.append(a)

    swa_k_p, swa_v_p, ssm_re_p, ssm_im_p, ssm_re_s, ssm_im_s = (jnp.stack(o) for o in outs)
    mem_shape = (depth, b, N_MEM, MEM_HEADS, MEM_HEAD_DIM)
    swa_k_s, swa_v_s = (jnp.transpose(c, (0, 1, 4, 2, 3)) for c in new_caches)
    return (yp.reshape(b, s, D_MODEL), ys.reshape(db, t, D_MODEL), swa_k_p, swa_v_p, ssm_re_p, ssm_im_p,
            mk.reshape(mem_shape), mv.reshape(mem_shape), swa_k_s, swa_v_s, ssm_re_s, ssm_im_s)
```

```python
import functools
import math

import jax
import jax.numpy as jnp
from jax import lax
from jax.experimental import pallas as pl
from jax.experimental.pallas import tpu as pltpu

F32 = jnp.float32
BF16 = jnp.bfloat16

D_MODEL = 1024
BRANCH_W = 512
HEAD_DIM = 64
N_HEADS = 8
N_KV_HEADS = 2
KV_GROUP = 4
KV_W = N_KV_HEADS * HEAD_DIM
WINDOW = 128
ROPE_THETA = 10000.0
PAST_LEN = 16384
SSM_GROUPS = 32
SSM_GROUP_CH = 16
SSM_STATE = 64
N_MEM = 256
MEM_HEADS = 4
MEM_HEAD_DIM = 128
D_FF = 2816
RMS_EPS = 1e-6

Q_OFF, K_OFF, V_OFF, U_OFF, MQ_OFF, G_OFF = 0, 512, 640, 768, 1280, 1792
IN_W = G_OFF + 3 * D_MODEL

LANES = 128
SUBLANES = 8
MXU_TILE = 256
GROUPS_PER_LANE_BLOCK = LANES // SSM_GROUP_CH
N_LANE_BLOCKS = BRANCH_W // LANES
STATE_W = GROUPS_PER_LANE_BLOCK * SSM_STATE
PROMPT_CHUNK = 8
ROW_TILE = 512
VMEM_LIMIT = 56 * 1024 * 1024


def _dot(a, b):
    return jnp.dot(a, b, preferred_element_type=F32)


def _dot_nt(a, b):
    return lax.dot_general(a, b, (((1,), (1,)), ((), ())), preferred_element_type=F32)


def _cparams(n_axes):
    return pltpu.CompilerParams(dimension_semantics=("arbitrary",) * n_axes,
                                vmem_limit_bytes=VMEM_LIMIT)


def _full(shape):
    nd = len(shape)
    return pl.BlockSpec(shape, lambda *_: (0,) * nd)


def _layer(shape, l):
    nd = len(shape)
    return pl.BlockSpec((None,) + tuple(shape), lambda *_: (l,) + (0,) * nd, pipeline_mode=pl.Buffered(1))


def _sigmoid(x):
    return 0.5 * jnp.tanh(0.5 * x) + 0.5


def _rms(x):
    return x * lax.rsqrt(jnp.mean(x * x, axis=-1, keepdims=True) + RMS_EPS)


def _memkv_kernel(mem_ref, g_ref, w_ref, kg_ref, k_ref, v_ref):
    l = pl.program_id(0)
    h = _rms(mem_ref[...]) * g_ref[pl.ds(l, 1), :]
    kv = _dot(h.astype(BF16), w_ref[...])
    kg = kg_ref[pl.ds(l, 1), :]
    for hd in range(MEM_HEADS):
        sl = slice(hd * MEM_HEAD_DIM, (hd + 1) * MEM_HEAD_DIM)
        k_ref[:, sl] = _rms(kv[:, sl]) * kg
    v_ref[...] = kv[:, BRANCH_W:]


def _memkv(mem2d, g, w_bf, kg):
    n = mem2d.shape[0]
    depth = w_bf.shape[0]
    out = pl.BlockSpec((None, ROW_TILE, BRANCH_W), lambda l, i: (l, i, 0))
    return pl.pallas_call(
        _memkv_kernel, name="memkv",
        grid=(depth, n // ROW_TILE),
        in_specs=[pl.BlockSpec((ROW_TILE, D_MODEL), lambda l, i: (i, 0)), _full(g.shape),
                  pl.BlockSpec((None, D_MODEL, 2 * BRANCH_W), lambda l, i: (l, 0, 0)), _full(kg.shape)],
        out_specs=[out, out],
        out_shape=[jax.ShapeDtypeStruct((depth, n, BRANCH_W), F32)] * 2,
        compiler_params=_cparams(2),
    )(mem2d, g, w_bf, kg)


def _rotary(x, cos, sin_lo, sin_hi):
    w = x.shape[-1]
    half = HEAD_DIM // 2
    return x * cos + pltpu.roll(x, w - half, 1) * sin_lo + pltpu.roll(x, half, 1) * sin_hi


GATE_CHUNK = 512
N_GATE_CHUNKS = 3 * D_MODEL // GATE_CHUNK


class _Projection:
    def __init__(self, x, g, w_ref, rope, qg, kg, mqg, ones):
        self.hb = (_rms(x) * g).astype(BF16)
        self.w_ref, self.qg, self.kg, self.mqg = w_ref, qg, kg, mqg
        self.ones = ones
        self.rope = rope

    def queries(self):
        zq = _dot(self.hb, self.w_ref[:, Q_OFF:K_OFF])
        sq = (zq * zq).astype(BF16)
        ssq = jnp.concatenate([_dot(sq[:, :256], self.ones), _dot(sq[:, 256:], self.ones)], axis=1)
        qn = zq * lax.rsqrt(ssq * (1.0 / HEAD_DIM) + RMS_EPS) * self.qg
        rope4 = (jnp.concatenate([t] * 4, axis=1) for t in self.rope)
        return (_rotary(qn, *rope4) * (1.0 / math.sqrt(HEAD_DIM))).astype(BF16)

    def keys_values(self):
        zkv = _dot(self.hb, self.w_ref[:, K_OFF:U_OFF])
        zk = zkv[:, :KV_W]
        ssk = _dot((zk * zk).astype(BF16), self.ones[:KV_W, :KV_W])
        kn = zk * lax.rsqrt(ssk * (1.0 / HEAD_DIM) + RMS_EPS) * self.kg
        return _rotary(kn, *self.rope), zkv[:, KV_W:]

    def ssm_input(self):
        return _dot(self.hb, self.w_ref[:, U_OFF:MQ_OFF])

    def memory_queries(self):
        zm = _dot(self.hb, self.w_ref[:, MQ_OFF:G_OFF])
        mqg = self.mqg * (1.0 / math.sqrt(MEM_HEAD_DIM))
        return jnp.concatenate([_rms(zm[:, hd * MEM_HEAD_DIM:(hd + 1) * MEM_HEAD_DIM]) * mqg
                                for hd in range(MEM_HEADS)], axis=1).astype(BF16)

    def gates(self, c):
        lo = G_OFF + c * GATE_CHUNK
        return _sigmoid(_dot(self.hb, self.w_ref[:, lo:lo + GATE_CHUNK])).astype(BF16)


ATTN_TILE = 512


def _softmax_rows(s):
    m = jnp.max(s, axis=-1, keepdims=True)
    p = jnp.exp(s - m)
    return p, jnp.sum(p, axis=-1, keepdims=True)


def _sink_column(sink_ref, l, heads, rows_per_head):
    n = len(heads) * rows_per_head
    rcol = lax.broadcasted_iota(jnp.int32, (n, 1), 0)
    sink = jnp.full((n, 1), sink_ref[l, heads[-1]], F32)
    for g in range(len(heads) - 2, -1, -1):
        sink = jnp.where(rcol < (g + 1) * rows_per_head, sink_ref[l, heads[g]], sink)
    return sink


def _sink_softmax(s, sink):
    m = jnp.maximum(jnp.max(s, axis=-1, keepdims=True), sink)
    p = jnp.exp(s - m)
    return p, jnp.sum(p, axis=-1, keepdims=True) + jnp.exp(sink - m)


def _attend_sub_block(sub, mask, sinks, q_ref, mq_ref, k_ref, v_ref, mk_ref, mv_ref, oa_ref, oc_ref):
    qrows = slice(sub * WINDOW, (sub + 1) * WINDOW)
    krows = slice(sub * WINDOW, (sub + 2) * WINDOW)
    ksl = [slice(kvh * HEAD_DIM, (kvh + 1) * HEAD_DIM) for kvh in range(N_KV_HEADS)]
    assert ATTN_TILE // WINDOW == MEM_HEADS
    msl = [slice(sub * MEM_HEAD_DIM, (sub + 1) * MEM_HEAD_DIM)]
    mrows = slice(0, ATTN_TILE)
    heads = [[kvh * KV_GROUP + g for g in range(KV_GROUP)] for kvh in range(N_KV_HEADS)]
    scores = []
    for kvh in range(N_KV_HEADS):
        qg = jnp.concatenate([q_ref[qrows, hd * HEAD_DIM:(hd + 1) * HEAD_DIM] for hd in heads[kvh]], axis=0)
        scores.append(jnp.where(mask, _dot_nt(qg, k_ref[krows, ksl[kvh]]), -jnp.inf))
    mscores = [_dot_nt(mq_ref[mrows, sl], mk_ref[:, sl]) for sl in msl]
    yield
    probs = [_sink_softmax(scores[kvh], sinks[kvh]) for kvh in range(N_KV_HEADS)]
    mprobs = [_softmax_rows(s) for s in mscores]
    yield
    outs = [_dot(probs[kvh][0].astype(BF16), v_ref[krows, ksl[kvh]]) / probs[kvh][1]
            for kvh in range(N_KV_HEADS)]
    mouts = [_dot(mprobs[hd][0].astype(BF16), mv_ref[:, msl[hd]]) / mprobs[hd][1] for hd in range(len(msl))]
    for kvh in range(N_KV_HEADS):
        for g, hd in enumerate(heads[kvh]):
            oa_ref[qrows, hd * HEAD_DIM:(hd + 1) * HEAD_DIM] = outs[kvh][g * WINDOW:(g + 1) * WINDOW].astype(BF16)
    for hd in range(len(msl)):
        oc_ref[mrows, msl[hd]] = mouts[hd].astype(BF16)


def _inproj_attn_kernel(l, tiles_per_seq, sink_ref, x_ref, xs_ref, g_ref, w_ref, cos_ref, slo_ref, shi_ref,
                        qg_ref, kg_ref, mqg_ref, ones_ref, mk_ref, mv_ref,
                        k_ref, v_ref, u_ref, gates_ref, oa_ref, oc_ref, qs_ref, mqs_ref,
                        q_scr, mq_scr, k_scr, v_scr, mk_scr, mv_scr):
    i = pl.program_id(0)
    cur, prv = i % 2, (i + 1) % 2
    decode_step = i == pl.num_programs(0) - 1

    @pl.when(i == 0)
    def _():
        for scr in (q_scr, mq_scr, k_scr, v_scr, mk_scr, mv_scr):
            scr[...] = jnp.zeros(scr.shape, scr.dtype)

    @pl.when(i % tiles_per_seq == 1)
    def _():
        mk_scr[...] = mk_ref[...].astype(BF16)
        mv_scr[...] = mv_ref[...].astype(BF16)

    rows = KV_GROUP * WINDOW
    r = lax.broadcasted_iota(jnp.int32, (rows, 2 * WINDOW), 0)
    j = lax.broadcasted_iota(jnp.int32, (rows, 2 * WINDOW), 1)
    qi = r % WINDOW
    band = (j > qi) & (j <= qi + WINDOW)
    first_band = band & ((j >= WINDOW) | (i % tiles_per_seq != 1))
    sinks = [_sink_column(sink_ref, l, [kvh * KV_GROUP + g for g in range(KV_GROUP)], WINDOW)
             for kvh in range(N_KV_HEADS)]

    def attention_stages():
        for sub in range(ATTN_TILE // WINDOW):
            yield from _attend_sub_block(sub, first_band if sub == 0 else band, sinks, q_scr.at[prv],
                                         mq_scr.at[prv], k_scr.at[prv], v_scr.at[prv], mk_scr, mv_scr,
                                         oa_ref, oc_ref)
            yield

    proj = _Projection(jnp.where(decode_step, xs_ref[...], x_ref[...]), g_ref[l:l + 1, :], w_ref,
                       (cos_ref[...], slo_ref[...], shi_ref[...]), qg_ref[l:l + 1, :], kg_ref[l:l + 1, :],
                       mqg_ref[l:l + 1, :], ones_ref[...])

    def store_queries():
        q = proj.queries()
        q_scr[cur] = q
        qs_ref[...] = q

    def store_keys_values():
        k, v = proj.keys_values()
        k_ref[...], v_ref[...] = k, v
        k_scr[cur, :WINDOW, :] = k_scr[prv, ATTN_TILE:, :]
        v_scr[cur, :WINDOW, :] = v_scr[prv, ATTN_TILE:, :]
        k_scr[cur, WINDOW:, :] = k.astype(BF16)
        v_scr[cur, WINDOW:, :] = v.astype(BF16)

    def store_ssm_input():
        u_ref[...] = proj.ssm_input()

    def store_memory_queries():
        mq = proj.memory_queries()
        mq_scr[cur] = mq
        mqs_ref[...] = mq

    def store_gates(c):
        gates_ref[:, c * GATE_CHUNK:(c + 1) * GATE_CHUNK] = proj.gates(c)

    segments = [store_queries, store_keys_values, store_ssm_input, store_memory_queries]
    segments += [functools.partial(store_gates, c) for c in range(N_GATE_CHUNKS)]
    stages = attention_stages()
    n_stages = 3 * (ATTN_TILE // WINDOW)
    for n, segment in enumerate(segments):
        for _ in range((n + 1) * n_stages // len(segments) - n * n_stages // len(segments)):
            next(stages)
        segment()


def _inproj_attn(l, sinks, x2d, xs2d, seq_len, g, w_bf, rope, qg, kg, mqg, ones, mk, mv):
    n = x2d.shape[0]
    n_tiles = n // ROW_TILE
    tiles_per_seq = seq_len // ROW_TILE
    assert ROW_TILE == ATTN_TILE and tiles_per_seq > 1 and xs2d.shape[0] == ROW_TILE
    attn_tile = lambda i: jnp.maximum(i - 1, 0)
    xrow = pl.BlockSpec((ROW_TILE, D_MODEL), lambda i: (jnp.minimum(i, n_tiles - 1), 0))
    orow = lambda w: pl.BlockSpec((ROW_TILE, w), lambda i: (i, 0))
    arow = pl.BlockSpec((ROW_TILE, BRANCH_W), lambda i: (attn_tile(i), 0))
    rspec = pl.BlockSpec((ROW_TILE, LANES), lambda i: (jnp.where(i == n_tiles, tiles_per_seq, i % tiles_per_seq), 0))
    mem = pl.BlockSpec((None, None, N_MEM, BRANCH_W), lambda i: (l, attn_tile(i) // tiles_per_seq, 0, 0))
    widths = (KV_W, KV_W, BRANCH_W, 3 * D_MODEL)
    dtypes = (F32, F32, F32, BF16)
    return pl.pallas_call(
        functools.partial(_inproj_attn_kernel, l, tiles_per_seq), name="inproj_attn",
        grid=(n_tiles + 1,),
        in_specs=[pl.BlockSpec(memory_space=pltpu.SMEM), xrow, _full((ROW_TILE, D_MODEL)), _full(g.shape),
                  _layer((D_MODEL, IN_W), l), rspec, rspec, rspec, _full(qg.shape), _full(kg.shape),
                  _full(mqg.shape), _full((256, 256)), mem, mem],
        out_specs=[orow(w) for w in widths] + [arow, arow] + [_full((ROW_TILE, BRANCH_W))] * 2,
        out_shape=[jax.ShapeDtypeStruct((n + ROW_TILE, w), d) for w, d in zip(widths, dtypes)]
                  + [jax.ShapeDtypeStruct((n, BRANCH_W), BF16)] * 2
                  + [jax.ShapeDtypeStruct((ROW_TILE, BRANCH_W), BF16)] * 2,
        scratch_shapes=[pltpu.VMEM((2, ROW_TILE, BRANCH_W), BF16)] * 2
                       + [pltpu.VMEM((2, WINDOW + ROW_TILE, KV_W), BF16)] * 2
                       + [pltpu.VMEM((N_MEM, BRANCH_W), BF16)] * 2,
        compiler_params=_cparams(1),
    )(sinks, x2d, xs2d, g, w_bf, *rope, qg, kg, mqg, ones, mk, mv)


DEC_BLOCK = 8
TOK_PAD = 8


def _pad_rows(x, n):
    return jnp.concatenate([x, jnp.zeros((n - x.shape[0], x.shape[1]), x.dtype)], axis=0)


def _attn_sample_kernel(l, t, sink_ref, q_ref, kn_ref, vn_ref, ckt_ref, cvt_ref, qm_ref, mk_ref, mv_ref,
                        *aliased_and_outputs):
    oa_ref, oc_ref, nkt_ref, nvt_ref = aliased_and_outputs[-4:]
    keys = 2 * WINDOW
    rows = N_HEADS * TOK_PAD
    rk = lax.broadcasted_iota(jnp.int32, (rows, keys), 0)
    jk = lax.broadcasted_iota(jnp.int32, (rows, keys), 1)
    tk = rk % TOK_PAD
    mask = (tk < t) & (((jk < WINDOW) & (jk > tk)) | ((jk >= WINDOW) & (jk - WINDOW <= tk)))
    sink = _sink_column(sink_ref, l, list(range(N_HEADS)), TOK_PAD)
    mem_rows = N_MEM * MEM_HEADS
    mem_own = (lax.broadcasted_iota(jnp.int32, (MEM_HEADS * TOK_PAD, mem_rows), 1) % MEM_HEADS
               == lax.broadcasted_iota(jnp.int32, (MEM_HEADS * TOK_PAD, mem_rows), 0) // TOK_PAD)
    is_new = lax.broadcasted_iota(jnp.int32, (HEAD_DIM, WINDOW), 1) >= WINDOW - t
    dsl = [slice(kvh * HEAD_DIM, (kvh + 1) * HEAD_DIM) for kvh in range(N_KV_HEADS)]
    group_rows = KV_GROUP * TOK_PAD

    q_all, qm_all = q_ref[...].astype(F32), qm_ref[...].astype(F32)
    seqs = range(DEC_BLOCK)
    toks = [slice(bi * t, (bi + 1) * t) for bi in seqs]
    knp = [_pad_rows(kn_ref[toks[bi], :], WINDOW) for bi in seqs]
    vnp = [_pad_rows(vn_ref[toks[bi], :], WINDOW) for bi in seqs]
    knt, vnt = [x.T for x in knp], [x.T for x in vnp]
    for bi in seqs:
        for kvh in range(N_KV_HEADS):
            nkt_ref[bi, kvh] = jnp.where(is_new, pltpu.roll(knt[bi][dsl[kvh], :], WINDOW - t, 1),
                                         pltpu.roll(ckt_ref[bi, kvh], WINDOW - t, 1))
            nvt_ref[bi, kvh] = jnp.where(is_new, pltpu.roll(vnt[bi][dsl[kvh], :], WINDOW - t, 1),
                                         pltpu.roll(cvt_ref[bi, kvh], WINDOW - t, 1))
    qg = []
    for bi in seqs:
        q8 = _pad_rows(q_all[toks[bi]], TOK_PAD)
        qg.append([jnp.concatenate([q8[:, hd * HEAD_DIM:(hd + 1) * HEAD_DIM]
                                    for hd in range(kvh * KV_GROUP, (kvh + 1) * KV_GROUP)], axis=0).astype(BF16)
                   for kvh in range(N_KV_HEADS)])
    scores = []
    for bi in seqs:
        per_kvh = [jnp.concatenate([_dot(qg[bi][kvh], ckt_ref[bi, kvh].astype(BF16)),
                                    _dot_nt(qg[bi][kvh], knp[bi][:, dsl[kvh]].astype(BF16))], axis=1)
                   for kvh in range(N_KV_HEADS)]
        scores.append(jnp.where(mask, jnp.concatenate(per_kvh, axis=0), -jnp.inf))
    mscores = []
    for bi in seqs:
        qm8 = jnp.concatenate([qm_all[toks[bi]], jnp.zeros((TOK_PAD - t, BRANCH_W), F32)], axis=0).astype(BF16)
        qst = jnp.concatenate([qm8[:, hd * MEM_HEAD_DIM:(hd + 1) * MEM_HEAD_DIM] for hd in range(MEM_HEADS)],
                              axis=0)
        mscores.append(jnp.where(mem_own, _dot_nt(qst, mk_ref[bi].astype(BF16)), -jnp.inf))
    probs = [_sink_softmax(scores[bi], sink) for bi in seqs]
    mprobs = [_softmax_rows(mscores[bi]) for bi in seqs]
    outs = []
    for bi in seqs:
        p = probs[bi][0].astype(BF16)
        per_kvh = []
        for kvh in range(N_KV_HEADS):
            pk = p[kvh * group_rows:(kvh + 1) * group_rows]
            per_kvh.append(_dot_nt(pk[:, :WINDOW], cvt_ref[bi, kvh].astype(BF16))
                           + _dot(pk[:, WINDOW:], vnp[bi][:, dsl[kvh]].astype(BF16)))
        outs.append(jnp.concatenate(per_kvh, axis=0) / probs[bi][1])
    mouts = [_dot(mprobs[bi][0].astype(BF16), mv_ref[bi].astype(BF16)) / mprobs[bi][1] for bi in seqs]
    for bi in seqs:
        oa_ref[toks[bi], :] = jnp.concatenate(
            [outs[bi][hd * TOK_PAD:(hd + 1) * TOK_PAD] for hd in range(N_HEADS)], axis=1)[0:t]
        oc_ref[toks[bi], :] = jnp.concatenate(
            [mouts[bi][hd * TOK_PAD:(hd + 1) * TOK_PAD] for hd in range(MEM_HEADS)], axis=1)[0:t]


def _attn_sample(l, t, sinks, q, kn, vn, kv_row0, ckt, cvt, qm, mk, mv, new_caches):
    n = q.shape[0]
    tok = lambda w, row0=0: pl.BlockSpec((DEC_BLOCK * t, w), lambda i: (i + row0 // (DEC_BLOCK * t), 0))
    cache = pl.BlockSpec((None, DEC_BLOCK, N_KV_HEADS, HEAD_DIM, WINDOW), lambda i: (l, i, 0, 0, 0))
    mem = pl.BlockSpec((None, DEC_BLOCK, N_MEM * MEM_HEADS, MEM_HEAD_DIM), lambda i: (l, i, 0, 0))
    args = [sinks, q, kn, vn, ckt, cvt, qm, mk, mv]
    in_specs = [pl.BlockSpec(memory_space=pltpu.SMEM), tok(BRANCH_W), tok(KV_W, kv_row0), tok(KV_W, kv_row0),
                cache, cache, tok(BRANCH_W), mem, mem]
    aliases = {}
    if new_caches is not None:
        aliases = {len(args): 2, len(args) + 1: 3}
        args += list(new_caches)
        in_specs += [pl.BlockSpec(memory_space=pl.ANY)] * 2
    return pl.pallas_call(
        functools.partial(_attn_sample_kernel, l, t), name="attn_sample",
        grid=(n // (DEC_BLOCK * t),),
        in_specs=in_specs,
        out_specs=[tok(BRANCH_W), tok(BRANCH_W), cache, cache],
        out_shape=[jax.ShapeDtypeStruct((n, BRANCH_W), F32)] * 2 + [jax.ShapeDtypeStruct(ckt.shape, F32)] * 2,
        input_output_aliases=aliases,
        compiler_params=_cparams(1),
    )(*args)


def _ssm_tables(a_re, a_im, log_dt, b_re, b_im, c_re, c_im, chunk):
    hp = lax.Precision.HIGHEST
    dt = jnp.exp(log_dt)[..., None]
    mag = jnp.exp(a_re * dt)
    lam_re, lam_im = mag * jnp.cos(a_im * dt), mag * jnp.sin(a_im * dt)
    den = a_re * a_re + a_im * a_im
    nr, ni = lam_re - 1.0, lam_im
    g_re = (nr * a_re + ni * a_im) / den
    g_im = (ni * a_re - nr * a_im) / den
    bg_re = g_re[..., None] * b_re - g_im[..., None] * b_im
    bg_im = g_re[..., None] * b_im + g_im[..., None] * b_re
    def powers(d):
        d = d.astype(F32)[:, None, None, None]
        pm = jnp.exp(d * (a_re * dt)[None])
        return pm * jnp.cos(d * (a_im * dt)[None]), pm * jnp.sin(d * (a_im * dt)[None])

    def times_gb(p_re, p_im):
        return (p_re[..., None] * bg_re[None] - p_im[..., None] * bg_im[None],
                p_re[..., None] * bg_im[None] + p_im[..., None] * bg_re[None])

    pw_re, pw_im = powers(jnp.arange(chunk + 1))
    z_re, z_im = times_gb(pw_re[:chunk], pw_im[:chunk])
    zrev_re, zrev_im = times_gb(*powers(jnp.arange(chunk - 1, -1, -1)))
    kern = (jnp.einsum('lgop,dlgpi->dlgio', c_re, z_re, precision=hp)
            - jnp.einsum('lgop,dlgpi->dlgio', c_im, z_im, precision=hp))
    nl = a_re.shape[0]
    nb, gb = N_LANE_BLOCKS, GROUPS_PER_LANE_BLOCK
    kd = jnp.transpose(kern.reshape(chunk, nl, nb, gb, SSM_GROUP_CH, SSM_GROUP_CH), (1, 2, 0, 3, 4, 5))
    kd = kd.reshape(nl, nb, chunk, LANES, SSM_GROUP_CH)

    def to_rows(z):
        z = z.reshape(chunk, nl, nb, gb, SSM_STATE, SSM_GROUP_CH)
        return jnp.transpose(z, (1, 2, 0, 3, 5, 4)).reshape(nl, nb, chunk, LANES, SSM_STATE)

    zr, zi = to_rows(zrev_re), to_rows(zrev_im)
    zc = jnp.concatenate([zr, zr, zi, zi], axis=-1)

    pr, pi = pw_re[1:], pw_im[1:]
    f_a = c_re[None] * pr[:, :, :, None, :] - c_im[None] * pi[:, :, :, None, :]
    f_b = c_re[None] * pi[:, :, :, None, :] + c_im[None] * pr[:, :, :, None, :]

    def from_cols(f):
        f = f.reshape(chunk, nl, nb, gb, SSM_GROUP_CH, SSM_STATE)
        return jnp.transpose(f, (1, 2, 0, 4, 3, 5)).reshape(nl, nb, chunk, SSM_GROUP_CH, STATE_W)

    ft = jnp.concatenate([from_cols(f_a), from_cols(-f_b)], axis=-1)

    kw = chunk * LANES
    spec = lambda *shape: pl.BlockSpec((None, None) + shape, lambda l, n: (l, n) + (0,) * len(shape))
    intra, to_st, from_st_t = pl.pallas_call(
        _ssm_table_kernel, name="ssm_tables",
        grid=(nl, nb),
        in_specs=[spec(chunk, LANES, SSM_GROUP_CH), spec(chunk, LANES, 2 * LANES),
                  spec(chunk, SSM_GROUP_CH, 2 * STATE_W)],
        out_specs=[spec(kw, kw), spec(kw, 2 * STATE_W), spec(kw, 2 * STATE_W)],
        out_shape=[jax.ShapeDtypeStruct((nl, nb, kw, kw), BF16),
                   jax.ShapeDtypeStruct((nl, nb, kw, 2 * STATE_W), BF16),
                   jax.ShapeDtypeStruct((nl, nb, kw, 2 * STATE_W), BF16)],
        compiler_params=_cparams(2),
    )(kd, zc, ft)
    decay = lambda d: (pw_re[d].reshape(nl, nb, 1, STATE_W), pw_im[d].reshape(nl, nb, 1, STATE_W))
    return (intra, to_st, from_st_t), decay


def _ssm_table_kernel(kd_ref, zc_ref, ft_ref, intra_ref, to_ref, fromt_ref):
    chunk = kd_ref.shape[0]
    gch = SSM_GROUP_CH
    r = lax.broadcasted_iota(jnp.int32, (LANES, LANES), 0)
    c = lax.broadcasted_iota(jnp.int32, (LANES, LANES), 1)
    same_group = (r // gch) == (c // gch)
    spread = (lax.broadcasted_iota(jnp.int32, (gch, LANES), 1) % gch
              == lax.broadcasted_iota(jnp.int32, (gch, LANES), 0)).astype(BF16)
    spread_t = (lax.broadcasted_iota(jnp.int32, (LANES, gch), 0) % gch
                == lax.broadcasted_iota(jnp.int32, (LANES, gch), 1)).astype(BF16)
    rs = lax.broadcasted_iota(jnp.int32, (LANES, 2 * STATE_W), 0)
    cs = lax.broadcasted_iota(jnp.int32, (LANES, 2 * STATE_W), 1)
    own_state = (rs // gch) == ((cs % STATE_W) // SSM_STATE)

    lag_blocks = [jnp.where(same_group, _dot(kd_ref[d].astype(BF16), spread), 0.0).astype(BF16)
                  for d in range(chunk)]
    zero = jnp.zeros((LANES, LANES), BF16)
    for s in range(chunk):
        for t in range(chunk):
            intra_ref[s * LANES:(s + 1) * LANES, t * LANES:(t + 1) * LANES] = (
                lag_blocks[t - s] if t >= s else zero)
        z = zc_ref[s]
        full = jnp.concatenate([z[:, :LANES]] * (STATE_W // LANES) + [z[:, LANES:]] * (STATE_W // LANES), axis=1)
        to_ref[s * LANES:(s + 1) * LANES, :] = jnp.where(own_state, full, 0.0).astype(BF16)
        f = _dot(spread_t, ft_ref[s].astype(BF16))
        fromt_ref[s * LANES:(s + 1) * LANES, :] = jnp.where(own_state, f, 0.0).astype(BF16)


SSM_SEQS = 2


def _chunk_rows(ref, chunk):
    n = ref.shape[0] // chunk
    return jnp.concatenate([ref[pl.ds(s, n, stride=chunk), :] for s in range(chunk)], axis=1)


def _store_chunk_rows(ref, y, chunk):
    n = ref.shape[0] // chunk
    for t in range(chunk):
        ref[pl.ds(t, n, stride=chunk), :] = y[:, t * LANES:(t + 1) * LANES]


def _ssm_prompt_kernel(u_ref, intra_ref, to_ref, fromt_ref, dre_ref, dim_ref,
                       y_ref, hre_ref, him_ref, st_scr):
    chunk = PROMPT_CHUNK
    nchunks = u_ref.shape[0] // (chunk * SSM_SEQS)
    x = _chunk_rows(u_ref, chunk).astype(BF16)
    kw = chunk * LANES
    y = jnp.concatenate([_dot(x[:, :c + MXU_TILE], intra_ref[:c + MXU_TILE, c:c + MXU_TILE])
                         for c in range(0, kw, MXU_TILE)], axis=1)
    add_all = _dot(x, to_ref[...])
    nblk = STATE_W // LANES
    assert SSM_SEQS * nblk == SUBLANES
    slots = [(seq, cb) for seq in range(SSM_SEQS) for cb in range(nblk)]

    def slot_rows(seq, cb):
        return pl.ds(seq * nblk + cb, nchunks, stride=SUBLANES)

    for c in range(2):
        for seq, cb in slots:
            col = (c * nblk + cb) * LANES
            st_scr[c, slot_rows(seq, cb), :] = add_all[seq * nchunks:(seq + 1) * nchunks, col:col + LANES]
    dre = jnp.concatenate([dre_ref[:, cb * LANES:(cb + 1) * LANES] for _, cb in slots], axis=0)
    dim = jnp.concatenate([dim_ref[:, cb * LANES:(cb + 1) * LANES] for _, cb in slots], axis=0)

    def step(k, carry):
        hr, hi = carry
        tile = pl.ds(pl.multiple_of(k * SUBLANES, SUBLANES), SUBLANES)
        add_r, add_i = st_scr[0, tile, :], st_scr[1, tile, :]
        st_scr[0, tile, :] = hr
        st_scr[1, tile, :] = hi
        return dre * hr - dim * hi + add_r, dre * hi + dim * hr + add_i

    zero = jnp.zeros((SUBLANES, LANES), F32)
    hr, hi = lax.fori_loop(0, nchunks, step, (zero, zero), unroll=8)
    hre_ref[...] = hr
    him_ref[...] = hi
    h_in = jnp.concatenate(
        [jnp.concatenate([st_scr[c, slot_rows(seq, cb), :] for seq in range(SSM_SEQS)], axis=0)
         for c in range(2) for cb in range(nblk)], axis=1).astype(BF16)
    _store_chunk_rows(y_ref, y + _dot_nt(h_in, fromt_ref[...]), chunk)


def _ssm_prompt(l, u2d, n_seqs, seq_len, intra, to_st, from_st_t, dec_re, dec_im):
    n = n_seqs * seq_len
    chunk = PROMPT_CHUNK
    rows = SSM_SEQS * seq_len
    steps = n // rows
    kw = chunk * LANES
    data = pl.BlockSpec((rows, LANES), lambda j, h: (h, j))
    wspec = lambda r, c: pl.BlockSpec((None, None, r, c), lambda j, h: (l, j, 0, 0))
    st_out = pl.BlockSpec((None, None, SUBLANES, LANES), lambda j, h: (j, h, 0, 0))
    st_shape = jax.ShapeDtypeStruct((N_LANE_BLOCKS, steps, SUBLANES, LANES), F32)
    return pl.pallas_call(
        _ssm_prompt_kernel, name="ssm_prompt",
        grid=(N_LANE_BLOCKS, steps),
        in_specs=[data, wspec(kw, kw), wspec(kw, 2 * STATE_W), wspec(kw, 2 * STATE_W),
                  wspec(1, STATE_W), wspec(1, STATE_W)],
        out_specs=[data, st_out, st_out],
        out_shape=[jax.ShapeDtypeStruct((n, BRANCH_W), F32), st_shape, st_shape],
        scratch_shapes=[pltpu.VMEM((2, SUBLANES * seq_len // chunk, LANES), F32)],
        compiler_params=_cparams(2),
    )(u2d, intra, to_st, from_st_t, dec_re, dec_im)


def _ssm_sample_kernel(chunk, u_ref, h0r_ref, h0i_ref, intra_ref, to_ref, fromt_ref, dre_ref, dim_ref,
                       y_ref, hre_ref, him_ref):
    x = _chunk_rows(u_ref, chunk).astype(BF16)
    hr, hi = h0r_ref[...], h0i_ref[...]
    dre, dim = dre_ref[...], dim_ref[...]
    add = _dot(x, to_ref[...])
    hre_ref[...] = dre * hr - dim * hi + add[:, :STATE_W]
    him_ref[...] = dre * hi + dim * hr + add[:, STATE_W:]
    h0 = jnp.concatenate([hr, hi], axis=1).astype(BF16)
    _store_chunk_rows(y_ref, _dot(x, intra_ref[...]) + _dot_nt(h0, fromt_ref[...]), chunk)


def _ssm_sample(l, u2d, row0, n, t, h0_re, h0_im, intra, to_st, from_st_t, dec_re, dec_im):
    b = n // t
    kw = t * LANES
    assert PROMPT_CHUNK % t == 0 and row0 % n == 0
    last = PROMPT_CHUNK // t - 1
    data_in = pl.BlockSpec((n, LANES), lambda j: (row0 // n, j))
    data = pl.BlockSpec((n, LANES), lambda j: (0, j))
    st_in = pl.BlockSpec((None, b, STATE_W), lambda j: (l, 0, j))
    st_out = pl.BlockSpec((b, STATE_W), lambda j: (0, j))
    wspec = lambda r, c, rb=0: pl.BlockSpec((None, None, r, c), lambda j: (l, j, rb, 0))
    return pl.pallas_call(
        functools.partial(_ssm_sample_kernel, t), name="ssm_sample",
        grid=(N_LANE_BLOCKS,),
        in_specs=[data_in, st_in, st_in, wspec(kw, kw), wspec(kw, 2 * STATE_W, last), wspec(kw, 2 * STATE_W),
                  wspec(1, STATE_W), wspec(1, STATE_W)],
        out_specs=[data, st_out, st_out],
        out_shape=[jax.ShapeDtypeStruct((n, BRANCH_W), F32),
                   jax.ShapeDtypeStruct((b, N_LANE_BLOCKS * STATE_W), F32),
                   jax.ShapeDtypeStruct((b, N_LANE_BLOCKS * STATE_W), F32)],
        compiler_params=_cparams(1),
    )(u2d, h0_re, h0_im, intra, to_st, from_st_t, dec_re, dec_im)


FF_CHUNKS = ((0, 512), (512, 512), (1024, 512), (1536, 512), (2048, 512), (2560, 256))


def _merge_ffn_kernel(l, x_ref, oa_ref, y_ref, u_ref, oc_ref, gates_ref, d_ref, wglu_ref, wb_ref, wout_ref,
                      g_ref, wup_ref, wdn_ref, o_ref):
    z = jax.nn.gelu(y_ref[...] + d_ref[l:l + 1, :] * u_ref[...])
    ob = (z * _sigmoid(_dot(z.astype(BF16), wglu_ref[...]))).astype(BF16)
    gate = lambda n: gates_ref[:, n * D_MODEL:(n + 1) * D_MODEL].astype(F32)
    merged = gate(0) * _dot(oa_ref[...], wb_ref[0])
    merged = merged + gate(1) * _dot(ob, wb_ref[1])
    merged = merged + gate(2) * _dot(oc_ref[...], wb_ref[2])
    x = x_ref[...] + _dot(merged.astype(BF16), wout_ref[...])
    h = (_rms(x) * g_ref[l:l + 1, :]).astype(BF16)
    acc = x
    for lo, w in FF_CHUNKS:
        up_gate = _dot(h, wup_ref[:, lo:lo + w])
        up = _dot(h, wup_ref[:, D_FF + lo:D_FF + lo + w])
        acc = acc + _dot((up_gate * _sigmoid(up_gate) * up).astype(BF16), wdn_ref[lo:lo + w, :])
    o_ref[...] = acc


def _merge_ffn(l, x2d, oa, y, u, oc, gates, proj_row0, d, wglu, wb, wout, g_ffn, wup, wdn):
    n = x2d.shape[0]
    assert proj_row0 % ROW_TILE == 0
    row = lambda w, row0=0: pl.BlockSpec((ROW_TILE, w), lambda i: (i + row0 // ROW_TILE, 0))
    return pl.pallas_call(
        functools.partial(_merge_ffn_kernel, l), name="merge_ffn",
        grid=(n // ROW_TILE,),
        in_specs=[row(D_MODEL), row(BRANCH_W), row(BRANCH_W), row(BRANCH_W, proj_row0), row(BRANCH_W),
                  row(3 * D_MODEL, proj_row0), _full(d.shape), _layer((BRANCH_W, BRANCH_W), l),
                  _layer((3, BRANCH_W, D_MODEL), l), _layer((D_MODEL, D_MODEL), l),
                  _full(g_ffn.shape), _layer((D_MODEL, 2 * D_FF), l), _layer((D_FF, D_MODEL), l)],
        out_specs=row(D_MODEL),
        out_shape=jax.ShapeDtypeStruct((n, D_MODEL), F32),
        compiler_params=_cparams(1),
    )(x2d, oa, y, u, oc, gates, d, wglu, wb, wout, g_ffn, wup, wdn)


def _rope_tables(pos):
    half = HEAD_DIM // 2
    inv = ROPE_THETA ** (-jnp.arange(half, dtype=F32) / half)
    ang = pos[:, None] * inv[None, :]
    cos, sin = jnp.cos(ang), jnp.sin(ang)
    zero = jnp.zeros_like(sin)
    two = lambda a: jnp.concatenate([a, a], axis=1)
    return (two(jnp.concatenate([cos, cos], axis=1)), two(jnp.concatenate([-sin, zero], axis=1)),
            two(jnp.concatenate([zero, sin], axis=1)))


def _state_from_blocks(h):
    nb, steps, _, _ = h.shape
    h = h.reshape(nb, steps, SSM_SEQS, STATE_W // LANES, LANES)
    return jnp.transpose(h, (1, 2, 0, 3, 4)).reshape(steps * SSM_SEQS, SSM_GROUPS, SSM_STATE)


def kernel(x_prompt, x_sample, cache_swa_k, cache_swa_v, state_ssm_re, state_ssm_im, cache_mem_k,
           cache_mem_v, mem_prompt, attn_norm, w_in, q_norm, k_norm, attn_sinks, ssm_a_re, ssm_a_im,
           ssm_log_dt, ssm_b_re, ssm_b_im, ssm_c_re, ssm_c_im, ssm_d, ssm_w_glu, mem_norm, w_mem_kv,
           mem_q_norm, mem_k_norm, w_branch, w_out, ffn_norm, w_ffn_up, w_ffn_down):
    depth = w_in.shape[0]
    b, s, _ = x_prompt.shape
    db, t, _ = x_sample.shape
    assert db * t == ROW_TILE and s % ROW_TILE == 0

    rope = _rope_tables(jnp.concatenate([jnp.arange(s, dtype=F32),
                                         jnp.tile(PAST_LEN + jnp.arange(t, dtype=F32), db)]))
    ssm_params = (ssm_a_re, ssm_a_im, ssm_log_dt, ssm_b_re, ssm_b_im, ssm_c_re, ssm_c_im)
    tables, decay = _ssm_tables(*ssm_params, PROMPT_CHUNK)
    tab_p, tab_s = tables + decay(PROMPT_CHUNK), tables + decay(t)
    blk = jnp.arange(256) // HEAD_DIM
    ones = (blk[:, None] == blk[None, :]).astype(BF16)

    w_in_bf, w_mem_bf = w_in.astype(BF16), w_mem_kv.astype(BF16)
    wglu, wb, wout = ssm_w_glu.astype(BF16), w_branch.astype(BF16), w_out.astype(BF16)
    wup, wdn = w_ffn_up.astype(BF16), w_ffn_down.astype(BF16)
    qg, kg = jnp.tile(q_norm, (1, N_HEADS)), jnp.tile(k_norm, (1, N_KV_HEADS))
    ckt = jnp.transpose(cache_swa_k, (0, 1, 3, 4, 2))
    cvt = jnp.transpose(cache_swa_v, (0, 1, 3, 4, 2))
    cmk = cache_mem_k.reshape(depth, db, N_MEM * MEM_HEADS, MEM_HEAD_DIM)
    cmv = cache_mem_v.reshape(depth, db, N_MEM * MEM_HEADS, MEM_HEAD_DIM)
    h0r = state_ssm_re.reshape(depth, db, SSM_GROUPS * SSM_STATE)
    h0i = state_ssm_im.reshape(depth, db, SSM_GROUPS * SSM_STATE)

    yp = x_prompt.reshape(b * s, D_MODEL)
    ys = x_sample.reshape(db * t, D_MODEL)
    mem2d = mem_prompt.reshape(b * N_MEM, D_MODEL)
    outs = [[] for _ in range(6)]
    n = b * s
    mk, mv = _memkv(mem2d, mem_norm, w_mem_bf, mem_k_norm)
    mk4, mv4 = mk.reshape(depth, b, N_MEM, BRANCH_W), mv.reshape(depth, b, N_MEM, BRANCH_W)
    new_caches = None
    for l in range(depth):
        k, v, u, gates, oa, oc, qs, mqs = _inproj_attn(l, attn_sinks, yp, ys, s, attn_norm, w_in_bf, rope, qg, kg,
                                                       mem_q_norm, ones, mk4, mv4)
        ysm, hre, him = _ssm_prompt(l, u, b, s, *tab_p)
        yp = _merge_ffn(l, yp, oa, ysm, u, oc, gates, 0, ssm_d, wglu, wb, wout, ffn_norm, wup, wdn)

        oas, ocs, *new_caches = _attn_sample(l, t, attn_sinks, qs, k, v, n, ckt, cvt, mqs, cmk, cmv, new_caches)
        yss, hrs, his = _ssm_sample(l, u, n, db * t, t, h0r, h0i, *tab_s)
        ys = _merge_ffn(l, ys, oas.astype(BF16), yss, u, ocs.astype(BF16), gates, n, ssm_d, wglu, wb, wout,
                        ffn_norm, wup, wdn)

        last_rows = lambda a: jnp.stack([a[(bi + 1) * s - WINDOW:(bi + 1) * s] for bi in range(b)]).reshape(
            b, WINDOW, N_KV_HEADS, HEAD_DIM)
        new = (last_rows(k), last_rows(v), _state_from_blocks(hre), _state_from_blocks(him),
               hrs.reshape(db, SSM_GROUPS, SSM_STATE), his.reshape(db, SSM_GROUPS, SSM_STATE))
        for lst, a in zip(outs, new):
            lst.append(a)

    swa_k_p, swa_v_p, ssm_re_p, ssm_im_p, ssm_re_s, ssm_im_s = (jnp.stack(o) for o in outs)
    mem_shape = (depth, b, N_MEM, MEM_HEADS, MEM_HEAD_DIM)
    swa_k_s, swa_v_s = (jnp.transpose(c, (0, 1, 4, 2, 3)) for c in new_caches)
    return (yp.reshape(b, s, D_MODEL), ys.reshape(db, t, D_MODEL), swa_k_p, swa_v_p, ssm_re_p, ssm_im_p,
            mk.reshape(mem_shape), mv.reshape(mem_shape), swa_k_s, swa_v_s, ssm_re_s, ssm_im_s)
```

```python
import functools
import math

import jax
import jax.numpy as jnp
from jax import lax
from jax.experimental import pallas as pl
from jax.experimental.pallas import tpu as pltpu

F32 = jnp.float32
BF16 = jnp.bfloat16

D_MODEL = 1024
BRANCH_W = 512
HEAD_DIM = 64
N_HEADS = 8
N_KV_HEADS = 2
KV_GROUP = 4
KV_W = N_KV_HEADS * HEAD_DIM
WINDOW = 128
ROPE_THETA = 10000.0
PAST_LEN = 16384
SSM_GROUPS = 32
SSM_GROUP_CH = 16
SSM_STATE = 64
N_MEM = 256
MEM_HEADS = 4
MEM_HEAD_DIM = 128
D_FF = 2816
RMS_EPS = 1e-6

Q_OFF, K_OFF, V_OFF, U_OFF, MQ_OFF, G_OFF = 0, 512, 640, 768, 1280, 1792
IN_W = G_OFF + 3 * D_MODEL

LANES = 128
SUBLANES = 8
MXU_TILE = 256
GROUPS_PER_LANE_BLOCK = LANES // SSM_GROUP_CH
N_LANE_BLOCKS = BRANCH_W // LANES
STATE_W = GROUPS_PER_LANE_BLOCK * SSM_STATE
PROMPT_CHUNK = 8
ROW_TILE = 512
VMEM_LIMIT = 56 * 1024 * 1024


def _dot(a, b):
    return jnp.dot(a, b, preferred_element_type=F32)


def _dot_nt(a, b):
    return lax.dot_general(a, b, (((1,), (1,)), ((), ())), preferred_element_type=F32)


def _cparams(n_axes):
    return pltpu.CompilerParams(dimension_semantics=("arbitrary",) * n_axes,
                                vmem_limit_bytes=VMEM_LIMIT)


def _full(shape):
    nd = len(shape)
    return pl.BlockSpec(shape, lambda *_: (0,) * nd)


def _layer(shape, l):
    nd = len(shape)
    return pl.BlockSpec((None,) + tuple(shape), lambda *_: (l,) + (0,) * nd, pipeline_mode=pl.Buffered(1))


def _sigmoid(x):
    return 0.5 * jnp.tanh(0.5 * x) + 0.5


def _rms(x):
    return x * lax.rsqrt(jnp.mean(x * x, axis=-1, keepdims=True) + RMS_EPS)


def _memkv_kernel(mem_ref, g_ref, w_ref, kg_ref, k_ref, v_ref):
    l = pl.program_id(0)
    h = _rms(mem_ref[...]) * g_ref[pl.ds(l, 1), :]
    kv = _dot(h.astype(BF16), w_ref[...])
    kg = kg_ref[pl.ds(l, 1), :]
    for hd in range(MEM_HEADS):
        sl = slice(hd * MEM_HEAD_DIM, (hd + 1) * MEM_HEAD_DIM)
        k_ref[:, sl] = _rms(kv[:, sl]) * kg
    v_ref[...] = kv[:, BRANCH_W:]


def _memkv(mem2d, g, w_bf, kg):
    n = mem2d.shape[0]
    depth = w_bf.shape[0]
    out = pl.BlockSpec((None, ROW_TILE, BRANCH_W), lambda l, i: (l, i, 0))
    return pl.pallas_call(
        _memkv_kernel, name="memkv",
        grid=(depth, n // ROW_TILE),
        in_specs=[pl.BlockSpec((ROW_TILE, D_MODEL), lambda l, i: (i, 0)), _full(g.shape),
                  pl.BlockSpec((None, D_MODEL, 2 * BRANCH_W), lambda l, i: (l, 0, 0)), _full(kg.shape)],
        out_specs=[out, out],
        out_shape=[jax.ShapeDtypeStruct((depth, n, BRANCH_W), F32)] * 2,
        compiler_params=_cparams(2),
    )(mem2d, g, w_bf, kg)


def _rotary(x, cos, sin_lo, sin_hi):
    w = x.shape[-1]
    half = HEAD_DIM // 2
    return x * cos + pltpu.roll(x, w - half, 1) * sin_lo + pltpu.roll(x, half, 1) * sin_hi


GATE_CHUNK = 512
N_GATE_CHUNKS = 3 * D_MODEL // GATE_CHUNK


class _Projection:
    def __init__(self, x, g, w_ref, rope, qg, kg, mqg, ones):
        self.hb = (_rms(x) * g).astype(BF16)
        self.w_ref, self.qg, self.kg, self.mqg = w_ref, qg, kg, mqg
        self.ones = ones
        self.rope = rope

    def queries(self):
        zq = _dot(self.hb, self.w_ref[:, Q_OFF:K_OFF])
        sq = (zq * zq).astype(BF16)
        ssq = jnp.concatenate([_dot(sq[:, :256], self.ones), _dot(sq[:, 256:], self.ones)], axis=1)
        qn = zq * lax.rsqrt(ssq * (1.0 / HEAD_DIM) + RMS_EPS) * self.qg
        rope4 = (jnp.concatenate([t] * 4, axis=1) for t in self.rope)
        return (_rotary(qn, *rope4) * (1.0 / math.sqrt(HEAD_DIM))).astype(BF16)

    def keys_values(self):
        zkv = _dot(self.hb, self.w_ref[:, K_OFF:U_OFF])
        zk = zkv[:, :KV_W]
        ssk = _dot((zk * zk).astype(BF16), self.ones[:KV_W, :KV_W])
        kn = zk * lax.rsqrt(ssk * (1.0 / HEAD_DIM) + RMS_EPS) * self.kg
        return _rotary(kn, *self.rope), zkv[:, KV_W:]

    def ssm_input(self):
        return _dot(self.hb, self.w_ref[:, U_OFF:MQ_OFF])

    def memory_queries(self):
        zm = _dot(self.hb, self.w_ref[:, MQ_OFF:G_OFF])
        mqg = self.mqg * (1.0 / math.sqrt(MEM_HEAD_DIM))
        return jnp.concatenate([_rms(zm[:, hd * MEM_HEAD_DIM:(hd + 1) * MEM_HEAD_DIM]) * mqg
                                for hd in range(MEM_HEADS)], axis=1).astype(BF16)

    def gates(self, c):
        lo = G_OFF + c * GATE_CHUNK
        return _sigmoid(_dot(self.hb, self.w_ref[:, lo:lo + GATE_CHUNK])).astype(BF16)


ATTN_TILE = 512


def _softmax_rows(s):
    m = jnp.max(s, axis=-1, keepdims=True)
    p = jnp.exp(s - m)
    return p, jnp.sum(p, axis=-1, keepdims=True)


def _sink_column(sink_ref, l, heads, rows_per_head):
    n = len(heads) * rows_per_head
    rcol = lax.broadcasted_iota(jnp.int32, (n, 1), 0)
    sink = jnp.full((n, 1), sink_ref[l, heads[-1]], F32)
    for g in range(len(heads) - 2, -1, -1):
        sink = jnp.where(rcol < (g + 1) * rows_per_head, sink_ref[l, heads[g]], sink)
    return sink


def _sink_softmax(s, sink):
    m = jnp.maximum(jnp.max(s, axis=-1, keepdims=True), sink)
    p = jnp.exp(s - m)
    return p, jnp.sum(p, axis=-1, keepdims=True) + jnp.exp(sink - m)


def _attend_sub_block(sub, mask, sinks, q_ref, mq_ref, k_ref, v_ref, mk_ref, mv_ref, oa_ref, oc_ref):
    qrows = slice(sub * WINDOW, (sub + 1) * WINDOW)
    krows = slice(sub * WINDOW, (sub + 2) * WINDOW)
    ksl = [slice(kvh * HEAD_DIM, (kvh + 1) * HEAD_DIM) for kvh in range(N_KV_HEADS)]
    assert ATTN_TILE // WINDOW == MEM_HEADS
    msl = [slice(sub * MEM_HEAD_DIM, (sub + 1) * MEM_HEAD_DIM)]
    mrows = slice(0, ATTN_TILE)
    heads = [[kvh * KV_GROUP + g for g in range(KV_GROUP)] for kvh in range(N_KV_HEADS)]
    scores = []
    for kvh in range(N_KV_HEADS):
        qg = jnp.concatenate([q_ref[qrows, hd * HEAD_DIM:(hd + 1) * HEAD_DIM] for hd in heads[kvh]], axis=0)
        scores.append(jnp.where(mask, _dot_nt(qg, k_ref[krows, ksl[kvh]]), -jnp.inf))
    mscores = [_dot_nt(mq_ref[mrows, sl], mk_ref[:, sl]) for sl in msl]
    yield
    probs = [_sink_softmax(scores[kvh], sinks[kvh]) for kvh in range(N_KV_HEADS)]
    mprobs = [_softmax_rows(s) for s in mscores]
    yield
    outs = [_dot(probs[kvh][0].astype(BF16), v_ref[krows, ksl[kvh]]) / probs[kvh][1]
            for kvh in range(N_KV_HEADS)]
    mouts = [_dot(mprobs[hd][0].astype(BF16), mv_ref[:, msl[hd]]) / mprobs[hd][1] for hd in range(len(msl))]
    for kvh in range(N_KV_HEADS):
        for g, hd in enumerate(heads[kvh]):
            oa_ref[qrows, hd * HEAD_DIM:(hd + 1) * HEAD_DIM] = outs[kvh][g * WINDOW:(g + 1) * WINDOW].astype(BF16)
    for hd in range(len(msl)):
        oc_ref[mrows, msl[hd]] = mouts[hd].astype(BF16)


def _inproj_attn_kernel(l, tiles_per_seq, sink_ref, x_ref, xs_ref, g_ref, w_ref, cos_ref, slo_ref, shi_ref,
                        qg_ref, kg_ref, mqg_ref, ones_ref, mk_ref, mv_ref,
                        k_ref, v_ref, u_ref, gates_ref, oa_ref, oc_ref, qs_ref, mqs_ref,
                        q_scr, mq_scr, k_scr, v_scr, mk_scr, mv_scr):
    i = pl.program_id(0)
    cur, prv = i % 2, (i + 1) % 2
    decode_step = i == pl.num_programs(0) - 1

    @pl.when(i == 0)
    def _():
        for scr in (q_scr, mq_scr, k_scr, v_scr, mk_scr, mv_scr):
            scr[...] = jnp.zeros(scr.shape, scr.dtype)

    @pl.when(i % tiles_per_seq == 1)
    def _():
        mk_scr[...] = mk_ref[...].astype(BF16)
        mv_scr[...] = mv_ref[...].astype(BF16)

    rows = KV_GROUP * WINDOW
    r = lax.broadcasted_iota(jnp.int32, (rows, 2 * WINDOW), 0)
    j = lax.broadcasted_iota(jnp.int32, (rows, 2 * WINDOW), 1)
    qi = r % WINDOW
    band = (j > qi) & (j <= qi + WINDOW)
    first_band = band & ((j >= WINDOW) | (i % tiles_per_seq != 1))
    sinks = [_sink_column(sink_ref, l, [kvh * KV_GROUP + g for g in range(KV_GROUP)], WINDOW)
             for kvh in range(N_KV_HEADS)]

    def attention_stages():
        for sub in range(ATTN_TILE // WINDOW):
            yield from _attend_sub_block(sub, first_band if sub == 0 else band, sinks, q_scr.at[prv],
                                         mq_scr.at[prv], k_scr.at[prv], v_scr.at[prv], mk_scr, mv_scr,
                                         oa_ref, oc_ref)
            yield

    proj = _Projection(jnp.where(decode_step, xs_ref[...], x_ref[...]), g_ref[l:l + 1, :], w_ref,
                       (cos_ref[...], slo_ref[...], shi_ref[...]), qg_ref[l:l + 1, :], kg_ref[l:l + 1, :],
                       mqg_ref[l:l + 1, :], ones_ref[...])

    def store_queries():
        q = proj.queries()
        q_scr[cur] = q
        qs_ref[...] = q

    def store_keys_values():
        k, v = proj.keys_values()
        k_ref[...], v_ref[...] = k, v
        k_scr[cur, :WINDOW, :] = k_scr[prv, ATTN_TILE:, :]
        v_scr[cur, :WINDOW, :] = v_scr[prv, ATTN_TILE:, :]
        k_scr[cur, WINDOW:, :] = k.astype(BF16)
        v_scr[cur, WINDOW:, :] = v.astype(BF16)

    def store_ssm_input():
        u_ref[...] = proj.ssm_input()

    def store_memory_queries():
        mq = proj.memory_queries()
        mq_scr[cur] = mq
        mqs_ref[...] = mq

    def store_gates(c):
        gates_ref[:, c * GATE_CHUNK:(c + 1) * GATE_CHUNK] = proj.gates(c)

    segments = [store_queries, store_keys_values, store_ssm_input, store_memory_queries]
    segments += [functools.partial(store_gates, c) for c in range(N_GATE_CHUNKS)]
    stages = attention_stages()
    n_stages = 3 * (ATTN_TILE // WINDOW)
    for n, segment in enumerate(segments):
        for _ in range((n + 1) * n_stages // len(segments) - n * n_stages // len(segments)):
            next(stages)
        segment()


def _inproj_attn(l, sinks, x2d, xs2d, seq_len, g, w_bf, rope, qg, kg, mqg, ones, mk, mv):
    n = x2d.shape[0]
    n_tiles = n // ROW_TILE
    tiles_per_seq = seq_len // ROW_TILE
    assert ROW_TILE == ATTN_TILE and tiles_per_seq > 1 and xs2d.shape[0] == ROW_TILE
    attn_tile = lambda i: jnp.maximum(i - 1, 0)
    xrow = pl.BlockSpec((ROW_TILE, D_MODEL), lambda i: (jnp.minimum(i, n_tiles - 1), 0))
    orow = lambda w: pl.BlockSpec((ROW_TILE, w), lambda i: (i, 0))
    arow = pl.BlockSpec((ROW_TILE, BRANCH_W), lambda i: (attn_tile(i), 0))
    rspec = pl.BlockSpec((ROW_TILE, LANES), lambda i: (jnp.where(i == n_tiles, tiles_per_seq, i % tiles_per_seq), 0))
    mem = pl.BlockSpec((None, None, N_MEM, BRANCH_W), lambda i: (l, attn_tile(i) // tiles_per_seq, 0, 0))
    widths = (KV_W, KV_W, BRANCH_W, 3 * D_MODEL)
    dtypes = (F32, F32, F32, BF16)
    return pl.pallas_call(
        functools.partial(_inproj_attn_kernel, l, tiles_per_seq), name="inproj_attn",
        grid=(n_tiles + 1,),
        in_specs=[pl.BlockSpec(memory_space=pltpu.SMEM), xrow, _full((ROW_TILE, D_MODEL)), _full(g.shape),
                  _layer((D_MODEL, IN_W), l), rspec, rspec, rspec, _full(qg.shape), _full(kg.shape),
                  _full(mqg.shape), _full((256, 256)), mem, mem],
        out_specs=[orow(w) for w in widths] + [arow, arow] + [_full((ROW_TILE, BRANCH_W))] * 2,
        out_shape=[jax.ShapeDtypeStruct((n + ROW_TILE, w), d) for w, d in zip(widths, dtypes)]
                  + [jax.ShapeDtypeStruct((n, BRANCH_W), BF16)] * 2
                  + [jax.ShapeDtypeStruct((ROW_TILE, BRANCH_W), BF16)] * 2,
        scratch_shapes=[pltpu.VMEM((2, ROW_TILE, BRANCH_W), BF16)] * 2
                       + [pltpu.VMEM((2, WINDOW + ROW_TILE, KV_W), BF16)] * 2
                       + [pltpu.VMEM((N_MEM, BRANCH_W), BF16)] * 2,
        compiler_params=_cparams(1),
    )(sinks, x2d, xs2d, g, w_bf, *rope, qg, kg, mqg, ones, mk, mv)


DEC_BLOCK = 8
TOK_PAD = 8


def _pad_rows(x, n):
    return jnp.concatenate([x, jnp.zeros((n - x.shape[0], x.shape[1]), x.dtype)], axis=0)


def _attn_sample_kernel(l, t, sink_ref, q_ref, kn_ref, vn_ref, ckt_ref, cvt_ref, qm_ref, mk_ref, mv_ref,
                        *earlier_and_outputs):
    oa_ref, oc_ref, nkt_ref, nvt_ref = earlier_and_outputs[-4:]
    earlier = earlier_and_outputs[:-4]
    if earlier:
        for j in range(l):
            nkt_ref[j] = earlier[j][...]
            nvt_ref[j] = earlier[l + j][...]
        nkt_ref, nvt_ref = nkt_ref.at[l], nvt_ref.at[l]
    keys = 2 * WINDOW
    rows = N_HEADS * TOK_PAD
    rk = lax.broadcasted_iota(jnp.int32, (rows, keys), 0)
    jk = lax.broadcasted_iota(jnp.int32, (rows, keys), 1)
    tk = rk % TOK_PAD
    mask = (tk < t) & (((jk < WINDOW) & (jk > tk)) | ((jk >= WINDOW) & (jk - WINDOW <= tk)))
    sink = _sink_column(sink_ref, l, list(range(N_HEADS)), TOK_PAD)
    mem_rows = N_MEM * MEM_HEADS
    mem_own = (lax.broadcasted_iota(jnp.int32, (MEM_HEADS * TOK_PAD, mem_rows), 1) % MEM_HEADS
               == lax.broadcasted_iota(jnp.int32, (MEM_HEADS * TOK_PAD, mem_rows), 0) // TOK_PAD)
    is_new = lax.broadcasted_iota(jnp.int32, (HEAD_DIM, WINDOW), 1) >= WINDOW - t
    dsl = [slice(kvh * HEAD_DIM, (kvh + 1) * HEAD_DIM) for kvh in range(N_KV_HEADS)]
    group_rows = KV_GROUP * TOK_PAD

    q_all, qm_all = q_ref[...].astype(F32), qm_ref[...].astype(F32)
    seqs = range(DEC_BLOCK)
    toks = [slice(bi * t, (bi + 1) * t) for bi in seqs]
    knp = [_pad_rows(kn_ref[toks[bi], :], WINDOW) for bi in seqs]
    vnp = [_pad_rows(vn_ref[toks[bi], :], WINDOW) for bi in seqs]
    knt, vnt = [x.T for x in knp], [x.T for x in vnp]
    for bi in seqs:
        for kvh in range(N_KV_HEADS):
            nkt_ref[bi, kvh] = jnp.where(is_new, pltpu.roll(knt[bi][dsl[kvh], :], WINDOW - t, 1),
                                         pltpu.roll(ckt_ref[bi, kvh], WINDOW - t, 1))
            nvt_ref[bi, kvh] = jnp.where(is_new, pltpu.roll(vnt[bi][dsl[kvh], :], WINDOW - t, 1),
                                         pltpu.roll(cvt_ref[bi, kvh], WINDOW - t, 1))
    qg = []
    for bi in seqs:
        q8 = _pad_rows(q_all[toks[bi]], TOK_PAD)
        qg.append([jnp.concatenate([q8[:, hd * HEAD_DIM:(hd + 1) * HEAD_DIM]
                                    for hd in range(kvh * KV_GROUP, (kvh + 1) * KV_GROUP)], axis=0).astype(BF16)
                   for kvh in range(N_KV_HEADS)])
    scores = []
    for bi in seqs:
        per_kvh = [jnp.concatenate([_dot(qg[bi][kvh], ckt_ref[bi, kvh].astype(BF16)),
                                    _dot_nt(qg[bi][kvh], knp[bi][:, dsl[kvh]].astype(BF16))], axis=1)
                   for kvh in range(N_KV_HEADS)]
        scores.append(jnp.where(mask, jnp.concatenate(per_kvh, axis=0), -jnp.inf))
    mscores = []
    for bi in seqs:
        qm8 = jnp.concatenate([qm_all[toks[bi]], jnp.zeros((TOK_PAD - t, BRANCH_W), F32)], axis=0).astype(BF16)
        qst = jnp.concatenate([qm8[:, hd * MEM_HEAD_DIM:(hd + 1) * MEM_HEAD_DIM] for hd in range(MEM_HEADS)],
                              axis=0)
        mscores.append(jnp.where(mem_own, _dot_nt(qst, mk_ref[bi].astype(BF16)), -jnp.inf))
    probs = [_sink_softmax(scores[bi], sink) for bi in seqs]
    mprobs = [_softmax_rows(mscores[bi]) for bi in seqs]
    outs = []
    for bi in seqs:
        p = probs[bi][0].astype(BF16)
        per_kvh = []
        for kvh in range(N_KV_HEADS):
            pk = p[kvh * group_rows:(kvh + 1) * group_rows]
            per_kvh.append(_dot_nt(pk[:, :WINDOW], cvt_ref[bi, kvh].astype(BF16))
                           + _dot(pk[:, WINDOW:], vnp[bi][:, dsl[kvh]].astype(BF16)))
        outs.append(jnp.concatenate(per_kvh, axis=0) / probs[bi][1])
    mouts = [_dot(mprobs[bi][0].astype(BF16), mv_ref[bi].astype(BF16)) / mprobs[bi][1] for bi in seqs]
    for bi in seqs:
        oa_ref[toks[bi], :] = jnp.concatenate(
            [outs[bi][hd * TOK_PAD:(hd + 1) * TOK_PAD] for hd in range(N_HEADS)], axis=1)[0:t]
        oc_ref[toks[bi], :] = jnp.concatenate(
            [mouts[bi][hd * TOK_PAD:(hd + 1) * TOK_PAD] for hd in range(MEM_HEADS)], axis=1)[0:t]


def _attn_sample(l, t, sinks, q, kn, vn, kv_row0, ckt, cvt, qm, mk, mv, earlier_k, earlier_v):
    n = q.shape[0]
    tok = lambda w, row0=0: pl.BlockSpec((DEC_BLOCK * t, w), lambda i: (i + row0 // (DEC_BLOCK * t), 0))
    cache_tile = (DEC_BLOCK, N_KV_HEADS, HEAD_DIM, WINDOW)
    cache_in = pl.BlockSpec((None,) + cache_tile, lambda i: (l, i, 0, 0, 0))
    one_layer = pl.BlockSpec(cache_tile, lambda i: (i, 0, 0, 0))
    mem = pl.BlockSpec((None, DEC_BLOCK, N_MEM * MEM_HEADS, MEM_HEAD_DIM), lambda i: (l, i, 0, 0))
    assert len(earlier_k) == len(earlier_v) and len(earlier_k) in (0, l)
    if earlier_k:
        cache_out = pl.BlockSpec((l + 1,) + cache_tile, lambda i: (0, i, 0, 0, 0))
        cache_shape = jax.ShapeDtypeStruct((l + 1,) + ckt.shape[1:], F32)
    else:
        cache_out, cache_shape = one_layer, jax.ShapeDtypeStruct(ckt.shape[1:], F32)
    return pl.pallas_call(
        functools.partial(_attn_sample_kernel, l, t), name="attn_sample",
        grid=(n // (DEC_BLOCK * t),),
        in_specs=[pl.BlockSpec(memory_space=pltpu.SMEM), tok(BRANCH_W), tok(KV_W, kv_row0), tok(KV_W, kv_row0),
                  cache_in, cache_in, tok(BRANCH_W), mem, mem] + [one_layer] * (2 * len(earlier_k)),
        out_specs=[tok(BRANCH_W), tok(BRANCH_W), cache_out, cache_out],
        out_shape=[jax.ShapeDtypeStruct((n, BRANCH_W), F32)] * 2 + [cache_shape] * 2,
        compiler_params=_cparams(1),
    )(sinks, q, kn, vn, ckt, cvt, qm, mk, mv, *earlier_k, *earlier_v)


def _ssm_tables(a_re, a_im, log_dt, b_re, b_im, c_re, c_im, chunk):
    hp = lax.Precision.HIGHEST
    dt = jnp.exp(log_dt)[..., None]
    mag = jnp.exp(a_re * dt)
    lam_re, lam_im = mag * jnp.cos(a_im * dt), mag * jnp.sin(a_im * dt)
    den = a_re * a_re + a_im * a_im
    nr, ni = lam_re - 1.0, lam_im
    g_re = (nr * a_re + ni * a_im) / den
    g_im = (ni * a_re - nr * a_im) / den
    bg_re = g_re[..., None] * b_re - g_im[..., None] * b_im
    bg_im = g_re[..., None] * b_im + g_im[..., None] * b_re
    def powers(d):
        d = d.astype(F32)[:, None, None, None]
        pm = jnp.exp(d * (a_re * dt)[None])
        return pm * jnp.cos(d * (a_im * dt)[None]), pm * jnp.sin(d * (a_im * dt)[None])

    def times_gb(p_re, p_im):
        return (p_re[..., None] * bg_re[None] - p_im[..., None] * bg_im[None],
                p_re[..., None] * bg_im[None] + p_im[..., None] * bg_re[None])

    pw_re, pw_im = powers(jnp.arange(chunk + 1))
    z_re, z_im = times_gb(pw_re[:chunk], pw_im[:chunk])
    zrev_re, zrev_im = times_gb(*powers(jnp.arange(chunk - 1, -1, -1)))
    kern = (jnp.einsum('lgop,dlgpi->dlgio', c_re, z_re, precision=hp)
            - jnp.einsum('lgop,dlgpi->dlgio', c_im, z_im, precision=hp))
    nl = a_re.shape[0]
    nb, gb = N_LANE_BLOCKS, GROUPS_PER_LANE_BLOCK
    kd = jnp.transpose(kern.reshape(chunk, nl, nb, gb, SSM_GROUP_CH, SSM_GROUP_CH), (1, 2, 0, 3, 4, 5))
    kd = kd.reshape(nl, nb, chunk, LANES, SSM_GROUP_CH)

    def to_rows(z):
        z = z.reshape(chunk, nl, nb, gb, SSM_STATE, SSM_GROUP_CH)
        return jnp.transpose(z, (1, 2, 0, 3, 5, 4)).reshape(nl, nb, chunk, LANES, SSM_STATE)

    zr, zi = to_rows(zrev_re), to_rows(zrev_im)
    zc = jnp.concatenate([zr, zr, zi, zi], axis=-1)

    pr, pi = pw_re[1:], pw_im[1:]
    f_a = c_re[None] * pr[:, :, :, None, :] - c_im[None] * pi[:, :, :, None, :]
    f_b = c_re[None] * pi[:, :, :, None, :] + c_im[None] * pr[:, :, :, None, :]

    def from_cols(f):
        f = f.reshape(chunk, nl, nb, gb, SSM_GROUP_CH, SSM_STATE)
        return jnp.transpose(f, (1, 2, 0, 4, 3, 5)).reshape(nl, nb, chunk, SSM_GROUP_CH, STATE_W)

    ft = jnp.concatenate([from_cols(f_a), from_cols(-f_b)], axis=-1)

    kw = chunk * LANES
    spec = lambda *shape: pl.BlockSpec((None, None) + shape, lambda l, n: (l, n) + (0,) * len(shape))
    intra, to_st, from_st_t = pl.pallas_call(
        _ssm_table_kernel, name="ssm_tables",
        grid=(nl, nb),
        in_specs=[spec(chunk, LANES, SSM_GROUP_CH), spec(chunk, LANES, 2 * LANES),
                  spec(chunk, SSM_GROUP_CH, 2 * STATE_W)],
        out_specs=[spec(kw, kw), spec(kw, 2 * STATE_W), spec(kw, 2 * STATE_W)],
        out_shape=[jax.ShapeDtypeStruct((nl, nb, kw, kw), BF16),
                   jax.ShapeDtypeStruct((nl, nb, kw, 2 * STATE_W), BF16),
                   jax.ShapeDtypeStruct((nl, nb, kw, 2 * STATE_W), BF16)],
        compiler_params=_cparams(2),
    )(kd, zc, ft)
    decay = lambda d: (pw_re[d].reshape(nl, nb, 1, STATE_W), pw_im[d].reshape(nl, nb, 1, STATE_W))
    return (intra, to_st, from_st_t), decay


def _ssm_table_kernel(kd_ref, zc_ref, ft_ref, intra_ref, to_ref, fromt_ref):
    chunk = kd_ref.shape[0]
    gch = SSM_GROUP_CH
    r = lax.broadcasted_iota(jnp.int32, (LANES, LANES), 0)
    c = lax.broadcasted_iota(jnp.int32, (LANES, LANES), 1)
    same_group = (r // gch) == (c // gch)
    spread = (lax.broadcasted_iota(jnp.int32, (gch, LANES), 1) % gch
              == lax.broadcasted_iota(jnp.int32, (gch, LANES), 0)).astype(BF16)
    spread_t = (lax.broadcasted_iota(jnp.int32, (LANES, gch), 0) % gch
                == lax.broadcasted_iota(jnp.int32, (LANES, gch), 1)).astype(BF16)
    rs = lax.broadcasted_iota(jnp.int32, (LANES, 2 * STATE_W), 0)
    cs = lax.broadcasted_iota(jnp.int32, (LANES, 2 * STATE_W), 1)
    own_state = (rs // gch) == ((cs % STATE_W) // SSM_STATE)

    lag_blocks = [jnp.where(same_group, _dot(kd_ref[d].astype(BF16), spread), 0.0).astype(BF16)
                  for d in range(chunk)]
    zero = jnp.zeros((LANES, LANES), BF16)
    for s in range(chunk):
        for t in range(chunk):
            intra_ref[s * LANES:(s + 1) * LANES, t * LANES:(t + 1) * LANES] = (
                lag_blocks[t - s] if t >= s else zero)
        z = zc_ref[s]
        full = jnp.concatenate([z[:, :LANES]] * (STATE_W // LANES) + [z[:, LANES:]] * (STATE_W // LANES), axis=1)
        to_ref[s * LANES:(s + 1) * LANES, :] = jnp.where(own_state, full, 0.0).astype(BF16)
        f = _dot(spread_t, ft_ref[s].astype(BF16))
        fromt_ref[s * LANES:(s + 1) * LANES, :] = jnp.where(own_state, f, 0.0).astype(BF16)


SSM_SEQS = 2


def _chunk_rows(ref, chunk):
    n = ref.shape[0] // chunk
    return jnp.concatenate([ref[pl.ds(s, n, stride=chunk), :] for s in range(chunk)], axis=1)


def _store_chunk_rows(ref, y, chunk):
    n = ref.shape[0] // chunk
    for t in range(chunk):
        ref[pl.ds(t, n, stride=chunk), :] = y[:, t * LANES:(t + 1) * LANES]


def _ssm_prompt_kernel(u_ref, intra_ref, to_ref, fromt_ref, dre_ref, dim_ref,
                       y_ref, hre_ref, him_ref, st_scr):
    chunk = PROMPT_CHUNK
    nchunks = u_ref.shape[0] // (chunk * SSM_SEQS)
    x = _chunk_rows(u_ref, chunk).astype(BF16)
    kw = chunk * LANES
    y = jnp.concatenate([_dot(x[:, :c + MXU_TILE], intra_ref[:c + MXU_TILE, c:c + MXU_TILE])
                         for c in range(0, kw, MXU_TILE)], axis=1)
    add_all = _dot(x, to_ref[...])
    nblk = STATE_W // LANES
    assert SSM_SEQS * nblk == SUBLANES
    slots = [(seq, cb) for seq in range(SSM_SEQS) for cb in range(nblk)]

    def slot_rows(seq, cb):
        return pl.ds(seq * nblk + cb, nchunks, stride=SUBLANES)

    for c in range(2):
        for seq, cb in slots:
            col = (c * nblk + cb) * LANES
            st_scr[c, slot_rows(seq, cb), :] = add_all[seq * nchunks:(seq + 1) * nchunks, col:col + LANES]
    dre = jnp.concatenate([dre_ref[:, cb * LANES:(cb + 1) * LANES] for _, cb in slots], axis=0)
    dim = jnp.concatenate([dim_ref[:, cb * LANES:(cb + 1) * LANES] for _, cb in slots], axis=0)

    def step(k, carry):
        hr, hi = carry
        tile = pl.ds(pl.multiple_of(k * SUBLANES, SUBLANES), SUBLANES)
        add_r, add_i = st_scr[0, tile, :], st_scr[1, tile, :]
        st_scr[0, tile, :] = hr
        st_scr[1, tile, :] = hi
        return dre * hr - dim * hi + add_r, dre * hi + dim * hr + add_i

    zero = jnp.zeros((SUBLANES, LANES), F32)
    hr, hi = lax.fori_loop(0, nchunks, step, (zero, zero), unroll=8)
    hre_ref[...] = hr
    him_ref[...] = hi
    h_in = jnp.concatenate(
        [jnp.concatenate([st_scr[c, slot_rows(seq, cb), :] for seq in range(SSM_SEQS)], axis=0)
         for c in range(2) for cb in range(nblk)], axis=1).astype(BF16)
    _store_chunk_rows(y_ref, y + _dot_nt(h_in, fromt_ref[...]), chunk)


def _ssm_prompt(l, u2d, n_seqs, seq_len, intra, to_st, from_st_t, dec_re, dec_im):
    n = n_seqs * seq_len
    chunk = PROMPT_CHUNK
    rows = SSM_SEQS * seq_len
    steps = n // rows
    kw = chunk * LANES
    data = pl.BlockSpec((rows, LANES), lambda j, h: (h, j))
    wspec = lambda r, c: pl.BlockSpec((None, None, r, c), lambda j, h: (l, j, 0, 0))
    st_out = pl.BlockSpec((None, None, SUBLANES, LANES), lambda j, h: (j, h, 0, 0))
    st_shape = jax.ShapeDtypeStruct((N_LANE_BLOCKS, steps, SUBLANES, LANES), F32)
    return pl.pallas_call(
        _ssm_prompt_kernel, name="ssm_prompt",
        grid=(N_LANE_BLOCKS, steps),
        in_specs=[data, wspec(kw, kw), wspec(kw, 2 * STATE_W), wspec(kw, 2 * STATE_W),
                  wspec(1, STATE_W), wspec(1, STATE_W)],
        out_specs=[data, st_out, st_out],
        out_shape=[jax.ShapeDtypeStruct((n, BRANCH_W), F32), st_shape, st_shape],
        scratch_shapes=[pltpu.VMEM((2, SUBLANES * seq_len // chunk, LANES), F32)],
        compiler_params=_cparams(2),
    )(u2d, intra, to_st, from_st_t, dec_re, dec_im)


def _ssm_sample_kernel(chunk, u_ref, h0r_ref, h0i_ref, intra_ref, to_ref, fromt_ref, dre_ref, dim_ref,
                       y_ref, hre_ref, him_ref):
    x = _chunk_rows(u_ref, chunk).astype(BF16)
    hr, hi = h0r_ref[...], h0i_ref[...]
    dre, dim = dre_ref[...], dim_ref[...]
    add = _dot(x, to_ref[...])
    hre_ref[...] = dre * hr - dim * hi + add[:, :STATE_W]
    him_ref[...] = dre * hi + dim * hr + add[:, STATE_W:]
    h0 = jnp.concatenate([hr, hi], axis=1).astype(BF16)
    _store_chunk_rows(y_ref, _dot(x, intra_ref[...]) + _dot_nt(h0, fromt_ref[...]), chunk)


def _ssm_sample(l, u2d, row0, n, t, h0_re, h0_im, intra, to_st, from_st_t, dec_re, dec_im):
    b = n // t
    kw = t * LANES
    assert PROMPT_CHUNK % t == 0 and row0 % n == 0
    last = PROMPT_CHUNK // t - 1
    data_in = pl.BlockSpec((n, LANES), lambda j: (row0 // n, j))
    data = pl.BlockSpec((n, LANES), lambda j: (0, j))
    st_in = pl.BlockSpec((None, b, STATE_W), lambda j: (l, 0, j))
    st_out = pl.BlockSpec((b, STATE_W), lambda j: (0, j))
    wspec = lambda r, c, rb=0: pl.BlockSpec((None, None, r, c), lambda j: (l, j, rb, 0))
    return pl.pallas_call(
        functools.partial(_ssm_sample_kernel, t), name="ssm_sample",
        grid=(N_LANE_BLOCKS,),
        in_specs=[data_in, st_in, st_in, wspec(kw, kw), wspec(kw, 2 * STATE_W, last), wspec(kw, 2 * STATE_W),
                  wspec(1, STATE_W), wspec(1, STATE_W)],
        out_specs=[data, st_out, st_out],
        out_shape=[jax.ShapeDtypeStruct((n, BRANCH_W), F32),
                   jax.ShapeDtypeStruct((b, N_LANE_BLOCKS * STATE_W), F32),
                   jax.ShapeDtypeStruct((b, N_LANE_BLOCKS * STATE_W), F32)],
        compiler_params=_cparams(1),
    )(u2d, h0_re, h0_im, intra, to_st, from_st_t, dec_re, dec_im)


FF_CHUNKS = ((0, 512), (512, 512), (1024, 512), (1536, 512), (2048, 512), (2560, 256))


def _merge_ffn_kernel(l, x_ref, oa_ref, y_ref, u_ref, oc_ref, gates_ref, d_ref, wglu_ref, wb_ref, wout_ref,
                      g_ref, wup_ref, wdn_ref, o_ref):
    z = jax.nn.gelu(y_ref[...] + d_ref[l:l + 1, :] * u_ref[...])
    ob = (z * _sigmoid(_dot(z.astype(BF16), wglu_ref[...]))).astype(BF16)
    gate = lambda n: gates_ref[:, n * D_MODEL:(n + 1) * D_MODEL].astype(F32)
    merged = gate(0) * _dot(oa_ref[...], wb_ref[0])
    merged = merged + gate(1) * _dot(ob, wb_ref[1])
    merged = merged + gate(2) * _dot(oc_ref[...], wb_ref[2])
    x = x_ref[...] + _dot(merged.astype(BF16), wout_ref[...])
    h = (_rms(x) * g_ref[l:l + 1, :]).astype(BF16)
    acc = x
    for lo, w in FF_CHUNKS:
        up_gate = _dot(h, wup_ref[:, lo:lo + w])
        up = _dot(h, wup_ref[:, D_FF + lo:D_FF + lo + w])
        acc = acc + _dot((up_gate * _sigmoid(up_gate) * up).astype(BF16), wdn_ref[lo:lo + w, :])
    o_ref[...] = acc


def _merge_ffn(l, x2d, oa, y, u, oc, gates, proj_row0, d, wglu, wb, wout, g_ffn, wup, wdn):
    n = x2d.shape[0]
    assert proj_row0 % ROW_TILE == 0
    row = lambda w, row0=0: pl.BlockSpec((ROW_TILE, w), lambda i: (i + row0 // ROW_TILE, 0))
    return pl.pallas_call(
        functools.partial(_merge_ffn_kernel, l), name="merge_ffn",
        grid=(n // ROW_TILE,),
        in_specs=[row(D_MODEL), row(BRANCH_W), row(BRANCH_W), row(BRANCH_W, proj_row0), row(BRANCH_W),
                  row(3 * D_MODEL, proj_row0), _full(d.shape), _layer((BRANCH_W, BRANCH_W), l),
                  _layer((3, BRANCH_W, D_MODEL), l), _layer((D_MODEL, D_MODEL), l),
                  _full(g_ffn.shape), _layer((D_MODEL, 2 * D_FF), l), _layer((D_FF, D_MODEL), l)],
        out_specs=row(D_MODEL),
        out_shape=jax.ShapeDtypeStruct((n, D_MODEL), F32),
        compiler_params=_cparams(1),
    )(x2d, oa, y, u, oc, gates, d, wglu, wb, wout, g_ffn, wup, wdn)


def _rope_tables(pos):
    half = HEAD_DIM // 2
    inv = ROPE_THETA ** (-jnp.arange(half, dtype=F32) / half)
    ang = pos[:, None] * inv[None, :]
    cos, sin = jnp.cos(ang), jnp.sin(ang)
    zero = jnp.zeros_like(sin)
    two = lambda a: jnp.concatenate([a, a], axis=1)
    return (two(jnp.concatenate([cos, cos], axis=1)), two(jnp.concatenate([-sin, zero], axis=1)),
            two(jnp.concatenate([zero, sin], axis=1)))


def _state_from_blocks(h):
    nb, steps, _, _ = h.shape
    h = h.reshape(nb, steps, SSM_SEQS, STATE_W // LANES, LANES)
    return jnp.transpose(h, (1, 2, 0, 3, 4)).reshape(steps * SSM_SEQS, SSM_GROUPS, SSM_STATE)


def kernel(x_prompt, x_sample, cache_swa_k, cache_swa_v, state_ssm_re, state_ssm_im, cache_mem_k,
           cache_mem_v, mem_prompt, attn_norm, w_in, q_norm, k_norm, attn_sinks, ssm_a_re, ssm_a_im,
           ssm_log_dt, ssm_b_re, ssm_b_im, ssm_c_re, ssm_c_im, ssm_d, ssm_w_glu, mem_norm, w_mem_kv,
           mem_q_norm, mem_k_norm, w_branch, w_out, ffn_norm, w_ffn_up, w_ffn_down):
    depth = w_in.shape[0]
    b, s, _ = x_prompt.shape
    db, t, _ = x_sample.shape
    assert db * t == ROW_TILE and s % ROW_TILE == 0

    rope = _rope_tables(jnp.concatenate([jnp.arange(s, dtype=F32),
                                         jnp.tile(PAST_LEN + jnp.arange(t, dtype=F32), db)]))
    ssm_params = (ssm_a_re, ssm_a_im, ssm_log_dt, ssm_b_re, ssm_b_im, ssm_c_re, ssm_c_im)
    tables, decay = _ssm_tables(*ssm_params, PROMPT_CHUNK)
    tab_p, tab_s = tables + decay(PROMPT_CHUNK), tables + decay(t)
    blk = jnp.arange(256) // HEAD_DIM
    ones = (blk[:, None] == blk[None, :]).astype(BF16)

    w_in_bf, w_mem_bf = w_in.astype(BF16), w_mem_kv.astype(BF16)
    wglu, wb, wout = ssm_w_glu.astype(BF16), w_branch.astype(BF16), w_out.astype(BF16)
    wup, wdn = w_ffn_up.astype(BF16), w_ffn_down.astype(BF16)
    qg, kg = jnp.tile(q_norm, (1, N_HEADS)), jnp.tile(k_norm, (1, N_KV_HEADS))
    ckt = jnp.transpose(cache_swa_k, (0, 1, 3, 4, 2))
    cvt = jnp.transpose(cache_swa_v, (0, 1, 3, 4, 2))
    cmk = cache_mem_k.reshape(depth, db, N_MEM * MEM_HEADS, MEM_HEAD_DIM)
    cmv = cache_mem_v.reshape(depth, db, N_MEM * MEM_HEADS, MEM_HEAD_DIM)
    h0r = state_ssm_re.reshape(depth, db, SSM_GROUPS * SSM_STATE)
    h0i = state_ssm_im.reshape(depth, db, SSM_GROUPS * SSM_STATE)

    yp = x_prompt.reshape(b * s, D_MODEL)
    ys = x_sample.reshape(db * t, D_MODEL)
    mem2d = mem_prompt.reshape(b * N_MEM, D_MODEL)
    outs = [[] for _ in range(6)]
    n = b * s
    mk, mv = _memkv(mem2d, mem_norm, w_mem_bf, mem_k_norm)
    mk4, mv4 = mk.reshape(depth, b, N_MEM, BRANCH_W), mv.reshape(depth, b, N_MEM, BRANCH_W)
    new_k, new_v = [], []
    for l in range(depth):
        k, v, u, gates, oa, oc, qs, mqs = _inproj_attn(l, attn_sinks, yp, ys, s, attn_norm, w_in_bf, rope, qg, kg,
                                                       mem_q_norm, ones, mk4, mv4)
        ysm, hre, him = _ssm_prompt(l, u, b, s, *tab_p)
        yp = _merge_ffn(l, yp, oa, ysm, u, oc, gates, 0, ssm_d, wglu, wb, wout, ffn_norm, wup, wdn)

        stack_now = l == depth - 1 and depth > 1
        oas, ocs, nkt, nvt = _attn_sample(l, t, attn_sinks, qs, k, v, n, ckt, cvt, mqs, cmk, cmv,
                                          new_k if stack_now else [], new_v if stack_now else [])
        new_k.append(nkt)
        new_v.append(nvt)
        yss, hrs, his = _ssm_sample(l, u, n, db * t, t, h0r, h0i, *tab_s)
        ys = _merge_ffn(l, ys, oas.astype(BF16), yss, u, ocs.astype(BF16), gates, n, ssm_d, wglu, wb, wout,
                        ffn_norm, wup, wdn)

        last_rows = lambda a: jnp.stack([a[(bi + 1) * s - WINDOW:(bi + 1) * s] for bi in range(b)]).reshape(
            b, WINDOW, N_KV_HEADS, HEAD_DIM)
        new = (last_rows(k), last_rows(v), _state_from_blocks(hre), _state_from_blocks(him),
               hrs.reshape(db, SSM_GROUPS, SSM_STATE), his.reshape(db, SSM_GROUPS, SSM_STATE))
        for lst, a in zip(outs, new):
            lst.append(a)

    swa_k_p, swa_v_p, ssm_re_p, ssm_im_p, ssm_re_s, ssm_im_s = (jnp.stack(o) for o in outs)
    mem_shape = (depth, b, N_MEM, MEM_HEADS, MEM_HEAD_DIM)
    stacked = (new_k[-1], new_v[-1]) if depth > 1 else (new_k[0][None], new_v[0][None])
    swa_k_s, swa_v_s = (jnp.transpose(c, (0, 1, 4, 2, 3)) for c in stacked)
    return (yp.reshape(b, s, D_MODEL), ys.reshape(db, t, D_MODEL), swa_k_p, swa_v_p, ssm_re_p, ssm_im_p,
            mk.reshape(mem_shape), mv.reshape(mem_shape), swa_k_s, swa_v_s, ssm_re_s, ssm_im_s)
```

```python
import functools
import math

import jax
import jax.numpy as jnp
from jax import lax
from jax.experimental import pallas as pl
from jax.experimental.pallas import tpu as pltpu

F32 = jnp.float32
BF16 = jnp.bfloat16

D_MODEL = 1024
BRANCH_W = 512
HEAD_DIM = 64
N_HEADS = 8
N_KV_HEADS = 2
KV_GROUP = 4
KV_W = N_KV_HEADS * HEAD_DIM
WINDOW = 128
ROPE_THETA = 10000.0
PAST_LEN = 16384
SSM_GROUPS = 32
SSM_GROUP_CH = 16
SSM_STATE = 64
N_MEM = 256
MEM_HEADS = 4
MEM_HEAD_DIM = 128
D_FF = 2816
RMS_EPS = 1e-6

Q_OFF, K_OFF, V_OFF, U_OFF, MQ_OFF, G_OFF = 0, 512, 640, 768, 1280, 1792
IN_W = G_OFF + 3 * D_MODEL

LANES = 128
SUBLANES = 8
MXU_TILE = 256
GROUPS_PER_LANE_BLOCK = LANES // SSM_GROUP_CH
N_LANE_BLOCKS = BRANCH_W // LANES
STATE_W = GROUPS_PER_LANE_BLOCK * SSM_STATE
PROMPT_CHUNK = 8
ROW_TILE = 512
VMEM_LIMIT = 56 * 1024 * 1024


def _dot(a, b):
    return jnp.dot(a, b, preferred_element_type=F32)


def _dot_nt(a, b):
    return lax.dot_general(a, b, (((1,), (1,)), ((), ())), preferred_element_type=F32)


def _cparams(n_axes):
    return pltpu.CompilerParams(dimension_semantics=("arbitrary",) * n_axes,
                                vmem_limit_bytes=VMEM_LIMIT)


def _full(shape):
    nd = len(shape)
    return pl.BlockSpec(shape, lambda *_: (0,) * nd)


def _layer(shape, l):
    nd = len(shape)
    return pl.BlockSpec((None,) + tuple(shape), lambda *_: (l,) + (0,) * nd, pipeline_mode=pl.Buffered(1))


def _sigmoid(x):
    return 0.5 * jnp.tanh(0.5 * x) + 0.5


def _rms(x):
    return x * lax.rsqrt(jnp.mean(x * x, axis=-1, keepdims=True) + RMS_EPS)


def _memkv_kernel(mem_ref, g_ref, w_ref, kg_ref, k_ref, v_ref):
    l = pl.program_id(0)
    h = _rms(mem_ref[...]) * g_ref[pl.ds(l, 1), :]
    kv = _dot(h.astype(BF16), w_ref[...])
    kg = kg_ref[pl.ds(l, 1), :]
    for hd in range(MEM_HEADS):
        sl = slice(hd * MEM_HEAD_DIM, (hd + 1) * MEM_HEAD_DIM)
        k_ref[:, sl] = _rms(kv[:, sl]) * kg
    v_ref[...] = kv[:, BRANCH_W:]


def _memkv(mem2d, g, w_bf, kg):
    n = mem2d.shape[0]
    depth = w_bf.shape[0]
    out = pl.BlockSpec((None, ROW_TILE, BRANCH_W), lambda l, i: (l, i, 0))
    return pl.pallas_call(
        _memkv_kernel, name="memkv",
        grid=(depth, n // ROW_TILE),
        in_specs=[pl.BlockSpec((ROW_TILE, D_MODEL), lambda l, i: (i, 0)), _full(g.shape),
                  pl.BlockSpec((None, D_MODEL, 2 * BRANCH_W), lambda l, i: (l, 0, 0)), _full(kg.shape)],
        out_specs=[out, out],
        out_shape=[jax.ShapeDtypeStruct((depth, n, BRANCH_W), F32)] * 2,
        compiler_params=_cparams(2),
    )(mem2d, g, w_bf, kg)


def _rotary(x, cos, sin_lo, sin_hi):
    w = x.shape[-1]
    half = HEAD_DIM // 2
    return x * cos + pltpu.roll(x, w - half, 1) * sin_lo + pltpu.roll(x, half, 1) * sin_hi


GATE_CHUNK = 512
N_GATE_CHUNKS = 3 * D_MODEL // GATE_CHUNK


class _Projection:
    def __init__(self, x, g, w_ref, rope, qg, kg, mqg, ones):
        self.hb = (_rms(x) * g).astype(BF16)
        self.w_ref, self.qg, self.kg, self.mqg = w_ref, qg, kg, mqg
        self.ones = ones
        self.rope = rope

    def queries(self):
        zq = _dot(self.hb, self.w_ref[:, Q_OFF:K_OFF])
        sq = (zq * zq).astype(BF16)
        ssq = jnp.concatenate([_dot(sq[:, :256], self.ones), _dot(sq[:, 256:], self.ones)], axis=1)
        qn = zq * lax.rsqrt(ssq * (1.0 / HEAD_DIM) + RMS_EPS) * self.qg
        rope4 = (jnp.concatenate([t] * 4, axis=1) for t in self.rope)
        return (_rotary(qn, *rope4) * (1.0 / math.sqrt(HEAD_DIM))).astype(BF16)

    def keys_values(self):
        zkv = _dot(self.hb, self.w_ref[:, K_OFF:U_OFF])
        zk = zkv[:, :KV_W]
        ssk = _dot((zk * zk).astype(BF16), self.ones[:KV_W, :KV_W])
        kn = zk * lax.rsqrt(ssk * (1.0 / HEAD_DIM) + RMS_EPS) * self.kg
        return _rotary(kn, *self.rope), zkv[:, KV_W:]

    def ssm_input(self):
        return _dot(self.hb, self.w_ref[:, U_OFF:MQ_OFF])

    def memory_queries(self):
        zm = _dot(self.hb, self.w_ref[:, MQ_OFF:G_OFF])
        mqg = self.mqg * (1.0 / math.sqrt(MEM_HEAD_DIM))
        return jnp.concatenate([_rms(zm[:, hd * MEM_HEAD_DIM:(hd + 1) * MEM_HEAD_DIM]) * mqg
                                for hd in range(MEM_HEADS)], axis=1).astype(BF16)

    def gates(self, c):
        lo = G_OFF + c * GATE_CHUNK
        return _sigmoid(_dot(self.hb, self.w_ref[:, lo:lo + GATE_CHUNK])).astype(BF16)


ATTN_TILE = 512


def _softmax_rows(s):
    m = jnp.max(s, axis=-1, keepdims=True)
    p = jnp.exp(s - m)
    return p, jnp.sum(p, axis=-1, keepdims=True)


def _sink_column(sink_ref, l, heads, rows_per_head):
    n = len(heads) * rows_per_head
    rcol = lax.broadcasted_iota(jnp.int32, (n, 1), 0)
    sink = jnp.full((n, 1), sink_ref[l, heads[-1]], F32)
    for g in range(len(heads) - 2, -1, -1):
        sink = jnp.where(rcol < (g + 1) * rows_per_head, sink_ref[l, heads[g]], sink)
    return sink


def _sink_softmax(s, sink):
    m = jnp.maximum(jnp.max(s, axis=-1, keepdims=True), sink)
    p = jnp.exp(s - m)
    return p, jnp.sum(p, axis=-1, keepdims=True) + jnp.exp(sink - m)


def _attend_sub_block(sub, mask, sinks, q_ref, mq_ref, k_ref, v_ref, mk_ref, mv_ref, oa_ref, oc_ref):
    qrows = slice(sub * WINDOW, (sub + 1) * WINDOW)
    krows = slice(sub * WINDOW, (sub + 2) * WINDOW)
    ksl = [slice(kvh * HEAD_DIM, (kvh + 1) * HEAD_DIM) for kvh in range(N_KV_HEADS)]
    assert ATTN_TILE // WINDOW == MEM_HEADS
    msl = [slice(sub * MEM_HEAD_DIM, (sub + 1) * MEM_HEAD_DIM)]
    mrows = slice(0, ATTN_TILE)
    heads = [[kvh * KV_GROUP + g for g in range(KV_GROUP)] for kvh in range(N_KV_HEADS)]
    scores = []
    for kvh in range(N_KV_HEADS):
        qg = jnp.concatenate([q_ref[qrows, hd * HEAD_DIM:(hd + 1) * HEAD_DIM] for hd in heads[kvh]], axis=0)
        scores.append(jnp.where(mask, _dot_nt(qg, k_ref[krows, ksl[kvh]]), -jnp.inf))
    mscores = [_dot_nt(mq_ref[mrows, sl], mk_ref[:, sl]) for sl in msl]
    yield
    probs = [_sink_softmax(scores[kvh], sinks[kvh]) for kvh in range(N_KV_HEADS)]
    mprobs = [_softmax_rows(s) for s in mscores]
    yield
    outs = [_dot(probs[kvh][0].astype(BF16), v_ref[krows, ksl[kvh]]) / probs[kvh][1]
            for kvh in range(N_KV_HEADS)]
    mouts = [_dot(mprobs[hd][0].astype(BF16), mv_ref[:, msl[hd]]) / mprobs[hd][1] for hd in range(len(msl))]
    for kvh in range(N_KV_HEADS):
        for g, hd in enumerate(heads[kvh]):
            oa_ref[qrows, hd * HEAD_DIM:(hd + 1) * HEAD_DIM] = outs[kvh][g * WINDOW:(g + 1) * WINDOW].astype(BF16)
    for hd in range(len(msl)):
        oc_ref[mrows, msl[hd]] = mouts[hd].astype(BF16)


def _inproj_attn_kernel(l, tiles_per_seq, sink_ref, x_ref, xs_ref, g_ref, w_ref, cos_ref, slo_ref, shi_ref,
                        qg_ref, kg_ref, mqg_ref, ones_ref, mk_ref, mv_ref,
                        k_ref, v_ref, u_ref, gates_ref, oa_ref, oc_ref, qs_ref, mqs_ref,
                        q_scr, mq_scr, k_scr, v_scr, mk_scr, mv_scr):
    i = pl.program_id(0)
    cur, prv = i % 2, (i + 1) % 2
    decode_step = i == pl.num_programs(0) - 1

    @pl.when(i == 0)
    def _():
        for scr in (q_scr, mq_scr, k_scr, v_scr, mk_scr, mv_scr):
            scr[...] = jnp.zeros(scr.shape, scr.dtype)

    @pl.when(i % tiles_per_seq == 1)
    def _():
        mk_scr[...] = mk_ref[...].astype(BF16)
        mv_scr[...] = mv_ref[...].astype(BF16)

    rows = KV_GROUP * WINDOW
    r = lax.broadcasted_iota(jnp.int32, (rows, 2 * WINDOW), 0)
    j = lax.broadcasted_iota(jnp.int32, (rows, 2 * WINDOW), 1)
    qi = r % WINDOW
    band = (j > qi) & (j <= qi + WINDOW)
    first_band = band & ((j >= WINDOW) | (i % tiles_per_seq != 1))
    sinks = [_sink_column(sink_ref, l, [kvh * KV_GROUP + g for g in range(KV_GROUP)], WINDOW)
             for kvh in range(N_KV_HEADS)]

    def attention_stages():
        for sub in range(ATTN_TILE // WINDOW):
            yield from _attend_sub_block(sub, first_band if sub == 0 else band, sinks, q_scr.at[prv],
                                         mq_scr.at[prv], k_scr.at[prv], v_scr.at[prv], mk_scr, mv_scr,
                                         oa_ref, oc_ref)
            yield

    proj = _Projection(jnp.where(decode_step, xs_ref[...], x_ref[...]), g_ref[l:l + 1, :], w_ref,
                       (cos_ref[...], slo_ref[...], shi_ref[...]), qg_ref[l:l + 1, :], kg_ref[l:l + 1, :],
                       mqg_ref[l:l + 1, :], ones_ref[...])

    def store_queries():
        q = proj.queries()
        q_scr[cur] = q
        qs_ref[...] = q

    def store_keys_values():
        k, v = proj.keys_values()
        k_ref[...], v_ref[...] = k, v
        k_scr[cur, :WINDOW, :] = k_scr[prv, ATTN_TILE:, :]
        v_scr[cur, :WINDOW, :] = v_scr[prv, ATTN_TILE:, :]
        k_scr[cur, WINDOW:, :] = k.astype(BF16)
        v_scr[cur, WINDOW:, :] = v.astype(BF16)

    def store_ssm_input():
        u_ref[...] = proj.ssm_input()

    def store_memory_queries():
        mq = proj.memory_queries()
        mq_scr[cur] = mq
        mqs_ref[...] = mq

    def store_gates(c):
        gates_ref[:, c * GATE_CHUNK:(c + 1) * GATE_CHUNK] = proj.gates(c)

    segments = [store_queries, store_keys_values, store_ssm_input, store_memory_queries]
    segments += [functools.partial(store_gates, c) for c in range(N_GATE_CHUNKS)]
    stages = attention_stages()
    n_stages = 3 * (ATTN_TILE // WINDOW)
    for n, segment in enumerate(segments):
        for _ in range((n + 1) * n_stages // len(segments) - n * n_stages // len(segments)):
            next(stages)
        segment()


def _inproj_attn(l, sinks, x2d, xs2d, seq_len, g, w_bf, rope, qg, kg, mqg, ones, mk, mv):
    n = x2d.shape[0]
    n_tiles = n // ROW_TILE
    tiles_per_seq = seq_len // ROW_TILE
    assert ROW_TILE == ATTN_TILE and tiles_per_seq > 1 and xs2d.shape[0] == ROW_TILE
    attn_tile = lambda i: jnp.maximum(i - 1, 0)
    xrow = pl.BlockSpec((ROW_TILE, D_MODEL), lambda i: (jnp.minimum(i, n_tiles - 1), 0))
    orow = lambda w: pl.BlockSpec((ROW_TILE, w), lambda i: (i, 0))
    arow = pl.BlockSpec((ROW_TILE, BRANCH_W), lambda i: (attn_tile(i), 0))
    rspec = pl.BlockSpec((ROW_TILE, LANES), lambda i: (jnp.where(i == n_tiles, tiles_per_seq, i % tiles_per_seq), 0))
    mem = pl.BlockSpec((None, None, N_MEM, BRANCH_W), lambda i: (l, attn_tile(i) // tiles_per_seq, 0, 0))
    widths = (KV_W, KV_W, BRANCH_W, 3 * D_MODEL)
    dtypes = (F32, F32, F32, BF16)
    return pl.pallas_call(
        functools.partial(_inproj_attn_kernel, l, tiles_per_seq), name="inproj_attn",
        grid=(n_tiles + 1,),
        in_specs=[pl.BlockSpec(memory_space=pltpu.SMEM), xrow, _full((ROW_TILE, D_MODEL)), _full(g.shape),
                  _layer((D_MODEL, IN_W), l), rspec, rspec, rspec, _full(qg.shape), _full(kg.shape),
                  _full(mqg.shape), _full((256, 256)), mem, mem],
        out_specs=[orow(w) for w in widths] + [arow, arow] + [_full((ROW_TILE, BRANCH_W))] * 2,
        out_shape=[jax.ShapeDtypeStruct((n + ROW_TILE, w), d) for w, d in zip(widths, dtypes)]
                  + [jax.ShapeDtypeStruct((n, BRANCH_W), BF16)] * 2
                  + [jax.ShapeDtypeStruct((ROW_TILE, BRANCH_W), BF16)] * 2,
        scratch_shapes=[pltpu.VMEM((2, ROW_TILE, BRANCH_W), BF16)] * 2
                       + [pltpu.VMEM((2, WINDOW + ROW_TILE, KV_W), BF16)] * 2
                       + [pltpu.VMEM((N_MEM, BRANCH_W), BF16)] * 2,
        compiler_params=_cparams(1),
    )(sinks, x2d, xs2d, g, w_bf, *rope, qg, kg, mqg, ones, mk, mv)


DEC_BLOCK = 16
TOK_PAD = 8


def _pad_rows(x, n):
    return jnp.concatenate([x, jnp.zeros((n - x.shape[0], x.shape[1]), x.dtype)], axis=0)


def _attn_sample_kernel(l, t, sink_ref, q_ref, kn_ref, vn_ref, ckt_ref, cvt_ref, qm_ref, mk_ref, mv_ref,
                        *earlier_and_outputs):
    oa_ref, oc_ref, nkt_ref, nvt_ref = earlier_and_outputs[-4:]
    earlier = earlier_and_outputs[:-4]
    if earlier:
        for j in range(l):
            nkt_ref[j] = earlier[j][...]
            nvt_ref[j] = earlier[l + j][...]
        nkt_ref, nvt_ref = nkt_ref.at[l], nvt_ref.at[l]
    keys = 2 * WINDOW
    rows = N_HEADS * TOK_PAD
    rk = lax.broadcasted_iota(jnp.int32, (rows, keys), 0)
    jk = lax.broadcasted_iota(jnp.int32, (rows, keys), 1)
    tk = rk % TOK_PAD
    mask = (tk < t) & (((jk < WINDOW) & (jk > tk)) | ((jk >= WINDOW) & (jk - WINDOW <= tk)))
    sink = _sink_column(sink_ref, l, list(range(N_HEADS)), TOK_PAD)
    mem_rows = N_MEM * MEM_HEADS
    mem_own = (lax.broadcasted_iota(jnp.int32, (MEM_HEADS * TOK_PAD, mem_rows), 1) % MEM_HEADS
               == lax.broadcasted_iota(jnp.int32, (MEM_HEADS * TOK_PAD, mem_rows), 0) // TOK_PAD)
    is_new = lax.broadcasted_iota(jnp.int32, (HEAD_DIM, WINDOW), 1) >= WINDOW - t
    dsl = [slice(kvh * HEAD_DIM, (kvh + 1) * HEAD_DIM) for kvh in range(N_KV_HEADS)]
    group_rows = KV_GROUP * TOK_PAD

    q_all, qm_all = q_ref[...].astype(F32), qm_ref[...].astype(F32)
    seqs = range(DEC_BLOCK)
    toks = [slice(bi * t, (bi + 1) * t) for bi in seqs]
    knp = [_pad_rows(kn_ref[toks[bi], :], WINDOW) for bi in seqs]
    vnp = [_pad_rows(vn_ref[toks[bi], :], WINDOW) for bi in seqs]
    knt, vnt = [x.T for x in knp], [x.T for x in vnp]
    for bi in seqs:
        for kvh in range(N_KV_HEADS):
            nkt_ref[bi, kvh] = jnp.where(is_new, pltpu.roll(knt[bi][dsl[kvh], :], WINDOW - t, 1),
                                         pltpu.roll(ckt_ref[bi, kvh], WINDOW - t, 1))
            nvt_ref[bi, kvh] = jnp.where(is_new, pltpu.roll(vnt[bi][dsl[kvh], :], WINDOW - t, 1),
                                         pltpu.roll(cvt_ref[bi, kvh], WINDOW - t, 1))
    qg = []
    for bi in seqs:
        q8 = _pad_rows(q_all[toks[bi]], TOK_PAD)
        qg.append([jnp.concatenate([q8[:, hd * HEAD_DIM:(hd + 1) * HEAD_DIM]
                                    for hd in range(kvh * KV_GROUP, (kvh + 1) * KV_GROUP)], axis=0).astype(BF16)
                   for kvh in range(N_KV_HEADS)])
    scores = []
    for bi in seqs:
        per_kvh = [jnp.concatenate([_dot(qg[bi][kvh], ckt_ref[bi, kvh].astype(BF16)),
                                    _dot_nt(qg[bi][kvh], knp[bi][:, dsl[kvh]].astype(BF16))], axis=1)
                   for kvh in range(N_KV_HEADS)]
        scores.append(jnp.where(mask, jnp.concatenate(per_kvh, axis=0), -jnp.inf))
    mscores = []
    for bi in seqs:
        qm8 = jnp.concatenate([qm_all[toks[bi]], jnp.zeros((TOK_PAD - t, BRANCH_W), F32)], axis=0).astype(BF16)
        qst = jnp.concatenate([qm8[:, hd * MEM_HEAD_DIM:(hd + 1) * MEM_HEAD_DIM] for hd in range(MEM_HEADS)],
                              axis=0)
        mscores.append(jnp.where(mem_own, _dot_nt(qst, mk_ref[bi].astype(BF16)), -jnp.inf))
    probs = [_sink_softmax(scores[bi], sink) for bi in seqs]
    mprobs = [_softmax_rows(mscores[bi]) for bi in seqs]
    outs = []
    for bi in seqs:
        p = probs[bi][0].astype(BF16)
        per_kvh = []
        for kvh in range(N_KV_HEADS):
            pk = p[kvh * group_rows:(kvh + 1) * group_rows]
            per_kvh.append(_dot_nt(pk[:, :WINDOW], cvt_ref[bi, kvh].astype(BF16))
                           + _dot(pk[:, WINDOW:], vnp[bi][:, dsl[kvh]].astype(BF16)))
        outs.append(jnp.concatenate(per_kvh, axis=0) / probs[bi][1])
    mouts = [_dot(mprobs[bi][0].astype(BF16), mv_ref[bi].astype(BF16)) / mprobs[bi][1] for bi in seqs]
    for bi in seqs:
        oa_ref[toks[bi], :] = jnp.concatenate(
            [outs[bi][hd * TOK_PAD:(hd + 1) * TOK_PAD] for hd in range(N_HEADS)], axis=1)[0:t]
        oc_ref[toks[bi], :] = jnp.concatenate(
            [mouts[bi][hd * TOK_PAD:(hd + 1) * TOK_PAD] for hd in range(MEM_HEADS)], axis=1)[0:t]


def _attn_sample(l, t, sinks, q, kn, vn, kv_row0, ckt, cvt, qm, mk, mv, earlier_k, earlier_v):
    n = q.shape[0]
    tok = lambda w, row0=0: pl.BlockSpec((DEC_BLOCK * t, w), lambda i: (i + row0 // (DEC_BLOCK * t), 0))
    cache_tile = (DEC_BLOCK, N_KV_HEADS, HEAD_DIM, WINDOW)
    cache_in = pl.BlockSpec((None,) + cache_tile, lambda i: (l, i, 0, 0, 0))
    one_layer = pl.BlockSpec(cache_tile, lambda i: (i, 0, 0, 0))
    mem = pl.BlockSpec((None, DEC_BLOCK, N_MEM * MEM_HEADS, MEM_HEAD_DIM), lambda i: (l, i, 0, 0))
    assert len(earlier_k) == len(earlier_v) and len(earlier_k) in (0, l)
    if earlier_k:
        cache_out = pl.BlockSpec((l + 1,) + cache_tile, lambda i: (0, i, 0, 0, 0))
        cache_shape = jax.ShapeDtypeStruct((l + 1,) + ckt.shape[1:], F32)
    else:
        cache_out, cache_shape = one_layer, jax.ShapeDtypeStruct(ckt.shape[1:], F32)
    return pl.pallas_call(
        functools.partial(_attn_sample_kernel, l, t), name="attn_sample",
        grid=(n // (DEC_BLOCK * t),),
        in_specs=[pl.BlockSpec(memory_space=pltpu.SMEM), tok(BRANCH_W), tok(KV_W, kv_row0), tok(KV_W, kv_row0),
                  cache_in, cache_in, tok(BRANCH_W), mem, mem] + [one_layer] * (2 * len(earlier_k)),
        out_specs=[tok(BRANCH_W), tok(BRANCH_W), cache_out, cache_out],
        out_shape=[jax.ShapeDtypeStruct((n, BRANCH_W), F32)] * 2 + [cache_shape] * 2,
        compiler_params=_cparams(1),
    )(sinks, q, kn, vn, ckt, cvt, qm, mk, mv, *earlier_k, *earlier_v)


def _ssm_tables(a_re, a_im, log_dt, b_re, b_im, c_re, c_im, chunk):
    hp = lax.Precision.HIGHEST
    dt = jnp.exp(log_dt)[..., None]
    mag = jnp.exp(a_re * dt)
    lam_re, lam_im = mag * jnp.cos(a_im * dt), mag * jnp.sin(a_im * dt)
    den = a_re * a_re + a_im * a_im
    nr, ni = lam_re - 1.0, lam_im
    g_re = (nr * a_re + ni * a_im) / den
    g_im = (ni * a_re - nr * a_im) / den
    bg_re = g_re[..., None] * b_re - g_im[..., None] * b_im
    bg_im = g_re[..., None] * b_im + g_im[..., None] * b_re
    def powers(d):
        d = d.astype(F32)[:, None, None, None]
        pm = jnp.exp(d * (a_re * dt)[None])
        return pm * jnp.cos(d * (a_im * dt)[None]), pm * jnp.sin(d * (a_im * dt)[None])

    def times_gb(p_re, p_im):
        return (p_re[..., None] * bg_re[None] - p_im[..., None] * bg_im[None],
                p_re[..., None] * bg_im[None] + p_im[..., None] * bg_re[None])

    pw_re, pw_im = powers(jnp.arange(chunk + 1))
    z_re, z_im = times_gb(pw_re[:chunk], pw_im[:chunk])
    zrev_re, zrev_im = times_gb(*powers(jnp.arange(chunk - 1, -1, -1)))
    kern = (jnp.einsum('lgop,dlgpi->dlgio', c_re, z_re, precision=hp)
            - jnp.einsum('lgop,dlgpi->dlgio', c_im, z_im, precision=hp))
    nl = a_re.shape[0]
    nb, gb = N_LANE_BLOCKS, GROUPS_PER_LANE_BLOCK
    kd = jnp.transpose(kern.reshape(chunk, nl, nb, gb, SSM_GROUP_CH, SSM_GROUP_CH), (1, 2, 0, 3, 4, 5))
    kd = kd.reshape(nl, nb, chunk, LANES, SSM_GROUP_CH)

    def to_rows(z):
        z = z.reshape(chunk, nl, nb, gb, SSM_STATE, SSM_GROUP_CH)
        return jnp.transpose(z, (1, 2, 0, 3, 5, 4)).reshape(nl, nb, chunk, LANES, SSM_STATE)

    zr, zi = to_rows(zrev_re), to_rows(zrev_im)
    zc = jnp.concatenate([zr, zr, zi, zi], axis=-1)

    pr, pi = pw_re[1:], pw_im[1:]
    f_a = c_re[None] * pr[:, :, :, None, :] - c_im[None] * pi[:, :, :, None, :]
    f_b = c_re[None] * pi[:, :, :, None, :] + c_im[None] * pr[:, :, :, None, :]

    def from_cols(f):
        f = f.reshape(chunk, nl, nb, gb, SSM_GROUP_CH, SSM_STATE)
        return jnp.transpose(f, (1, 2, 0, 4, 3, 5)).reshape(nl, nb, chunk, SSM_GROUP_CH, STATE_W)

    ft = jnp.concatenate([from_cols(f_a), from_cols(-f_b)], axis=-1)

    kw = chunk * LANES
    spec = lambda *shape: pl.BlockSpec((None, None) + shape, lambda l, n: (l, n) + (0,) * len(shape))
    intra, to_st, from_st_t = pl.pallas_call(
        _ssm_table_kernel, name="ssm_tables",
        grid=(nl, nb),
        in_specs=[spec(chunk, LANES, SSM_GROUP_CH), spec(chunk, LANES, 2 * LANES),
                  spec(chunk, SSM_GROUP_CH, 2 * STATE_W)],
        out_specs=[spec(kw, kw), spec(kw, 2 * STATE_W), spec(kw, 2 * STATE_W)],
        out_shape=[jax.ShapeDtypeStruct((nl, nb, kw, kw), BF16),
                   jax.ShapeDtypeStruct((nl, nb, kw, 2 * STATE_W), BF16),
                   jax.ShapeDtypeStruct((nl, nb, kw, 2 * STATE_W), BF16)],
        compiler_params=_cparams(2),
    )(kd, zc, ft)
    decay = lambda d: (pw_re[d].reshape(nl, nb, 1, STATE_W), pw_im[d].reshape(nl, nb, 1, STATE_W))
    return (intra, to_st, from_st_t), decay


def _ssm_table_kernel(kd_ref, zc_ref, ft_ref, intra_ref, to_ref, fromt_ref):
    chunk = kd_ref.shape[0]
    gch = SSM_GROUP_CH
    r = lax.broadcasted_iota(jnp.int32, (LANES, LANES), 0)
    c = lax.broadcasted_iota(jnp.int32, (LANES, LANES), 1)
    same_group = (r // gch) == (c // gch)
    spread = (lax.broadcasted_iota(jnp.int32, (gch, LANES), 1) % gch
              == lax.broadcasted_iota(jnp.int32, (gch, LANES), 0)).astype(BF16)
    spread_t = (lax.broadcasted_iota(jnp.int32, (LANES, gch), 0) % gch
                == lax.broadcasted_iota(jnp.int32, (LANES, gch), 1)).astype(BF16)
    rs = lax.broadcasted_iota(jnp.int32, (LANES, 2 * STATE_W), 0)
    cs = lax.broadcasted_iota(jnp.int32, (LANES, 2 * STATE_W), 1)
    own_state = (rs // gch) == ((cs % STATE_W) // SSM_STATE)

    lag_blocks = [jnp.where(same_group, _dot(kd_ref[d].astype(BF16), spread), 0.0).astype(BF16)
                  for d in range(chunk)]
    zero = jnp.zeros((LANES, LANES), BF16)
    for s in range(chunk):
        for t in range(chunk):
            intra_ref[s * LANES:(s + 1) * LANES, t * LANES:(t + 1) * LANES] = (
                lag_blocks[t - s] if t >= s else zero)
        z = zc_ref[s]
        full = jnp.concatenate([z[:, :LANES]] * (STATE_W // LANES) + [z[:, LANES:]] * (STATE_W // LANES), axis=1)
        to_ref[s * LANES:(s + 1) * LANES, :] = jnp.where(own_state, full, 0.0).astype(BF16)
        f = _dot(spread_t, ft_ref[s].astype(BF16))
        fromt_ref[s * LANES:(s + 1) * LANES, :] = jnp.where(own_state, f, 0.0).astype(BF16)


SSM_SEQS = 2


def _chunk_rows(ref, chunk):
    n = ref.shape[0] // chunk
    return jnp.concatenate([ref[pl.ds(s, n, stride=chunk), :] for s in range(chunk)], axis=1)


def _store_chunk_rows(ref, y, chunk):
    n = ref.shape[0] // chunk
    for t in range(chunk):
        ref[pl.ds(t, n, stride=chunk), :] = y[:, t * LANES:(t + 1) * LANES]


def _ssm_prompt_kernel(u_ref, intra_ref, to_ref, fromt_ref, dre_ref, dim_ref,
                       y_ref, hre_ref, him_ref, st_scr):
    chunk = PROMPT_CHUNK
    nchunks = u_ref.shape[0] // (chunk * SSM_SEQS)
    x = _chunk_rows(u_ref, chunk).astype(BF16)
    kw = chunk * LANES
    y = jnp.concatenate([_dot(x[:, :c + MXU_TILE], intra_ref[:c + MXU_TILE, c:c + MXU_TILE])
                         for c in range(0, kw, MXU_TILE)], axis=1)
    add_all = _dot(x, to_ref[...])
    nblk = STATE_W // LANES
    assert SSM_SEQS * nblk == SUBLANES
    slots = [(seq, cb) for seq in range(SSM_SEQS) for cb in range(nblk)]

    def slot_rows(seq, cb):
        return pl.ds(seq * nblk + cb, nchunks, stride=SUBLANES)

    for c in range(2):
        for seq, cb in slots:
            col = (c * nblk + cb) * LANES
            st_scr[c, slot_rows(seq, cb), :] = add_all[seq * nchunks:(seq + 1) * nchunks, col:col + LANES]
    dre = jnp.concatenate([dre_ref[:, cb * LANES:(cb + 1) * LANES] for _, cb in slots], axis=0)
    dim = jnp.concatenate([dim_ref[:, cb * LANES:(cb + 1) * LANES] for _, cb in slots], axis=0)

    def step(k, carry):
        hr, hi = carry
        tile = pl.ds(pl.multiple_of(k * SUBLANES, SUBLANES), SUBLANES)
        add_r, add_i = st_scr[0, tile, :], st_scr[1, tile, :]
        st_scr[0, tile, :] = hr
        st_scr[1, tile, :] = hi
        return dre * hr - dim * hi + add_r, dre * hi + dim * hr + add_i

    zero = jnp.zeros((SUBLANES, LANES), F32)
    hr, hi = lax.fori_loop(0, nchunks, step, (zero, zero), unroll=8)
    hre_ref[...] = hr
    him_ref[...] = hi
    h_in = jnp.concatenate(
        [jnp.concatenate([st_scr[c, slot_rows(seq, cb), :] for seq in range(SSM_SEQS)], axis=0)
         for c in range(2) for cb in range(nblk)], axis=1).astype(BF16)
    _store_chunk_rows(y_ref, y + _dot_nt(h_in, fromt_ref[...]), chunk)


def _ssm_prompt(l, u2d, n_seqs, seq_len, intra, to_st, from_st_t, dec_re, dec_im):
    n = n_seqs * seq_len
    chunk = PROMPT_CHUNK
    rows = SSM_SEQS * seq_len
    steps = n // rows
    kw = chunk * LANES
    data = pl.BlockSpec((rows, LANES), lambda j, h: (h, j))
    wspec = lambda r, c: pl.BlockSpec((None, None, r, c), lambda j, h: (l, j, 0, 0))
    st_out = pl.BlockSpec((None, None, SUBLANES, LANES), lambda j, h: (j, h, 0, 0))
    st_shape = jax.ShapeDtypeStruct((N_LANE_BLOCKS, steps, SUBLANES, LANES), F32)
    return pl.pallas_call(
        _ssm_prompt_kernel, name="ssm_prompt",
        grid=(N_LANE_BLOCKS, steps),
        in_specs=[data, wspec(kw, kw), wspec(kw, 2 * STATE_W), wspec(kw, 2 * STATE_W),
                  wspec(1, STATE_W), wspec(1, STATE_W)],
        out_specs=[data, st_out, st_out],
        out_shape=[jax.ShapeDtypeStruct((n, BRANCH_W), F32), st_shape, st_shape],
        scratch_shapes=[pltpu.VMEM((2, SUBLANES * seq_len // chunk, LANES), F32)],
        compiler_params=_cparams(2),
    )(u2d, intra, to_st, from_st_t, dec_re, dec_im)


def _ssm_sample_kernel(chunk, u_ref, h0r_ref, h0i_ref, intra_ref, to_ref, fromt_ref, dre_ref, dim_ref,
                       y_ref, hre_ref, him_ref):
    x = _chunk_rows(u_ref, chunk).astype(BF16)
    hr, hi = h0r_ref[...], h0i_ref[...]
    dre, dim = dre_ref[...], dim_ref[...]
    add = _dot(x, to_ref[...])
    hre_ref[...] = dre * hr - dim * hi + add[:, :STATE_W]
    him_ref[...] = dre * hi + dim * hr + add[:, STATE_W:]
    h0 = jnp.concatenate([hr, hi], axis=1).astype(BF16)
    _store_chunk_rows(y_ref, _dot(x, intra_ref[...]) + _dot_nt(h0, fromt_ref[...]), chunk)


def _ssm_sample(l, u2d, row0, n, t, h0_re, h0_im, intra, to_st, from_st_t, dec_re, dec_im):
    b = n // t
    kw = t * LANES
    assert PROMPT_CHUNK % t == 0 and row0 % n == 0
    last = PROMPT_CHUNK // t - 1
    data_in = pl.BlockSpec((n, LANES), lambda j: (row0 // n, j))
    data = pl.BlockSpec((n, LANES), lambda j: (0, j))
    st_in = pl.BlockSpec((None, b, STATE_W), lambda j: (l, 0, j))
    st_out = pl.BlockSpec((b, STATE_W), lambda j: (0, j))
    wspec = lambda r, c, rb=0: pl.BlockSpec((None, None, r, c), lambda j: (l, j, rb, 0))
    return pl.pallas_call(
        functools.partial(_ssm_sample_kernel, t), name="ssm_sample",
        grid=(N_LANE_BLOCKS,),
        in_specs=[data_in, st_in, st_in, wspec(kw, kw), wspec(kw, 2 * STATE_W, last), wspec(kw, 2 * STATE_W),
                  wspec(1, STATE_W), wspec(1, STATE_W)],
        out_specs=[data, st_out, st_out],
        out_shape=[jax.ShapeDtypeStruct((n, BRANCH_W), F32),
                   jax.ShapeDtypeStruct((b, N_LANE_BLOCKS * STATE_W), F32),
                   jax.ShapeDtypeStruct((b, N_LANE_BLOCKS * STATE_W), F32)],
        compiler_params=_cparams(1),
    )(u2d, h0_re, h0_im, intra, to_st, from_st_t, dec_re, dec_im)


FF_CHUNKS = ((0, 512), (512, 512), (1024, 512), (1536, 512), (2048, 512), (2560, 256))


def _merge_ffn_kernel(l, x_ref, oa_ref, y_ref, u_ref, oc_ref, gates_ref, d_ref, wglu_ref, wb_ref, wout_ref,
                      g_ref, wup_ref, wdn_ref, o_ref):
    z = jax.nn.gelu(y_ref[...] + d_ref[l:l + 1, :] * u_ref[...])
    ob = (z * _sigmoid(_dot(z.astype(BF16), wglu_ref[...]))).astype(BF16)
    gate = lambda n: gates_ref[:, n * D_MODEL:(n + 1) * D_MODEL].astype(F32)
    merged = gate(0) * _dot(oa_ref[...], wb_ref[0])
    merged = merged + gate(1) * _dot(ob, wb_ref[1])
    merged = merged + gate(2) * _dot(oc_ref[...], wb_ref[2])
    x = x_ref[...] + _dot(merged.astype(BF16), wout_ref[...])
    h = (_rms(x) * g_ref[l:l + 1, :]).astype(BF16)
    acc = x
    for lo, w in FF_CHUNKS:
        up_gate = _dot(h, wup_ref[:, lo:lo + w])
        up = _dot(h, wup_ref[:, D_FF + lo:D_FF + lo + w])
        acc = acc + _dot((up_gate * _sigmoid(up_gate) * up).astype(BF16), wdn_ref[lo:lo + w, :])
    o_ref[...] = acc


def _merge_ffn(l, x2d, oa, y, u, oc, gates, proj_row0, d, wglu, wb, wout, g_ffn, wup, wdn):
    n = x2d.shape[0]
    assert proj_row0 % ROW_TILE == 0
    row = lambda w, row0=0: pl.BlockSpec((ROW_TILE, w), lambda i: (i + row0 // ROW_TILE, 0))
    return pl.pallas_call(
        functools.partial(_merge_ffn_kernel, l), name="merge_ffn",
        grid=(n // ROW_TILE,),
        in_specs=[row(D_MODEL), row(BRANCH_W), row(BRANCH_W), row(BRANCH_W, proj_row0), row(BRANCH_W),
                  row(3 * D_MODEL, proj_row0), _full(d.shape), _layer((BRANCH_W, BRANCH_W), l),
                  _layer((3, BRANCH_W, D_MODEL), l), _layer((D_MODEL, D_MODEL), l),
                  _full(g_ffn.shape), _layer((D_MODEL, 2 * D_FF), l), _layer((D_FF, D_MODEL), l)],
        out_specs=row(D_MODEL),
        out_shape=jax.ShapeDtypeStruct((n, D_MODEL), F32),
        compiler_params=_cparams(1),
    )(x2d, oa, y, u, oc, gates, d, wglu, wb, wout, g_ffn, wup, wdn)


def _rope_tables(pos):
    half = HEAD_DIM // 2
    inv = ROPE_THETA ** (-jnp.arange(half, dtype=F32) / half)
    ang = pos[:, None] * inv[None, :]
    cos, sin = jnp.cos(ang), jnp.sin(ang)
    zero = jnp.zeros_like(sin)
    two = lambda a: jnp.concatenate([a, a], axis=1)
    return (two(jnp.concatenate([cos, cos], axis=1)), two(jnp.concatenate([-sin, zero], axis=1)),
            two(jnp.concatenate([zero, sin], axis=1)))


def _state_from_blocks(h):
    nb, steps, _, _ = h.shape
    h = h.reshape(nb, steps, SSM_SEQS, STATE_W // LANES, LANES)
    return jnp.transpose(h, (1, 2, 0, 3, 4)).reshape(steps * SSM_SEQS, SSM_GROUPS, SSM_STATE)


def kernel(x_prompt, x_sample, cache_swa_k, cache_swa_v, state_ssm_re, state_ssm_im, cache_mem_k,
           cache_mem_v, mem_prompt, attn_norm, w_in, q_norm, k_norm, attn_sinks, ssm_a_re, ssm_a_im,
           ssm_log_dt, ssm_b_re, ssm_b_im, ssm_c_re, ssm_c_im, ssm_d, ssm_w_glu, mem_norm, w_mem_kv,
           mem_q_norm, mem_k_norm, w_branch, w_out, ffn_norm, w_ffn_up, w_ffn_down):
    depth = w_in.shape[0]
    b, s, _ = x_prompt.shape
    db, t, _ = x_sample.shape
    assert db * t == ROW_TILE and s % ROW_TILE == 0

    rope = _rope_tables(jnp.concatenate([jnp.arange(s, dtype=F32),
                                         jnp.tile(PAST_LEN + jnp.arange(t, dtype=F32), db)]))
    ssm_params = (ssm_a_re, ssm_a_im, ssm_log_dt, ssm_b_re, ssm_b_im, ssm_c_re, ssm_c_im)
    tables, decay = _ssm_tables(*ssm_params, PROMPT_CHUNK)
    tab_p, tab_s = tables + decay(PROMPT_CHUNK), tables + decay(t)
    blk = jnp.arange(256) // HEAD_DIM
    ones = (blk[:, None] == blk[None, :]).astype(BF16)

    w_in_bf, w_mem_bf = w_in.astype(BF16), w_mem_kv.astype(BF16)
    wglu, wb, wout = ssm_w_glu.astype(BF16), w_branch.astype(BF16), w_out.astype(BF16)
    wup, wdn = w_ffn_up.astype(BF16), w_ffn_down.astype(BF16)
    qg, kg = jnp.tile(q_norm, (1, N_HEADS)), jnp.tile(k_norm, (1, N_KV_HEADS))
    ckt = jnp.transpose(cache_swa_k, (0, 1, 3, 4, 2))
    cvt = jnp.transpose(cache_swa_v, (0, 1, 3, 4, 2))
    cmk = cache_mem_k.reshape(depth, db, N_MEM * MEM_HEADS, MEM_HEAD_DIM)
    cmv = cache_mem_v.reshape(depth, db, N_MEM * MEM_HEADS, MEM_HEAD_DIM)
    h0r = state_ssm_re.reshape(depth, db, SSM_GROUPS * SSM_STATE)
    h0i = state_ssm_im.reshape(depth, db, SSM_GROUPS * SSM_STATE)

    yp = x_prompt.reshape(b * s, D_MODEL)
    ys = x_sample.reshape(db * t, D_MODEL)
    mem2d = mem_prompt.reshape(b * N_MEM, D_MODEL)
    outs = [[] for _ in range(6)]
    n = b * s
    mk, mv = _memkv(mem2d, mem_norm, w_mem_bf, mem_k_norm)
    mk4, mv4 = mk.reshape(depth, b, N_MEM, BRANCH_W), mv.reshape(depth, b, N_MEM, BRANCH_W)
    new_k, new_v = [], []
    for l in range(depth):
        k, v, u, gates, oa, oc, qs, mqs = _inproj_attn(l, attn_sinks, yp, ys, s, attn_norm, w_in_bf, rope, qg, kg,
                                                       mem_q_norm, ones, mk4, mv4)
        ysm, hre, him = _ssm_prompt(l, u, b, s, *tab_p)
        yp = _merge_ffn(l, yp, oa, ysm, u, oc, gates, 0, ssm_d, wglu, wb, wout, ffn_norm, wup, wdn)

        stack_now = l == depth - 1 and depth > 1
        oas, ocs, nkt, nvt = _attn_sample(l, t, attn_sinks, qs, k, v, n, ckt, cvt, mqs, cmk, cmv,
                                          new_k if stack_now else [], new_v if stack_now else [])
        new_k.append(nkt)
        new_v.append(nvt)
        yss, hrs, his = _ssm_sample(l, u, n, db * t, t, h0r, h0i, *tab_s)
        ys = _merge_ffn(l, ys, oas.astype(BF16), yss, u, ocs.astype(BF16), gates, n, ssm_d, wglu, wb, wout,
                        ffn_norm, wup, wdn)

        last_rows = lambda a: jnp.stack([a[(bi + 1) * s - WINDOW:(bi + 1) * s] for bi in range(b)]).reshape(
            b, WINDOW, N_KV_HEADS, HEAD_DIM)
        new = (last_rows(k), last_rows(v), _state_from_blocks(hre), _state_from_blocks(him),
               hrs.reshape(db, SSM_GROUPS, SSM_STATE), his.reshape(db, SSM_GROUPS, SSM_STATE))
        for lst, a in zip(outs, new):
            lst.append(a)

    swa_k_p, swa_v_p, ssm_re_p, ssm_im_p, ssm_re_s, ssm_im_s = (jnp.stack(o) for o in outs)
    mem_shape = (depth, b, N_MEM, MEM_HEADS, MEM_HEAD_DIM)
    stacked = (new_k[-1], new_v[-1]) if depth > 1 else (new_k[0][None], new_v[0][None])
    swa_k_s, swa_v_s = (jnp.transpose(c, (0, 1, 4, 2, 3)) for c in stacked)
    return (yp.reshape(b, s, D_MODEL), ys.reshape(db, t, D_MODEL), swa_k_p, swa_v_p, ssm_re_p, ssm_im_p,
            mk.reshape(mem_shape), mv.reshape(mem_shape), swa_k_s, swa_v_s, ssm_re_s, ssm_im_s)
```

```python
import functools
import math

import jax
import jax.numpy as jnp
from jax import lax
from jax.experimental import pallas as pl
from jax.experimental.pallas import tpu as pltpu

F32 = jnp.float32
BF16 = jnp.bfloat16

D_MODEL = 1024
BRANCH_W = 512
HEAD_DIM = 64
N_HEADS = 8
N_KV_HEADS = 2
KV_GROUP = 4
KV_W = N_KV_HEADS * HEAD_DIM
WINDOW = 128
ROPE_THETA = 10000.0
PAST_LEN = 16384
SSM_GROUPS = 32
SSM_GROUP_CH = 16
SSM_STATE = 64
N_MEM = 256
MEM_HEADS = 4
MEM_HEAD_DIM = 128
D_FF = 2816
RMS_EPS = 1e-6

Q_OFF, K_OFF, V_OFF, U_OFF, MQ_OFF, G_OFF = 0, 512, 640, 768, 1280, 1792
IN_W = G_OFF + 3 * D_MODEL

LANES = 128
SUBLANES = 8
MXU_TILE = 256
GROUPS_PER_LANE_BLOCK = LANES // SSM_GROUP_CH
N_LANE_BLOCKS = BRANCH_W // LANES
STATE_W = GROUPS_PER_LANE_BLOCK * SSM_STATE
PROMPT_CHUNK = 8
ROW_TILE = 512
VMEM_LIMIT = 56 * 1024 * 1024


def _dot(a, b):
    return jnp.dot(a, b, preferred_element_type=F32)


def _dot_nt(a, b):
    return lax.dot_general(a, b, (((1,), (1,)), ((), ())), preferred_element_type=F32)


def _cparams(n_axes):
    return pltpu.CompilerParams(dimension_semantics=("arbitrary",) * n_axes,
                                vmem_limit_bytes=VMEM_LIMIT)


def _full(shape):
    nd = len(shape)
    return pl.BlockSpec(shape, lambda *_: (0,) * nd)


def _layer(shape, l):
    nd = len(shape)
    return pl.BlockSpec((None,) + tuple(shape), lambda *_: (l,) + (0,) * nd, pipeline_mode=pl.Buffered(1))


def _sigmoid(x):
    return 0.5 * jnp.tanh(0.5 * x) + 0.5


def _rms(x):
    return x * lax.rsqrt(jnp.mean(x * x, axis=-1, keepdims=True) + RMS_EPS)


def _memkv_kernel(mem_ref, g_ref, w_ref, kg_ref, k_ref, v_ref):
    l = pl.program_id(0)
    h = _rms(mem_ref[...]) * g_ref[pl.ds(l, 1), :]
    kv = _dot(h.astype(BF16), w_ref[...])
    kg = kg_ref[pl.ds(l, 1), :]
    for hd in range(MEM_HEADS):
        sl = slice(hd * MEM_HEAD_DIM, (hd + 1) * MEM_HEAD_DIM)
        k_ref[:, sl] = _rms(kv[:, sl]) * kg
    v_ref[...] = kv[:, BRANCH_W:]


def _memkv(mem2d, g, w_bf, kg):
    n = mem2d.shape[0]
    depth = w_bf.shape[0]
    out = pl.BlockSpec((None, ROW_TILE, BRANCH_W), lambda l, i: (l, i, 0))
    return pl.pallas_call(
        _memkv_kernel, name="memkv",
        grid=(depth, n // ROW_TILE),
        in_specs=[pl.BlockSpec((ROW_TILE, D_MODEL), lambda l, i: (i, 0)), _full(g.shape),
                  pl.BlockSpec((None, D_MODEL, 2 * BRANCH_W), lambda l, i: (l, 0, 0)), _full(kg.shape)],
        out_specs=[out, out],
        out_shape=[jax.ShapeDtypeStruct((depth, n, BRANCH_W), F32)] * 2,
        compiler_params=_cparams(2),
    )(mem2d, g, w_bf, kg)


def _rotary(x, cos, sin_lo, sin_hi):
    w = x.shape[-1]
    half = HEAD_DIM // 2
    return x * cos + pltpu.roll(x, w - half, 1) * sin_lo + pltpu.roll(x, half, 1) * sin_hi


GATE_CHUNK = 512
N_GATE_CHUNKS = 3 * D_MODEL // GATE_CHUNK


class _Projection:
    def __init__(self, x, g, w_ref, rope, qg, kg, mqg, ones):
        self.hb = (_rms(x) * g).astype(BF16)
        self.w_ref, self.qg, self.kg, self.mqg = w_ref, qg, kg, mqg
        self.ones = ones
        self.rope = rope

    def queries(self):
        zq = _dot(self.hb, self.w_ref[:, Q_OFF:K_OFF])
        sq = (zq * zq).astype(BF16)
        ssq = jnp.concatenate([_dot(sq[:, :256], self.ones), _dot(sq[:, 256:], self.ones)], axis=1)
        qn = zq * lax.rsqrt(ssq * (1.0 / HEAD_DIM) + RMS_EPS) * self.qg
        rope4 = (jnp.concatenate([t] * 4, axis=1) for t in self.rope)
        return (_rotary(qn, *rope4) * (1.0 / math.sqrt(HEAD_DIM))).astype(BF16)

    def keys_values(self):
        zkv = _dot(self.hb, self.w_ref[:, K_OFF:U_OFF])
        zk = zkv[:, :KV_W]
        ssk = _dot((zk * zk).astype(BF16), self.ones[:KV_W, :KV_W])
        kn = zk * lax.rsqrt(ssk * (1.0 / HEAD_DIM) + RMS_EPS) * self.kg
        return _rotary(kn, *self.rope), zkv[:, KV_W:]

    def ssm_input(self):
        return _dot(self.hb, self.w_ref[:, U_OFF:MQ_OFF])

    def memory_queries(self):
        zm = _dot(self.hb, self.w_ref[:, MQ_OFF:G_OFF])
        mqg = self.mqg * (1.0 / math.sqrt(MEM_HEAD_DIM))
        return jnp.concatenate([_rms(zm[:, hd * MEM_HEAD_DIM:(hd + 1) * MEM_HEAD_DIM]) * mqg
                                for hd in range(MEM_HEADS)], axis=1).astype(BF16)

    def gates(self, c):
        lo = G_OFF + c * GATE_CHUNK
        return _sigmoid(_dot(self.hb, self.w_ref[:, lo:lo + GATE_CHUNK])).astype(BF16)


ATTN_TILE = 512


def _softmax_rows(s):
    m = jnp.max(s, axis=-1, keepdims=True)
    p = jnp.exp(s - m)
    return p, jnp.sum(p, axis=-1, keepdims=True)


def _sink_column(sink_ref, l, heads, rows_per_head):
    n = len(heads) * rows_per_head
    rcol = lax.broadcasted_iota(jnp.int32, (n, 1), 0)
    sink = jnp.full((n, 1), sink_ref[l, heads[-1]], F32)
    for g in range(len(heads) - 2, -1, -1):
        sink = jnp.where(rcol < (g + 1) * rows_per_head, sink_ref[l, heads[g]], sink)
    return sink


def _sink_softmax(s, sink):
    m = jnp.maximum(jnp.max(s, axis=-1, keepdims=True), sink)
    p = jnp.exp(s - m)
    return p, jnp.sum(p, axis=-1, keepdims=True) + jnp.exp(sink - m)


def _attend_sub_block(sub, mask, sinks, q_ref, mq_ref, k_ref, v_ref, mk_ref, mv_ref, oa_ref, oc_ref):
    qrows = slice(sub * WINDOW, (sub + 1) * WINDOW)
    krows = slice(sub * WINDOW, (sub + 2) * WINDOW)
    ksl = [slice(kvh * HEAD_DIM, (kvh + 1) * HEAD_DIM) for kvh in range(N_KV_HEADS)]
    assert ATTN_TILE // WINDOW == MEM_HEADS
    msl = [slice(sub * MEM_HEAD_DIM, (sub + 1) * MEM_HEAD_DIM)]
    mrows = slice(0, ATTN_TILE)
    heads = [[kvh * KV_GROUP + g for g in range(KV_GROUP)] for kvh in range(N_KV_HEADS)]
    scores = []
    for kvh in range(N_KV_HEADS):
        qg = jnp.concatenate([q_ref[qrows, hd * HEAD_DIM:(hd + 1) * HEAD_DIM] for hd in heads[kvh]], axis=0)
        scores.append(jnp.where(mask, _dot_nt(qg, k_ref[krows, ksl[kvh]]), -jnp.inf))
    mscores = [_dot_nt(mq_ref[mrows, sl], mk_ref[:, sl]) for sl in msl]
    yield
    probs = [_sink_softmax(scores[kvh], sinks[kvh]) for kvh in range(N_KV_HEADS)]
    mprobs = [_softmax_rows(s) for s in mscores]
    yield
    outs = [_dot(probs[kvh][0].astype(BF16), v_ref[krows, ksl[kvh]]) / probs[kvh][1]
            for kvh in range(N_KV_HEADS)]
    mouts = [_dot(mprobs[hd][0].astype(BF16), mv_ref[:, msl[hd]]) / mprobs[hd][1] for hd in range(len(msl))]
    for kvh in range(N_KV_HEADS):
        for g, hd in enumerate(heads[kvh]):
            oa_ref[qrows, hd * HEAD_DIM:(hd + 1) * HEAD_DIM] = outs[kvh][g * WINDOW:(g + 1) * WINDOW].astype(BF16)
    for hd in range(len(msl)):
        oc_ref[mrows, msl[hd]] = mouts[hd].astype(BF16)


def _inproj_attn_kernel(l, tiles_per_seq, sink_ref, x_ref, xs_ref, g_ref, w_ref, cos_ref, slo_ref, shi_ref,
                        qg_ref, kg_ref, mqg_ref, ones_ref, mk_ref, mv_ref,
                        k_ref, v_ref, u_ref, gates_ref, oa_ref, oc_ref, qs_ref, mqs_ref,
                        q_scr, mq_scr, k_scr, v_scr, mk_scr, mv_scr):
    i = pl.program_id(0)
    cur, prv = i % 2, (i + 1) % 2
    decode_step = i == pl.num_programs(0) - 1

    @pl.when(i == 0)
    def _():
        for scr in (q_scr, mq_scr, k_scr, v_scr, mk_scr, mv_scr):
            scr[...] = jnp.zeros(scr.shape, scr.dtype)

    @pl.when(i % tiles_per_seq == 1)
    def _():
        mk_scr[...] = mk_ref[...].astype(BF16)
        mv_scr[...] = mv_ref[...].astype(BF16)

    rows = KV_GROUP * WINDOW
    r = lax.broadcasted_iota(jnp.int32, (rows, 2 * WINDOW), 0)
    j = lax.broadcasted_iota(jnp.int32, (rows, 2 * WINDOW), 1)
    qi = r % WINDOW
    band = (j > qi) & (j <= qi + WINDOW)
    first_band = band & ((j >= WINDOW) | (i % tiles_per_seq != 1))
    sinks = [_sink_column(sink_ref, l, [kvh * KV_GROUP + g for g in range(KV_GROUP)], WINDOW)
             for kvh in range(N_KV_HEADS)]

    def attention_stages():
        for sub in range(ATTN_TILE // WINDOW):
            yield from _attend_sub_block(sub, first_band if sub == 0 else band, sinks, q_scr.at[prv],
                                         mq_scr.at[prv], k_scr.at[prv], v_scr.at[prv], mk_scr, mv_scr,
                                         oa_ref, oc_ref)
            yield

    proj = _Projection(jnp.where(decode_step, xs_ref[...], x_ref[...]), g_ref[l:l + 1, :], w_ref,
                       (cos_ref[...], slo_ref[...], shi_ref[...]), qg_ref[l:l + 1, :], kg_ref[l:l + 1, :],
                       mqg_ref[l:l + 1, :], ones_ref[...])

    def store_queries():
        q = proj.queries()
        q_scr[cur] = q
        qs_ref[...] = q

    def store_keys_values():
        k, v = proj.keys_values()
        k_ref[...], v_ref[...] = k, v
        k_scr[cur, :WINDOW, :] = k_scr[prv, ATTN_TILE:, :]
        v_scr[cur, :WINDOW, :] = v_scr[prv, ATTN_TILE:, :]
        k_scr[cur, WINDOW:, :] = k.astype(BF16)
        v_scr[cur, WINDOW:, :] = v.astype(BF16)

    def store_ssm_input():
        u_ref[...] = proj.ssm_input()

    def store_memory_queries():
        mq = proj.memory_queries()
        mq_scr[cur] = mq
        mqs_ref[...] = mq

    def store_gates(c):
        gates_ref[:, c * GATE_CHUNK:(c + 1) * GATE_CHUNK] = proj.gates(c)

    segments = [store_queries, store_keys_values, store_ssm_input, store_memory_queries]
    segments += [functools.partial(store_gates, c) for c in range(N_GATE_CHUNKS)]
    stages = attention_stages()
    n_stages = 3 * (ATTN_TILE // WINDOW)
    for n, segment in enumerate(segments):
        for _ in range((n + 1) * n_stages // len(segments) - n * n_stages // len(segments)):
            next(stages)
        segment()


def _inproj_attn(l, sinks, x2d, xs2d, seq_len, g, w_bf, rope, qg, kg, mqg, ones, mk, mv):
    n = x2d.shape[0]
    n_tiles = n // ROW_TILE
    tiles_per_seq = seq_len // ROW_TILE
    assert ROW_TILE == ATTN_TILE and tiles_per_seq > 1 and xs2d.shape[0] == ROW_TILE
    attn_tile = lambda i: jnp.maximum(i - 1, 0)
    xrow = pl.BlockSpec((ROW_TILE, D_MODEL), lambda i: (jnp.minimum(i, n_tiles - 1), 0))
    orow = lambda w: pl.BlockSpec((ROW_TILE, w), lambda i: (i, 0))
    arow = pl.BlockSpec((ROW_TILE, BRANCH_W), lambda i: (attn_tile(i), 0))
    rspec = pl.BlockSpec((ROW_TILE, LANES), lambda i: (jnp.where(i == n_tiles, tiles_per_seq, i % tiles_per_seq), 0))
    mem = pl.BlockSpec((None, None, N_MEM, BRANCH_W), lambda i: (l, attn_tile(i) // tiles_per_seq, 0, 0))
    widths = (KV_W, KV_W, BRANCH_W, 3 * D_MODEL)
    dtypes = (F32, F32, F32, BF16)
    return pl.pallas_call(
        functools.partial(_inproj_attn_kernel, l, tiles_per_seq), name="inproj_attn",
        grid=(n_tiles + 1,),
        in_specs=[pl.BlockSpec(memory_space=pltpu.SMEM), xrow, _full((ROW_TILE, D_MODEL)), _full(g.shape),
                  _layer((D_MODEL, IN_W), l), rspec, rspec, rspec, _full(qg.shape), _full(kg.shape),
                  _full(mqg.shape), _full((256, 256)), mem, mem],
        out_specs=[orow(w) for w in widths] + [arow, arow] + [_full((ROW_TILE, BRANCH_W))] * 2,
        out_shape=[jax.ShapeDtypeStruct((n + ROW_TILE, w), d) for w, d in zip(widths, dtypes)]
                  + [jax.ShapeDtypeStruct((n, BRANCH_W), BF16)] * 2
                  + [jax.ShapeDtypeStruct((ROW_TILE, BRANCH_W), BF16)] * 2,
        scratch_shapes=[pltpu.VMEM((2, ROW_TILE, BRANCH_W), BF16)] * 2
                       + [pltpu.VMEM((2, WINDOW + ROW_TILE, KV_W), BF16)] * 2
                       + [pltpu.VMEM((N_MEM, BRANCH_W), BF16)] * 2,
        compiler_params=_cparams(1),
    )(sinks, x2d, xs2d, g, w_bf, *rope, qg, kg, mqg, ones, mk, mv)


DEC_BLOCK = 8
TOK_PAD = 8


def _pad_rows(x, n):
    return jnp.concatenate([x, jnp.zeros((n - x.shape[0], x.shape[1]), x.dtype)], axis=0)


def _attn_sample_kernel(l, t, sink_ref, q_ref, kn_ref, vn_ref, ckt_ref, cvt_ref, qm_ref, mk_ref, mv_ref,
                        *earlier_and_outputs):
    oa_ref, oc_ref, nkt_ref, nvt_ref = earlier_and_outputs[-4:]
    earlier = earlier_and_outputs[:-4]
    if earlier:
        for j in range(l):
            nkt_ref[j] = earlier[j][...]
            nvt_ref[j] = earlier[l + j][...]
        nkt_ref, nvt_ref = nkt_ref.at[l], nvt_ref.at[l]
    keys = 2 * WINDOW
    rows = N_HEADS * TOK_PAD
    rk = lax.broadcasted_iota(jnp.int32, (rows, keys), 0)
    jk = lax.broadcasted_iota(jnp.int32, (rows, keys), 1)
    tk = rk % TOK_PAD
    mask = (tk < t) & (((jk < WINDOW) & (jk > tk)) | ((jk >= WINDOW) & (jk - WINDOW <= tk)))
    sink = _sink_column(sink_ref, l, list(range(N_HEADS)), TOK_PAD)
    mem_rows = N_MEM * MEM_HEADS
    mem_own = (lax.broadcasted_iota(jnp.int32, (MEM_HEADS * TOK_PAD, mem_rows), 1) % MEM_HEADS
               == lax.broadcasted_iota(jnp.int32, (MEM_HEADS * TOK_PAD, mem_rows), 0) // TOK_PAD)
    is_new = lax.broadcasted_iota(jnp.int32, (HEAD_DIM, WINDOW), 1) >= WINDOW - t
    dsl = [slice(kvh * HEAD_DIM, (kvh + 1) * HEAD_DIM) for kvh in range(N_KV_HEADS)]
    group_rows = KV_GROUP * TOK_PAD

    q_all, qm_all = q_ref[...].astype(F32), qm_ref[...].astype(F32)
    seqs = range(DEC_BLOCK)
    toks = [slice(bi * t, (bi + 1) * t) for bi in seqs]
    knp = [_pad_rows(kn_ref[toks[bi], :], WINDOW) for bi in seqs]
    vnp = [_pad_rows(vn_ref[toks[bi], :], WINDOW) for bi in seqs]
    knt, vnt = [x.T for x in knp], [x.T for x in vnp]
    for bi in seqs:
        for kvh in range(N_KV_HEADS):
            nkt_ref[bi, kvh] = jnp.where(is_new, pltpu.roll(knt[bi][dsl[kvh], :], WINDOW - t, 1),
                                         pltpu.roll(ckt_ref[bi, kvh], WINDOW - t, 1))
            nvt_ref[bi, kvh] = jnp.where(is_new, pltpu.roll(vnt[bi][dsl[kvh], :], WINDOW - t, 1),
                                         pltpu.roll(cvt_ref[bi, kvh], WINDOW - t, 1))
    qg = []
    for bi in seqs:
        q8 = _pad_rows(q_all[toks[bi]], TOK_PAD)
        qg.append([jnp.concatenate([q8[:, hd * HEAD_DIM:(hd + 1) * HEAD_DIM]
                                    for hd in range(kvh * KV_GROUP, (kvh + 1) * KV_GROUP)], axis=0).astype(BF16)
                   for kvh in range(N_KV_HEADS)])
    scores = []
    for bi in seqs:
        per_kvh = [jnp.concatenate([_dot(qg[bi][kvh], ckt_ref[bi, kvh].astype(BF16)),
                                    _dot_nt(qg[bi][kvh], knp[bi][:, dsl[kvh]].astype(BF16))], axis=1)
                   for kvh in range(N_KV_HEADS)]
        scores.append(jnp.where(mask, jnp.concatenate(per_kvh, axis=0), -jnp.inf))
    mscores = []
    for bi in seqs:
        qm8 = jnp.concatenate([qm_all[toks[bi]], jnp.zeros((TOK_PAD - t, BRANCH_W), F32)], axis=0).astype(BF16)
        qst = jnp.concatenate([qm8[:, hd * MEM_HEAD_DIM:(hd + 1) * MEM_HEAD_DIM] for hd in range(MEM_HEADS)],
                              axis=0)
        mscores.append(jnp.where(mem_own, _dot_nt(qst, mk_ref[bi].astype(BF16)), -jnp.inf))
    probs = [_sink_softmax(scores[bi], sink) for bi in seqs]
    mprobs = [_softmax_rows(mscores[bi]) for bi in seqs]
    outs = []
    for bi in seqs:
        p = probs[bi][0].astype(BF16)
        per_kvh = []
        for kvh in range(N_KV_HEADS):
            pk = p[kvh * group_rows:(kvh + 1) * group_rows]
            per_kvh.append(_dot_nt(pk[:, :WINDOW], cvt_ref[bi, kvh].astype(BF16))
                           + _dot(pk[:, WINDOW:], vnp[bi][:, dsl[kvh]].astype(BF16)))
        outs.append(jnp.concatenate(per_kvh, axis=0) / probs[bi][1])
    mouts = [_dot(mprobs[bi][0].astype(BF16), mv_ref[bi].astype(BF16)) / mprobs[bi][1] for bi in seqs]
    for bi in seqs:
        oa_ref[toks[bi], :] = jnp.concatenate(
            [outs[bi][hd * TOK_PAD:(hd + 1) * TOK_PAD] for hd in range(N_HEADS)], axis=1)[0:t]
        oc_ref[toks[bi], :] = jnp.concatenate(
            [mouts[bi][hd * TOK_PAD:(hd + 1) * TOK_PAD] for hd in range(MEM_HEADS)], axis=1)[0:t]


def _attn_sample(l, t, sinks, q, kn, vn, kv_row0, ckt, cvt, qm, mk, mv, earlier_k, earlier_v):
    n = q.shape[0]
    tok = lambda w, row0=0: pl.BlockSpec((DEC_BLOCK * t, w), lambda i: (i + row0 // (DEC_BLOCK * t), 0))
    cache_tile = (DEC_BLOCK, N_KV_HEADS, HEAD_DIM, WINDOW)
    cache_in = pl.BlockSpec((None,) + cache_tile, lambda i: (l, i, 0, 0, 0))
    one_layer = pl.BlockSpec(cache_tile, lambda i: (i, 0, 0, 0))
    mem = pl.BlockSpec((None, DEC_BLOCK, N_MEM * MEM_HEADS, MEM_HEAD_DIM), lambda i: (l, i, 0, 0))
    assert len(earlier_k) == len(earlier_v) and len(earlier_k) in (0, l)
    if earlier_k:
        cache_out = pl.BlockSpec((l + 1,) + cache_tile, lambda i: (0, i, 0, 0, 0))
        cache_shape = jax.ShapeDtypeStruct((l + 1,) + ckt.shape[1:], F32)
    else:
        cache_out, cache_shape = one_layer, jax.ShapeDtypeStruct(ckt.shape[1:], F32)
    return pl.pallas_call(
        functools.partial(_attn_sample_kernel, l, t), name="attn_sample",
        grid=(n // (DEC_BLOCK * t),),
        in_specs=[pl.BlockSpec(memory_space=pltpu.SMEM), tok(BRANCH_W), tok(KV_W, kv_row0), tok(KV_W, kv_row0),
                  cache_in, cache_in, tok(BRANCH_W), mem, mem] + [one_layer] * (2 * len(earlier_k)),
        out_specs=[tok(BRANCH_W), tok(BRANCH_W), cache_out, cache_out],
        out_shape=[jax.ShapeDtypeStruct((n, BRANCH_W), F32)] * 2 + [cache_shape] * 2,
        compiler_params=_cparams(1),
    )(sinks, q, kn, vn, ckt, cvt, qm, mk, mv, *earlier_k, *earlier_v)


def _ssm_tables(a_re, a_im, log_dt, b_re, b_im, c_re, c_im, chunk):
    hp = lax.Precision.HIGHEST
    dt = jnp.exp(log_dt)[..., None]
    mag = jnp.exp(a_re * dt)
    lam_re, lam_im = mag * jnp.cos(a_im * dt), mag * jnp.sin(a_im * dt)
    den = a_re * a_re + a_im * a_im
    nr, ni = lam_re - 1.0, lam_im
    g_re = (nr * a_re + ni * a_im) / den
    g_im = (ni * a_re - nr * a_im) / den
    bg_re = g_re[..., None] * b_re - g_im[..., None] * b_im
    bg_im = g_re[..., None] * b_im + g_im[..., None] * b_re
    def powers(d):
        d = d.astype(F32)[:, None, None, None]
        pm = jnp.exp(d * (a_re * dt)[None])
        return pm * jnp.cos(d * (a_im * dt)[None]), pm * jnp.sin(d * (a_im * dt)[None])

    def times_gb(p_re, p_im):
        return (p_re[..., None] * bg_re[None] - p_im[..., None] * bg_im[None],
                p_re[..., None] * bg_im[None] + p_im[..., None] * bg_re[None])

    pw_re, pw_im = powers(jnp.arange(chunk + 1))
    z_re, z_im = times_gb(pw_re[:chunk], pw_im[:chunk])
    zrev_re, zrev_im = times_gb(*powers(jnp.arange(chunk - 1, -1, -1)))
    kern = (jnp.einsum('lgop,dlgpi->dlgio', c_re, z_re, precision=hp)
            - jnp.einsum('lgop,dlgpi->dlgio', c_im, z_im, precision=hp))
    nl = a_re.shape[0]
    nb, gb = N_LANE_BLOCKS, GROUPS_PER_LANE_BLOCK
    kd = jnp.transpose(kern.reshape(chunk, nl, nb, gb, SSM_GROUP_CH, SSM_GROUP_CH), (1, 2, 0, 3, 4, 5))
    kd = kd.reshape(nl, nb, chunk, LANES, SSM_GROUP_CH)

    def to_rows(z):
        z = z.reshape(chunk, nl, nb, gb, SSM_STATE, SSM_GROUP_CH)
        return jnp.transpose(z, (1, 2, 0, 3, 5, 4)).reshape(nl, nb, chunk, LANES, SSM_STATE)

    zr, zi = to_rows(zrev_re), to_rows(zrev_im)
    zc = jnp.concatenate([zr, zr, zi, zi], axis=-1)

    pr, pi = pw_re[1:], pw_im[1:]
    f_a = c_re[None] * pr[:, :, :, None, :] - c_im[None] * pi[:, :, :, None, :]
    f_b = c_re[None] * pi[:, :, :, None, :] + c_im[None] * pr[:, :, :, None, :]

    def from_cols(f):
        f = f.reshape(chunk, nl, nb, gb, SSM_GROUP_CH, SSM_STATE)
        return jnp.transpose(f, (1, 2, 0, 4, 3, 5)).reshape(nl, nb, chunk, SSM_GROUP_CH, STATE_W)

    ft = jnp.concatenate([from_cols(f_a), from_cols(-f_b)], axis=-1)

    kw = chunk * LANES
    spec = lambda *shape: pl.BlockSpec((None, None) + shape, lambda l, n: (l, n) + (0,) * len(shape))
    intra, to_st, from_st_t = pl.pallas_call(
        _ssm_table_kernel, name="ssm_tables",
        grid=(nl, nb),
        in_specs=[spec(chunk, LANES, SSM_GROUP_CH), spec(chunk, LANES, 2 * LANES),
                  spec(chunk, SSM_GROUP_CH, 2 * STATE_W)],
        out_specs=[spec(kw, kw), spec(kw, 2 * STATE_W), spec(kw, 2 * STATE_W)],
        out_shape=[jax.ShapeDtypeStruct((nl, nb, kw, kw), BF16),
                   jax.ShapeDtypeStruct((nl, nb, kw, 2 * STATE_W), BF16),
                   jax.ShapeDtypeStruct((nl, nb, kw, 2 * STATE_W), BF16)],
        compiler_params=_cparams(2),
    )(kd, zc, ft)
    decay = lambda d: (pw_re[d].reshape(nl, nb, 1, STATE_W), pw_im[d].reshape(nl, nb, 1, STATE_W))
    return (intra, to_st, from_st_t), decay


def _ssm_table_kernel(kd_ref, zc_ref, ft_ref, intra_ref, to_ref, fromt_ref):
    chunk = kd_ref.shape[0]
    gch = SSM_GROUP_CH
    r = lax.broadcasted_iota(jnp.int32, (LANES, LANES), 0)
    c = lax.broadcasted_iota(jnp.int32, (LANES, LANES), 1)
    same_group = (r // gch) == (c // gch)
    spread = (lax.broadcasted_iota(jnp.int32, (gch, LANES), 1) % gch
              == lax.broadcasted_iota(jnp.int32, (gch, LANES), 0)).astype(BF16)
    spread_t = (lax.broadcasted_iota(jnp.int32, (LANES, gch), 0) % gch
                == lax.broadcasted_iota(jnp.int32, (LANES, gch), 1)).astype(BF16)
    rs = lax.broadcasted_iota(jnp.int32, (LANES, 2 * STATE_W), 0)
    cs = lax.broadcasted_iota(jnp.int32, (LANES, 2 * STATE_W), 1)
    own_state = (rs // gch) == ((cs % STATE_W) // SSM_STATE)

    lag_blocks = [jnp.where(same_group, _dot(kd_ref[d].astype(BF16), spread), 0.0).astype(BF16)
                  for d in range(chunk)]
    zero = jnp.zeros((LANES, LANES), BF16)
    for s in range(chunk):
        for t in range(chunk):
            intra_ref[s * LANES:(s + 1) * LANES, t * LANES:(t + 1) * LANES] = (
                lag_blocks[t - s] if t >= s else zero)
        z = zc_ref[s]
        full = jnp.concatenate([z[:, :LANES]] * (STATE_W // LANES) + [z[:, LANES:]] * (STATE_W // LANES), axis=1)
        to_ref[s * LANES:(s + 1) * LANES, :] = jnp.where(own_state, full, 0.0).astype(BF16)
        f = _dot(spread_t, ft_ref[s].astype(BF16))
        fromt_ref[s * LANES:(s + 1) * LANES, :] = jnp.where(own_state, f, 0.0).astype(BF16)


SSM_SEQS = 2


def _chunk_rows(ref, chunk):
    n = ref.shape[0] // chunk
    return jnp.concatenate([ref[pl.ds(s, n, stride=chunk), :] for s in range(chunk)], axis=1)


def _store_chunk_rows(ref, y, chunk):
    n = ref.shape[0] // chunk
    for t in range(chunk):
        ref[pl.ds(t, n, stride=chunk), :] = y[:, t * LANES:(t + 1) * LANES]


def _ssm_prompt_kernel(u_ref, intra_ref, to_ref, fromt_ref, dre_ref, dim_ref,
                       y_ref, hre_ref, him_ref, st_scr):
    chunk = PROMPT_CHUNK
    nchunks = u_ref.shape[0] // (chunk * SSM_SEQS)
    x = _chunk_rows(u_ref, chunk).astype(BF16)
    kw = chunk * LANES
    y = jnp.concatenate([_dot(x[:, :c + MXU_TILE], intra_ref[:c + MXU_TILE, c:c + MXU_TILE])
                         for c in range(0, kw, MXU_TILE)], axis=1)
    add_all = _dot(x, to_ref[...])
    nblk = STATE_W // LANES
    assert SSM_SEQS * nblk == SUBLANES
    slots = [(seq, cb) for seq in range(SSM_SEQS) for cb in range(nblk)]

    def slot_rows(seq, cb):
        return pl.ds(seq * nblk + cb, nchunks, stride=SUBLANES)

    for c in range(2):
        for seq, cb in slots:
            col = (c * nblk + cb) * LANES
            st_scr[c, slot_rows(seq, cb), :] = add_all[seq * nchunks:(seq + 1) * nchunks, col:col + LANES]
    dre = jnp.concatenate([dre_ref[:, cb * LANES:(cb + 1) * LANES] for _, cb in slots], axis=0)
    dim = jnp.concatenate([dim_ref[:, cb * LANES:(cb + 1) * LANES] for _, cb in slots], axis=0)

    def cmul(ar, ai, br, bi):
        return ar * br - ai * bi, ar * bi + ai * br

    d2r, d2i = cmul(dre, dim, dre, dim)

    def step(p, carry):
        hr, hi = carry
        first = pl.ds(pl.multiple_of(2 * p * SUBLANES, SUBLANES), SUBLANES)
        second = pl.ds(pl.multiple_of((2 * p + 1) * SUBLANES, SUBLANES), SUBLANES)
        s0r, s0i = st_scr[0, first, :], st_scr[1, first, :]
        s1r, s1i = st_scr[0, second, :], st_scr[1, second, :]
        st_scr[0, first, :] = hr
        st_scr[1, first, :] = hi
        mr, mi = cmul(dre, dim, hr, hi)
        st_scr[0, second, :] = mr + s0r
        st_scr[1, second, :] = mi + s0i
        tr, ti = cmul(dre, dim, s0r, s0i)
        nr, ni = cmul(d2r, d2i, hr, hi)
        return nr + (tr + s1r), ni + (ti + s1i)

    assert nchunks % 2 == 0
    zero = jnp.zeros((SUBLANES, LANES), F32)
    hr, hi = lax.fori_loop(0, nchunks // 2, step, (zero, zero), unroll=4)
    hre_ref[...] = hr
    him_ref[...] = hi
    h_in = jnp.concatenate(
        [jnp.concatenate([st_scr[c, slot_rows(seq, cb), :] for seq in range(SSM_SEQS)], axis=0)
         for c in range(2) for cb in range(nblk)], axis=1).astype(BF16)
    _store_chunk_rows(y_ref, y + _dot_nt(h_in, fromt_ref[...]), chunk)


def _ssm_prompt(l, u2d, n_seqs, seq_len, intra, to_st, from_st_t, dec_re, dec_im):
    n = n_seqs * seq_len
    chunk = PROMPT_CHUNK
    rows = SSM_SEQS * seq_len
    steps = n // rows
    kw = chunk * LANES
    data = pl.BlockSpec((rows, LANES), lambda j, h: (h, j))
    wspec = lambda r, c: pl.BlockSpec((None, None, r, c), lambda j, h: (l, j, 0, 0))
    st_out = pl.BlockSpec((None, None, SUBLANES, LANES), lambda j, h: (j, h, 0, 0))
    st_shape = jax.ShapeDtypeStruct((N_LANE_BLOCKS, steps, SUBLANES, LANES), F32)
    return pl.pallas_call(
        _ssm_prompt_kernel, name="ssm_prompt",
        grid=(N_LANE_BLOCKS, steps),
        in_specs=[data, wspec(kw, kw), wspec(kw, 2 * STATE_W), wspec(kw, 2 * STATE_W),
                  wspec(1, STATE_W), wspec(1, STATE_W)],
        out_specs=[data, st_out, st_out],
        out_shape=[jax.ShapeDtypeStruct((n, BRANCH_W), F32), st_shape, st_shape],
        scratch_shapes=[pltpu.VMEM((2, SUBLANES * seq_len // chunk, LANES), F32)],
        compiler_params=_cparams(2),
    )(u2d, intra, to_st, from_st_t, dec_re, dec_im)


def _ssm_sample_kernel(chunk, u_ref, h0r_ref, h0i_ref, intra_ref, to_ref, fromt_ref, dre_ref, dim_ref,
                       y_ref, hre_ref, him_ref):
    x = _chunk_rows(u_ref, chunk).astype(BF16)
    hr, hi = h0r_ref[...], h0i_ref[...]
    dre, dim = dre_ref[...], dim_ref[...]
    add = _dot(x, to_ref[...])
    hre_ref[...] = dre * hr - dim * hi + add[:, :STATE_W]
    him_ref[...] = dre * hi + dim * hr + add[:, STATE_W:]
    h0 = jnp.concatenate([hr, hi], axis=1).astype(BF16)
    _store_chunk_rows(y_ref, _dot(x, intra_ref[...]) + _dot_nt(h0, fromt_ref[...]), chunk)


def _ssm_sample(l, u2d, row0, n, t, h0_re, h0_im, intra, to_st, from_st_t, dec_re, dec_im):
    b = n // t
    kw = t * LANES
    assert PROMPT_CHUNK % t == 0 and row0 % n == 0
    last = PROMPT_CHUNK // t - 1
    data_in = pl.BlockSpec((n, LANES), lambda j: (row0 // n, j))
    data = pl.BlockSpec((n, LANES), lambda j: (0, j))
    st_in = pl.BlockSpec((None, b, STATE_W), lambda j: (l, 0, j))
    st_out = pl.BlockSpec((b, STATE_W), lambda j: (0, j))
    wspec = lambda r, c, rb=0: pl.BlockSpec((None, None, r, c), lambda j: (l, j, rb, 0))
    return pl.pallas_call(
        functools.partial(_ssm_sample_kernel, t), name="ssm_sample",
        grid=(N_LANE_BLOCKS,),
        in_specs=[data_in, st_in, st_in, wspec(kw, kw), wspec(kw, 2 * STATE_W, last), wspec(kw, 2 * STATE_W),
                  wspec(1, STATE_W), wspec(1, STATE_W)],
        out_specs=[data, st_out, st_out],
        out_shape=[jax.ShapeDtypeStruct((n, BRANCH_W), F32),
                   jax.ShapeDtypeStruct((b, N_LANE_BLOCKS * STATE_W), F32),
                   jax.ShapeDtypeStruct((b, N_LANE_BLOCKS * STATE_W), F32)],
        compiler_params=_cparams(1),
    )(u2d, h0_re, h0_im, intra, to_st, from_st_t, dec_re, dec_im)


FF_CHUNKS = ((0, 512), (512, 512), (1024, 512), (1536, 512), (2048, 512), (2560, 256))


def _merge_ffn_kernel(l, x_ref, oa_ref, y_ref, u_ref, oc_ref, gates_ref, d_ref, wglu_ref, wb_ref, wout_ref,
                      g_ref, wup_ref, wdn_ref, o_ref):
    z = jax.nn.gelu(y_ref[...] + d_ref[l:l + 1, :] * u_ref[...])
    ob = (z * _sigmoid(_dot(z.astype(BF16), wglu_ref[...]))).astype(BF16)
    gate = lambda n: gates_ref[:, n * D_MODEL:(n + 1) * D_MODEL].astype(F32)
    merged = gate(0) * _dot(oa_ref[...], wb_ref[0])
    merged = merged + gate(1) * _dot(ob, wb_ref[1])
    merged = merged + gate(2) * _dot(oc_ref[...], wb_ref[2])
    x = x_ref[...] + _dot(merged.astype(BF16), wout_ref[...])
    h = (_rms(x) * g_ref[l:l + 1, :]).astype(BF16)
    acc = x
    for lo, w in FF_CHUNKS:
        up_gate = _dot(h, wup_ref[:, lo:lo + w])
        up = _dot(h, wup_ref[:, D_FF + lo:D_FF + lo + w])
        acc = acc + _dot((up_gate * _sigmoid(up_gate) * up).astype(BF16), wdn_ref[lo:lo + w, :])
    o_ref[...] = acc


def _merge_ffn(l, x2d, oa, y, u, oc, gates, proj_row0, d, wglu, wb, wout, g_ffn, wup, wdn):
    n = x2d.shape[0]
    assert proj_row0 % ROW_TILE == 0
    row = lambda w, row0=0: pl.BlockSpec((ROW_TILE, w), lambda i: (i + row0 // ROW_TILE, 0))
    return pl.pallas_call(
        functools.partial(_merge_ffn_kernel, l), name="merge_ffn",
        grid=(n // ROW_TILE,),
        in_specs=[row(D_MODEL), row(BRANCH_W), row(BRANCH_W), row(BRANCH_W, proj_row0), row(BRANCH_W),
                  row(3 * D_MODEL, proj_row0), _full(d.shape), _layer((BRANCH_W, BRANCH_W), l),
                  _layer((3, BRANCH_W, D_MODEL), l), _layer((D_MODEL, D_MODEL), l),
                  _full(g_ffn.shape), _layer((D_MODEL, 2 * D_FF), l), _layer((D_FF, D_MODEL), l)],
        out_specs=row(D_MODEL),
        out_shape=jax.ShapeDtypeStruct((n, D_MODEL), F32),
        compiler_params=_cparams(1),
    )(x2d, oa, y, u, oc, gates, d, wglu, wb, wout, g_ffn, wup, wdn)


def _rope_tables(pos):
    half = HEAD_DIM // 2
    inv = ROPE_THETA ** (-jnp.arange(half, dtype=F32) / half)
    ang = pos[:, None] * inv[None, :]
    cos, sin = jnp.cos(ang), jnp.sin(ang)
    zero = jnp.zeros_like(sin)
    two = lambda a: jnp.concatenate([a, a], axis=1)
    return (two(jnp.concatenate([cos, cos], axis=1)), two(jnp.concatenate([-sin, zero], axis=1)),
            two(jnp.concatenate([zero, sin], axis=1)))


def _state_from_blocks(h):
    nb, steps, _, _ = h.shape
    h = h.reshape(nb, steps, SSM_SEQS, STATE_W // LANES, LANES)
    return jnp.transpose(h, (1, 2, 0, 3, 4)).reshape(steps * SSM_SEQS, SSM_GROUPS, SSM_STATE)


def kernel(x_prompt, x_sample, cache_swa_k, cache_swa_v, state_ssm_re, state_ssm_im, cache_mem_k,
           cache_mem_v, mem_prompt, attn_norm, w_in, q_norm, k_norm, attn_sinks, ssm_a_re, ssm_a_im,
           ssm_log_dt, ssm_b_re, ssm_b_im, ssm_c_re, ssm_c_im, ssm_d, ssm_w_glu, mem_norm, w_mem_kv,
           mem_q_norm, mem_k_norm, w_branch, w_out, ffn_norm, w_ffn_up, w_ffn_down):
    depth = w_in.shape[0]
    b, s, _ = x_prompt.shape
    db, t, _ = x_sample.shape
    assert db * t == ROW_TILE and s % ROW_TILE == 0

    rope = _rope_tables(jnp.concatenate([jnp.arange(s, dtype=F32),
                                         jnp.tile(PAST_LEN + jnp.arange(t, dtype=F32), db)]))
    ssm_params = (ssm_a_re, ssm_a_im, ssm_log_dt, ssm_b_re, ssm_b_im, ssm_c_re, ssm_c_im)
    tables, decay = _ssm_tables(*ssm_params, PROMPT_CHUNK)
    tab_p, tab_s = tables + decay(PROMPT_CHUNK), tables + decay(t)
    blk = jnp.arange(256) // HEAD_DIM
    ones = (blk[:, None] == blk[None, :]).astype(BF16)

    w_in_bf, w_mem_bf = w_in.astype(BF16), w_mem_kv.astype(BF16)
    wglu, wb, wout = ssm_w_glu.astype(BF16), w_branch.astype(BF16), w_out.astype(BF16)
    wup, wdn = w_ffn_up.astype(BF16), w_ffn_down.astype(BF16)
    qg, kg = jnp.tile(q_norm, (1, N_HEADS)), jnp.tile(k_norm, (1, N_KV_HEADS))
    ckt = jnp.transpose(cache_swa_k, (0, 1, 3, 4, 2))
    cvt = jnp.transpose(cache_swa_v, (0, 1, 3, 4, 2))
    cmk = cache_mem_k.reshape(depth, db, N_MEM * MEM_HEADS, MEM_HEAD_DIM)
    cmv = cache_mem_v.reshape(depth, db, N_MEM * MEM_HEADS, MEM_HEAD_DIM)
    h0r = state_ssm_re.reshape(depth, db, SSM_GROUPS * SSM_STATE)
    h0i = state_ssm_im.reshape(depth, db, SSM_GROUPS * SSM_STATE)

    yp = x_prompt.reshape(b * s, D_MODEL)
    ys = x_sample.reshape(db * t, D_MODEL)
    mem2d = mem_prompt.reshape(b * N_MEM, D_MODEL)
    outs = [[] for _ in range(6)]
    n = b * s
    mk, mv = _memkv(mem2d, mem_norm, w_mem_bf, mem_k_norm)
    mk4, mv4 = mk.reshape(depth, b, N_MEM, BRANCH_W), mv.reshape(depth, b, N_MEM, BRANCH_W)
    new_k, new_v = [], []
    for l in range(depth):
        k, v, u, gates, oa, oc, qs, mqs = _inproj_attn(l, attn_sinks, yp, ys, s, attn_norm, w_in_bf, rope, qg, kg,
                                                       mem_q_norm, ones, mk4, mv4)
        ysm, hre, him = _ssm_prompt(l, u, b, s, *tab_p)
        yp = _merge_ffn(l, yp, oa, ysm, u, oc, gates, 0, ssm_d, wglu, wb, wout, ffn_norm, wup, wdn)

        stack_now = l == depth - 1 and depth > 1
        oas, ocs, nkt, nvt = _attn_sample(l, t, attn_sinks, qs, k, v, n, ckt, cvt, mqs, cmk, cmv,
                                          new_k if stack_now else [], new_v if stack_now else [])
        new_k.append(nkt)
        new_v.append(nvt)
        yss, hrs, his = _ssm_sample(l, u, n, db * t, t, h0r, h0i, *tab_s)
        ys = _merge_ffn(l, ys, oas.astype(BF16), yss, u, ocs.astype(BF16), gates, n, ssm_d, wglu, wb, wout,
                        ffn_norm, wup, wdn)

        last_rows = lambda a: jnp.stack([a[(bi + 1) * s - WINDOW:(bi + 1) * s] for bi in range(b)]).reshape(
            b, WINDOW, N_KV_HEADS, HEAD_DIM)
        new = (last_rows(k), last_rows(v), _state_from_blocks(hre), _state_from_blocks(him),
               hrs.reshape(db, SSM_GROUPS, SSM_STATE), his.reshape(db, SSM_GROUPS, SSM_STATE))
        for lst, a in zip(outs, new):
            lst.append(a)

    swa_k_p, swa_v_p, ssm_re_p, ssm_im_p, ssm_re_s, ssm_im_s = (jnp.stack(o) for o in outs)
    mem_shape = (depth, b, N_MEM, MEM_HEADS, MEM_HEAD_DIM)
    stacked = (new_k[-1], new_v[-1]) if depth > 1 else (new_k[0][None], new_v[0][None])
    swa_k_s, swa_v_s = (jnp.transpose(c, (0, 1, 4, 2, 3)) for c in stacked)
    return (yp.reshape(b, s, D_MODEL), ys.reshape(db, t, D_MODEL), swa_k_p, swa_v_p, ssm_re_p, ssm_im_p,
            mk.reshape(mem_shape), mv.reshape(mem_shape), swa_k_s, swa_v_s, ssm_re_s, ssm_im_s)
```

```python
import functools
import math

import jax
import jax.numpy as jnp
from jax import lax
from jax.experimental import pallas as pl
from jax.experimental.pallas import tpu as pltpu

F32 = jnp.float32
BF16 = jnp.bfloat16

D_MODEL = 1024
BRANCH_W = 512
HEAD_DIM = 64
N_HEADS = 8
N_KV_HEADS = 2
KV_GROUP = 4
KV_W = N_KV_HEADS * HEAD_DIM
WINDOW = 128
ROPE_THETA = 10000.0
PAST_LEN = 16384
SSM_GROUPS = 32
SSM_GROUP_CH = 16
SSM_STATE = 64
N_MEM = 256
MEM_HEADS = 4
MEM_HEAD_DIM = 128
D_FF = 2816
RMS_EPS = 1e-6

Q_OFF, K_OFF, V_OFF, U_OFF, MQ_OFF, G_OFF = 0, 512, 640, 768, 1280, 1792
IN_W = G_OFF + 3 * D_MODEL

LANES = 128
SUBLANES = 8
MXU_TILE = 256
GROUPS_PER_LANE_BLOCK = LANES // SSM_GROUP_CH
N_LANE_BLOCKS = BRANCH_W // LANES
STATE_W = GROUPS_PER_LANE_BLOCK * SSM_STATE
PROMPT_CHUNK = 8
ROW_TILE = 512
VMEM_LIMIT = 56 * 1024 * 1024


def _dot(a, b):
    return jnp.dot(a, b, preferred_element_type=F32)


def _dot_nt(a, b):
    return lax.dot_general(a, b, (((1,), (1,)), ((), ())), preferred_element_type=F32)


def _cparams(n_axes):
    return pltpu.CompilerParams(dimension_semantics=("arbitrary",) * n_axes,
                                vmem_limit_bytes=VMEM_LIMIT)


def _full(shape):
    nd = len(shape)
    return pl.BlockSpec(shape, lambda *_: (0,) * nd)


def _layer(shape, l):
    nd = len(shape)
    return pl.BlockSpec((None,) + tuple(shape), lambda *_: (l,) + (0,) * nd, pipeline_mode=pl.Buffered(1))


def _sigmoid(x):
    return 0.5 * jnp.tanh(0.5 * x) + 0.5


def _rms(x):
    return x * lax.rsqrt(jnp.mean(x * x, axis=-1, keepdims=True) + RMS_EPS)


def _memkv_kernel(mem_ref, g_ref, w_ref, kg_ref, k_ref, v_ref):
    l = pl.program_id(0)
    h = _rms(mem_ref[...]) * g_ref[pl.ds(l, 1), :]
    kv = _dot(h.astype(BF16), w_ref[...])
    kg = kg_ref[pl.ds(l, 1), :]
    for hd in range(MEM_HEADS):
        sl = slice(hd * MEM_HEAD_DIM, (hd + 1) * MEM_HEAD_DIM)
        k_ref[:, sl] = _rms(kv[:, sl]) * kg
    v_ref[...] = kv[:, BRANCH_W:]


def _memkv(mem2d, g, w_bf, kg):
    n = mem2d.shape[0]
    depth = w_bf.shape[0]
    out = pl.BlockSpec((None, ROW_TILE, BRANCH_W), lambda l, i: (l, i, 0))
    return pl.pallas_call(
        _memkv_kernel, name="memkv",
        grid=(depth, n // ROW_TILE),
        in_specs=[pl.BlockSpec((ROW_TILE, D_MODEL), lambda l, i: (i, 0)), _full(g.shape),
                  pl.BlockSpec((None, D_MODEL, 2 * BRANCH_W), lambda l, i: (l, 0, 0)), _full(kg.shape)],
        out_specs=[out, out],
        out_shape=[jax.ShapeDtypeStruct((depth, n, BRANCH_W), F32)] * 2,
        compiler_params=_cparams(2),
    )(mem2d, g, w_bf, kg)


def _rotary(x, cos, sin_lo, sin_hi):
    w = x.shape[-1]
    half = HEAD_DIM // 2
    return x * cos + pltpu.roll(x, w - half, 1) * sin_lo + pltpu.roll(x, half, 1) * sin_hi


GATE_CHUNK = 512
N_GATE_CHUNKS = 3 * D_MODEL // GATE_CHUNK


class _Projection:
    def __init__(self, x, g, w_ref, rope, qg, kg, mqg, ones):
        self.hb = (_rms(x) * g).astype(BF16)
        self.w_ref, self.qg, self.kg, self.mqg = w_ref, qg, kg, mqg
        self.ones = ones
        self.rope = rope

    def queries(self):
        zq = _dot(self.hb, self.w_ref[:, Q_OFF:K_OFF])
        sq = (zq * zq).astype(BF16)
        ssq = jnp.concatenate([_dot(sq[:, :256], self.ones), _dot(sq[:, 256:], self.ones)], axis=1)
        qn = zq * lax.rsqrt(ssq * (1.0 / HEAD_DIM) + RMS_EPS) * self.qg
        rope4 = (jnp.concatenate([t] * 4, axis=1) for t in self.rope)
        return (_rotary(qn, *rope4) * (1.0 / math.sqrt(HEAD_DIM))).astype(BF16)

    def keys_values(self):
        zkv = _dot(self.hb, self.w_ref[:, K_OFF:U_OFF])
        zk = zkv[:, :KV_W]
        ssk = _dot((zk * zk).astype(BF16), self.ones[:KV_W, :KV_W])
        kn = zk * lax.rsqrt(ssk * (1.0 / HEAD_DIM) + RMS_EPS) * self.kg
        return _rotary(kn, *self.rope), zkv[:, KV_W:]

    def ssm_input(self):
        return _dot(self.hb, self.w_ref[:, U_OFF:MQ_OFF])

    def memory_queries(self):
        zm = _dot(self.hb, self.w_ref[:, MQ_OFF:G_OFF])
        mqg = self.mqg * (1.0 / math.sqrt(MEM_HEAD_DIM))
        return jnp.concatenate([_rms(zm[:, hd * MEM_HEAD_DIM:(hd + 1) * MEM_HEAD_DIM]) * mqg
                                for hd in range(MEM_HEADS)], axis=1).astype(BF16)

    def gates(self, c):
        lo = G_OFF + c * GATE_CHUNK
        return _sigmoid(_dot(self.hb, self.w_ref[:, lo:lo + GATE_CHUNK])).astype(BF16)


ATTN_TILE = 512


def _softmax_rows(s):
    m = jnp.max(s, axis=-1, keepdims=True)
    p = jnp.exp(s - m)
    return p, jnp.sum(p, axis=-1, keepdims=True)


def _sink_column(sink_ref, l, heads, rows_per_head):
    n = len(heads) * rows_per_head
    rcol = lax.broadcasted_iota(jnp.int32, (n, 1), 0)
    sink = jnp.full((n, 1), sink_ref[l, heads[-1]], F32)
    for g in range(len(heads) - 2, -1, -1):
        sink = jnp.where(rcol < (g + 1) * rows_per_head, sink_ref[l, heads[g]], sink)
    return sink


def _sink_softmax(s, sink):
    m = jnp.maximum(jnp.max(s, axis=-1, keepdims=True), sink)
    p = jnp.exp(s - m)
    return p, jnp.sum(p, axis=-1, keepdims=True) + jnp.exp(sink - m)


def _attend_sub_block(sub, mask, sinks, q_ref, mq_ref, k_ref, v_ref, mk_ref, mv_ref, oa_ref, oc_ref):
    qrows = slice(sub * WINDOW, (sub + 1) * WINDOW)
    krows = slice(sub * WINDOW, (sub + 2) * WINDOW)
    ksl = [slice(kvh * HEAD_DIM, (kvh + 1) * HEAD_DIM) for kvh in range(N_KV_HEADS)]
    assert ATTN_TILE // WINDOW == MEM_HEADS
    msl = [slice(sub * MEM_HEAD_DIM, (sub + 1) * MEM_HEAD_DIM)]
    mrows = slice(0, ATTN_TILE)
    heads = [[kvh * KV_GROUP + g for g in range(KV_GROUP)] for kvh in range(N_KV_HEADS)]
    scores = []
    for kvh in range(N_KV_HEADS):
        qg = jnp.concatenate([q_ref[qrows, hd * HEAD_DIM:(hd + 1) * HEAD_DIM] for hd in heads[kvh]], axis=0)
        scores.append(jnp.where(mask, _dot_nt(qg, k_ref[krows, ksl[kvh]]), -jnp.inf))
    mscores = [_dot_nt(mq_ref[mrows, sl], mk_ref[:, sl]) for sl in msl]
    yield
    probs = [_sink_softmax(scores[kvh], sinks[kvh]) for kvh in range(N_KV_HEADS)]
    mprobs = [_softmax_rows(s) for s in mscores]
    yield
    outs = [_dot(probs[kvh][0].astype(BF16), v_ref[krows, ksl[kvh]]) / probs[kvh][1]
            for kvh in range(N_KV_HEADS)]
    mouts = [_dot(mprobs[hd][0].astype(BF16), mv_ref[:, msl[hd]]) / mprobs[hd][1] for hd in range(len(msl))]
    for kvh in range(N_KV_HEADS):
        for g, hd in enumerate(heads[kvh]):
            oa_ref[qrows, hd * HEAD_DIM:(hd + 1) * HEAD_DIM] = outs[kvh][g * WINDOW:(g + 1) * WINDOW].astype(BF16)
    for hd in range(len(msl)):
        oc_ref[mrows, msl[hd]] = mouts[hd].astype(BF16)


def _inproj_attn_kernel(l, tiles_per_seq, sink_ref, x_ref, xs_ref, g_ref, w_ref, cos_ref, slo_ref, shi_ref,
                        qg_ref, kg_ref, mqg_ref, ones_ref, mk_ref, mv_ref,
                        k_ref, v_ref, u_ref, gates_ref, oa_ref, oc_ref, qs_ref, mqs_ref,
                        q_scr, mq_scr, k_scr, v_scr, mk_scr, mv_scr):
    i = pl.program_id(0)
    cur, prv = i % 2, (i + 1) % 2
    decode_step = i == pl.num_programs(0) - 1

    @pl.when(i == 0)
    def _():
        for scr in (q_scr, mq_scr, k_scr, v_scr, mk_scr, mv_scr):
            scr[...] = jnp.zeros(scr.shape, scr.dtype)

    @pl.when(i % tiles_per_seq == 1)
    def _():
        mk_scr[...] = mk_ref[...].astype(BF16)
        mv_scr[...] = mv_ref[...].astype(BF16)

    rows = KV_GROUP * WINDOW
    r = lax.broadcasted_iota(jnp.int32, (rows, 2 * WINDOW), 0)
    j = lax.broadcasted_iota(jnp.int32, (rows, 2 * WINDOW), 1)
    qi = r % WINDOW
    band = (j > qi) & (j <= qi + WINDOW)
    first_band = band & ((j >= WINDOW) | (i % tiles_per_seq != 1))
    sinks = [_sink_column(sink_ref, l, [kvh * KV_GROUP + g for g in range(KV_GROUP)], WINDOW)
             for kvh in range(N_KV_HEADS)]

    def attention_stages():
        for sub in range(ATTN_TILE // WINDOW):
            yield from _attend_sub_block(sub, first_band if sub == 0 else band, sinks, q_scr.at[prv],
                                         mq_scr.at[prv], k_scr.at[prv], v_scr.at[prv], mk_scr, mv_scr,
                                         oa_ref, oc_ref)
            yield

    proj = _Projection(jnp.where(decode_step, xs_ref[...], x_ref[...]), g_ref[l:l + 1, :], w_ref,
                       (cos_ref[...], slo_ref[...], shi_ref[...]), qg_ref[l:l + 1, :], kg_ref[l:l + 1, :],
                       mqg_ref[l:l + 1, :], ones_ref[...])

    def store_queries():
        q = proj.queries()
        q_scr[cur] = q
        qs_ref[...] = q

    def store_keys_values():
        k, v = proj.keys_values()
        k_ref[...], v_ref[...] = k, v
        k_scr[cur, :WINDOW, :] = k_scr[prv, ATTN_TILE:, :]
        v_scr[cur, :WINDOW, :] = v_scr[prv, ATTN_TILE:, :]
        k_scr[cur, WINDOW:, :] = k.astype(BF16)
        v_scr[cur, WINDOW:, :] = v.astype(BF16)

    def store_ssm_input():
        u_ref[...] = proj.ssm_input()

    def store_memory_queries():
        mq = proj.memory_queries()
        mq_scr[cur] = mq
        mqs_ref[...] = mq

    def store_gates(c):
        gates_ref[:, c * GATE_CHUNK:(c + 1) * GATE_CHUNK] = proj.gates(c)

    segments = [store_queries, store_keys_values, store_ssm_input, store_memory_queries]
    segments += [functools.partial(store_gates, c) for c in range(N_GATE_CHUNKS)]
    stages = attention_stages()
    n_stages = 3 * (ATTN_TILE // WINDOW)
    for n, segment in enumerate(segments):
        for _ in range((n + 1) * n_stages // len(segments) - n * n_stages // len(segments)):
            next(stages)
        segment()


def _inproj_attn(l, sinks, x2d, xs2d, seq_len, g, w_bf, rope, qg, kg, mqg, ones, mk, mv):
    n = x2d.shape[0]
    n_tiles = n // ROW_TILE
    tiles_per_seq = seq_len // ROW_TILE
    assert ROW_TILE == ATTN_TILE and tiles_per_seq > 1 and xs2d.shape[0] == ROW_TILE
    attn_tile = lambda i: jnp.maximum(i - 1, 0)
    xrow = pl.BlockSpec((ROW_TILE, D_MODEL), lambda i: (jnp.minimum(i, n_tiles - 1), 0))
    orow = lambda w: pl.BlockSpec((ROW_TILE, w), lambda i: (i, 0))
    arow = pl.BlockSpec((ROW_TILE, BRANCH_W), lambda i: (attn_tile(i), 0))
    rspec = pl.BlockSpec((ROW_TILE, LANES), lambda i: (jnp.where(i == n_tiles, tiles_per_seq, i % tiles_per_seq), 0))
    mem = pl.BlockSpec((None, None, N_MEM, BRANCH_W), lambda i: (l, attn_tile(i) // tiles_per_seq, 0, 0))
    widths = (KV_W, KV_W, BRANCH_W, 3 * D_MODEL)
    dtypes = (F32, F32, F32, BF16)
    return pl.pallas_call(
        functools.partial(_inproj_attn_kernel, l, tiles_per_seq), name="inproj_attn",
        grid=(n_tiles + 1,),
        in_specs=[pl.BlockSpec(memory_space=pltpu.SMEM), xrow, _full((ROW_TILE, D_MODEL)), _full(g.shape),
                  _layer((D_MODEL, IN_W), l), rspec, rspec, rspec, _full(qg.shape), _full(kg.shape),
                  _full(mqg.shape), _full((256, 256)), mem, mem],
        out_specs=[orow(w) for w in widths] + [arow, arow] + [_full((ROW_TILE, BRANCH_W))] * 2,
        out_shape=[jax.ShapeDtypeStruct((n + ROW_TILE, w), d) for w, d in zip(widths, dtypes)]
                  + [jax.ShapeDtypeStruct((n, BRANCH_W), BF16)] * 2
                  + [jax.ShapeDtypeStruct((ROW_TILE, BRANCH_W), BF16)] * 2,
        scratch_shapes=[pltpu.VMEM((2, ROW_TILE, BRANCH_W), BF16)] * 2
                       + [pltpu.VMEM((2, WINDOW + ROW_TILE, KV_W), BF16)] * 2
                       + [pltpu.VMEM((N_MEM, BRANCH_W), BF16)] * 2,
        compiler_params=_cparams(1),
    )(sinks, x2d, xs2d, g, w_bf, *rope, qg, kg, mqg, ones, mk, mv)


DEC_BLOCK = 8
TOK_PAD = 8


def _pad_rows(x, n):
    return jnp.concatenate([x, jnp.zeros((n - x.shape[0], x.shape[1]), x.dtype)], axis=0)


def _attn_sample_kernel(l, t, sink_ref, q_ref, kn_ref, vn_ref, ckt_ref, cvt_ref, qm_ref, mk_ref, mv_ref,
                        *earlier_and_outputs):
    oa_ref, oc_ref, nkt_ref, nvt_ref = earlier_and_outputs[-4:]
    earlier = earlier_and_outputs[:-4]
    if earlier:
        for j in range(l):
            nkt_ref[j] = earlier[j][...]
            nvt_ref[j] = earlier[l + j][...]
        nkt_ref, nvt_ref = nkt_ref.at[l], nvt_ref.at[l]
    keys = 2 * WINDOW
    rows = N_HEADS * TOK_PAD
    rk = lax.broadcasted_iota(jnp.int32, (rows, keys), 0)
    jk = lax.broadcasted_iota(jnp.int32, (rows, keys), 1)
    tk = rk % TOK_PAD
    mask = (tk < t) & (((jk < WINDOW) & (jk > tk)) | ((jk >= WINDOW) & (jk - WINDOW <= tk)))
    sink = _sink_column(sink_ref, l, list(range(N_HEADS)), TOK_PAD)
    mem_rows = N_MEM * MEM_HEADS
    mem_own = (lax.broadcasted_iota(jnp.int32, (MEM_HEADS * TOK_PAD, mem_rows), 1) % MEM_HEADS
               == lax.broadcasted_iota(jnp.int32, (MEM_HEADS * TOK_PAD, mem_rows), 0) // TOK_PAD)
    is_new = lax.broadcasted_iota(jnp.int32, (HEAD_DIM, WINDOW), 1) >= WINDOW - t
    dsl = [slice(kvh * HEAD_DIM, (kvh + 1) * HEAD_DIM) for kvh in range(N_KV_HEADS)]
    group_rows = KV_GROUP * TOK_PAD

    q_all, qm_all = q_ref[...].astype(F32), qm_ref[...].astype(F32)
    seqs = range(DEC_BLOCK)
    toks = [slice(bi * t, (bi + 1) * t) for bi in seqs]
    knp = [_pad_rows(kn_ref[toks[bi], :], WINDOW) for bi in seqs]
    vnp = [_pad_rows(vn_ref[toks[bi], :], WINDOW) for bi in seqs]
    knt, vnt = [x.T for x in knp], [x.T for x in vnp]
    for bi in seqs:
        for kvh in range(N_KV_HEADS):
            nkt_ref[bi, kvh] = jnp.where(is_new, pltpu.roll(knt[bi][dsl[kvh], :], WINDOW - t, 1),
                                         pltpu.roll(ckt_ref[bi, kvh], WINDOW - t, 1))
            nvt_ref[bi, kvh] = jnp.where(is_new, pltpu.roll(vnt[bi][dsl[kvh], :], WINDOW - t, 1),
                                         pltpu.roll(cvt_ref[bi, kvh], WINDOW - t, 1))
    qg = []
    for bi in seqs:
        q8 = _pad_rows(q_all[toks[bi]], TOK_PAD)
        qg.append([jnp.concatenate([q8[:, hd * HEAD_DIM:(hd + 1) * HEAD_DIM]
                                    for hd in range(kvh * KV_GROUP, (kvh + 1) * KV_GROUP)], axis=0).astype(BF16)
                   for kvh in range(N_KV_HEADS)])
    scores = []
    for bi in seqs:
        per_kvh = [jnp.concatenate([_dot(qg[bi][kvh], ckt_ref[bi, kvh].astype(BF16)),
                                    _dot_nt(qg[bi][kvh], knp[bi][:, dsl[kvh]].astype(BF16))], axis=1)
                   for kvh in range(N_KV_HEADS)]
        scores.append(jnp.where(mask, jnp.concatenate(per_kvh, axis=0), -jnp.inf))
    mscores = []
    for bi in seqs:
        qm8 = jnp.concatenate([qm_all[toks[bi]], jnp.zeros((TOK_PAD - t, BRANCH_W), F32)], axis=0).astype(BF16)
        qst = jnp.concatenate([qm8[:, hd * MEM_HEAD_DIM:(hd + 1) * MEM_HEAD_DIM] for hd in range(MEM_HEADS)],
                              axis=0)
        mscores.append(jnp.where(mem_own, _dot_nt(qst, mk_ref[bi].astype(BF16)), -jnp.inf))
    probs = [_sink_softmax(scores[bi], sink) for bi in seqs]
    mprobs = [_softmax_rows(mscores[bi]) for bi in seqs]
    outs = []
    for bi in seqs:
        p = probs[bi][0].astype(BF16)
        per_kvh = []
        for kvh in range(N_KV_HEADS):
            pk = p[kvh * group_rows:(kvh + 1) * group_rows]
            per_kvh.append(_dot_nt(pk[:, :WINDOW], cvt_ref[bi, kvh].astype(BF16))
                           + _dot(pk[:, WINDOW:], vnp[bi][:, dsl[kvh]].astype(BF16)))
        outs.append(jnp.concatenate(per_kvh, axis=0) / probs[bi][1])
    mouts = [_dot(mprobs[bi][0].astype(BF16), mv_ref[bi].astype(BF16)) / mprobs[bi][1] for bi in seqs]
    for bi in seqs:
        oa_ref[toks[bi], :] = jnp.concatenate(
            [outs[bi][hd * TOK_PAD:(hd + 1) * TOK_PAD] for hd in range(N_HEADS)], axis=1)[0:t]
        oc_ref[toks[bi], :] = jnp.concatenate(
            [mouts[bi][hd * TOK_PAD:(hd + 1) * TOK_PAD] for hd in range(MEM_HEADS)], axis=1)[0:t]


def _attn_sample(l, t, sinks, q, kn, vn, kv_row0, ckt, cvt, qm, mk, mv, earlier_k, earlier_v):
    n = q.shape[0]
    tok = lambda w, row0=0: pl.BlockSpec((DEC_BLOCK * t, w), lambda i: (i + row0 // (DEC_BLOCK * t), 0))
    cache_tile = (DEC_BLOCK, N_KV_HEADS, HEAD_DIM, WINDOW)
    cache_in = pl.BlockSpec((None,) + cache_tile, lambda i: (l, i, 0, 0, 0))
    one_layer = pl.BlockSpec(cache_tile, lambda i: (i, 0, 0, 0))
    mem = pl.BlockSpec((None, DEC_BLOCK, N_MEM * MEM_HEADS, MEM_HEAD_DIM), lambda i: (l, i, 0, 0))
    assert len(earlier_k) == len(earlier_v) and len(earlier_k) in (0, l)
    if earlier_k:
        cache_out = pl.BlockSpec((l + 1,) + cache_tile, lambda i: (0, i, 0, 0, 0))
        cache_shape = jax.ShapeDtypeStruct((l + 1,) + ckt.shape[1:], F32)
    else:
        cache_out, cache_shape = one_layer, jax.ShapeDtypeStruct(ckt.shape[1:], F32)
    return pl.pallas_call(
        functools.partial(_attn_sample_kernel, l, t), name="attn_sample",
        grid=(n // (DEC_BLOCK * t),),
        in_specs=[pl.BlockSpec(memory_space=pltpu.SMEM), tok(BRANCH_W), tok(KV_W, kv_row0), tok(KV_W, kv_row0),
                  cache_in, cache_in, tok(BRANCH_W), mem, mem] + [one_layer] * (2 * len(earlier_k)),
        out_specs=[tok(BRANCH_W), tok(BRANCH_W), cache_out, cache_out],
        out_shape=[jax.ShapeDtypeStruct((n, BRANCH_W), F32)] * 2 + [cache_shape] * 2,
        compiler_params=_cparams(1),
    )(sinks, q, kn, vn, ckt, cvt, qm, mk, mv, *earlier_k, *earlier_v)


def _ssm_tables(a_re, a_im, log_dt, b_re, b_im, c_re, c_im, chunk):
    hp = lax.Precision.HIGHEST
    dt = jnp.exp(log_dt)[..., None]
    mag = jnp.exp(a_re * dt)
    lam_re, lam_im = mag * jnp.cos(a_im * dt), mag * jnp.sin(a_im * dt)
    den = a_re * a_re + a_im * a_im
    nr, ni = lam_re - 1.0, lam_im
    g_re = (nr * a_re + ni * a_im) / den
    g_im = (ni * a_re - nr * a_im) / den
    bg_re = g_re[..., None] * b_re - g_im[..., None] * b_im
    bg_im = g_re[..., None] * b_im + g_im[..., None] * b_re
    def powers(d):
        d = d.astype(F32)[:, None, None, None]
        pm = jnp.exp(d * (a_re * dt)[None])
        return pm * jnp.cos(d * (a_im * dt)[None]), pm * jnp.sin(d * (a_im * dt)[None])

    def times_gb(p_re, p_im):
        return (p_re[..., None] * bg_re[None] - p_im[..., None] * bg_im[None],
                p_re[..., None] * bg_im[None] + p_im[..., None] * bg_re[None])

    pw_re, pw_im = powers(jnp.arange(chunk + 1))
    z_re, z_im = times_gb(pw_re[:chunk], pw_im[:chunk])
    zrev_re, zrev_im = times_gb(*powers(jnp.arange(chunk - 1, -1, -1)))
    kern = (jnp.einsum('lgop,dlgpi->dlgio', c_re, z_re, precision=hp)
            - jnp.einsum('lgop,dlgpi->dlgio', c_im, z_im, precision=hp))
    nl = a_re.shape[0]
    nb, gb = N_LANE_BLOCKS, GROUPS_PER_LANE_BLOCK
    kd = jnp.transpose(kern.reshape(chunk, nl, nb, gb, SSM_GROUP_CH, SSM_GROUP_CH), (1, 2, 0, 3, 4, 5))
    kd = kd.reshape(nl, nb, chunk, LANES, SSM_GROUP_CH)

    def to_rows(z):
        z = z.reshape(chunk, nl, nb, gb, SSM_STATE, SSM_GROUP_CH)
        return jnp.transpose(z, (1, 2, 0, 3, 5, 4)).reshape(nl, nb, chunk, LANES, SSM_STATE)

    zr, zi = to_rows(zrev_re), to_rows(zrev_im)
    zc = jnp.concatenate([zr, zr, zi, zi], axis=-1)

    pr, pi = pw_re[1:], pw_im[1:]
    f_a = c_re[None] * pr[:, :, :, None, :] - c_im[None] * pi[:, :, :, None, :]
    f_b = c_re[None] * pi[:, :, :, None, :] + c_im[None] * pr[:, :, :, None, :]

    def from_cols(f):
        f = f.reshape(chunk, nl, nb, gb, SSM_GROUP_CH, SSM_STATE)
        return jnp.transpose(f, (1, 2, 0, 4, 3, 5)).reshape(nl, nb, chunk, SSM_GROUP_CH, STATE_W)

    ft = jnp.concatenate([from_cols(f_a), from_cols(-f_b)], axis=-1)

    kw = chunk * LANES
    spec = lambda *shape: pl.BlockSpec((None, None) + shape, lambda l, n: (l, n) + (0,) * len(shape))
    intra, to_st, from_st_t = pl.pallas_call(
        _ssm_table_kernel, name="ssm_tables",
        grid=(nl, nb),
        in_specs=[spec(chunk, LANES, SSM_GROUP_CH), spec(chunk, LANES, 2 * LANES),
                  spec(chunk, SSM_GROUP_CH, 2 * STATE_W)],
        out_specs=[spec(kw, kw), spec(kw, 2 * STATE_W), spec(kw, 2 * STATE_W)],
        out_shape=[jax.ShapeDtypeStruct((nl, nb, kw, kw), BF16),
                   jax.ShapeDtypeStruct((nl, nb, kw, 2 * STATE_W), BF16),
                   jax.ShapeDtypeStruct((nl, nb, kw, 2 * STATE_W), BF16)],
        compiler_params=_cparams(2),
    )(kd, zc, ft)
    decay = lambda d: (pw_re[d].reshape(nl, nb, 1, STATE_W), pw_im[d].reshape(nl, nb, 1, STATE_W))
    return (intra, to_st, from_st_t), decay


def _ssm_table_kernel(kd_ref, zc_ref, ft_ref, intra_ref, to_ref, fromt_ref):
    chunk = kd_ref.shape[0]
    gch = SSM_GROUP_CH
    r = lax.broadcasted_iota(jnp.int32, (LANES, LANES), 0)
    c = lax.broadcasted_iota(jnp.int32, (LANES, LANES), 1)
    same_group = (r // gch) == (c // gch)
    spread = (lax.broadcasted_iota(jnp.int32, (gch, LANES), 1) % gch
              == lax.broadcasted_iota(jnp.int32, (gch, LANES), 0)).astype(BF16)
    spread_t = (lax.broadcasted_iota(jnp.int32, (LANES, gch), 0) % gch
                == lax.broadcasted_iota(jnp.int32, (LANES, gch), 1)).astype(BF16)
    rs = lax.broadcasted_iota(jnp.int32, (LANES, 2 * STATE_W), 0)
    cs = lax.broadcasted_iota(jnp.int32, (LANES, 2 * STATE_W), 1)
    own_state = (rs // gch) == ((cs % STATE_W) // SSM_STATE)

    lag_blocks = [jnp.where(same_group, _dot(kd_ref[d].astype(BF16), spread), 0.0).astype(BF16)
                  for d in range(chunk)]
    zero = jnp.zeros((LANES, LANES), BF16)
    for s in range(chunk):
        for t in range(chunk):
            intra_ref[s * LANES:(s + 1) * LANES, t * LANES:(t + 1) * LANES] = (
                lag_blocks[t - s] if t >= s else zero)
        z = zc_ref[s]
        full = jnp.concatenate([z[:, :LANES]] * (STATE_W // LANES) + [z[:, LANES:]] * (STATE_W // LANES), axis=1)
        to_ref[s * LANES:(s + 1) * LANES, :] = jnp.where(own_state, full, 0.0).astype(BF16)
        f = _dot(spread_t, ft_ref[s].astype(BF16))
        fromt_ref[s * LANES:(s + 1) * LANES, :] = jnp.where(own_state, f, 0.0).astype(BF16)


SSM_SEQS = 2


def _chunk_rows(ref, chunk):
    n = ref.shape[0] // chunk
    return jnp.concatenate([ref[pl.ds(s, n, stride=chunk), :] for s in range(chunk)], axis=1)


def _store_chunk_rows(ref, y, chunk):
    n = ref.shape[0] // chunk
    for t in range(chunk):
        ref[pl.ds(t, n, stride=chunk), :] = y[:, t * LANES:(t + 1) * LANES]


def _ssm_prompt_kernel(u_ref, intra_ref, to_ref, fromt_ref, dre_ref, dim_ref,
                       y_ref, hre_ref, him_ref, st_scr):
    chunk = PROMPT_CHUNK
    nchunks = u_ref.shape[0] // (chunk * SSM_SEQS)
    x = _chunk_rows(u_ref, chunk).astype(BF16)
    kw = chunk * LANES
    y = jnp.concatenate([_dot(x[:, :c + MXU_TILE], intra_ref[:c + MXU_TILE, c:c + MXU_TILE])
                         for c in range(0, kw, MXU_TILE)], axis=1)
    add_all = _dot(x, to_ref[...])
    nblk = STATE_W // LANES
    assert SSM_SEQS * nblk == SUBLANES
    slots = [(seq, cb) for seq in range(SSM_SEQS) for cb in range(nblk)]

    def slot_rows(seq, cb):
        return pl.ds(seq * nblk + cb, nchunks, stride=SUBLANES)

    for c in range(2):
        for seq, cb in slots:
            col = (c * nblk + cb) * LANES
            st_scr[c, slot_rows(seq, cb), :] = add_all[seq * nchunks:(seq + 1) * nchunks, col:col + LANES]
    dre = jnp.concatenate([dre_ref[:, cb * LANES:(cb + 1) * LANES] for _, cb in slots], axis=0)
    dim = jnp.concatenate([dim_ref[:, cb * LANES:(cb + 1) * LANES] for _, cb in slots], axis=0)

    def cmul(ar, ai, br, bi):
        return ar * br - ai * bi, ar * bi + ai * br

    d2r, d2i = cmul(dre, dim, dre, dim)

    def step(p, carry):
        hr, hi = carry
        first = pl.ds(pl.multiple_of(2 * p * SUBLANES, SUBLANES), SUBLANES)
        second = pl.ds(pl.multiple_of((2 * p + 1) * SUBLANES, SUBLANES), SUBLANES)
        s0r, s0i = st_scr[0, first, :], st_scr[1, first, :]
        s1r, s1i = st_scr[0, second, :], st_scr[1, second, :]
        st_scr[0, first, :] = hr
        st_scr[1, first, :] = hi
        mr, mi = cmul(dre, dim, hr, hi)
        st_scr[0, second, :] = mr + s0r
        st_scr[1, second, :] = mi + s0i
        tr, ti = cmul(dre, dim, s0r, s0i)
        nr, ni = cmul(d2r, d2i, hr, hi)
        return nr + (tr + s1r), ni + (ti + s1i)

    assert nchunks % 2 == 0
    zero = jnp.zeros((SUBLANES, LANES), F32)
    hr, hi = lax.fori_loop(0, nchunks // 2, step, (zero, zero), unroll=4)
    hre_ref[...] = hr
    him_ref[...] = hi
    h_in = jnp.concatenate(
        [jnp.concatenate([st_scr[c, slot_rows(seq, cb), :] for seq in range(SSM_SEQS)], axis=0)
         for c in range(2) for cb in range(nblk)], axis=1).astype(BF16)
    _store_chunk_rows(y_ref, y + _dot_nt(h_in, fromt_ref[...]), chunk)


def _ssm_prompt(l, u2d, n_seqs, seq_len, intra, to_st, from_st_t, dec_re, dec_im):
    n = n_seqs * seq_len
    chunk = PROMPT_CHUNK
    rows = SSM_SEQS * seq_len
    steps = n // rows
    kw = chunk * LANES
    data = pl.BlockSpec((rows, LANES), lambda j, h: (h, j))
    wspec = lambda r, c: pl.BlockSpec((None, None, r, c), lambda j, h: (l, j, 0, 0))
    st_out = pl.BlockSpec((None, None, SUBLANES, LANES), lambda j, h: (j, h, 0, 0))
    st_shape = jax.ShapeDtypeStruct((N_LANE_BLOCKS, steps, SUBLANES, LANES), F32)
    return pl.pallas_call(
        _ssm_prompt_kernel, name="ssm_prompt",
        grid=(N_LANE_BLOCKS, steps),
        in_specs=[data, wspec(kw, kw), wspec(kw, 2 * STATE_W), wspec(kw, 2 * STATE_W),
                  wspec(1, STATE_W), wspec(1, STATE_W)],
        out_specs=[data, st_out, st_out],
        out_shape=[jax.ShapeDtypeStruct((n, BRANCH_W), F32), st_shape, st_shape],
        scratch_shapes=[pltpu.VMEM((2, SUBLANES * seq_len // chunk, LANES), F32)],
        compiler_params=_cparams(2),
    )(u2d, intra, to_st, from_st_t, dec_re, dec_im)


def _ssm_sample_kernel(chunk, u_ref, h0r_ref, h0i_ref, intra_ref, to_ref, fromt_ref, dre_ref, dim_ref,
                       y_ref, hre_ref, him_ref):
    x = _chunk_rows(u_ref, chunk).astype(BF16)
    hr, hi = h0r_ref[...], h0i_ref[...]
    dre, dim = dre_ref[...], dim_ref[...]
    add = _dot(x, to_ref[...])
    hre_ref[...] = dre * hr - dim * hi + add[:, :STATE_W]
    him_ref[...] = dre * hi + dim * hr + add[:, STATE_W:]
    h0 = jnp.concatenate([hr, hi], axis=1).astype(BF16)
    _store_chunk_rows(y_ref, _dot(x, intra_ref[...]) + _dot_nt(h0, fromt_ref[...]), chunk)


def _ssm_sample(l, u2d, row0, n, t, h0_re, h0_im, intra, to_st, from_st_t, dec_re, dec_im):
    b = n // t
    kw = t * LANES
    assert PROMPT_CHUNK % t == 0 and row0 % n == 0
    last = PROMPT_CHUNK // t - 1
    data_in = pl.BlockSpec((n, LANES), lambda j: (row0 // n, j))
    data = pl.BlockSpec((n, LANES), lambda j: (0, j))
    st_in = pl.BlockSpec((None, b, STATE_W), lambda j: (l, 0, j))
    st_out = pl.BlockSpec((b, STATE_W), lambda j: (0, j))
    wspec = lambda r, c, rb=0: pl.BlockSpec((None, None, r, c), lambda j: (l, j, rb, 0))
    return pl.pallas_call(
        functools.partial(_ssm_sample_kernel, t), name="ssm_sample",
        grid=(N_LANE_BLOCKS,),
        in_specs=[data_in, st_in, st_in, wspec(kw, kw), wspec(kw, 2 * STATE_W, last), wspec(kw, 2 * STATE_W),
                  wspec(1, STATE_W), wspec(1, STATE_W)],
        out_specs=[data, st_out, st_out],
        out_shape=[jax.ShapeDtypeStruct((n, BRANCH_W), F32),
                   jax.ShapeDtypeStruct((b, N_LANE_BLOCKS * STATE_W), F32),
                   jax.ShapeDtypeStruct((b, N_LANE_BLOCKS * STATE_W), F32)],
        compiler_params=_cparams(1),
    )(u2d, h0_re, h0_im, intra, to_st, from_st_t, dec_re, dec_im)


FF_CHUNKS = ((0, 512), (512, 512), (1024, 512), (1536, 512), (2048, 512), (2560, 256))


def _merge_ffn_kernel(l, x_ref, oa_ref, y_ref, u_ref, oc_ref, gates_ref, d_ref, wglu_ref, wb_ref, wout_ref,
                      g_ref, wup_ref, wdn_ref, o_ref):
    z = jax.nn.gelu(y_ref[...] + d_ref[l:l + 1, :] * u_ref[...])
    ob = (z * _sigmoid(_dot(z.astype(BF16), wglu_ref[...]))).astype(BF16)
    gate = lambda n: gates_ref[:, n * D_MODEL:(n + 1) * D_MODEL].astype(F32)
    merged = gate(0) * _dot(oa_ref[...], wb_ref[0])
    merged = merged + gate(1) * _dot(ob, wb_ref[1])
    merged = merged + gate(2) * _dot(oc_ref[...], wb_ref[2])
    x = x_ref[...] + _dot(merged.astype(BF16), wout_ref[...])
    h = (_rms(x) * g_ref[l:l + 1, :]).astype(BF16)
    def up_proj(lo, w):
        return _dot(h, wup_ref[:, lo:lo + w]), _dot(h, wup_ref[:, D_FF + lo:D_FF + lo + w])

    acc = x
    nxt = up_proj(*FF_CHUNKS[0])
    for c, (lo, w) in enumerate(FF_CHUNKS):
        up_gate, up = nxt
        if c + 1 < len(FF_CHUNKS):
            nxt = up_proj(*FF_CHUNKS[c + 1])
        acc = acc + _dot((up_gate * _sigmoid(up_gate) * up).astype(BF16), wdn_ref[lo:lo + w, :])
    o_ref[...] = acc


def _merge_ffn(l, x2d, oa, y, u, oc, gates, proj_row0, d, wglu, wb, wout, g_ffn, wup, wdn):
    n = x2d.shape[0]
    assert proj_row0 % ROW_TILE == 0
    row = lambda w, row0=0: pl.BlockSpec((ROW_TILE, w), lambda i: (i + row0 // ROW_TILE, 0))
    return pl.pallas_call(
        functools.partial(_merge_ffn_kernel, l), name="merge_ffn",
        grid=(n // ROW_TILE,),
        in_specs=[row(D_MODEL), row(BRANCH_W), row(BRANCH_W), row(BRANCH_W, proj_row0), row(BRANCH_W),
                  row(3 * D_MODEL, proj_row0), _full(d.shape), _layer((BRANCH_W, BRANCH_W), l),
                  _layer((3, BRANCH_W, D_MODEL), l), _layer((D_MODEL, D_MODEL), l),
                  _full(g_ffn.shape), _layer((D_MODEL, 2 * D_FF), l), _layer((D_FF, D_MODEL), l)],
        out_specs=row(D_MODEL),
        out_shape=jax.ShapeDtypeStruct((n, D_MODEL), F32),
        compiler_params=_cparams(1),
    )(x2d, oa, y, u, oc, gates, d, wglu, wb, wout, g_ffn, wup, wdn)


def _rope_tables(pos):
    half = HEAD_DIM // 2
    inv = ROPE_THETA ** (-jnp.arange(half, dtype=F32) / half)
    ang = pos[:, None] * inv[None, :]
    cos, sin = jnp.cos(ang), jnp.sin(ang)
    zero = jnp.zeros_like(sin)
    two = lambda a: jnp.concatenate([a, a], axis=1)
    return (two(jnp.concatenate([cos, cos], axis=1)), two(jnp.concatenate([-sin, zero], axis=1)),
            two(jnp.concatenate([zero, sin], axis=1)))


def _state_from_blocks(h):
    nb, steps, _, _ = h.shape
    h = h.reshape(nb, steps, SSM_SEQS, STATE_W // LANES, LANES)
    return jnp.transpose(h, (1, 2, 0, 3, 4)).reshape(steps * SSM_SEQS, SSM_GROUPS, SSM_STATE)


def kernel(x_prompt, x_sample, cache_swa_k, cache_swa_v, state_ssm_re, state_ssm_im, cache_mem_k,
           cache_mem_v, mem_prompt, attn_norm, w_in, q_norm, k_norm, attn_sinks, ssm_a_re, ssm_a_im,
           ssm_log_dt, ssm_b_re, ssm_b_im, ssm_c_re, ssm_c_im, ssm_d, ssm_w_glu, mem_norm, w_mem_kv,
           mem_q_norm, mem_k_norm, w_branch, w_out, ffn_norm, w_ffn_up, w_ffn_down):
    depth = w_in.shape[0]
    b, s, _ = x_prompt.shape
    db, t, _ = x_sample.shape
    assert db * t == ROW_TILE and s % ROW_TILE == 0

    rope = _rope_tables(jnp.concatenate([jnp.arange(s, dtype=F32),
                                         jnp.tile(PAST_LEN + jnp.arange(t, dtype=F32), db)]))
    ssm_params = (ssm_a_re, ssm_a_im, ssm_log_dt, ssm_b_re, ssm_b_im, ssm_c_re, ssm_c_im)
    tables, decay = _ssm_tables(*ssm_params, PROMPT_CHUNK)
    tab_p, tab_s = tables + decay(PROMPT_CHUNK), tables + decay(t)
    blk = jnp.arange(256) // HEAD_DIM
    ones = (blk[:, None] == blk[None, :]).astype(BF16)

    w_in_bf, w_mem_bf = w_in.astype(BF16), w_mem_kv.astype(BF16)
    wglu, wb, wout = ssm_w_glu.astype(BF16), w_branch.astype(BF16), w_out.astype(BF16)
    wup, wdn = w_ffn_up.astype(BF16), w_ffn_down.astype(BF16)
    qg, kg = jnp.tile(q_norm, (1, N_HEADS)), jnp.tile(k_norm, (1, N_KV_HEADS))
    ckt = jnp.transpose(cache_swa_k, (0, 1, 3, 4, 2))
    cvt = jnp.transpose(cache_swa_v, (0, 1, 3, 4, 2))
    cmk = cache_mem_k.reshape(depth, db, N_MEM * MEM_HEADS, MEM_HEAD_DIM)
    cmv = cache_mem_v.reshape(depth, db, N_MEM * MEM_HEADS, MEM_HEAD_DIM)
    h0r = state_ssm_re.reshape(depth, db, SSM_GROUPS * SSM_STATE)
    h0i = state_ssm_im.reshape(depth, db, SSM_GROUPS * SSM_STATE)

    yp = x_prompt.reshape(b * s, D_MODEL)
    ys = x_sample.reshape(db * t, D_MODEL)
    mem2d = mem_prompt.reshape(b * N_MEM, D_MODEL)
    outs = [[] for _ in range(6)]
    n = b * s
    mk, mv = _memkv(mem2d, mem_norm, w_mem_bf, mem_k_norm)
    mk4, mv4 = mk.reshape(depth, b, N_MEM, BRANCH_W), mv.reshape(depth, b, N_MEM, BRANCH_W)
    new_k, new_v = [], []
    for l in range(depth):
        k, v, u, gates, oa, oc, qs, mqs = _inproj_attn(l, attn_sinks, yp, ys, s, attn_norm, w_in_bf, rope, qg, kg,
                                                       mem_q_norm, ones, mk4, mv4)
        ysm, hre, him = _ssm_prompt(l, u, b, s, *tab_p)
        yp = _merge_ffn(l, yp, oa, ysm, u, oc, gates, 0, ssm_d, wglu, wb, wout, ffn_norm, wup, wdn)

        stack_now = l == depth - 1 and depth > 1
        oas, ocs, nkt, nvt = _attn_sample(l, t, attn_sinks, qs, k, v, n, ckt, cvt, mqs, cmk, cmv,
                                          new_k if stack_now else [], new_v if stack_now else [])
        new_k.append(nkt)
        new_v.append(nvt)
        yss, hrs, his = _ssm_sample(l, u, n, db * t, t, h0r, h0i, *tab_s)
        ys = _merge_ffn(l, ys, oas.astype(BF16), yss, u, ocs.astype(BF16), gates, n, ssm_d, wglu, wb, wout,
                        ffn_norm, wup, wdn)

        last_rows = lambda a: jnp.stack([a[(bi + 1) * s - WINDOW:(bi + 1) * s] for bi in range(b)]).reshape(
            b, WINDOW, N_KV_HEADS, HEAD_DIM)
        new = (last_rows(k), last_rows(v), _state_from_blocks(hre), _state_from_blocks(him),
               hrs.reshape(db, SSM_GROUPS, SSM_STATE), his.reshape(db, SSM_GROUPS, SSM_STATE))
        for lst, a in zip(outs, new):
            lst.append(a)

    swa_k_p, swa_v_p, ssm_re_p, ssm_im_p, ssm_re_s, ssm_im_s = (jnp.stack(o) for o in outs)
    mem_shape = (depth, b, N_MEM, MEM_HEADS, MEM_HEAD_DIM)
    stacked = (new_k[-1], new_v[-1]) if depth > 1 else (new_k[0][None], new_v[0][None])
    swa_k_s, swa_v_s = (jnp.transpose(c, (0, 1, 4, 2, 3)) for c in stacked)
    return (yp.reshape(b, s, D_MODEL), ys.reshape(db, t, D_MODEL), swa_k_p, swa_v_p, ssm_re_p, ssm_im_p,
            mk.reshape(mem_shape), mv.reshape(mem_shape), swa_k_s, swa_v_s, ssm_re_s, ssm_im_s)
```
